```python
import math, functools
import jax, jax.numpy as jnp
from jax import lax
import numpy as np

D_MODEL = 1024
BATCH = 8
SEQ = 2048
DEPTH = 1
DEC_BATCH = 32
DEC_SEQ = 32
PAST_LEN = 4096

CHUNK = 64
Q_BLOCK = 128
EPS = 1e-6
A_HEADS = 8
A_NOPE = 64
A_ROPE = 32
A_V = 64
A_Q_LORA = 256
A_KV_LORA = 128
ROPE_THETA = 10000.0
A_WIDTH = A_HEADS * A_V
B_HEADS = 8
B_KV_HEADS = 2
B_HEAD_DIM = 64
B_WIDTH = B_HEADS * B_HEAD_DIM
IDX_HEADS = 8
IDX_DIM = 64
TOP_K_MAX = 256
N_BUCKETS = 32
MAX_DISTANCE = 128

D_MIX = A_WIDTH + B_WIDTH
SPLITS = (A_Q_LORA, A_KV_LORA, A_ROPE, A_WIDTH,
          B_WIDTH, B_KV_HEADS * B_HEAD_DIM, B_KV_HEADS * B_HEAD_DIM,
          IDX_HEADS * IDX_DIM, IDX_DIM, IDX_HEADS, B_WIDTH)
IN_WIDTH = (A_Q_LORA + A_KV_LORA + A_ROPE + A_WIDTH + B_WIDTH + 2 * B_KV_HEADS * B_HEAD_DIM
            + IDX_HEADS * IDX_DIM + IDX_DIM + IDX_HEADS + B_WIDTH)

kernel_name = "hybrid_mla_dsa_streaming_step"


def rmsnorm(x, g):
    xf = x.astype(jnp.float32)
    y = xf * lax.rsqrt(jnp.mean(xf * xf, axis=-1, keepdims=True) + EPS)
    return (y * g.astype(jnp.float32)).astype(x.dtype)


def rope(x, pos):
    half = A_ROPE // 2
    inv = ROPE_THETA ** (-jnp.arange(half, dtype=jnp.float32) / half)
    ang = pos.astype(jnp.float32)[:, None] * inv[None, :]
    bshape = (pos.shape[0],) + (1,) * (x.ndim - 3) + (half,)
    cos = jnp.cos(ang).reshape(bshape)
    sin = jnp.sin(ang).reshape(bshape)
    xf = x.astype(jnp.float32)
    x1, x2 = xf[..., :half], xf[..., half:]
    return jnp.concatenate([x1 * cos - x2 * sin, x1 * sin + x2 * cos], axis=-1).astype(x.dtype)


def t5_bucket(rel):
    nb = N_BUCKETS // 2
    max_exact = nb // 2
    ret = (rel > 0).astype(jnp.int32) * nb
    n = jnp.abs(rel)
    nf = jnp.maximum(n, 1).astype(jnp.float32)
    large = max_exact + (jnp.log(nf / max_exact) / math.log(MAX_DISTANCE / max_exact)
                         * (nb - max_exact)).astype(jnp.int32)
    large = jnp.minimum(large, nb - 1)
    return ret + jnp.where(n < max_exact, n, large)


def chunk_admissible(pos_q, pos_k):
    return (pos_q[:, None] // CHUNK) >= (pos_k[None, :] // CHUNK)


def project(x, pos, norm_g, w_in, q_norm_g, kv_norm_g, w_uq, w_uk):
    b, t = x.shape[0], x.shape[1]
    h = rmsnorm(x, norm_g)
    z = jnp.einsum('btd,de->bte', h, w_in)
    offs = [int(o) for o in np.cumsum(SPLITS)[:-1]]
    (c_q, c_kv, k_pe, gate_a, q_b, k_b, v_b, q_idx, k_idx, w_idx, gate_b) = jnp.split(z, offs, axis=-1)
    cq = rmsnorm(c_q, q_norm_g)
    q = jnp.einsum('btc,che->bthe', cq, w_uq)
    q_nope, q_pe = q[..., :A_NOPE], rope(q[..., A_NOPE:], pos)
    q_lat = jnp.einsum('bthn,chn->bthc', q_nope, w_uk)
    ckv = rmsnorm(c_kv, kv_norm_g)
    kpe = rope(k_pe, pos)
    q_b = q_b.reshape(b, t, B_HEADS, B_HEAD_DIM)
    k_b = k_b.reshape(b, t, B_KV_HEADS, B_HEAD_DIM)
    v_b = v_b.reshape(b, t, B_KV_HEADS, B_HEAD_DIM)
    q_idx = q_idx.reshape(b, t, IDX_HEADS, IDX_DIM)
    w_idx = w_idx * (IDX_HEADS ** -0.5)
    return q_lat, q_pe, ckv, kpe, gate_a, q_b, k_b, v_b, q_idx, k_idx, w_idx, gate_b


def mla_attend(ckv, kpe, pos_k, q_lat, q_pe, pos_q):
    s = (jnp.einsum('bthc,bsc->bhts', q_lat, ckv) + jnp.einsum('bthr,bsr->bhts', q_pe, kpe)).astype(jnp.float32)
    s = s * ((A_NOPE + A_ROPE) ** -0.5)
    s = jnp.where(chunk_admissible(pos_q, pos_k)[None, None], s, -jnp.inf)
    p = jax.nn.softmax(s, axis=-1).astype(ckv.dtype)
    return jnp.einsum('bhts,bsc->bthc', p, ckv)


def dsa_attend(k, v, k_idx, pos_k, rel_bias, n_top, q, q_idx, w_idx, pos_q):
    b, tq = q.shape[0], q.shape[1]
    dots = jnp.einsum('bthd,bsd->bths', q_idx, k_idx).astype(jnp.float32) * (IDX_DIM ** -0.5)
    score = jnp.einsum('bths,bth->bts', jax.nn.relu(dots), w_idx.astype(jnp.float32))
    score = jnp.where(chunk_admissible(pos_q, pos_k)[None], score, -jnp.inf)
    _, sel = lax.top_k(score, n_top)
    gather = jax.vmap(lambda rows, ib: rows[ib])
    k_sel = gather(k, sel)
    v_sel = gather(v, sel)
    pos_sel = pos_k[sel]
    valid = (pos_sel // CHUNK) <= (pos_q[None, :, None] // CHUNK)
    bias = rel_bias[t5_bucket(pos_sel - pos_q[None, :, None])].astype(jnp.float32)
    g = B_HEADS // B_KV_HEADS
    bias = jnp.moveaxis(bias.reshape(b, tq, n_top, B_KV_HEADS, g), 2, -1)
    qg = q.reshape(b, tq, B_KV_HEADS, g, B_HEAD_DIM)
    logits = jnp.einsum('btkgd,btjkd->btkgj', qg, k_sel).astype(jnp.float32) * (B_HEAD_DIM ** -0.5) + bias
    logits = jnp.where(valid[:, :, None, None, :], logits, -jnp.inf)
    p = jax.nn.softmax(logits, axis=-1).astype(v.dtype)
    o = jnp.einsum('btkgj,btjkd->btkgd', p, v_sel)
    return o.reshape(b, tq, B_HEADS, B_HEAD_DIM)


def sweep(fn, q_args, pos_q):
    b, t = q_args[0].shape[0], q_args[0].shape[1]
    nb = t // Q_BLOCK
    blk = lambda a: jnp.moveaxis(a.reshape((b, nb, Q_BLOCK) + a.shape[2:]), 1, 0)
    xs = tuple(blk(a) for a in q_args) + (pos_q.reshape(nb, Q_BLOCK),)
    out = lax.map(lambda args: fn(*args), xs)
    return jnp.moveaxis(out, 0, 1).reshape((b, t) + out.shape[3:])


def combine(x, o_lat, o_b, gate_a, gate_b, w_uv, w_out):
    b, t = x.shape[0], x.shape[1]
    o_a = jnp.einsum('bthc,chv->bthv', o_lat, w_uv).reshape(b, t, A_WIDTH)
    mix = jnp.concatenate([o_a * jax.nn.silu(gate_a), o_b.reshape(b, t, B_WIDTH) * jax.nn.silu(gate_b)], axis=-1)
    return x + jnp.einsum('bte,ed->btd', mix, w_out)


def setup_inputs(seed: int = 0) -> dict:
    key = jax.random.key(seed)
    ks = jax.random.split(key, 20)
    f = jnp.float32
    nrm = lambda k, shape, scale: jax.random.normal(k, shape, f) * scale
    return {
        "x_prompt": nrm(ks[0], (BATCH, SEQ, D_MODEL), 1.0),
        "x_sample": nrm(ks[1], (DEC_BATCH, DEC_SEQ, D_MODEL), 1.0),
        "cache_mla_ckv": nrm(ks[2], (DEPTH, DEC_BATCH, PAST_LEN, A_KV_LORA), 1.0),
        "cache_mla_kpe": nrm(ks[3], (DEPTH, DEC_BATCH, PAST_LEN, A_ROPE), 1.0),
        "cache_dsa_k": nrm(ks[4], (DEPTH, DEC_BATCH, PAST_LEN, B_KV_HEADS, B_HEAD_DIM), 1.0),
        "cache_dsa_v": nrm(ks[5], (DEPTH, DEC_BATCH, PAST_LEN, B_KV_HEADS, B_HEAD_DIM), 1.0),
        "cache_dsa_kidx": nrm(ks[6], (DEPTH, DEC_BATCH, PAST_LEN, IDX_DIM), 1.0),
        "norm_g": 1.0 + nrm(ks[7], (DEPTH, D_MODEL), 0.01),
        "w_in": nrm(ks[8], (DEPTH, D_MODEL, IN_WIDTH), D_MODEL ** -0.5),
        "mla_q_norm_g": 1.0 + nrm(ks[9], (DEPTH, A_Q_LORA), 0.01),
        "mla_kv_norm_g": 1.0 + nrm(ks[10], (DEPTH, A_KV_LORA), 0.01),
        "mla_w_uq": nrm(ks[11], (DEPTH, A_Q_LORA, A_HEADS, A_NOPE + A_ROPE), A_Q_LORA ** -0.5),
        "mla_w_uk": nrm(ks[12], (DEPTH, A_KV_LORA, A_HEADS, A_NOPE), A_KV_LORA ** -0.5),
        "mla_w_uv": nrm(ks[13], (DEPTH, A_KV_LORA, A_HEADS, A_V), A_KV_LORA ** -0.5),
        "rel_bias": nrm(ks[14], (N_BUCKETS, B_HEADS), 0.1),
        "w_out": nrm(ks[15], (DEPTH, D_MIX, D_MODEL), D_MIX ** -0.5),
        "final_norm_g": 1.0 + nrm(ks[16], (D_MODEL,), 0.01),
    }


def reference(x_prompt, x_sample, cache_mla_ckv, cache_mla_kpe, cache_dsa_k, cache_dsa_v, cache_dsa_kidx,
              norm_g, w_in, mla_q_norm_g, mla_kv_norm_g, mla_w_uq, mla_w_uk, mla_w_uv, rel_bias, w_out,
              final_norm_g):
    t_p = x_prompt.shape[1]
    t_s = x_sample.shape[1]
    past = cache_mla_ckv.shape[2]
    pos_p = jnp.arange(t_p, dtype=jnp.int32)
    pos_s = past + jnp.arange(t_s, dtype=jnp.int32)
    pos_all = jnp.arange(past + t_s, dtype=jnp.int32)
    n_top_p = min(TOP_K_MAX, t_p // 4)
    n_top_s = min(TOP_K_MAX, (past + t_s) // 4)

    xp, xs = x_prompt, x_sample
    p_ckv, p_kpe, p_k, p_v, p_kidx = [], [], [], [], []
    s_ckv, s_kpe, s_k, s_v, s_kidx = [], [], [], [], []
    for l in range(DEPTH):
        lw = (norm_g[l], w_in[l], mla_q_norm_g[l], mla_kv_norm_g[l], mla_w_uq[l], mla_w_uk[l])
        (q_lat, q_pe, ckv, kpe, gate_a, q_b, k_b, v_b, q_idx, k_idx, w_idx, gate_b) = project(xp, pos_p, *lw)
        o_lat = sweep(functools.partial(mla_attend, ckv, kpe, pos_p), (q_lat, q_pe), pos_p)
        o_b = sweep(functools.partial(dsa_attend, k_b, v_b, k_idx, pos_p, rel_bias, n_top_p),
                    (q_b, q_idx, w_idx), pos_p)
        xp = combine(xp, o_lat, o_b, gate_a, gate_b, mla_w_uv[l], w_out[l])
        p_ckv.append(ckv); p_kpe.append(kpe); p_k.append(k_b); p_v.append(v_b); p_kidx.append(k_idx)
        (q_lat, q_pe, ckv, kpe, gate_a, q_b, k_b, v_b, q_idx, k_idx, w_idx, gate_b) = project(xs, pos_s, *lw)
        ckv_all = jnp.concatenate([cache_mla_ckv[l], ckv], axis=1)
        kpe_all = jnp.concatenate([cache_mla_kpe[l], kpe], axis=1)
        k_all = jnp.concatenate([cache_dsa_k[l], k_b], axis=1)
        v_all = jnp.concatenate([cache_dsa_v[l], v_b], axis=1)
        kidx_all = jnp.concatenate([cache_dsa_kidx[l], k_idx], axis=1)
        o_lat = mla_attend(ckv_all, kpe_all, pos_all, q_lat, q_pe, pos_s)
        o_b = dsa_attend(k_all, v_all, kidx_all, pos_all, rel_bias, n_top_s, q_b, q_idx, w_idx, pos_s)
        xs = combine(xs, o_lat, o_b, gate_a, gate_b, mla_w_uv[l], w_out[l])
        s_ckv.append(ckv); s_kpe.append(kpe); s_k.append(k_b); s_v.append(v_b); s_kidx.append(k_idx)

    y_prompt = rmsnorm(xp, final_norm_g)
    y_sample = rmsnorm(xs, final_norm_g)
    return (y_prompt, y_sample,
            jnp.stack(p_ckv), jnp.stack(p_kpe), jnp.stack(p_k), jnp.stack(p_v), jnp.stack(p_kidx),
            jnp.stack(s_ckv), jnp.stack(s_kpe), jnp.stack(s_k), jnp.stack(s_v), jnp.stack(s_kidx))
```

```python
import functools
import math

import jax
import jax.numpy as jnp
import numpy as np
from jax import lax
from jax.experimental import pallas as pl
from jax.experimental.pallas import tpu as pltpu

F32 = jnp.float32
BF16 = jnp.bfloat16
I32 = jnp.int32

CHUNK = 64
EPS = 1e-6
A_HEADS = 8
A_NOPE = 64
A_ROPE = 32
A_V = 64
A_Q_LORA = 256
A_KV_LORA = 128
ROPE_THETA = 10000.0
A_WIDTH = A_HEADS * A_V
B_HEADS = 8
B_KV_HEADS = 2
B_HEAD_DIM = 64
B_WIDTH = B_HEADS * B_HEAD_DIM
B_GROUP = B_HEADS // B_KV_HEADS
IDX_HEADS = 8
IDX_DIM = 64
TOP_K_MAX = 256
N_BUCKETS = 32
MAX_DISTANCE = 128

LANES = 128
VMEM_LIMIT_BYTES = 56 * 1024 * 1024

LOG2E = 1.4426950408889634
NEG_BIG = -1e30
INT_MIN = -(2 ** 31)

C_CQ = 0
C_CKV = C_CQ + A_Q_LORA
C_MISC = C_CKV + A_KV_LORA
C_GA = C_MISC + LANES
C_QB = C_GA + A_WIDTH
C_KB = C_QB + B_WIDTH
C_VB = C_KB + B_KV_HEADS * B_HEAD_DIM
C_QI = C_VB + B_KV_HEADS * B_HEAD_DIM
C_GB = C_QI + IDX_HEADS * IDX_DIM
IN_PAD = C_GB + B_WIDTH
M_KPE = 0
M_KIDX = A_ROPE
M_WIDX = A_ROPE + IDX_DIM

QCAT = 2 * LANES
TQ = 256
PROJ_TM = 256


def _dot(a, b):
    return jnp.dot(a, b, preferred_element_type=F32)


def _dot_nt(a, b):
    return lax.dot_general(a, b, (((1,), (1,)), ((), ())), preferred_element_type=F32)


def _rms(x, g):
    return x * lax.rsqrt(jnp.mean(x * x, axis=-1, keepdims=True) + EPS) * g


def _bias_kernel(rb_ref, out_ref, *, offsets, rows_per_head):
    nb = N_BUCKETS // 2
    max_exact = nb // 2
    n_r, n_c = out_ref.shape[2], out_ref.shape[3]
    row = lax.broadcasted_iota(I32, (n_r, n_c), 0)
    col = lax.broadcasted_iota(I32, (n_r, n_c), 1)
    for p, off in enumerate(offsets):
        rel = off + col - row
        ret = jnp.where(rel > 0, nb, 0)
        n = jnp.abs(rel)
        nf = jnp.maximum(n, 1).astype(F32)
        large = max_exact + (jnp.log(nf / max_exact) / math.log(MAX_DISTANCE / max_exact)
                             * (nb - max_exact)).astype(I32)
        large = jnp.minimum(large, nb - 1)
        bucket = ret + jnp.where(n < max_exact, n, large)
        for h in range(B_HEADS):
            acc = jnp.zeros((n_r, n_c), F32)
            for b in range(N_BUCKETS):
                acc = jnp.where(bucket == b, rb_ref[b, h], acc)
            out_ref[p, h] = acc * LOG2E


def _bias_tables(rel_bias, offsets, n_r, n_c):
    return pl.pallas_call(
        functools.partial(_bias_kernel, offsets=tuple(offsets), rows_per_head=n_r),
        out_shape=jax.ShapeDtypeStruct((len(offsets), B_HEADS, n_r, n_c), F32),
        in_specs=[pl.BlockSpec(memory_space=pltpu.SMEM)],
        out_specs=pl.BlockSpec(memory_space=pltpu.VMEM),
        compiler_params=pltpu.CompilerParams(vmem_limit_bytes=VMEM_LIMIT_BYTES),
        name="bias_tables",
    )(rel_bias)


def _proj_kernel(x_ref, ng_ref, win_ref, qng_ref, kvng_ref, wq_ref, wuk_ref, pmat_ref,
                 cosq_ref, sinq_ref, rc_ref, rs1_ref, rs2_ref,
                 ckv_ref, kpe_ref, kb_ref, vb_ref, kidx_ref,
                 kcat_ref, kb16_ref, vb16_ref, kidx16_ref,
                 qcat_ref, qb_ref, qidx_ref, widx_ref, sga_ref, sgb_ref):
    x = x_ref[...]
    h = _rms(x, ng_ref[...])
    z = _dot(h.astype(BF16), win_ref[...])

    cq = _rms(z[:, C_CQ:C_CQ + A_Q_LORA], qng_ref[...])
    q = _dot(cq.astype(BF16), wq_ref[...])
    n_nope = A_HEADS * A_NOPE
    x1 = q[:, n_nope:n_nope + LANES]
    x2 = q[:, n_nope + LANES:n_nope + 2 * LANES]
    cos8, sin8 = cosq_ref[...], sinq_ref[...]
    o1 = x1 * cos8 - x2 * sin8
    o2 = x1 * sin8 + x2 * cos8
    mla_scale = (A_NOPE + A_ROPE) ** -0.5 * LOG2E
    q_lat = _dot(q[:, :n_nope].astype(BF16), wuk_ref[...]) * mla_scale
    pe = jnp.concatenate([o1, o2], axis=1) * mla_scale
    q_pe = _dot(pe.astype(BF16), pmat_ref[...])
    for hh in range(A_HEADS):
        qcat_ref[:, hh * QCAT:hh * QCAT + LANES] = q_lat[:, hh * LANES:(hh + 1) * LANES].astype(BF16)
        qcat_ref[:, hh * QCAT + LANES:(hh + 1) * QCAT] = q_pe[:, hh * LANES:(hh + 1) * LANES].astype(BF16)

    ckv = _rms(z[:, C_CKV:C_CKV + A_KV_LORA], kvng_ref[...])
    ckv_ref[...] = ckv
    misc = z[:, C_MISC:C_MISC + LANES]
    rot = (misc * rc_ref[...] + pltpu.roll(misc, A_ROPE // 2, 1) * rs1_ref[...]
           + pltpu.roll(misc, LANES - A_ROPE // 2, 1) * rs2_ref[...])
    kpe_ref[...] = rot[:, :A_ROPE]
    kcat_ref[:, :LANES] = ckv.astype(BF16)
    kcat_ref[:, LANES:] = rot.astype(BF16)

    kidx = misc[:, M_KIDX:M_KIDX + IDX_DIM]
    kidx_ref[...] = kidx
    kidx16_ref[...] = kidx.astype(BF16)
    widx_ref[...] = misc[:, M_WIDX:M_WIDX + IDX_HEADS] * (IDX_HEADS ** -0.5)
    kb = z[:, C_KB:C_KB + LANES]
    vb = z[:, C_VB:C_VB + LANES]
    kb_ref[...] = kb
    vb_ref[...] = vb
    kb16_ref[...] = kb.astype(BF16)
    vb16_ref[...] = vb.astype(BF16)
    qb_ref[...] = (z[:, C_QB:C_QB + B_WIDTH] * (B_HEAD_DIM ** -0.5 * LOG2E)).astype(BF16)
    qidx_ref[...] = (z[:, C_QI:C_QI + IDX_HEADS * IDX_DIM] * (IDX_DIM ** -0.5)).astype(BF16)
    sga_ref[...] = jax.nn.silu(z[:, C_GA:C_GA + A_WIDTH])
    sgb_ref[...] = jax.nn.silu(z[:, C_GB:C_GB + B_WIDTH])


def _project(x2d, rope_tabs, lw, *, period):
    n, d = x2d.shape
    tm = PROJ_TM
    assert n % tm == 0 and period % tm == 0
    n_rep = period // tm
    tok = lambda c: pl.BlockSpec((tm, c), lambda i: (i, 0))
    full = lambda a: pl.BlockSpec(a.shape, lambda i: (0,) * a.ndim)
    tab = pl.BlockSpec((tm, LANES), lambda i: (i % n_rep, 0))
    outs = [
        ("ckv", A_KV_LORA, F32), ("kpe", A_ROPE, F32), ("kb", LANES, F32), ("vb", LANES, F32),
        ("kidx", IDX_DIM, F32),
        ("kcat", QCAT, BF16), ("kb16", LANES, BF16), ("vb16", LANES, BF16), ("kidx16", IDX_DIM, BF16),
        ("qcat", A_HEADS * QCAT, BF16), ("qb", B_WIDTH, BF16), ("qidx", IDX_HEADS * IDX_DIM, BF16),
        ("widx", IDX_HEADS, F32), ("sga", A_WIDTH, F32), ("sgb", B_WIDTH, F32),
    ]
    res = pl.pallas_call(
        _proj_kernel,
        grid=(n // tm,),
        in_specs=[tok(d), full(lw["ng"]), full(lw["win"]), full(lw["qng"]), full(lw["kvng"]),
                  full(lw["wq"]), full(lw["wuk"]), full(lw["pmat"]), tab, tab, tab, tab, tab],
        out_specs=[tok(c) for _, c, _ in outs],
        out_shape=[jax.ShapeDtypeStruct((n, c), dt) for _, c, dt in outs],
        compiler_params=pltpu.CompilerParams(dimension_semantics=("arbitrary",),
                                             vmem_limit_bytes=VMEM_LIMIT_BYTES),
        name="project",
    )(x2d, lw["ng"], lw["win"], lw["qng"], lw["kvng"], lw["wq"], lw["wuk"], lw["pmat"], *rope_tabs)
    return {name: r for (name, _, _), r in zip(outs, res)}


def _to_key(score):
    i = pltpu.bitcast(score, I32)
    i = jnp.where(i == INT_MIN, 0, i)
    return i ^ ((i >> 31) & 0x7FFFFFFF)


def _count(pred):
    return jnp.sum(jnp.where(pred, 1.0, 0.0), axis=1, keepdims=True)


def _prompt_attn_kernel(qcat_ref, qb_ref, qidx_ref, widx_ref, kcat_ref, kb_ref, vb_ref, kidx_ref,
                        bias_ref, olat_ref, ob_ref,
                        qs_ref, qbs_ref, qis_ref, wb_ref, keys_ref, m_ref, l_ref, acc_ref,
                        mb_ref, lb_ref, accb_ref, *, n_top):
    qi = pl.program_id(1)
    nblk = qi + 1
    tq = TQ
    rows = lax.broadcasted_iota(I32, (tq, tq), 0)
    cols = lax.broadcasted_iota(I32, (tq, tq), 1)
    shift = CHUNK.bit_length() - 1
    diag_ok = (rows >> shift) >= (cols >> shift)

    for h in range(A_HEADS):
        qs_ref[h * tq:(h + 1) * tq, :] = qcat_ref[:, h * QCAT:(h + 1) * QCAT]
    for h in range(B_HEADS):
        qbs_ref[h * tq:(h + 1) * tq, :] = qb_ref[:, h * B_HEAD_DIM:(h + 1) * B_HEAD_DIM]
    for h in range(IDX_HEADS):
        qis_ref[h * tq:(h + 1) * tq, :] = qidx_ref[:, h * IDX_DIM:(h + 1) * IDX_DIM]
        wb_ref[h] = jnp.broadcast_to(widx_ref[:, h:h + 1], (tq, tq))

    m_ref[...] = jnp.full(m_ref.shape, NEG_BIG, F32)
    l_ref[...] = jnp.zeros(l_ref.shape, F32)
    acc_ref[...] = jnp.zeros(acc_ref.shape, F32)

    def mla_block(j, masked):
        start = pl.multiple_of(j * tq, tq)
        kblk = kcat_ref[pl.ds(start, tq), :]
        s = _dot_nt(qs_ref[...], kblk)
        for h in range(A_HEADS):
            sl = slice(h * tq, (h + 1) * tq)
            sh = s[sl]
            if masked:
                sh = jnp.where(diag_ok, sh, NEG_BIG)
            m_prev = m_ref[sl]
            m_new = jnp.maximum(m_prev, jnp.max(sh, axis=1, keepdims=True))
            alpha = jnp.exp2(m_prev - m_new)
            p = jnp.exp2(sh - m_new[:, :1])
            l_ref[sl] = alpha * l_ref[sl] + jnp.sum(p, axis=1, keepdims=True)
            acc_ref[sl] = alpha * acc_ref[sl] + _dot(p.astype(BF16), kblk[:, :A_KV_LORA])
            m_ref[sl] = m_new

    def mla_body(j, c):
        mla_block(j, False)
        return c

    lax.fori_loop(0, qi, mla_body, 0)
    mla_block(qi, True)
    for h in range(A_HEADS):
        sl = slice(h * tq, (h + 1) * tq)
        olat_ref[:, h * LANES:(h + 1) * LANES] = (acc_ref[sl] / l_ref[sl]).astype(olat_ref.dtype)

    def idx_block(j, masked):
        start = pl.multiple_of(j * tq, tq)
        dots = _dot_nt(qis_ref[...], kidx_ref[pl.ds(start, tq), :])
        score = jnp.zeros((tq, tq), F32)
        for h in range(IDX_HEADS):
            score = score + jnp.maximum(dots[h * tq:(h + 1) * tq], 0.0) * wb_ref[h]
        key = _to_key(score)
        if masked:
            key = jnp.where(diag_ok, key, INT_MIN)
        keys_ref[j] = key

    def idx_body(j, c):
        idx_block(j, False)
        return c

    lax.fori_loop(0, qi, idx_body, 0)
    idx_block(qi, True)

    def count_ge(cand):
        def body(j, c):
            return c + jnp.where(keys_ref[j] >= cand, 1.0, 0.0)
        part = lax.fori_loop(0, nblk, body, jnp.zeros((tq, tq), F32))
        return jnp.sum(part, axis=1, keepdims=True)

    kf = float(n_top)

    def bis_body(it, carry):
        lo, cnt_lo = carry
        cand = lo + lax.shift_left(jnp.int32(1), 31 - it)
        cnt = count_ge(cand)
        take = cnt >= kf
        return jnp.where(take, cand, lo), jnp.where(take, cnt, cnt_lo)

    thr, cnt_ge = lax.fori_loop(
        0, 32, bis_body, (jnp.full((tq, 1), INT_MIN, I32), jnp.full((tq, 1), 1e9, F32)))
    few = thr == INT_MIN
    cut_all = jnp.where(few, -1, jnp.int32(2 ** 30))
    has_tie = jnp.max(jnp.where((cnt_ge > kf) & (~few), 1.0, 0.0))

    def tie_path():
        def gt_body(j, c):
            return c + jnp.where(keys_ref[j] > thr, 1.0, 0.0)
        cnt_gt = jnp.sum(lax.fori_loop(0, nblk, gt_body, jnp.zeros((tq, tq), F32)), axis=1, keepdims=True)
        need = kf - cnt_gt

        def cut_body(it, cpos):
            cand = cpos + lax.shift_left(jnp.int32(1), 11 - it)
            def body(j, c):
                colj = cols + j * tq
                return c + jnp.where((keys_ref[j] == thr) & (colj < cand), 1.0, 0.0)
            cnt = jnp.sum(lax.fori_loop(0, nblk, body, jnp.zeros((tq, tq), F32)), axis=1, keepdims=True)
            return jnp.where(cnt < need, cand, cpos)

        cpos = lax.fori_loop(0, 12, cut_body, jnp.zeros((tq, 1), I32))
        return jnp.where(few, -1, cpos)

    cut = lax.cond(has_tie > 0.0, tie_path, lambda: cut_all)

    mb_ref[...] = jnp.full(mb_ref.shape, NEG_BIG, F32)
    lb_ref[...] = jnp.zeros(lb_ref.shape, F32)
    accb_ref[...] = jnp.zeros(accb_ref.shape, F32)

    def dsa_body(j, c):
        start = pl.multiple_of(j * tq, tq)
        key = keys_ref[j]
        sel = (key > thr) | ((key == thr) & ((cols + j * tq) <= cut))
        pat = jnp.minimum(qi - j, 2)
        kblk = kb_ref[pl.ds(start, tq), :]
        vblk = vb_ref[pl.ds(start, tq), :]
        for g in range(B_KV_HEADS):
            kg = kblk[:, g * B_HEAD_DIM:(g + 1) * B_HEAD_DIM]
            vg = vblk[:, g * B_HEAD_DIM:(g + 1) * B_HEAD_DIM]
            s = _dot_nt(qbs_ref[g * B_GROUP * tq:(g + 1) * B_GROUP * tq, :], kg)
            for hh in range(B_GROUP):
                h = g * B_GROUP + hh
                sl = slice(h * tq, (h + 1) * tq)
                sh = jnp.where(sel, s[hh * tq:(hh + 1) * tq] + bias_ref[pat, h], NEG_BIG)
                m_prev = mb_ref[sl]
                m_new = jnp.maximum(m_prev, jnp.max(sh, axis=1, keepdims=True))
                alpha = jnp.exp2(m_prev - m_new)
                p = jnp.exp2(sh - m_new[:, :1])
                lb_ref[sl] = alpha * lb_ref[sl] + jnp.sum(p, axis=1, keepdims=True)
                accb_ref[sl] = alpha[:, :B_HEAD_DIM] * accb_ref[sl] + _dot(p.astype(BF16), vg)
                mb_ref[sl] = m_new
        return c

    lax.fori_loop(0, nblk, dsa_body, 0)
    for h in range(B_HEADS):
        sl = slice(h * tq, (h + 1) * tq)
        ob_ref[:, h * B_HEAD_DIM:(h + 1) * B_HEAD_DIM] = accb_ref[sl] / lb_ref[sl][:, :B_HEAD_DIM]


def _prompt_attention(pr, bias_p, b, t, n_top):
    tq = TQ
    assert t % tq == 0
    nq = t // tq
    r3 = lambda a: a.reshape(b, t, a.shape[-1])
    qblk = lambda c: pl.BlockSpec((None, tq, c), lambda bi, qi: (bi, qi, 0))
    kall = lambda c: pl.BlockSpec((None, t, c), lambda bi, qi: (bi, 0, 0))
    olat, ob = pl.pallas_call(
        functools.partial(_prompt_attn_kernel, n_top=n_top),
        grid=(b, nq),
        in_specs=[qblk(A_HEADS * QCAT), qblk(B_WIDTH), qblk(IDX_HEADS * IDX_DIM), qblk(IDX_HEADS),
                  kall(QCAT), kall(LANES), kall(LANES), kall(IDX_DIM),
                  pl.BlockSpec(bias_p.shape, lambda bi, qi: (0, 0, 0, 0))],
        out_specs=[qblk(A_HEADS * A_KV_LORA), qblk(B_WIDTH)],
        out_shape=[jax.ShapeDtypeStruct((b, t, A_HEADS * A_KV_LORA), BF16),
                   jax.ShapeDtypeStruct((b, t, B_WIDTH), F32)],
        scratch_shapes=[
            pltpu.VMEM((A_HEADS * tq, QCAT), BF16),
            pltpu.VMEM((B_HEADS * tq, B_HEAD_DIM), BF16),
            pltpu.VMEM((IDX_HEADS * tq, IDX_DIM), BF16),
            pltpu.VMEM((IDX_HEADS, tq, tq), F32),
            pltpu.VMEM((nq, tq, tq), I32),
            pltpu.VMEM((A_HEADS * tq, LANES), F32),
            pltpu.VMEM((A_HEADS * tq, LANES), F32),
            pltpu.VMEM((A_HEADS * tq, A_KV_LORA), F32),
            pltpu.VMEM((B_HEADS * tq, LANES), F32),
            pltpu.VMEM((B_HEADS * tq, LANES), F32),
            pltpu.VMEM((B_HEADS * tq, B_HEAD_DIM), F32),
        ],
        compiler_params=pltpu.CompilerParams(dimension_semantics=("arbitrary", "arbitrary"),
                                             vmem_limit_bytes=VMEM_LIMIT_BYTES),
        name="prompt_attention",
    )(r3(pr["qcat"]), r3(pr["qb"]), r3(pr["qidx"]), r3(pr["widx"]),
      r3(pr["kcat"]), r3(pr["kb16"]), r3(pr["vb16"]), r3(pr["kidx16"]), bias_p)
    return olat.reshape(b * t, -1), ob.reshape(b * t, -1)


def _sample_attn_kernel(qcat_ref, qb_ref, qidx_ref, widx_ref, kcatn_ref, kbn_ref, vbn_ref, kidxn_ref,
                        cckv_ref, ckpe_ref, ck_ref, cv_ref, ckidx_ref, bias_ref,
                        olat_ref, ob_ref, *, n_top, t_new, past):
    tq = t_new
    pad = LANES
    n_keys = past + pad
    colmask = lax.broadcasted_iota(I32, (tq, n_keys), 1) < past + t_new

    def padrows(a):
        return jnp.concatenate([a, jnp.zeros((pad - t_new, a.shape[1]), a.dtype)], axis=0)

    kpe_c = ckpe_ref[...].astype(BF16)
    kcat_c = jnp.concatenate(
        [cckv_ref[...].astype(BF16), kpe_c, jnp.zeros((past, QCAT - A_KV_LORA - A_ROPE), BF16)], axis=1)
    kcat_n = padrows(kcatn_ref[...])
    k_c = ck_ref[...].astype(BF16)
    v_c = cv_ref[...].astype(BF16)
    k_n = padrows(kbn_ref[...])
    v_n = padrows(vbn_ref[...])
    kidx_c = ckidx_ref[...].astype(BF16)
    kidx_n = padrows(kidxn_ref[...])

    def softmax_pv(s, v_c_, v_n_):
        m = jnp.max(s, axis=1, keepdims=True)
        p = jnp.exp2(s - m)
        l = jnp.sum(p, axis=1, keepdims=True)
        pb = p.astype(BF16)
        o = _dot(pb[:, :past], v_c_) + _dot(pb[:, past:], v_n_)
        return o / l

    qs = jnp.concatenate([qcat_ref[:, h * QCAT:(h + 1) * QCAT] for h in range(A_HEADS)], axis=0)
    s = jnp.concatenate([_dot_nt(qs, kcat_c), _dot_nt(qs, kcat_n)], axis=1)
    cm8 = lax.broadcasted_iota(I32, (A_HEADS * tq, n_keys), 1) < past + t_new
    s = jnp.where(cm8, s, NEG_BIG)
    o = softmax_pv(s, kcat_c[:, :A_KV_LORA], kcat_n[:, :A_KV_LORA])
    for h in range(A_HEADS):
        olat_ref[:, h * LANES:(h + 1) * LANES] = o[h * tq:(h + 1) * tq].astype(olat_ref.dtype)

    qis = jnp.concatenate([qidx_ref[:, h * IDX_DIM:(h + 1) * IDX_DIM] for h in range(IDX_HEADS)], axis=0)
    dots = jnp.concatenate([_dot_nt(qis, kidx_c), _dot_nt(qis, kidx_n)], axis=1)
    score = jnp.zeros((tq, n_keys), F32)
    for h in range(IDX_HEADS):
        score = score + jnp.maximum(dots[h * tq:(h + 1) * tq], 0.0) * widx_ref[:, h:h + 1]
    key = jnp.where(colmask, _to_key(score), INT_MIN)
    kf = float(n_top)

    def bis_body(it, carry):
        lo, cnt_lo = carry
        cand = lo + lax.shift_left(jnp.int32(1), 31 - it)
        cnt = _count(key >= cand)
        take = cnt >= kf
        return jnp.where(take, cand, lo), jnp.where(take, cnt, cnt_lo)

    thr, cnt_ge = lax.fori_loop(
        0, 32, bis_body, (jnp.full((tq, 1), INT_MIN, I32), jnp.full((tq, 1), 1e9, F32)))
    few = thr == INT_MIN
    cols = lax.broadcasted_iota(I32, (tq, n_keys), 1)
    has_tie = jnp.max(jnp.where((cnt_ge > kf) & (~few), 1.0, 0.0))

    def tie_path():
        need = kf - _count(key > thr)
        eq = key == thr

        def cut_body(it, cpos):
            cand = cpos + lax.shift_left(jnp.int32(1), 13 - it)
            cnt = _count(eq & (cols < cand))
            return jnp.where(cnt < need, cand, cpos)

        cpos = lax.fori_loop(0, 14, cut_body, jnp.zeros((tq, 1), I32))
        return jnp.where(few, -1, cpos)

    cut = lax.cond(has_tie > 0.0, tie_path, lambda: jnp.where(few, -1, jnp.int32(2 ** 30)))
    sel = (key > thr) | ((key == thr) & (cols <= cut))

    n_far = n_keys - bias_ref.shape[-1]
    for g in range(B_KV_HEADS):
        lanes = slice(g * B_HEAD_DIM, (g + 1) * B_HEAD_DIM)
        qg = jnp.concatenate(
            [qb_ref[:, h * B_HEAD_DIM:(h + 1) * B_HEAD_DIM] for h in range(g * B_GROUP, (g + 1) * B_GROUP)],
            axis=0)
        sg = jnp.concatenate([_dot_nt(qg, k_c[:, lanes]), _dot_nt(qg, k_n[:, lanes])], axis=1)
        for hh in range(B_GROUP):
            h = g * B_GROUP + hh
            near = bias_ref[0, h]
            far = bias_ref[1, h][:, :1]
            sh = sg[hh * tq:(hh + 1) * tq]
            sh = jnp.concatenate([sh[:, :n_far] + far, sh[:, n_far:] + near], axis=1)
            sh = jnp.where(sel, sh, NEG_BIG)
            oh = softmax_pv(sh, v_c[:, lanes], v_n[:, lanes])
            ob_ref[:, h * B_HEAD_DIM:(h + 1) * B_HEAD_DIM] = oh


def _sample_attention(pr, caches, bias_s, b, t_new, past, n_top):
    r3 = lambda a: a.reshape(b, t_new, a.shape[-1])
    per_b = lambda n, c: pl.BlockSpec((None, n, c), lambda bi: (bi, 0, 0))
    news = [pr["qcat"], pr["qb"], pr["qidx"], pr["widx"], pr["kcat"], pr["kb16"], pr["vb16"], pr["kidx16"]]
    olat, ob = pl.pallas_call(
        functools.partial(_sample_attn_kernel, n_top=n_top, t_new=t_new, past=past),
        grid=(b,),
        in_specs=[per_b(t_new, a.shape[-1]) for a in news]
                 + [per_b(past, c.shape[-1]) for c in caches]
                 + [pl.BlockSpec(bias_s.shape, lambda bi: (0, 0, 0, 0))],
        out_specs=[per_b(t_new, A_HEADS * A_KV_LORA), per_b(t_new, B_WIDTH)],
        out_shape=[jax.ShapeDtypeStruct((b, t_new, A_HEADS * A_KV_LORA), BF16),
                   jax.ShapeDtypeStruct((b, t_new, B_WIDTH), F32)],
        compiler_params=pltpu.CompilerParams(dimension_semantics=("arbitrary",),
                                             vmem_limit_bytes=VMEM_LIMIT_BYTES),
        name="sample_attention",
    )(*[r3(a) for a in news], *caches, bias_s)
    return olat.reshape(b * t_new, -1), ob.reshape(b * t_new, -1)


def _combine_kernel(x_ref, olat_ref, ob_ref, sga_ref, sgb_ref, wuv_ref, wout_ref, fg_ref, y_ref, *, final):
    o_a = _dot(olat_ref[...], wuv_ref[...])
    mix = jnp.concatenate([o_a * sga_ref[...], ob_ref[...] * sgb_ref[...]], axis=1)
    y = x_ref[...] + _dot(mix.astype(BF16), wout_ref[...])
    if final:
        y = _rms(y, fg_ref[...])
    y_ref[...] = y


def _combine(x2d, olat, ob, pr, lw, fg, final):
    n, d = x2d.shape
    tm = PROJ_TM
    tok = lambda c: pl.BlockSpec((tm, c), lambda i: (i, 0))
    full = lambda a: pl.BlockSpec(a.shape, lambda i: (0,) * a.ndim)
    return pl.pallas_call(
        functools.partial(_combine_kernel, final=final),
        grid=(n // tm,),
        in_specs=[tok(d), tok(olat.shape[1]), tok(ob.shape[1]), tok(A_WIDTH), tok(B_WIDTH),
                  full(lw["wuv"]), full(lw["wout"]), full(fg)],
        out_specs=tok(d),
        out_shape=jax.ShapeDtypeStruct((n, d), F32),
        compiler_params=pltpu.CompilerParams(dimension_semantics=("arbitrary",),
                                             vmem_limit_bytes=VMEM_LIMIT_BYTES),
        name="combine",
    )(x2d, olat, ob, pr["sga"], pr["sgb"], lw["wuv"], lw["wout"], fg)


def _layer_weights(norm_g, w_in, q_norm_g, kv_norm_g, w_uq, w_uk, w_uv, w_out):
    d = w_in.shape[0]
    o = np.cumsum([0, A_Q_LORA, A_KV_LORA, A_ROPE, A_WIDTH, B_WIDTH, B_KV_HEADS * B_HEAD_DIM,
                   B_KV_HEADS * B_HEAD_DIM, IDX_HEADS * IDX_DIM, IDX_DIM, IDX_HEADS, B_WIDTH])
    seg = lambda i: w_in[:, int(o[i]):int(o[i + 1])]
    misc = jnp.concatenate([seg(2), seg(8), seg(9),
                            jnp.zeros((d, LANES - A_ROPE - IDX_DIM - IDX_HEADS), w_in.dtype)], axis=1)
    win = jnp.concatenate([seg(0), seg(1), misc, seg(3), seg(4), seg(5), seg(6), seg(7), seg(10)], axis=1)
    assert win.shape[1] == IN_PAD
    half = A_ROPE // 2
    wq = jnp.concatenate([
        w_uq[:, :, :A_NOPE].reshape(A_Q_LORA, A_HEADS * A_NOPE),
        w_uq[:, :, A_NOPE:A_NOPE + half].reshape(A_Q_LORA, A_HEADS * half),
        w_uq[:, :, A_NOPE + half:].reshape(A_Q_LORA, A_HEADS * half)], axis=1)
    eye = jnp.eye(A_HEADS, dtype=w_uk.dtype)
    wuk = jnp.einsum('chn,hg->hngc', w_uk, eye).reshape(A_HEADS * A_NOPE, A_HEADS * A_KV_LORA)
    wuv = jnp.einsum('chv,hg->hcgv', w_uv, eye).reshape(A_HEADS * A_KV_LORA, A_HEADS * A_V)
    pm = np.zeros((2 * LANES, A_HEADS * LANES), np.float32)
    for h in range(A_HEADS):
        for i in range(half):
            pm[h * half + i, h * LANES + i] = 1.0
            pm[LANES + h * half + i, h * LANES + half + i] = 1.0
    return {
        "ng": norm_g.reshape(1, -1), "win": win.astype(BF16),
        "qng": q_norm_g.reshape(1, -1), "kvng": kv_norm_g.reshape(1, -1),
        "wq": wq.astype(BF16), "wuk": wuk.astype(BF16), "wuv": wuv.astype(BF16),
        "pmat": jnp.asarray(pm, BF16), "wout": w_out.astype(BF16),
    }


def _rope_tables(pos):
    half = A_ROPE // 2
    inv = ROPE_THETA ** (-jnp.arange(half, dtype=F32) / half)
    ang = pos.astype(F32)[:, None] * inv[None, :]
    cos, sin = jnp.cos(ang), jnp.sin(ang)
    z = jnp.zeros((pos.shape[0], LANES - A_ROPE), F32)
    zh = jnp.zeros_like(cos)
    cosq = jnp.tile(cos, (1, A_HEADS))
    sinq = jnp.tile(sin, (1, A_HEADS))
    rc = jnp.concatenate([cos, cos, z], axis=1)
    rs1 = jnp.concatenate([zh, sin, z], axis=1)
    rs2 = jnp.concatenate([-sin, zh, z], axis=1)
    return cosq, sinq, rc, rs1, rs2


def kernel(x_prompt, x_sample, cache_mla_ckv, cache_mla_kpe, cache_dsa_k, cache_dsa_v, cache_dsa_kidx,
           norm_g, w_in, mla_q_norm_g, mla_kv_norm_g, mla_w_uq, mla_w_uk, mla_w_uv, rel_bias, w_out,
           final_norm_g):
    bp, tp, d = x_prompt.shape
    bs, ts, _ = x_sample.shape
    depth = w_in.shape[0]
    past = cache_mla_ckv.shape[2]
    n_top_p = min(TOP_K_MAX, tp // 4)
    n_top_s = min(TOP_K_MAX, (past + ts) // 4)
    assert ts <= CHUNK and past % CHUNK == 0 and past % LANES == 0

    rope_p = _rope_tables(jnp.arange(tp, dtype=jnp.int32))
    reps = PROJ_TM // ts
    rope_s = tuple(jnp.tile(a, (reps, 1)) for a in _rope_tables(past + jnp.arange(ts, dtype=jnp.int32)))

    bias_p = _bias_tables(rel_bias, (0, -TQ, -3 * TQ), TQ, TQ)
    win_s = 2 * LANES
    bias_s = _bias_tables(rel_bias, (-(win_s - LANES), -(past + win_s)), ts, win_s)
    fg = final_norm_g.reshape(1, -1)

    xp = x_prompt.reshape(bp * tp, d)
    xs = x_sample.reshape(bs * ts, d)
    outs_p, outs_s = [], []
    for l in range(depth):
        lw = _layer_weights(norm_g[l], w_in[l], mla_q_norm_g[l], mla_kv_norm_g[l],
                            mla_w_uq[l], mla_w_uk[l], mla_w_uv[l], w_out[l])
        final = l == depth - 1
        pr = _project(xp, rope_p, lw, period=tp)
        olat, ob = _prompt_attention(pr, bias_p, bp, tp, n_top_p)
        xp = _combine(xp, olat, ob, pr, lw, fg, final)
        outs_p.append((pr["ckv"].reshape(bp, tp, A_KV_LORA), pr["kpe"].reshape(bp, tp, A_ROPE),
                       pr["kb"].reshape(bp, tp, B_KV_HEADS, B_HEAD_DIM),
                       pr["vb"].reshape(bp, tp, B_KV_HEADS, B_HEAD_DIM),
                       pr["kidx"].reshape(bp, tp, IDX_DIM)))
        ps = _project(xs, rope_s, lw, period=PROJ_TM)
        caches = (cache_mla_ckv[l], cache_mla_kpe[l],
                  cache_dsa_k[l].reshape(bs, past, B_KV_HEADS * B_HEAD_DIM),
                  cache_dsa_v[l].reshape(bs, past, B_KV_HEADS * B_HEAD_DIM), cache_dsa_kidx[l])
        olat, ob = _sample_attention(ps, caches, bias_s, bs, ts, past, n_top_s)
        xs = _combine(xs, olat, ob, ps, lw, fg, final)
        outs_s.append((ps["ckv"].reshape(bs, ts, A_KV_LORA), ps["kpe"].reshape(bs, ts, A_ROPE),
                       ps["kb"].reshape(bs, ts, B_KV_HEADS, B_HEAD_DIM),
                       ps["vb"].reshape(bs, ts, B_KV_HEADS, B_HEAD_DIM),
                       ps["kidx"].reshape(bs, ts, IDX_DIM)))

    stack = lambda outs, i: jnp.stack([o[i] for o in outs])
    return ((xp.reshape(bp, tp, d), xs.reshape(bs, ts, d))
            + tuple(stack(outs_p, i) for i in range(5)) + tuple(stack(outs_s, i) for i in range(5)))
```

```python
import functools
import math

import jax
import jax.numpy as jnp
import numpy as np
from jax import lax
from jax.experimental import pallas as pl
from jax.experimental.pallas import tpu as pltpu

F32 = jnp.float32
BF16 = jnp.bfloat16
I32 = jnp.int32

CHUNK = 64
EPS = 1e-6
A_HEADS = 8
A_NOPE = 64
A_ROPE = 32
A_V = 64
A_Q_LORA = 256
A_KV_LORA = 128
ROPE_THETA = 10000.0
A_WIDTH = A_HEADS * A_V
B_HEADS = 8
B_KV_HEADS = 2
B_HEAD_DIM = 64
B_WIDTH = B_HEADS * B_HEAD_DIM
B_GROUP = B_HEADS // B_KV_HEADS
IDX_HEADS = 8
IDX_DIM = 64
TOP_K_MAX = 256
N_BUCKETS = 32
MAX_DISTANCE = 128

LANES = 128
VMEM_LIMIT_BYTES = 56 * 1024 * 1024

LOG2E = 1.4426950408889634
NEG_BIG = -1e30
INT_MIN = -(2 ** 31)

C_CQ = 0
C_CKV = C_CQ + A_Q_LORA
C_MISC = C_CKV + A_KV_LORA
C_GA = C_MISC + LANES
C_QB = C_GA + A_WIDTH
C_KB = C_QB + B_WIDTH
C_VB = C_KB + B_KV_HEADS * B_HEAD_DIM
C_QI = C_VB + B_KV_HEADS * B_HEAD_DIM
C_GB = C_QI + IDX_HEADS * IDX_DIM
IN_PAD = C_GB + B_WIDTH
M_KPE = 0
M_KIDX = A_ROPE
M_WIDX = A_ROPE + IDX_DIM

QCAT = 2 * LANES
TQ = 256
PROJ_TM = 256


def _dot(a, b):
    return jnp.dot(a, b, preferred_element_type=F32)


def _dot_nt(a, b):
    return lax.dot_general(a, b, (((1,), (1,)), ((), ())), preferred_element_type=F32)


def _rms(x, g):
    return x * lax.rsqrt(jnp.mean(x * x, axis=-1, keepdims=True) + EPS) * g


def _bias_kernel(rb_ref, out_ref, *, offsets, keys_on_rows):
    nb = N_BUCKETS // 2
    max_exact = nb // 2
    n_r, n_c = out_ref.shape[2], out_ref.shape[3]
    row = lax.broadcasted_iota(I32, (n_r, n_c), 0)
    col = lax.broadcasted_iota(I32, (n_r, n_c), 1)
    for p, off in enumerate(offsets):
        rel = off + (row - col if keys_on_rows else col - row)
        ret = jnp.where(rel > 0, nb, 0)
        n = jnp.abs(rel)
        nf = jnp.maximum(n, 1).astype(F32)
        large = max_exact + (jnp.log(nf / max_exact) / math.log(MAX_DISTANCE / max_exact)
                             * (nb - max_exact)).astype(I32)
        large = jnp.minimum(large, nb - 1)
        bucket = ret + jnp.where(n < max_exact, n, large)
        for h in range(B_HEADS):
            acc = jnp.zeros((n_r, n_c), F32)
            for b in range(N_BUCKETS):
                acc = jnp.where(bucket == b, rb_ref[b, h], acc)
            out_ref[p, h] = acc * LOG2E


def _bias_tables(rel_bias, offsets, n_r, n_c, keys_on_rows):
    return pl.pallas_call(
        functools.partial(_bias_kernel, offsets=tuple(offsets), keys_on_rows=keys_on_rows),
        out_shape=jax.ShapeDtypeStruct((len(offsets), B_HEADS, n_r, n_c), F32),
        in_specs=[pl.BlockSpec(memory_space=pltpu.SMEM)],
        out_specs=pl.BlockSpec(memory_space=pltpu.VMEM),
        compiler_params=pltpu.CompilerParams(vmem_limit_bytes=VMEM_LIMIT_BYTES),
        name="bias_tables",
    )(rel_bias)


def _proj_kernel(x_ref, ng_ref, win_ref, qng_ref, kvng_ref, wq_ref, wuk_ref, pmat_ref,
                 cosq_ref, sinq_ref, rc_ref, rs1_ref, rs2_ref,
                 ckv_ref, kpe_ref, kb_ref, vb_ref, kidx_ref,
                 kcat_ref, kb16_ref, vb16_ref, kidx16_ref,
                 qcat_ref, qb_ref, qidx_ref, widx_ref, sga_ref, sgb_ref):
    x = x_ref[...]
    h = _rms(x, ng_ref[...])
    z = _dot(h.astype(BF16), win_ref[...])

    cq = _rms(z[:, C_CQ:C_CQ + A_Q_LORA], qng_ref[...])
    q = _dot(cq.astype(BF16), wq_ref[...])
    n_nope = A_HEADS * A_NOPE
    x1 = q[:, n_nope:n_nope + LANES]
    x2 = q[:, n_nope + LANES:n_nope + 2 * LANES]
    cos8, sin8 = cosq_ref[...], sinq_ref[...]
    o1 = x1 * cos8 - x2 * sin8
    o2 = x1 * sin8 + x2 * cos8
    mla_scale = (A_NOPE + A_ROPE) ** -0.5 * LOG2E
    q_lat = _dot(q[:, :n_nope].astype(BF16), wuk_ref[...]) * mla_scale
    pe = jnp.concatenate([o1, o2], axis=1) * mla_scale
    q_pe = _dot(pe.astype(BF16), pmat_ref[...])
    for hh in range(A_HEADS):
        qcat_ref[:, hh * QCAT:hh * QCAT + LANES] = q_lat[:, hh * LANES:(hh + 1) * LANES].astype(BF16)
        qcat_ref[:, hh * QCAT + LANES:(hh + 1) * QCAT] = q_pe[:, hh * LANES:(hh + 1) * LANES].astype(BF16)

    ckv = _rms(z[:, C_CKV:C_CKV + A_KV_LORA], kvng_ref[...])
    ckv_ref[...] = ckv
    misc = z[:, C_MISC:C_MISC + LANES]
    rot = (misc * rc_ref[...] + pltpu.roll(misc, A_ROPE // 2, 1) * rs1_ref[...]
           + pltpu.roll(misc, LANES - A_ROPE // 2, 1) * rs2_ref[...])
    kpe_ref[...] = rot[:, :A_ROPE]
    kcat_ref[:, :LANES] = ckv.astype(BF16)
    kcat_ref[:, LANES:] = rot.astype(BF16)

    kidx = misc[:, M_KIDX:M_KIDX + IDX_DIM]
    kidx_ref[...] = kidx
    kidx16_ref[...] = kidx.astype(BF16)
    widx_ref[...] = misc[:, M_WIDX:M_WIDX + IDX_HEADS] * (IDX_HEADS ** -0.5)
    kb = z[:, C_KB:C_KB + LANES]
    vb = z[:, C_VB:C_VB + LANES]
    kb_ref[...] = kb
    vb_ref[...] = vb
    kb16_ref[...] = kb.astype(BF16)
    vb16_ref[...] = vb.astype(BF16)
    qb_ref[...] = (z[:, C_QB:C_QB + B_WIDTH] * (B_HEAD_DIM ** -0.5 * LOG2E)).astype(BF16)
    qidx_ref[...] = (z[:, C_QI:C_QI + IDX_HEADS * IDX_DIM] * (IDX_DIM ** -0.5)).astype(BF16)
    sga_ref[...] = jax.nn.silu(z[:, C_GA:C_GA + A_WIDTH])
    sgb_ref[...] = jax.nn.silu(z[:, C_GB:C_GB + B_WIDTH])


def _project(x2d, rope_tabs, lw, *, period):
    n, d = x2d.shape
    tm = PROJ_TM
    assert n % tm == 0 and period % tm == 0
    n_rep = period // tm
    tok = lambda c: pl.BlockSpec((tm, c), lambda i: (i, 0))
    full = lambda a: pl.BlockSpec(a.shape, lambda i: (0,) * a.ndim)
    tab = pl.BlockSpec((tm, LANES), lambda i: (i % n_rep, 0))
    outs = [
        ("ckv", A_KV_LORA, F32), ("kpe", A_ROPE, F32), ("kb", LANES, F32), ("vb", LANES, F32),
        ("kidx", IDX_DIM, F32),
        ("kcat", QCAT, BF16), ("kb16", LANES, BF16), ("vb16", LANES, BF16), ("kidx16", IDX_DIM, BF16),
        ("qcat", A_HEADS * QCAT, BF16), ("qb", B_WIDTH, BF16), ("qidx", IDX_HEADS * IDX_DIM, BF16),
        ("widx", IDX_HEADS, F32), ("sga", A_WIDTH, F32), ("sgb", B_WIDTH, F32),
    ]
    res = pl.pallas_call(
        _proj_kernel,
        grid=(n // tm,),
        in_specs=[tok(d), full(lw["ng"]), full(lw["win"]), full(lw["qng"]), full(lw["kvng"]),
                  full(lw["wq"]), full(lw["wuk"]), full(lw["pmat"]), tab, tab, tab, tab, tab],
        out_specs=[tok(c) for _, c, _ in outs],
        out_shape=[jax.ShapeDtypeStruct((n, c), dt) for _, c, dt in outs],
        compiler_params=pltpu.CompilerParams(dimension_semantics=("arbitrary",),
                                             vmem_limit_bytes=VMEM_LIMIT_BYTES),
        name="project",
    )(x2d, lw["ng"], lw["win"], lw["qng"], lw["kvng"], lw["wq"], lw["wuk"], lw["pmat"], *rope_tabs)
    return {name: r for (name, _, _), r in zip(outs, res)}


R_CQ = 0
R_CKV = R_CQ + A_Q_LORA
R_VB = R_CKV + A_KV_LORA
R_QB = R_VB + LANES
R_QI = R_QB + B_WIDTH
R_GA = R_QI + IDX_HEADS * IDX_DIM
R_GB = R_GA + A_WIDTH
R_WI = R_GB + B_WIDTH
ROWS_T = R_WI + 16
K_MISC = A_KV_LORA
K_KB = K_MISC + LANES
K_VB = K_KB + LANES
COLS_K = K_VB + LANES
MK_KPE = IDX_DIM
ONES_ROWS = 16
V_EXT = A_KV_LORA + ONES_ROWS


def _proj_t_kernel(x_ref, ng_ref, wk_ref, wt_ref, qng_ref, kvngc_ref, kvng_ref, wqt_ref, wukt_ref,
                   cost_ref, sint_ref, rc_ref, rs1_ref, rs2_ref,
                   ckv_ref, kpe_ref, kb_ref, vb_ref, kidx_ref, kcat_ref, kb16_ref, kidx16_ref,
                   qcatT_ref, qbT_ref, qidxT_ref, widxT_ref, vmlaT_ref, vdsaT_ref, sgaT_ref, sgbT_ref):
    x = x_ref[...]
    tm = x.shape[0]
    hb = _rms(x, ng_ref[...]).astype(BF16)

    zk = _dot(hb, wk_ref[...])
    ckv = _rms(zk[:, :A_KV_LORA], kvng_ref[...])
    ckv_ref[...] = ckv
    misc = zk[:, K_MISC:K_MISC + LANES]
    kidx_ref[...] = misc[:, :IDX_DIM]
    lane = lax.broadcasted_iota(I32, (tm, LANES), 1)
    kidx16_ref[...] = jnp.where(lane < IDX_DIM, misc, 0.0).astype(BF16)
    rot = (misc * rc_ref[...] + pltpu.roll(misc, A_ROPE // 2, 1) * rs1_ref[...]
           + pltpu.roll(misc, LANES - A_ROPE // 2, 1) * rs2_ref[...])
    kpe_ref[...] = rot[:, MK_KPE:MK_KPE + A_ROPE]
    kcat_ref[:, :LANES] = ckv.astype(BF16)
    kcat_ref[:, LANES:] = pltpu.roll(rot, LANES - MK_KPE, 1).astype(BF16)
    kb = zk[:, K_KB:K_KB + LANES]
    vb = zk[:, K_VB:K_VB + LANES]
    kb_ref[...] = kb
    vb_ref[...] = vb
    kb16_ref[...] = kb.astype(BF16)

    zt = _dot_nt(wt_ref[...], hb)

    def rms_t(c, g):
        return c * lax.rsqrt(jnp.mean(c * c, axis=0, keepdims=True) + EPS) * g

    cq = rms_t(zt[R_CQ:R_CQ + A_Q_LORA], qng_ref[...])
    qt = _dot(wqt_ref[...], cq.astype(BF16))
    n_nope = A_HEADS * A_NOPE
    x1 = qt[n_nope:n_nope + LANES]
    x2 = qt[n_nope + LANES:n_nope + 2 * LANES]
    cos8, sin8 = cost_ref[...], sint_ref[...]
    mla_scale = (A_NOPE + A_ROPE) ** -0.5 * LOG2E
    o1 = (x1 * cos8 - x2 * sin8) * mla_scale
    o2 = (x1 * sin8 + x2 * cos8) * mla_scale
    q_lat = _dot(wukt_ref[...], qt[:n_nope].astype(BF16)) * mla_scale
    half = A_ROPE // 2
    for h in range(A_HEADS):
        qcatT_ref[h, :LANES, :] = q_lat[h * LANES:(h + 1) * LANES].astype(BF16)
        qcatT_ref[h, LANES:LANES + half, :] = o1[h * half:(h + 1) * half].astype(BF16)
        qcatT_ref[h, LANES + half:LANES + A_ROPE, :] = o2[h * half:(h + 1) * half].astype(BF16)
        qcatT_ref[h, LANES + A_ROPE:, :] = jnp.zeros((QCAT - LANES - A_ROPE, tm), BF16)
    ones = jnp.ones((ONES_ROWS, tm), BF16)
    vmlaT_ref[:A_KV_LORA, :] = rms_t(zt[R_CKV:R_CKV + A_KV_LORA], kvngc_ref[...]).astype(BF16)
    vmlaT_ref[A_KV_LORA:, :] = ones
    vdsaT_ref[:LANES, :] = zt[R_VB:R_VB + LANES].astype(BF16)
    vdsaT_ref[LANES:, :] = ones
    qbT_ref[...] = (zt[R_QB:R_QB + B_WIDTH] * (B_HEAD_DIM ** -0.5 * LOG2E)).astype(BF16)
    qidxT_ref[...] = (zt[R_QI:R_QI + IDX_HEADS * IDX_DIM] * (IDX_DIM ** -0.5)).astype(BF16)
    sgaT_ref[...] = jax.nn.silu(zt[R_GA:R_GA + A_WIDTH])
    sgbT_ref[...] = jax.nn.silu(zt[R_GB:R_GB + B_WIDTH])
    widxT_ref[...] = zt[R_WI:R_WI + IDX_HEADS] * (IDX_HEADS ** -0.5)


def _project_t(x2d, tabs, lw, b, t):
    n, d = x2d.shape
    tm = TQ
    nt = t // tm
    tok = lambda c: pl.BlockSpec((tm, c), lambda i: (i, 0))
    full = lambda a: pl.BlockSpec(a.shape, lambda i: (0,) * a.ndim)
    tab_t = pl.BlockSpec((LANES, tm), lambda i: (0, i % nt))
    tab_k = pl.BlockSpec((tm, LANES), lambda i: (i % nt, 0))
    rows_t = lambda r: pl.BlockSpec((None, r, tm), lambda i: (i // nt, 0, i % nt))
    outs = [
        ("ckv", (n, A_KV_LORA), F32, tok(A_KV_LORA)), ("kpe", (n, A_ROPE), F32, tok(A_ROPE)),
        ("kb", (n, LANES), F32, tok(LANES)), ("vb", (n, LANES), F32, tok(LANES)),
        ("kidx", (n, IDX_DIM), F32, tok(IDX_DIM)),
        ("kcat", (n, QCAT), BF16, tok(QCAT)), ("kb16", (n, LANES), BF16, tok(LANES)),
        ("kidx16", (n, LANES), BF16, tok(LANES)),
        ("qcatT", (b, A_HEADS, QCAT, t), BF16,
         pl.BlockSpec((None, A_HEADS, QCAT, tm), lambda i: (i // nt, 0, 0, i % nt))),
        ("qbT", (b, B_WIDTH, t), BF16, rows_t(B_WIDTH)),
        ("qidxT", (b, IDX_HEADS * IDX_DIM, t), BF16, rows_t(IDX_HEADS * IDX_DIM)),
        ("widxT", (b, IDX_HEADS, t), F32, rows_t(IDX_HEADS)),
        ("vmlaT", (b, nt, V_EXT, tm), BF16, pl.BlockSpec((None, None, V_EXT, tm), lambda i: (i // nt, i % nt, 0, 0))),
        ("vdsaT", (b, nt, V_EXT, tm), BF16, pl.BlockSpec((None, None, V_EXT, tm), lambda i: (i // nt, i % nt, 0, 0))),
        ("sgaT", (b, A_WIDTH, t), F32, rows_t(A_WIDTH)), ("sgbT", (b, B_WIDTH, t), F32, rows_t(B_WIDTH)),
    ]
    cost, sint, rc, rs1, rs2 = tabs
    res = pl.pallas_call(
        _proj_t_kernel,
        grid=(n // tm,),
        in_specs=[tok(d), full(lw["ng"]), full(lw["wk"]), full(lw["wt"]), full(lw["qngc"]), full(lw["kvngc"]),
                  full(lw["kvng"]), full(lw["wqt"]), full(lw["wukt"]), tab_t, tab_t, tab_k, tab_k, tab_k],
        out_specs=[o[3] for o in outs],
        out_shape=[jax.ShapeDtypeStruct(o[1], o[2]) for o in outs],
        compiler_params=pltpu.CompilerParams(dimension_semantics=("arbitrary",),
                                             vmem_limit_bytes=VMEM_LIMIT_BYTES),
        name="project_prompt",
    )(x2d, lw["ng"], lw["wk"], lw["wt"], lw["qngc"], lw["kvngc"], lw["kvng"], lw["wqt"], lw["wukt"],
      cost, sint, rc, rs1, rs2)
    return {o[0]: r for o, r in zip(outs, res)}


def _to_key(score):
    i = pltpu.bitcast(score, I32)
    i = jnp.where(i == INT_MIN, 0, i)
    return i ^ ((i >> 31) & 0x7FFFFFFF)


def _count(pred):
    return jnp.sum(jnp.where(pred, 1.0, 0.0), axis=1, keepdims=True)


def _fold8(x):
    parts = [x[i:i + 8] for i in range(0, x.shape[0], 8)]
    while len(parts) > 1:
        parts = [a + b for a, b in zip(parts[::2], parts[1::2])]
    return parts[0]


def _prompt_attn_kernel(qcatT_ref, qbT_ref, qidxT_ref, widxT_ref, kcat_ref, kb_ref, kidx_ref,
                        vmlaT_ref, vdsaT_ref, bias_ref, olatT_ref, obT_ref,
                        qbp_ref, qip_ref, keys_ref, m_ref, l_ref, acc_ref, mb_ref, lb_ref, accb_ref, *, n_top):
    qi = pl.program_id(1)
    nblk = qi + 1
    tq = TQ
    krow = lax.broadcasted_iota(I32, (tq, tq), 0)
    qcol = lax.broadcasted_iota(I32, (tq, tq), 1)
    shift = CHUNK.bit_length() - 1
    diag_ok = (qcol >> shift) >= (krow >> shift)

    qbp_ref[...] = jnp.zeros(qbp_ref.shape, BF16)
    qip_ref[...] = jnp.zeros(qip_ref.shape, BF16)
    for h in range(B_HEADS):
        g = h // B_GROUP
        qbp_ref[h, g * B_HEAD_DIM:(g + 1) * B_HEAD_DIM, :] = qbT_ref[h * B_HEAD_DIM:(h + 1) * B_HEAD_DIM, :]
    for h in range(IDX_HEADS):
        qip_ref[h, :IDX_DIM, :] = qidxT_ref[h * IDX_DIM:(h + 1) * IDX_DIM, :]

    def flash_update(h, s_t, v_t, m_r, l_r, acc_r, rows):
        m_prev = m_r[h:h + 1, :]
        m_new = jnp.maximum(m_prev, jnp.max(s_t, axis=0, keepdims=True))
        alpha = jnp.exp2(m_prev - m_new)
        p_t = jnp.exp2(s_t - m_new).astype(BF16)
        pv = _dot(v_t, p_t)
        acc_r[h] = alpha * acc_r[h] + pv[rows]
        l_r[h] = alpha * l_r[h] + pv[A_KV_LORA:A_KV_LORA + 8]
        m_r[h:h + 1, :] = m_new

    m_ref[...] = jnp.full(m_ref.shape, NEG_BIG, F32)
    l_ref[...] = jnp.zeros(l_ref.shape, F32)
    acc_ref[...] = jnp.zeros(acc_ref.shape, F32)

    def mla_block(j, masked):
        start = pl.multiple_of(j * tq, tq)
        kblk = kcat_ref[pl.ds(start, tq), :]
        v_t = vmlaT_ref[j]
        for h in range(A_HEADS):
            s_t = _dot(kblk, qcatT_ref[h])
            if masked:
                s_t = jnp.where(diag_ok, s_t, NEG_BIG)
            flash_update(h, s_t, v_t, m_ref, l_ref, acc_ref, slice(0, A_KV_LORA))

    def mla_body(j, c):
        mla_block(j, False)
        return c

    lax.fori_loop(0, qi, mla_body, 0)
    mla_block(qi, True)
    for h in range(A_HEADS):
        inv = 1.0 / l_ref[h][0:1]
        olatT_ref[h * LANES:(h + 1) * LANES, :] = (acc_ref[h] * inv).astype(olatT_ref.dtype)

    def idx_block(j, masked):
        start = pl.multiple_of(j * tq, tq)
        kblk = kidx_ref[pl.ds(start, tq), :]
        score = jnp.zeros((tq, tq), F32)
        for h in range(IDX_HEADS):
            score = score + jnp.maximum(_dot(kblk, qip_ref[h]), 0.0) * widxT_ref[h:h + 1, :]
        key = _to_key(score)
        if masked:
            key = jnp.where(diag_ok, key, INT_MIN)
        keys_ref[j] = key

    def idx_body(j, c):
        idx_block(j, False)
        return c

    lax.fori_loop(0, qi, idx_body, 0)
    idx_block(qi, True)

    def count(pred):
        def body(j, c):
            return c + _fold8(jnp.where(pred(keys_ref[j], j), 1.0, 0.0))
        part = lax.fori_loop(0, nblk, body, jnp.zeros((8, tq), F32))
        return jnp.sum(part, axis=0, keepdims=True)

    kf = float(n_top)

    def bis_body(it, carry):
        lo, cnt_lo = carry
        cand = lo + lax.shift_left(jnp.int32(1), 31 - it)
        cnt = count(lambda key, j: key >= cand)
        take = cnt >= kf
        return jnp.where(take, cand, lo), jnp.where(take, cnt, cnt_lo)

    thr, cnt_ge = lax.fori_loop(
        0, 32, bis_body, (jnp.full((1, tq), INT_MIN, I32), jnp.full((1, tq), 1e9, F32)))
    few = thr == INT_MIN
    has_tie = jnp.max(jnp.where((cnt_ge > kf) & (~few), 1.0, 0.0))

    @pl.when(has_tie > 0.0)
    def _():
        need = kf - count(lambda key, j: key > thr)
        n_bits = (keys_ref.shape[0] * tq).bit_length()

        def cut_body(it, cpos):
            cand = cpos + lax.shift_left(jnp.int32(1), n_bits - 1 - it)
            cnt = count(lambda key, j: (key == thr) & ((krow + j * tq) < cand))
            return jnp.where(cnt < need, cand, cpos)

        cut = lax.fori_loop(0, n_bits, cut_body, jnp.zeros((1, tq), I32))

        def drop_body(j, c):
            key = keys_ref[j]
            keys_ref[j] = jnp.where((key == thr) & ((krow + j * tq) > cut), INT_MIN, key)
            return c

        lax.fori_loop(0, nblk, drop_body, 0)

    thr_sel = jnp.where(few, INT_MIN + 1, thr)

    mb_ref[...] = jnp.full(mb_ref.shape, NEG_BIG, F32)
    lb_ref[...] = jnp.zeros(lb_ref.shape, F32)
    accb_ref[...] = jnp.zeros(accb_ref.shape, F32)

    def dsa_body(j, c):
        start = pl.multiple_of(j * tq, tq)
        sel = keys_ref[j] >= thr_sel
        pat = jnp.minimum(qi - j, 2)
        kblk = kb_ref[pl.ds(start, tq), :]
        v_t = vdsaT_ref[j]
        for h in range(B_HEADS):
            g = h // B_GROUP
            s_t = jnp.where(sel, _dot(kblk, qbp_ref[h]) + bias_ref[pat, h], NEG_BIG)
            flash_update(h, s_t, v_t, mb_ref, lb_ref, accb_ref, slice(g * B_HEAD_DIM, (g + 1) * B_HEAD_DIM))
        return c

    lax.fori_loop(0, nblk, dsa_body, 0)
    for h in range(B_HEADS):
        inv = 1.0 / lb_ref[h][0:1]
        obT_ref[h * B_HEAD_DIM:(h + 1) * B_HEAD_DIM, :] = accb_ref[h] * inv


def _prompt_attention(pr, bias_p, b, t, n_top):
    tq = TQ
    assert t % tq == 0
    nq = t // tq
    r3 = lambda a: a.reshape(b, t, a.shape[-1])
    qrows = lambda r: pl.BlockSpec((None, r, tq), lambda bi, qi: (bi, 0, qi))
    kall = lambda c: pl.BlockSpec((None, t, c), lambda bi, qi: (bi, 0, 0))
    vall = pl.BlockSpec((None, nq, V_EXT, tq), lambda bi, qi: (bi, 0, 0, 0))
    olat, ob = pl.pallas_call(
        functools.partial(_prompt_attn_kernel, n_top=n_top),
        grid=(b, nq),
        in_specs=[pl.BlockSpec((None, A_HEADS, QCAT, tq), lambda bi, qi: (bi, 0, 0, qi)),
                  qrows(B_WIDTH), qrows(IDX_HEADS * IDX_DIM), qrows(IDX_HEADS),
                  kall(QCAT), kall(LANES), kall(LANES), vall, vall,
                  pl.BlockSpec(bias_p.shape, lambda bi, qi: (0, 0, 0, 0))],
        out_specs=[qrows(A_HEADS * A_KV_LORA), qrows(B_WIDTH)],
        out_shape=[jax.ShapeDtypeStruct((b, A_HEADS * A_KV_LORA, t), BF16),
                   jax.ShapeDtypeStruct((b, B_WIDTH, t), F32)],
        scratch_shapes=[
            pltpu.VMEM((B_HEADS, LANES, tq), BF16),
            pltpu.VMEM((IDX_HEADS, LANES, tq), BF16),
            pltpu.VMEM((nq, tq, tq), I32),
            pltpu.VMEM((A_HEADS, tq), F32),
            pltpu.VMEM((A_HEADS, 8, tq), F32),
            pltpu.VMEM((A_HEADS, A_KV_LORA, tq), F32),
            pltpu.VMEM((B_HEADS, tq), F32),
            pltpu.VMEM((B_HEADS, 8, tq), F32),
            pltpu.VMEM((B_HEADS, B_HEAD_DIM, tq), F32),
        ],
        compiler_params=pltpu.CompilerParams(dimension_semantics=("arbitrary", "arbitrary"),
                                             vmem_limit_bytes=VMEM_LIMIT_BYTES),
        name="prompt_attention",
    )(pr["qcatT"], pr["qbT"], pr["qidxT"], pr["widxT"],
      r3(pr["kcat"]), r3(pr["kb16"]), r3(pr["kidx16"]), pr["vmlaT"], pr["vdsaT"], bias_p)
    return olat, ob


def _sample_attn_kernel(qcat_ref, qb_ref, qidx_ref, widx_ref, kcatn_ref, kbn_ref, vbn_ref, kidxn_ref,
                        cckv_ref, ckpe_ref, ck_ref, cv_ref, ckidx_ref, bias_ref,
                        olat_ref, ob_ref, *, n_top, t_new, past):
    tq = t_new
    pad = LANES
    n_keys = past + pad
    colmask = lax.broadcasted_iota(I32, (tq, n_keys), 1) < past + t_new

    def padrows(a):
        return jnp.concatenate([a, jnp.zeros((pad - t_new, a.shape[1]), a.dtype)], axis=0)

    kpe_c = ckpe_ref[...].astype(BF16)
    kcat_c = jnp.concatenate(
        [cckv_ref[...].astype(BF16), kpe_c, jnp.zeros((past, QCAT - A_KV_LORA - A_ROPE), BF16)], axis=1)
    kcat_n = padrows(kcatn_ref[...])
    k_c = ck_ref[...].astype(BF16)
    v_c = cv_ref[...].astype(BF16)
    k_n = padrows(kbn_ref[...])
    v_n = padrows(vbn_ref[...])
    kidx_c = ckidx_ref[...].astype(BF16)
    kidx_n = padrows(kidxn_ref[...])

    def softmax_pv(s, v_c_, v_n_):
        m = jnp.max(s, axis=1, keepdims=True)
        p = jnp.exp2(s - m)
        l = jnp.sum(p, axis=1, keepdims=True)
        pb = p.astype(BF16)
        o = _dot(pb[:, :past], v_c_) + _dot(pb[:, past:], v_n_)
        return o / l

    qs = jnp.concatenate([qcat_ref[:, h * QCAT:(h + 1) * QCAT] for h in range(A_HEADS)], axis=0)
    s = jnp.concatenate([_dot_nt(qs, kcat_c), _dot_nt(qs, kcat_n)], axis=1)
    cm8 = lax.broadcasted_iota(I32, (A_HEADS * tq, n_keys), 1) < past + t_new
    s = jnp.where(cm8, s, NEG_BIG)
    o = softmax_pv(s, kcat_c[:, :A_KV_LORA], kcat_n[:, :A_KV_LORA])
    for h in range(A_HEADS):
        olat_ref[:, h * LANES:(h + 1) * LANES] = o[h * tq:(h + 1) * tq].astype(olat_ref.dtype)

    qis = jnp.concatenate([qidx_ref[:, h * IDX_DIM:(h + 1) * IDX_DIM] for h in range(IDX_HEADS)], axis=0)
    dots = jnp.concatenate([_dot_nt(qis, kidx_c), _dot_nt(qis, kidx_n)], axis=1)
    score = jnp.zeros((tq, n_keys), F32)
    for h in range(IDX_HEADS):
        score = score + jnp.maximum(dots[h * tq:(h + 1) * tq], 0.0) * widx_ref[:, h:h + 1]
    key = jnp.where(colmask, _to_key(score), INT_MIN)
    kf = float(n_top)

    def bis_body(it, carry):
        lo, cnt_lo = carry
        cand = lo + lax.shift_left(jnp.int32(1), 31 - it)
        cnt = _count(key >= cand)
        take = cnt >= kf
        return jnp.where(take, cand, lo), jnp.where(take, cnt, cnt_lo)

    thr, cnt_ge = lax.fori_loop(
        0, 32, bis_body, (jnp.full((tq, 1), INT_MIN, I32), jnp.full((tq, 1), 1e9, F32)))
    few = thr == INT_MIN
    cols = lax.broadcasted_iota(I32, (tq, n_keys), 1)
    has_tie = jnp.max(jnp.where((cnt_ge > kf) & (~few), 1.0, 0.0))

    def tie_path():
        need = kf - _count(key > thr)
        eq = key == thr

        def cut_body(it, cpos):
            cand = cpos + lax.shift_left(jnp.int32(1), 13 - it)
            cnt = _count(eq & (cols < cand))
            return jnp.where(cnt < need, cand, cpos)

        cpos = lax.fori_loop(0, 14, cut_body, jnp.zeros((tq, 1), I32))
        return jnp.where(few, -1, cpos)

    cut = lax.cond(has_tie > 0.0, tie_path, lambda: jnp.where(few, -1, jnp.int32(2 ** 30)))
    sel = (key > thr) | ((key == thr) & (cols <= cut))

    n_far = n_keys - bias_ref.shape[-1]
    for g in range(B_KV_HEADS):
        lanes = slice(g * B_HEAD_DIM, (g + 1) * B_HEAD_DIM)
        qg = jnp.concatenate(
            [qb_ref[:, h * B_HEAD_DIM:(h + 1) * B_HEAD_DIM] for h in range(g * B_GROUP, (g + 1) * B_GROUP)],
            axis=0)
        sg = jnp.concatenate([_dot_nt(qg, k_c[:, lanes]), _dot_nt(qg, k_n[:, lanes])], axis=1)
        for hh in range(B_GROUP):
            h = g * B_GROUP + hh
            near = bias_ref[0, h]
            far = bias_ref[1, h][:, :1]
            sh = sg[hh * tq:(hh + 1) * tq]
            sh = jnp.concatenate([sh[:, :n_far] + far, sh[:, n_far:] + near], axis=1)
            sh = jnp.where(sel, sh, NEG_BIG)
            oh = softmax_pv(sh, v_c[:, lanes], v_n[:, lanes])
            ob_ref[:, h * B_HEAD_DIM:(h + 1) * B_HEAD_DIM] = oh


def _sample_attention(pr, caches, bias_s, b, t_new, past, n_top):
    r3 = lambda a: a.reshape(b, t_new, a.shape[-1])
    per_b = lambda n, c: pl.BlockSpec((None, n, c), lambda bi: (bi, 0, 0))
    news = [pr["qcat"], pr["qb"], pr["qidx"], pr["widx"], pr["kcat"], pr["kb16"], pr["vb16"], pr["kidx16"]]
    olat, ob = pl.pallas_call(
        functools.partial(_sample_attn_kernel, n_top=n_top, t_new=t_new, past=past),
        grid=(b,),
        in_specs=[per_b(t_new, a.shape[-1]) for a in news]
                 + [per_b(past, c.shape[-1]) for c in caches]
                 + [pl.BlockSpec(bias_s.shape, lambda bi: (0, 0, 0, 0))],
        out_specs=[per_b(t_new, A_HEADS * A_KV_LORA), per_b(t_new, B_WIDTH)],
        out_shape=[jax.ShapeDtypeStruct((b, t_new, A_HEADS * A_KV_LORA), BF16),
                   jax.ShapeDtypeStruct((b, t_new, B_WIDTH), F32)],
        compiler_params=pltpu.CompilerParams(dimension_semantics=("arbitrary",),
                                             vmem_limit_bytes=VMEM_LIMIT_BYTES),
        name="sample_attention",
    )(*[r3(a) for a in news], *caches, bias_s)
    return olat.reshape(b * t_new, -1), ob.reshape(b * t_new, -1)


def _combine_kernel(x_ref, olat_ref, ob_ref, sga_ref, sgb_ref, wuv_ref, wout_ref, fg_ref, y_ref, *, final):
    o_a = _dot(olat_ref[...], wuv_ref[...])
    mix = jnp.concatenate([o_a * sga_ref[...], ob_ref[...] * sgb_ref[...]], axis=1)
    y = x_ref[...] + _dot(mix.astype(BF16), wout_ref[...])
    if final:
        y = _rms(y, fg_ref[...])
    y_ref[...] = y


def _combine(x2d, olat, ob, pr, lw, fg, final):
    n, d = x2d.shape
    tm = PROJ_TM
    tok = lambda c: pl.BlockSpec((tm, c), lambda i: (i, 0))
    full = lambda a: pl.BlockSpec(a.shape, lambda i: (0,) * a.ndim)
    return pl.pallas_call(
        functools.partial(_combine_kernel, final=final),
        grid=(n // tm,),
        in_specs=[tok(d), tok(olat.shape[1]), tok(ob.shape[1]), tok(A_WIDTH), tok(B_WIDTH),
                  full(lw["wuv"]), full(lw["wout"]), full(fg)],
        out_specs=tok(d),
        out_shape=jax.ShapeDtypeStruct((n, d), F32),
        compiler_params=pltpu.CompilerParams(dimension_semantics=("arbitrary",),
                                             vmem_limit_bytes=VMEM_LIMIT_BYTES),
        name="combine",
    )(x2d, olat, ob, pr["sga"], pr["sgb"], lw["wuv"], lw["wout"], fg)


def _combine_t_kernel(x_ref, olatT_ref, obT_ref, sgaT_ref, sgbT_ref, wuvt_ref, wout_ref, fg_ref, y_ref, *, final):
    o_a = _dot(wuvt_ref[...], olatT_ref[...])
    mix_t = jnp.concatenate([o_a * sgaT_ref[...], obT_ref[...] * sgbT_ref[...]], axis=0).astype(BF16)
    y = x_ref[...] + lax.dot_general(mix_t, wout_ref[...], (((0,), (0,)), ((), ())),
                                     preferred_element_type=F32)
    if final:
        y = _rms(y, fg_ref[...])
    y_ref[...] = y


def _combine_t(x3d, olat_t, ob_t, pr, lw, fg, final):
    b, t, d = x3d.shape
    tm = TQ
    rows_t = lambda r: pl.BlockSpec((None, r, tm), lambda bi, ti: (bi, 0, ti))
    full = lambda a: pl.BlockSpec(a.shape, lambda bi, ti: (0,) * a.ndim)
    xblk = pl.BlockSpec((None, tm, d), lambda bi, ti: (bi, ti, 0))
    return pl.pallas_call(
        functools.partial(_combine_t_kernel, final=final),
        grid=(b, t // tm),
        in_specs=[xblk, rows_t(A_HEADS * A_KV_LORA), rows_t(B_WIDTH), rows_t(A_WIDTH), rows_t(B_WIDTH),
                  full(lw["wuvt"]), full(lw["wout"]), full(fg)],
        out_specs=xblk,
        out_shape=jax.ShapeDtypeStruct((b, t, d), F32),
        compiler_params=pltpu.CompilerParams(dimension_semantics=("arbitrary", "arbitrary"),
                                             vmem_limit_bytes=VMEM_LIMIT_BYTES),
        name="combine_prompt",
    )(x3d, olat_t, ob_t, pr["sgaT"], pr["sgbT"], lw["wuvt"], lw["wout"], fg)


def _layer_weights(norm_g, w_in, q_norm_g, kv_norm_g, w_uq, w_uk, w_uv, w_out):
    d = w_in.shape[0]
    o = np.cumsum([0, A_Q_LORA, A_KV_LORA, A_ROPE, A_WIDTH, B_WIDTH, B_KV_HEADS * B_HEAD_DIM,
                   B_KV_HEADS * B_HEAD_DIM, IDX_HEADS * IDX_DIM, IDX_DIM, IDX_HEADS, B_WIDTH])
    seg = lambda i: w_in[:, int(o[i]):int(o[i + 1])]
    misc = jnp.concatenate([seg(2), seg(8), seg(9),
                            jnp.zeros((d, LANES - A_ROPE - IDX_DIM - IDX_HEADS), w_in.dtype)], axis=1)
    win = jnp.concatenate([seg(0), seg(1), misc, seg(3), seg(4), seg(5), seg(6), seg(7), seg(10)], axis=1)
    assert win.shape[1] == IN_PAD
    half = A_ROPE // 2
    wq = jnp.concatenate([
        w_uq[:, :, :A_NOPE].reshape(A_Q_LORA, A_HEADS * A_NOPE),
        w_uq[:, :, A_NOPE:A_NOPE + half].reshape(A_Q_LORA, A_HEADS * half),
        w_uq[:, :, A_NOPE + half:].reshape(A_Q_LORA, A_HEADS * half)], axis=1)
    eye = jnp.eye(A_HEADS, dtype=w_uk.dtype)
    wuk = jnp.einsum('chn,hg->hngc', w_uk, eye).reshape(A_HEADS * A_NOPE, A_HEADS * A_KV_LORA)
    wuv = jnp.einsum('chv,hg->hcgv', w_uv, eye).reshape(A_HEADS * A_KV_LORA, A_HEADS * A_V)
    pm = np.zeros((2 * LANES, A_HEADS * LANES), np.float32)
    for h in range(A_HEADS):
        for i in range(half):
            pm[h * half + i, h * LANES + i] = 1.0
            pm[LANES + h * half + i, h * LANES + half + i] = 1.0
    zpad = lambda c: jnp.zeros((d, c), w_in.dtype)
    wk = jnp.concatenate([seg(1), seg(8), seg(2), zpad(LANES - IDX_DIM - A_ROPE), seg(5), seg(6)], axis=1)
    wt = jnp.concatenate([seg(0), seg(1), seg(6), seg(4), seg(7), seg(3), seg(10), seg(9),
                          zpad(ROWS_T - R_WI - IDX_HEADS)], axis=1).T
    assert wk.shape[1] == COLS_K and wt.shape[0] == ROWS_T
    bc = lambda g: jnp.broadcast_to(g.reshape(-1, 1), (g.shape[0], TQ))
    return {
        "wk": wk.astype(BF16), "wt": wt.astype(BF16), "qngc": bc(q_norm_g), "kvngc": bc(kv_norm_g),
        "wqt": wq.T.astype(BF16), "wukt": wuk.T.astype(BF16), "wuvt": wuv.T.astype(BF16),
        "ng": norm_g.reshape(1, -1), "win": win.astype(BF16),
        "qng": q_norm_g.reshape(1, -1), "kvng": kv_norm_g.reshape(1, -1),
        "wq": wq.astype(BF16), "wuk": wuk.astype(BF16), "wuv": wuv.astype(BF16),
        "pmat": jnp.asarray(pm, BF16), "wout": w_out.astype(BF16),
    }


def _rope_tables(pos):
    half = A_ROPE // 2
    inv = ROPE_THETA ** (-jnp.arange(half, dtype=F32) / half)
    ang = pos.astype(F32)[:, None] * inv[None, :]
    cos, sin = jnp.cos(ang), jnp.sin(ang)
    z = jnp.zeros((pos.shape[0], LANES - A_ROPE), F32)
    zh = jnp.zeros_like(cos)
    cosq = jnp.tile(cos, (1, A_HEADS))
    sinq = jnp.tile(sin, (1, A_HEADS))
    rc = jnp.concatenate([cos, cos, z], axis=1)
    rs1 = jnp.concatenate([zh, sin, z], axis=1)
    rs2 = jnp.concatenate([-sin, zh, z], axis=1)
    return cosq, sinq, rc, rs1, rs2


def _rope_tables_t(pos):
    half = A_ROPE // 2
    inv = ROPE_THETA ** (-jnp.arange(half, dtype=F32) / half)
    ang = pos.astype(F32)[:, None] * inv[None, :]
    cos, sin = jnp.cos(ang), jnp.sin(ang)
    cost = jnp.tile(cos.T, (A_HEADS, 1))
    sint = jnp.tile(sin.T, (A_HEADS, 1))
    z0 = jnp.zeros((pos.shape[0], MK_KPE), F32)
    z1 = jnp.zeros((pos.shape[0], LANES - MK_KPE - A_ROPE), F32)
    zh = jnp.zeros_like(cos)
    rc = jnp.concatenate([z0, cos, cos, z1], axis=1)
    rs1 = jnp.concatenate([z0, zh, sin, z1], axis=1)
    rs2 = jnp.concatenate([z0, -sin, zh, z1], axis=1)
    return cost, sint, rc, rs1, rs2


def kernel(x_prompt, x_sample, cache_mla_ckv, cache_mla_kpe, cache_dsa_k, cache_dsa_v, cache_dsa_kidx,
           norm_g, w_in, mla_q_norm_g, mla_kv_norm_g, mla_w_uq, mla_w_uk, mla_w_uv, rel_bias, w_out,
           final_norm_g):
    bp, tp, d = x_prompt.shape
    bs, ts, _ = x_sample.shape
    depth = w_in.shape[0]
    past = cache_mla_ckv.shape[2]
    n_top_p = min(TOP_K_MAX, tp // 4)
    n_top_s = min(TOP_K_MAX, (past + ts) // 4)
    assert ts <= CHUNK and past % CHUNK == 0 and past % LANES == 0

    rope_p = _rope_tables_t(jnp.arange(tp, dtype=jnp.int32))
    reps = PROJ_TM // ts
    rope_s = tuple(jnp.tile(a, (reps, 1)) for a in _rope_tables(past + jnp.arange(ts, dtype=jnp.int32)))

    bias_p = _bias_tables(rel_bias, (0, -TQ, -3 * TQ), TQ, TQ, True)
    win_s = 2 * LANES
    bias_s = _bias_tables(rel_bias, (-(win_s - LANES), -(past + win_s)), ts, win_s, False)
    fg = final_norm_g.reshape(1, -1)

    xp = x_prompt
    xs = x_sample.reshape(bs * ts, d)
    outs_p, outs_s = [], []
    for l in range(depth):
        lw = _layer_weights(norm_g[l], w_in[l], mla_q_norm_g[l], mla_kv_norm_g[l],
                            mla_w_uq[l], mla_w_uk[l], mla_w_uv[l], w_out[l])
        final = l == depth - 1
        pr = _project_t(xp.reshape(bp * tp, d), rope_p, lw, bp, tp)
        olat, ob = _prompt_attention(pr, bias_p, bp, tp, n_top_p)
        xp = _combine_t(xp, olat, ob, pr, lw, fg, final)
        outs_p.append((pr["ckv"].reshape(bp, tp, A_KV_LORA), pr["kpe"].reshape(bp, tp, A_ROPE),
                       pr["kb"].reshape(bp, tp, B_KV_HEADS, B_HEAD_DIM),
                       pr["vb"].reshape(bp, tp, B_KV_HEADS, B_HEAD_DIM),
                       pr["kidx"].reshape(bp, tp, IDX_DIM)))
        ps = _project(xs, rope_s, lw, period=PROJ_TM)
        caches = (cache_mla_ckv[l], cache_mla_kpe[l],
                  cache_dsa_k[l].reshape(bs, past, B_KV_HEADS * B_HEAD_DIM),
                  cache_dsa_v[l].reshape(bs, past, B_KV_HEADS * B_HEAD_DIM), cache_dsa_kidx[l])
        olat, ob = _sample_attention(ps, caches, bias_s, bs, ts, past, n_top_s)
        xs = _combine(xs, olat, ob, ps, lw, fg, final)
        outs_s.append((ps["ckv"].reshape(bs, ts, A_KV_LORA), ps["kpe"].reshape(bs, ts, A_ROPE),
                       ps["kb"].reshape(bs, ts, B_KV_HEADS, B_HEAD_DIM),
                       ps["vb"].reshape(bs, ts, B_KV_HEADS, B_HEAD_DIM),
                       ps["kidx"].reshape(bs, ts, IDX_DIM)))

    stack = lambda outs, i: jnp.stack([o[i] for o in outs])
    return ((xp, xs.reshape(bs, ts, d))
            + tuple(stack(outs_p, i) for i in range(5)) + tuple(stack(outs_s, i) for i in range(5)))
```

```python
import functools
import math

import jax
import jax.numpy as jnp
import numpy as np
from jax import lax
from jax.experimental import pallas as pl
from jax.experimental.pallas import tpu as pltpu

F32 = jnp.float32
BF16 = jnp.bfloat16
I32 = jnp.int32

CHUNK = 64
EPS = 1e-6
A_HEADS = 8
A_NOPE = 64
A_ROPE = 32
A_V = 64
A_Q_LORA = 256
A_KV_LORA = 128
ROPE_THETA = 10000.0
A_WIDTH = A_HEADS * A_V
B_HEADS = 8
B_KV_HEADS = 2
B_HEAD_DIM = 64
B_WIDTH = B_HEADS * B_HEAD_DIM
B_GROUP = B_HEADS // B_KV_HEADS
IDX_HEADS = 8
IDX_DIM = 64
TOP_K_MAX = 256
N_BUCKETS = 32
MAX_DISTANCE = 128

LANES = 128
VMEM_LIMIT_BYTES = 56 * 1024 * 1024

LOG2E = 1.4426950408889634
NEG_BIG = -1e30
INT_MIN = -(2 ** 31)

C_CQ = 0
C_CKV = C_CQ + A_Q_LORA
C_MISC = C_CKV + A_KV_LORA
C_GA = C_MISC + LANES
C_QB = C_GA + A_WIDTH
C_KB = C_QB + B_WIDTH
C_VB = C_KB + B_KV_HEADS * B_HEAD_DIM
C_QI = C_VB + B_KV_HEADS * B_HEAD_DIM
C_GB = C_QI + IDX_HEADS * IDX_DIM
IN_PAD = C_GB + B_WIDTH
M_KPE = 0
M_KIDX = A_ROPE
M_WIDX = A_ROPE + IDX_DIM

QCAT = 2 * LANES
TQ = 256
PROJ_TM = 256


def _dot(a, b):
    return jnp.dot(a, b, preferred_element_type=F32)


def _dot_nt(a, b):
    return lax.dot_general(a, b, (((1,), (1,)), ((), ())), preferred_element_type=F32)


def _rms(x, g):
    return x * lax.rsqrt(jnp.mean(x * x, axis=-1, keepdims=True) + EPS) * g


def _bias_kernel(rb_ref, out_ref, *, offsets, keys_on_rows):
    nb = N_BUCKETS // 2
    max_exact = nb // 2
    n_r = out_ref.shape[1] if keys_on_rows else out_ref.shape[2]
    n_c = out_ref.shape[2] // B_HEADS if keys_on_rows else out_ref.shape[3]
    row = lax.broadcasted_iota(I32, (n_r, n_c), 0)
    col = lax.broadcasted_iota(I32, (n_r, n_c), 1)
    for p, off in enumerate(offsets):
        rel = off + (row - col if keys_on_rows else col - row)
        ret = jnp.where(rel > 0, nb, 0)
        n = jnp.abs(rel)
        nf = jnp.maximum(n, 1).astype(F32)
        large = max_exact + (jnp.log(nf / max_exact) / math.log(MAX_DISTANCE / max_exact)
                             * (nb - max_exact)).astype(I32)
        large = jnp.minimum(large, nb - 1)
        bucket = ret + jnp.where(n < max_exact, n, large)
        for h in range(B_HEADS):
            table = jnp.broadcast_to(rb_ref[h:h + 1, :], (n_r, LANES))
            acc = jnp.concatenate(
                [jnp.take_along_axis(table, bucket[:, c:c + LANES], axis=1) for c in range(0, n_c, LANES)],
                axis=1)
            if keys_on_rows:
                out_ref[p, :, h * n_c:(h + 1) * n_c] = acc * LOG2E
            else:
                out_ref[p, h] = acc * LOG2E


def _bias_tables(rel_bias, offsets, n_r, n_c, keys_on_rows):
    shape = (len(offsets), n_r, B_HEADS * n_c) if keys_on_rows else (len(offsets), B_HEADS, n_r, n_c)
    return pl.pallas_call(
        functools.partial(_bias_kernel, offsets=tuple(offsets), keys_on_rows=keys_on_rows),
        out_shape=jax.ShapeDtypeStruct(shape, F32),
        in_specs=[pl.BlockSpec(memory_space=pltpu.VMEM)],
        out_specs=pl.BlockSpec(memory_space=pltpu.VMEM),
        compiler_params=pltpu.CompilerParams(vmem_limit_bytes=VMEM_LIMIT_BYTES),
        name="bias_tables",
    )(jnp.pad(rel_bias.T, ((0, 0), (0, LANES - N_BUCKETS))))


def _proj_kernel(x_ref, ng_ref, win_ref, qng_ref, kvng_ref, wq_ref, wuk_ref, pmat_ref,
                 cosq_ref, sinq_ref, rc_ref, rs1_ref, rs2_ref,
                 ckv_ref, kpe_ref, kb_ref, vb_ref, kidx_ref,
                 kcat_ref, kb16_ref, vb16_ref, kidx16_ref,
                 qcat_ref, qb_ref, qidx_ref, widx_ref, sga_ref, sgb_ref):
    x = x_ref[...]
    h = _rms(x, ng_ref[...])
    z = _dot(h.astype(BF16), win_ref[...])

    cq = _rms(z[:, C_CQ:C_CQ + A_Q_LORA], qng_ref[...])
    q = _dot(cq.astype(BF16), wq_ref[...])
    n_nope = A_HEADS * A_NOPE
    x1 = q[:, n_nope:n_nope + LANES]
    x2 = q[:, n_nope + LANES:n_nope + 2 * LANES]
    cos8, sin8 = cosq_ref[...], sinq_ref[...]
    o1 = x1 * cos8 - x2 * sin8
    o2 = x1 * sin8 + x2 * cos8
    mla_scale = (A_NOPE + A_ROPE) ** -0.5 * LOG2E
    q_lat = _dot(q[:, :n_nope].astype(BF16), wuk_ref[...]) * mla_scale
    pe = jnp.concatenate([o1, o2], axis=1) * mla_scale
    q_pe = _dot(pe.astype(BF16), pmat_ref[...])
    for hh in range(A_HEADS):
        qcat_ref[:, hh * QCAT:hh * QCAT + LANES] = q_lat[:, hh * LANES:(hh + 1) * LANES].astype(BF16)
        qcat_ref[:, hh * QCAT + LANES:(hh + 1) * QCAT] = q_pe[:, hh * LANES:(hh + 1) * LANES].astype(BF16)

    ckv = _rms(z[:, C_CKV:C_CKV + A_KV_LORA], kvng_ref[...])
    ckv_ref[...] = ckv
    misc = z[:, C_MISC:C_MISC + LANES]
    rot = (misc * rc_ref[...] + pltpu.roll(misc, A_ROPE // 2, 1) * rs1_ref[...]
           + pltpu.roll(misc, LANES - A_ROPE // 2, 1) * rs2_ref[...])
    kpe_ref[...] = rot[:, :A_ROPE]
    kcat_ref[:, :LANES] = ckv.astype(BF16)
    kcat_ref[:, LANES:] = rot.astype(BF16)

    kidx = misc[:, M_KIDX:M_KIDX + IDX_DIM]
    kidx_ref[...] = kidx
    kidx16_ref[...] = kidx.astype(BF16)
    widx_ref[...] = misc[:, M_WIDX:M_WIDX + IDX_HEADS] * (IDX_HEADS ** -0.5)
    kb = z[:, C_KB:C_KB + LANES]
    vb = z[:, C_VB:C_VB + LANES]
    kb_ref[...] = kb
    vb_ref[...] = vb
    kb16_ref[...] = kb.astype(BF16)
    vb16_ref[...] = vb.astype(BF16)
    qb_ref[...] = (z[:, C_QB:C_QB + B_WIDTH] * (B_HEAD_DIM ** -0.5 * LOG2E)).astype(BF16)
    qidx_ref[...] = (z[:, C_QI:C_QI + IDX_HEADS * IDX_DIM] * (IDX_DIM ** -0.5)).astype(BF16)
    sga_ref[...] = jax.nn.silu(z[:, C_GA:C_GA + A_WIDTH])
    sgb_ref[...] = jax.nn.silu(z[:, C_GB:C_GB + B_WIDTH])


def _project(x2d, rope_tabs, lw, *, period):
    n, d = x2d.shape
    tm = PROJ_TM
    assert n % tm == 0 and period % tm == 0
    n_rep = period // tm
    tok = lambda c: pl.BlockSpec((tm, c), lambda i: (i, 0))
    full = lambda a: pl.BlockSpec(a.shape, lambda i: (0,) * a.ndim)
    tab = pl.BlockSpec((tm, LANES), lambda i: (i % n_rep, 0))
    outs = [
        ("ckv", A_KV_LORA, F32), ("kpe", A_ROPE, F32), ("kb", LANES, F32), ("vb", LANES, F32),
        ("kidx", IDX_DIM, F32),
        ("kcat", QCAT, BF16), ("kb16", LANES, BF16), ("vb16", LANES, BF16), ("kidx16", IDX_DIM, BF16),
        ("qcat", A_HEADS * QCAT, BF16), ("qb", B_WIDTH, BF16), ("qidx", IDX_HEADS * IDX_DIM, BF16),
        ("widx", IDX_HEADS, F32), ("sga", A_WIDTH, F32), ("sgb", B_WIDTH, F32),
    ]
    res = pl.pallas_call(
        _proj_kernel,
        grid=(n // tm,),
        in_specs=[tok(d), full(lw["ng"]), full(lw["win"]), full(lw["qng"]), full(lw["kvng"]),
                  full(lw["wq"]), full(lw["wuk"]), full(lw["pmat"]), tab, tab, tab, tab, tab],
        out_specs=[tok(c) for _, c, _ in outs],
        out_shape=[jax.ShapeDtypeStruct((n, c), dt) for _, c, dt in outs],
        compiler_params=pltpu.CompilerParams(dimension_semantics=("arbitrary",),
                                             vmem_limit_bytes=VMEM_LIMIT_BYTES),
        name="project",
    )(x2d, lw["ng"], lw["win"], lw["qng"], lw["kvng"], lw["wq"], lw["wuk"], lw["pmat"], *rope_tabs)
    return {name: r for (name, _, _), r in zip(outs, res)}


R_CQ = 0
R_CKV = R_CQ + A_Q_LORA
R_VB = R_CKV + A_KV_LORA
R_QB = R_VB + LANES
R_QI = R_QB + B_WIDTH
R_GA = R_QI + IDX_HEADS * IDX_DIM
R_GB = R_GA + A_WIDTH
R_WI = R_GB + B_WIDTH
ROWS_T = R_WI + 16
K_MISC = A_KV_LORA
K_KB = K_MISC + LANES
K_VB = K_KB + LANES
COLS_K = K_VB + LANES
MK_KPE = IDX_DIM
ONES_ROWS = 16
V_EXT = A_KV_LORA + ONES_ROWS


def _proj_t_kernel(x_ref, ng_ref, wk_ref, wt_ref, qng_ref, kvngc_ref, kvng_ref, wqt_ref, wukt_ref,
                   cost_ref, sint_ref, rc_ref, rs1_ref, rs2_ref,
                   ckv_ref, kpe_ref, kb_ref, vb_ref, kidx_ref, kcat_ref, kb16_ref, kidx16_ref,
                   qcatT_ref, qbT_ref, qidxT_ref, widxT_ref, vmlaT_ref, vdsaT_ref, sgaT_ref, sgbT_ref):
    x = x_ref[...]
    tm = x.shape[0]
    hb = _rms(x, ng_ref[...]).astype(BF16)

    zk = _dot(hb, wk_ref[...])
    ckv = _rms(zk[:, :A_KV_LORA], kvng_ref[...])
    ckv_ref[...] = ckv
    misc = zk[:, K_MISC:K_MISC + LANES]
    kidx_ref[...] = misc[:, :IDX_DIM]
    lane = lax.broadcasted_iota(I32, (tm, LANES), 1)
    kidx16_ref[...] = jnp.where(lane < IDX_DIM, misc, 0.0).astype(BF16)
    rot = (misc * rc_ref[...] + pltpu.roll(misc, A_ROPE // 2, 1) * rs1_ref[...]
           + pltpu.roll(misc, LANES - A_ROPE // 2, 1) * rs2_ref[...])
    kpe_ref[...] = rot[:, MK_KPE:MK_KPE + A_ROPE]
    kcat_ref[:, :LANES] = ckv.astype(BF16)
    kcat_ref[:, LANES:] = pltpu.roll(rot, LANES - MK_KPE, 1).astype(BF16)
    kb = zk[:, K_KB:K_KB + LANES]
    vb = zk[:, K_VB:K_VB + LANES]
    kb_ref[...] = kb
    vb_ref[...] = vb
    kb16_ref[...] = kb.astype(BF16)

    zt = _dot_nt(wt_ref[...], hb)

    def rms_t(c, g):
        return c * lax.rsqrt(jnp.mean(c * c, axis=0, keepdims=True) + EPS) * g

    cq = rms_t(zt[R_CQ:R_CQ + A_Q_LORA], qng_ref[...])
    qt = _dot(wqt_ref[...], cq.astype(BF16))
    n_nope = A_HEADS * A_NOPE
    x1 = qt[n_nope:n_nope + LANES]
    x2 = qt[n_nope + LANES:n_nope + 2 * LANES]
    cos8, sin8 = cost_ref[...], sint_ref[...]
    mla_scale = (A_NOPE + A_ROPE) ** -0.5 * LOG2E
    o1 = (x1 * cos8 - x2 * sin8) * mla_scale
    o2 = (x1 * sin8 + x2 * cos8) * mla_scale
    q_lat = _dot(wukt_ref[...], qt[:n_nope].astype(BF16)) * mla_scale
    half = A_ROPE // 2
    for h in range(A_HEADS):
        qcatT_ref[h, :LANES, :] = q_lat[h * LANES:(h + 1) * LANES].astype(BF16)
        qcatT_ref[h, LANES:LANES + half, :] = o1[h * half:(h + 1) * half].astype(BF16)
        qcatT_ref[h, LANES + half:LANES + A_ROPE, :] = o2[h * half:(h + 1) * half].astype(BF16)
        qcatT_ref[h, LANES + A_ROPE:, :] = jnp.zeros((QCAT - LANES - A_ROPE, tm), BF16)
    ones = jnp.ones((ONES_ROWS, tm), BF16)
    vmlaT_ref[:A_KV_LORA, :] = rms_t(zt[R_CKV:R_CKV + A_KV_LORA], kvngc_ref[...]).astype(BF16)
    vmlaT_ref[A_KV_LORA:, :] = ones
    vdsaT_ref[:LANES, :] = zt[R_VB:R_VB + LANES].astype(BF16)
    vdsaT_ref[LANES:, :] = ones
    qbT_ref[...] = (zt[R_QB:R_QB + B_WIDTH] * (B_HEAD_DIM ** -0.5 * LOG2E)).astype(BF16)
    qidxT_ref[...] = (zt[R_QI:R_QI + IDX_HEADS * IDX_DIM] * (IDX_DIM ** -0.5)).astype(BF16)
    sgaT_ref[...] = jax.nn.silu(zt[R_GA:R_GA + A_WIDTH])
    sgbT_ref[...] = jax.nn.silu(zt[R_GB:R_GB + B_WIDTH])
    widxT_ref[...] = zt[R_WI:R_WI + IDX_HEADS] * (IDX_HEADS ** -0.5)


def _project_t(x2d, tabs, lw, b, t):
    n, d = x2d.shape
    tm = TQ
    nt = t // tm
    tok = lambda c: pl.BlockSpec((tm, c), lambda i: (i, 0))
    full = lambda a: pl.BlockSpec(a.shape, lambda i: (0,) * a.ndim)
    tab_t = pl.BlockSpec((LANES, tm), lambda i: (0, i % nt))
    tab_k = pl.BlockSpec((tm, LANES), lambda i: (i % nt, 0))
    rows_t = lambda r: pl.BlockSpec((None, r, tm), lambda i: (i // nt, 0, i % nt))
    outs = [
        ("ckv", (n, A_KV_LORA), F32, tok(A_KV_LORA)), ("kpe", (n, A_ROPE), F32, tok(A_ROPE)),
        ("kb", (n, LANES), F32, tok(LANES)), ("vb", (n, LANES), F32, tok(LANES)),
        ("kidx", (n, IDX_DIM), F32, tok(IDX_DIM)),
        ("kcat", (n, QCAT), BF16, tok(QCAT)), ("kb16", (n, LANES), BF16, tok(LANES)),
        ("kidx16", (n, LANES), BF16, tok(LANES)),
        ("qcatT", (b, A_HEADS, QCAT, t), BF16,
         pl.BlockSpec((None, A_HEADS, QCAT, tm), lambda i: (i // nt, 0, 0, i % nt))),
        ("qbT", (b, B_WIDTH, t), BF16, rows_t(B_WIDTH)),
        ("qidxT", (b, IDX_HEADS * IDX_DIM, t), BF16, rows_t(IDX_HEADS * IDX_DIM)),
        ("widxT", (b, IDX_HEADS, t), F32, rows_t(IDX_HEADS)),
        ("vmlaT", (b, nt, V_EXT, tm), BF16, pl.BlockSpec((None, None, V_EXT, tm), lambda i: (i // nt, i % nt, 0, 0))),
        ("vdsaT", (b, nt, V_EXT, tm), BF16, pl.BlockSpec((None, None, V_EXT, tm), lambda i: (i // nt, i % nt, 0, 0))),
        ("sgaT", (b, A_WIDTH, t), F32, rows_t(A_WIDTH)), ("sgbT", (b, B_WIDTH, t), F32, rows_t(B_WIDTH)),
    ]
    cost, sint, rc, rs1, rs2 = tabs
    res = pl.pallas_call(
        _proj_t_kernel,
        grid=(n // tm,),
        in_specs=[tok(d), full(lw["ng"]), full(lw["wk"]), full(lw["wt"]), full(lw["qngc"]), full(lw["kvngc"]),
                  full(lw["kvng"]), full(lw["wqt"]), full(lw["wukt"]), tab_t, tab_t, tab_k, tab_k, tab_k],
        out_specs=[o[3] for o in outs],
        out_shape=[jax.ShapeDtypeStruct(o[1], o[2]) for o in outs],
        compiler_params=pltpu.CompilerParams(dimension_semantics=("arbitrary",),
                                             vmem_limit_bytes=VMEM_LIMIT_BYTES),
        name="project_prompt",
    )(x2d, lw["ng"], lw["wk"], lw["wt"], lw["qngc"], lw["kvngc"], lw["kvng"], lw["wqt"], lw["wukt"],
      cost, sint, rc, rs1, rs2)
    return {o[0]: r for o, r in zip(outs, res)}


NEG_FLT_MAX = -3.4028234663852886e38
KEY_NEG_FLT_MAX = INT_MIN + (1 << 23)


def _key_to_float(k):
    k = jnp.maximum(k, KEY_NEG_FLT_MAX)
    return pltpu.bitcast(k ^ ((k >> 31) & 0x7FFFFFFF), F32)


def _count(pred):
    return jnp.sum(jnp.where(pred, 1.0, 0.0), axis=1, keepdims=True)


def _fold8(x):
    parts = [x[i:i + 8] for i in range(0, x.shape[0], 8)]
    while len(parts) > 1:
        parts = [a + b for a, b in zip(parts[::2], parts[1::2])]
    return parts[0]


def _prompt_attn_kernel(qcatT_ref, qbT_ref, qidxT_ref, widxT_ref, kcat_ref, kb_ref, kidx_ref,
                        vmlaT_ref, vdsaT_ref, bias_ref, olatT_ref, obT_ref,
                        qa_ref, qbp_ref, qip_ref, sc_ref, m_ref, acc_ref, mb_ref, accb_ref, *, n_top):
    qi = pl.program_id(1)
    nblk = qi + 1
    tq = TQ
    krow = lax.broadcasted_iota(I32, (tq, tq), 0)
    qcol = lax.broadcasted_iota(I32, (tq, tq), 1)
    shift = CHUNK.bit_length() - 1
    diag_ok = (qcol >> shift) >= (krow >> shift)
    hcols = lambda h: slice(h * tq, (h + 1) * tq)

    def per_head(fn):
        return jnp.concatenate([fn(h) for h in range(A_HEADS)], axis=1)

    qbp_ref[...] = jnp.zeros(qbp_ref.shape, BF16)
    qip_ref[...] = jnp.zeros(qip_ref.shape, BF16)
    for h in range(A_HEADS):
        g = h // B_GROUP
        qa_ref[:, hcols(h)] = qcatT_ref[h]
        qbp_ref[g * B_HEAD_DIM:(g + 1) * B_HEAD_DIM, hcols(h)] = qbT_ref[h * B_HEAD_DIM:(h + 1) * B_HEAD_DIM, :]
        qip_ref[:IDX_DIM, hcols(h)] = qidxT_ref[h * IDX_DIM:(h + 1) * IDX_DIM, :]
    w_all = per_head(lambda h: widxT_ref[h:h + 1, :])

    def flash_update(s_t, v_t, m_r, acc_r):
        m_prev = m_r[0:1, :]
        m_new = jnp.maximum(m_prev, jnp.max(s_t, axis=0, keepdims=True))
        alpha = jnp.exp2(m_prev - m_new)
        p_t = jnp.exp2(s_t - m_new).astype(BF16)
        acc_r[...] = alpha * acc_r[...] + _dot(v_t, p_t)
        m_r[0:1, :] = m_new

    m_ref[...] = jnp.full(m_ref.shape, NEG_BIG, F32)
    acc_ref[...] = jnp.zeros(acc_ref.shape, F32)

    def mla_block(j, masked):
        start = pl.multiple_of(j * tq, tq)
        s_t = _dot(kcat_ref[pl.ds(start, tq), :], qa_ref[...])
        if masked:
            s_t = per_head(lambda h: jnp.where(diag_ok, s_t[:, hcols(h)], NEG_BIG))
        flash_update(s_t, vmlaT_ref[j], m_ref, acc_ref)

    def mla_body(j, c):
        mla_block(j, False)
        return c

    lax.fori_loop(0, qi, mla_body, 0)
    mla_block(qi, True)
    o_t = acc_ref[:A_KV_LORA, :] * (1.0 / acc_ref[A_KV_LORA:A_KV_LORA + 1, :])
    for h in range(A_HEADS):
        olatT_ref[h * LANES:(h + 1) * LANES, :] = o_t[:, hcols(h)].astype(olatT_ref.dtype)

    def idx_block(j, masked):
        start = pl.multiple_of(j * tq, tq)
        r = jnp.maximum(_dot(kidx_ref[pl.ds(start, tq), :], qip_ref[...]), 0.0) * w_all
        score = r[:, hcols(0)]
        for h in range(1, IDX_HEADS):
            score = score + r[:, hcols(h)]
        if masked:
            score = jnp.where(diag_ok, score, -jnp.inf)
        sc_ref[j] = score

    def idx_body(j, c):
        idx_block(j, False)
        return c

    lax.fori_loop(0, qi, idx_body, 0)
    idx_block(qi, True)

    def count(pred):
        def body(j, c):
            return c + _fold8(jnp.where(pred(sc_ref[j], j), 1.0, 0.0))
        part = lax.fori_loop(0, nblk, body, jnp.zeros((8, tq), F32))
        return jnp.sum(part, axis=0, keepdims=True)

    kf = float(n_top)

    def bis_body(it, carry):
        lo, cnt_lo = carry
        cand = lo + lax.shift_left(jnp.int32(1), 31 - it)
        cand_f = _key_to_float(cand)
        cnt = count(lambda s, j: s >= cand_f)
        take = cnt >= kf
        return jnp.where(take, cand, lo), jnp.where(take, cnt, cnt_lo)

    lo, cnt_ge = lax.fori_loop(
        0, 32, bis_body, (jnp.full((1, tq), INT_MIN, I32), jnp.full((1, tq), 1e9, F32)))
    few = lo == INT_MIN
    thr = _key_to_float(lo)
    has_tie = jnp.max(jnp.where((cnt_ge > kf) & (~few), 1.0, 0.0))

    @pl.when(has_tie > 0.0)
    def _():
        need = kf - count(lambda s, j: s > thr)
        n_bits = (sc_ref.shape[0] * tq).bit_length()

        def cut_body(it, cpos):
            cand = cpos + lax.shift_left(jnp.int32(1), n_bits - 1 - it)
            cnt = count(lambda s, j: (s == thr) & ((krow + j * tq) < cand))
            return jnp.where(cnt < need, cand, cpos)

        cut = lax.fori_loop(0, n_bits, cut_body, jnp.zeros((1, tq), I32))

        def drop_body(j, c):
            s = sc_ref[j]
            sc_ref[j] = jnp.where((s == thr) & ((krow + j * tq) > cut) & (~few), -jnp.inf, s)
            return c

        lax.fori_loop(0, nblk, drop_body, 0)

    mb_ref[...] = jnp.full(mb_ref.shape, NEG_BIG, F32)
    accb_ref[...] = jnp.zeros(accb_ref.shape, F32)

    def dsa_body(j, c):
        start = pl.multiple_of(j * tq, tq)
        sel = sc_ref[j] >= thr
        pat = jnp.minimum(qi - j, 2)
        s_t = _dot(kb_ref[pl.ds(start, tq), :], qbp_ref[...]) + bias_ref[pat]
        s_t = per_head(lambda h: jnp.where(sel, s_t[:, hcols(h)], NEG_BIG))
        flash_update(s_t, vdsaT_ref[j], mb_ref, accb_ref)
        return c

    lax.fori_loop(0, nblk, dsa_body, 0)
    inv_b = 1.0 / accb_ref[LANES:LANES + 1, :]
    for h in range(B_HEADS):
        g = h // B_GROUP
        obT_ref[h * B_HEAD_DIM:(h + 1) * B_HEAD_DIM, :] = (
            accb_ref[g * B_HEAD_DIM:(g + 1) * B_HEAD_DIM, hcols(h)] * inv_b[:, hcols(h)])


def _prompt_attention(pr, bias_p, b, t, n_top):
    tq = TQ
    assert t % tq == 0
    nq = t // tq
    r3 = lambda a: a.reshape(b, t, a.shape[-1])
    qrows = lambda r: pl.BlockSpec((None, r, tq), lambda bi, qi: (bi, 0, qi))
    kall = lambda c: pl.BlockSpec((None, t, c), lambda bi, qi: (bi, 0, 0))
    vall = pl.BlockSpec((None, nq, V_EXT, tq), lambda bi, qi: (bi, 0, 0, 0))
    olat, ob = pl.pallas_call(
        functools.partial(_prompt_attn_kernel, n_top=n_top),
        grid=(b, nq),
        in_specs=[pl.BlockSpec((None, A_HEADS, QCAT, tq), lambda bi, qi: (bi, 0, 0, qi)),
                  qrows(B_WIDTH), qrows(IDX_HEADS * IDX_DIM), qrows(IDX_HEADS),
                  kall(QCAT), kall(LANES), kall(LANES), vall, vall,
                  pl.BlockSpec(bias_p.shape, lambda bi, qi: (0, 0, 0))],
        out_specs=[qrows(A_HEADS * A_KV_LORA), qrows(B_WIDTH)],
        out_shape=[jax.ShapeDtypeStruct((b, A_HEADS * A_KV_LORA, t), BF16),
                   jax.ShapeDtypeStruct((b, B_WIDTH, t), F32)],
        scratch_shapes=[
            pltpu.VMEM((QCAT, A_HEADS * tq), BF16),
            pltpu.VMEM((LANES, B_HEADS * tq), BF16),
            pltpu.VMEM((LANES, IDX_HEADS * tq), BF16),
            pltpu.VMEM((nq, tq, tq), F32),
            pltpu.VMEM((8, A_HEADS * tq), F32),
            pltpu.VMEM((V_EXT, A_HEADS * tq), F32),
            pltpu.VMEM((8, B_HEADS * tq), F32),
            pltpu.VMEM((V_EXT, B_HEADS * tq), F32),
        ],
        compiler_params=pltpu.CompilerParams(dimension_semantics=("arbitrary", "arbitrary"),
                                             vmem_limit_bytes=VMEM_LIMIT_BYTES),
        name="prompt_attention",
    )(pr["qcatT"], pr["qbT"], pr["qidxT"], pr["widxT"],
      r3(pr["kcat"]), r3(pr["kb16"]), r3(pr["kidx16"]), pr["vmlaT"], pr["vdsaT"], bias_p)
    return olat, ob


def _sample_attn_kernel(qcat_ref, qb_ref, qidx_ref, widx_ref, kcatn_ref, kbn_ref, vbn_ref, kidxn_ref,
                        cckv_ref, ckpe_ref, ck_ref, cv_ref, ckidx_ref, bias_ref,
                        olat_ref, ob_ref, *, n_top, t_new, past):
    tq = t_new
    pad = LANES
    n_keys = past + pad
    colmask = lax.broadcasted_iota(I32, (tq, n_keys), 1) < past + t_new

    def padrows(a):
        return jnp.concatenate([a, jnp.zeros((pad - t_new, a.shape[1]), a.dtype)], axis=0)

    kpe_c = ckpe_ref[...].astype(BF16)
    kcat_c = jnp.concatenate(
        [cckv_ref[...].astype(BF16), kpe_c, jnp.zeros((past, QCAT - A_KV_LORA - A_ROPE), BF16)], axis=1)
    kcat_n = padrows(kcatn_ref[...])
    k_c = ck_ref[...].astype(BF16)
    v_c = cv_ref[...].astype(BF16)
    k_n = padrows(kbn_ref[...])
    v_n = padrows(vbn_ref[...])
    kidx_c = ckidx_ref[...].astype(BF16)
    kidx_n = padrows(kidxn_ref[...])

    def softmax_pv(s, v_c_, v_n_):
        m = jnp.max(s, axis=1, keepdims=True)
        p = jnp.exp2(s - m)
        l = jnp.sum(p, axis=1, keepdims=True)
        pb = p.astype(BF16)
        o = _dot(pb[:, :past], v_c_) + _dot(pb[:, past:], v_n_)
        return o / l

    qs = jnp.concatenate([qcat_ref[:, h * QCAT:(h + 1) * QCAT] for h in range(A_HEADS)], axis=0)
    s = jnp.concatenate([_dot_nt(qs, kcat_c), _dot_nt(qs, kcat_n)], axis=1)
    cm8 = lax.broadcasted_iota(I32, (A_HEADS * tq, n_keys), 1) < past + t_new
    s = jnp.where(cm8, s, NEG_BIG)
    o = softmax_pv(s, kcat_c[:, :A_KV_LORA], kcat_n[:, :A_KV_LORA])
    for h in range(A_HEADS):
        olat_ref[:, h * LANES:(h + 1) * LANES] = o[h * tq:(h + 1) * tq].astype(olat_ref.dtype)

    qis = jnp.concatenate([qidx_ref[:, h * IDX_DIM:(h + 1) * IDX_DIM] for h in range(IDX_HEADS)], axis=0)
    dots = jnp.concatenate([_dot_nt(qis, kidx_c), _dot_nt(qis, kidx_n)], axis=1)
    score = jnp.zeros((tq, n_keys), F32)
    for h in range(IDX_HEADS):
        score = score + jnp.maximum(dots[h * tq:(h + 1) * tq], 0.0) * widx_ref[:, h:h + 1]
    score = jnp.where(colmask, score, -jnp.inf)
    kf = float(n_top)

    def bis_body(it, carry):
        lo, cnt_lo = carry
        cand = lo + lax.shift_left(jnp.int32(1), 31 - it)
        cnt = _count(score >= _key_to_float(cand))
        take = cnt >= kf
        return jnp.where(take, cand, lo), jnp.where(take, cnt, cnt_lo)

    lo, cnt_ge = lax.fori_loop(
        0, 32, bis_body, (jnp.full((tq, 1), INT_MIN, I32), jnp.full((tq, 1), 1e9, F32)))
    few = lo == INT_MIN
    thr = _key_to_float(lo)
    cols = lax.broadcasted_iota(I32, (tq, n_keys), 1)
    has_tie = jnp.max(jnp.where((cnt_ge > kf) & (~few), 1.0, 0.0))
    n_bits = n_keys.bit_length()

    def tie_path():
        need = kf - _count(score > thr)
        eq = score == thr

        def cut_body(it, cpos):
            cand = cpos + lax.shift_left(jnp.int32(1), n_bits - 1 - it)
            cnt = _count(eq & (cols < cand))
            return jnp.where(cnt < need, cand, cpos)

        return lax.fori_loop(0, n_bits, cut_body, jnp.zeros((tq, 1), I32))

    cut = lax.cond(has_tie > 0.0, tie_path, lambda: jnp.full((tq, 1), 2 ** 30, I32))
    sel = (score > thr) | ((score == thr) & (cols <= cut))

    n_far = n_keys - bias_ref.shape[-1]
    for g in range(B_KV_HEADS):
        lanes = slice(g * B_HEAD_DIM, (g + 1) * B_HEAD_DIM)
        qg = jnp.concatenate(
            [qb_ref[:, h * B_HEAD_DIM:(h + 1) * B_HEAD_DIM] for h in range(g * B_GROUP, (g + 1) * B_GROUP)],
            axis=0)
        sg = jnp.concatenate([_dot_nt(qg, k_c[:, lanes]), _dot_nt(qg, k_n[:, lanes])], axis=1)
        for hh in range(B_GROUP):
            h = g * B_GROUP + hh
            near = bias_ref[0, h]
            far = bias_ref[1, h][:, :1]
            sh = sg[hh * tq:(hh + 1) * tq]
            sh = jnp.concatenate([sh[:, :n_far] + far, sh[:, n_far:] + near], axis=1)
            sh = jnp.where(sel, sh, NEG_BIG)
            oh = softmax_pv(sh, v_c[:, lanes], v_n[:, lanes])
            ob_ref[:, h * B_HEAD_DIM:(h + 1) * B_HEAD_DIM] = oh


def _sample_attention(pr, caches, bias_s, b, t_new, past, n_top):
    r3 = lambda a: a.reshape(b, t_new, a.shape[-1])
    per_b = lambda n, c: pl.BlockSpec((None, n, c), lambda bi: (bi, 0, 0))
    news = [pr["qcat"], pr["qb"], pr["qidx"], pr["widx"], pr["kcat"], pr["kb16"], pr["vb16"], pr["kidx16"]]
    olat, ob = pl.pallas_call(
        functools.partial(_sample_attn_kernel, n_top=n_top, t_new=t_new, past=past),
        grid=(b,),
        in_specs=[per_b(t_new, a.shape[-1]) for a in news]
                 + [per_b(past, c.shape[-1]) for c in caches]
                 + [pl.BlockSpec(bias_s.shape, lambda bi: (0, 0, 0, 0))],
        out_specs=[per_b(t_new, A_HEADS * A_KV_LORA), per_b(t_new, B_WIDTH)],
        out_shape=[jax.ShapeDtypeStruct((b, t_new, A_HEADS * A_KV_LORA), BF16),
                   jax.ShapeDtypeStruct((b, t_new, B_WIDTH), F32)],
        compiler_params=pltpu.CompilerParams(dimension_semantics=("arbitrary",),
                                             vmem_limit_bytes=VMEM_LIMIT_BYTES),
        name="sample_attention",
    )(*[r3(a) for a in news], *caches, bias_s)
    return olat.reshape(b * t_new, -1), ob.reshape(b * t_new, -1)


def _combine_kernel(x_ref, olat_ref, ob_ref, sga_ref, sgb_ref, wuv_ref, wout_ref, fg_ref, y_ref, *, final):
    o_a = _dot(olat_ref[...], wuv_ref[...])
    mix = jnp.concatenate([o_a * sga_ref[...], ob_ref[...] * sgb_ref[...]], axis=1)
    y = x_ref[...] + _dot(mix.astype(BF16), wout_ref[...])
    if final:
        y = _rms(y, fg_ref[...])
    y_ref[...] = y


def _combine(x2d, olat, ob, pr, lw, fg, final):
    n, d = x2d.shape
    tm = PROJ_TM
    tok = lambda c: pl.BlockSpec((tm, c), lambda i: (i, 0))
    full = lambda a: pl.BlockSpec(a.shape, lambda i: (0,) * a.ndim)
    return pl.pallas_call(
        functools.partial(_combine_kernel, final=final),
        grid=(n // tm,),
        in_specs=[tok(d), tok(olat.shape[1]), tok(ob.shape[1]), tok(A_WIDTH), tok(B_WIDTH),
                  full(lw["wuv"]), full(lw["wout"]), full(fg)],
        out_specs=tok(d),
        out_shape=jax.ShapeDtypeStruct((n, d), F32),
        compiler_params=pltpu.CompilerParams(dimension_semantics=("arbitrary",),
                                             vmem_limit_bytes=VMEM_LIMIT_BYTES),
        name="combine",
    )(x2d, olat, ob, pr["sga"], pr["sgb"], lw["wuv"], lw["wout"], fg)


def _combine_t_kernel(x_ref, olatT_ref, obT_ref, sgaT_ref, sgbT_ref, wuvt_ref, wout_ref, fg_ref, y_ref, *, final):
    o_a = _dot(wuvt_ref[...], olatT_ref[...])
    mix_t = jnp.concatenate([o_a * sgaT_ref[...], obT_ref[...] * sgbT_ref[...]], axis=0).astype(BF16)
    y = x_ref[...] + lax.dot_general(mix_t, wout_ref[...], (((0,), (0,)), ((), ())),
                                     preferred_element_type=F32)
    if final:
        y = _rms(y, fg_ref[...])
    y_ref[...] = y


def _combine_t(x3d, olat_t, ob_t, pr, lw, fg, final):
    b, t, d = x3d.shape
    tm = TQ
    rows_t = lambda r: pl.BlockSpec((None, r, tm), lambda bi, ti: (bi, 0, ti))
    full = lambda a: pl.BlockSpec(a.shape, lambda bi, ti: (0,) * a.ndim)
    xblk = pl.BlockSpec((None, tm, d), lambda bi, ti: (bi, ti, 0))
    return pl.pallas_call(
        functools.partial(_combine_t_kernel, final=final),
        grid=(b, t // tm),
        in_specs=[xblk, rows_t(A_HEADS * A_KV_LORA), rows_t(B_WIDTH), rows_t(A_WIDTH), rows_t(B_WIDTH),
                  full(lw["wuvt"]), full(lw["wout"]), full(fg)],
        out_specs=xblk,
        out_shape=jax.ShapeDtypeStruct((b, t, d), F32),
        compiler_params=pltpu.CompilerParams(dimension_semantics=("arbitrary", "arbitrary"),
                                             vmem_limit_bytes=VMEM_LIMIT_BYTES),
        name="combine_prompt",
    )(x3d, olat_t, ob_t, pr["sgaT"], pr["sgbT"], lw["wuvt"], lw["wout"], fg)


def _layer_weights(norm_g, w_in, q_norm_g, kv_norm_g, w_uq, w_uk, w_uv, w_out):
    d = w_in.shape[0]
    o = np.cumsum([0, A_Q_LORA, A_KV_LORA, A_ROPE, A_WIDTH, B_WIDTH, B_KV_HEADS * B_HEAD_DIM,
                   B_KV_HEADS * B_HEAD_DIM, IDX_HEADS * IDX_DIM, IDX_DIM, IDX_HEADS, B_WIDTH])
    seg = lambda i: w_in[:, int(o[i]):int(o[i + 1])]
    misc = jnp.concatenate([seg(2), seg(8), seg(9),
                            jnp.zeros((d, LANES - A_ROPE - IDX_DIM - IDX_HEADS), w_in.dtype)], axis=1)
    win = jnp.concatenate([seg(0), seg(1), misc, seg(3), seg(4), seg(5), seg(6), seg(7), seg(10)], axis=1)
    assert win.shape[1] == IN_PAD
    half = A_ROPE // 2
    wq = jnp.concatenate([
        w_uq[:, :, :A_NOPE].reshape(A_Q_LORA, A_HEADS * A_NOPE),
        w_uq[:, :, A_NOPE:A_NOPE + half].reshape(A_Q_LORA, A_HEADS * half),
        w_uq[:, :, A_NOPE + half:].reshape(A_Q_LORA, A_HEADS * half)], axis=1)
    eye = jnp.eye(A_HEADS, dtype=w_uk.dtype)
    wuk = jnp.einsum('chn,hg->hngc', w_uk, eye).reshape(A_HEADS * A_NOPE, A_HEADS * A_KV_LORA)
    wuv = jnp.einsum('chv,hg->hcgv', w_uv, eye).reshape(A_HEADS * A_KV_LORA, A_HEADS * A_V)
    pm = np.zeros((2 * LANES, A_HEADS * LANES), np.float32)
    for h in range(A_HEADS):
        for i in range(half):
            pm[h * half + i, h * LANES + i] = 1.0
            pm[LANES + h * half + i, h * LANES + half + i] = 1.0
    zpad = lambda c: jnp.zeros((d, c), w_in.dtype)
    wk = jnp.concatenate([seg(1), seg(8), seg(2), zpad(LANES - IDX_DIM - A_ROPE), seg(5), seg(6)], axis=1)
    wt = jnp.concatenate([seg(0), seg(1), seg(6), seg(4), seg(7), seg(3), seg(10), seg(9),
                          zpad(ROWS_T - R_WI - IDX_HEADS)], axis=1).T
    assert wk.shape[1] == COLS_K and wt.shape[0] == ROWS_T
    bc = lambda g: jnp.broadcast_to(g.reshape(-1, 1), (g.shape[0], TQ))
    return {
        "wk": wk.astype(BF16), "wt": wt.astype(BF16), "qngc": bc(q_norm_g), "kvngc": bc(kv_norm_g),
        "wqt": wq.T.astype(BF16), "wukt": wuk.T.astype(BF16), "wuvt": wuv.T.astype(BF16),
        "ng": norm_g.reshape(1, -1), "win": win.astype(BF16),
        "qng": q_norm_g.reshape(1, -1), "kvng": kv_norm_g.reshape(1, -1),
        "wq": wq.astype(BF16), "wuk": wuk.astype(BF16), "wuv": wuv.astype(BF16),
        "pmat": jnp.asarray(pm, BF16), "wout": w_out.astype(BF16),
    }


def _rope_tables(pos):
    half = A_ROPE // 2
    inv = ROPE_THETA ** (-jnp.arange(half, dtype=F32) / half)
    ang = pos.astype(F32)[:, None] * inv[None, :]
    cos, sin = jnp.cos(ang), jnp.sin(ang)
    z = jnp.zeros((pos.shape[0], LANES - A_ROPE), F32)
    zh = jnp.zeros_like(cos)
    cosq = jnp.tile(cos, (1, A_HEADS))
    sinq = jnp.tile(sin, (1, A_HEADS))
    rc = jnp.concatenate([cos, cos, z], axis=1)
    rs1 = jnp.concatenate([zh, sin, z], axis=1)
    rs2 = jnp.concatenate([-sin, zh, z], axis=1)
    return cosq, sinq, rc, rs1, rs2


def _rope_tables_t(pos):
    half = A_ROPE // 2
    inv = ROPE_THETA ** (-jnp.arange(half, dtype=F32) / half)
    ang = pos.astype(F32)[:, None] * inv[None, :]
    cos, sin = jnp.cos(ang), jnp.sin(ang)
    cost = jnp.tile(cos.T, (A_HEADS, 1))
    sint = jnp.tile(sin.T, (A_HEADS, 1))
    z0 = jnp.zeros((pos.shape[0], MK_KPE), F32)
    z1 = jnp.zeros((pos.shape[0], LANES - MK_KPE - A_ROPE), F32)
    zh = jnp.zeros_like(cos)
    rc = jnp.concatenate([z0, cos, cos, z1], axis=1)
    rs1 = jnp.concatenate([z0, zh, sin, z1], axis=1)
    rs2 = jnp.concatenate([z0, -sin, zh, z1], axis=1)
    return cost, sint, rc, rs1, rs2


def kernel(x_prompt, x_sample, cache_mla_ckv, cache_mla_kpe, cache_dsa_k, cache_dsa_v, cache_dsa_kidx,
           norm_g, w_in, mla_q_norm_g, mla_kv_norm_g, mla_w_uq, mla_w_uk, mla_w_uv, rel_bias, w_out,
           final_norm_g):
    bp, tp, d = x_prompt.shape
    bs, ts, _ = x_sample.shape
    depth = w_in.shape[0]
    past = cache_mla_ckv.shape[2]
    n_top_p = min(TOP_K_MAX, tp // 4)
    n_top_s = min(TOP_K_MAX, (past + ts) // 4)
    assert ts <= CHUNK and past % CHUNK == 0 and past % LANES == 0

    rope_p = _rope_tables_t(jnp.arange(tp, dtype=jnp.int32))
    reps = PROJ_TM // ts
    rope_s = tuple(jnp.tile(a, (reps, 1)) for a in _rope_tables(past + jnp.arange(ts, dtype=jnp.int32)))

    bias_p = _bias_tables(rel_bias, (0, -TQ, -3 * TQ), TQ, TQ, True)
    win_s = 2 * LANES
    bias_s = _bias_tables(rel_bias, (-(win_s - LANES), -(past + win_s)), ts, win_s, False)
    fg = final_norm_g.reshape(1, -1)

    xp = x_prompt
    xs = x_sample.reshape(bs * ts, d)
    outs_p, outs_s = [], []
    for l in range(depth):
        lw = _layer_weights(norm_g[l], w_in[l], mla_q_norm_g[l], mla_kv_norm_g[l],
                            mla_w_uq[l], mla_w_uk[l], mla_w_uv[l], w_out[l])
        final = l == depth - 1
        pr = _project_t(xp.reshape(bp * tp, d), rope_p, lw, bp, tp)
        olat, ob = _prompt_attention(pr, bias_p, bp, tp, n_top_p)
        xp = _combine_t(xp, olat, ob, pr, lw, fg, final)
        outs_p.append((pr["ckv"].reshape(bp, tp, A_KV_LORA), pr["kpe"].reshape(bp, tp, A_ROPE),
                       pr["kb"].reshape(bp, tp, B_KV_HEADS, B_HEAD_DIM),
                       pr["vb"].reshape(bp, tp, B_KV_HEADS, B_HEAD_DIM),
                       pr["kidx"].reshape(bp, tp, IDX_DIM)))
        ps = _project(xs, rope_s, lw, period=PROJ_TM)
        caches = (cache_mla_ckv[l], cache_mla_kpe[l],
                  cache_dsa_k[l].reshape(bs, past, B_KV_HEADS * B_HEAD_DIM),
                  cache_dsa_v[l].reshape(bs, past, B_KV_HEADS * B_HEAD_DIM), cache_dsa_kidx[l])
        olat, ob = _sample_attention(ps, caches, bias_s, bs, ts, past, n_top_s)
        xs = _combine(xs, olat, ob, ps, lw, fg, final)
        outs_s.append((ps["ckv"].reshape(bs, ts, A_KV_LORA), ps["kpe"].reshape(bs, ts, A_ROPE),
                       ps["kb"].reshape(bs, ts, B_KV_HEADS, B_HEAD_DIM),
                       ps["vb"].reshape(bs, ts, B_KV_HEADS, B_HEAD_DIM),
                       ps["kidx"].reshape(bs, ts, IDX_DIM)))

    stack = lambda outs, i: jnp.stack([o[i] for o in outs])
    return ((xp, xs.reshape(bs, ts, d))
            + tuple(stack(outs_p, i) for i in range(5)) + tuple(stack(outs_s, i) for i in range(5)))
```

```python
import functools
import math

import jax
import jax.numpy as jnp
import numpy as np
from jax import lax
from jax.experimental import pallas as pl
from jax.experimental.pallas import tpu as pltpu

F32 = jnp.float32
BF16 = jnp.bfloat16
I32 = jnp.int32

CHUNK = 64
EPS = 1e-6
A_HEADS = 8
A_NOPE = 64
A_ROPE = 32
A_V = 64
A_Q_LORA = 256
A_KV_LORA = 128
ROPE_THETA = 10000.0
A_WIDTH = A_HEADS * A_V
B_HEADS = 8
B_KV_HEADS = 2
B_HEAD_DIM = 64
B_WIDTH = B_HEADS * B_HEAD_DIM
B_GROUP = B_HEADS // B_KV_HEADS
IDX_HEADS = 8
IDX_DIM = 64
TOP_K_MAX = 256
N_BUCKETS = 32
MAX_DISTANCE = 128

LANES = 128
VMEM_LIMIT_BYTES = 56 * 1024 * 1024

LOG2E = 1.4426950408889634
NEG_BIG = -1e30
INT_MIN = -(2 ** 31)

C_CQ = 0
C_CKV = C_CQ + A_Q_LORA
C_MISC = C_CKV + A_KV_LORA
C_GA = C_MISC + LANES
C_QB = C_GA + A_WIDTH
C_KB = C_QB + B_WIDTH
C_VB = C_KB + B_KV_HEADS * B_HEAD_DIM
C_QI = C_VB + B_KV_HEADS * B_HEAD_DIM
C_GB = C_QI + IDX_HEADS * IDX_DIM
IN_PAD = C_GB + B_WIDTH
M_KPE = 0
M_KIDX = A_ROPE
M_WIDX = A_ROPE + IDX_DIM

QCAT = 2 * LANES
TQ = 256
PROJ_TM = 256


def _dot(a, b):
    return jnp.dot(a, b, preferred_element_type=F32)


def _dot_nt(a, b):
    return lax.dot_general(a, b, (((1,), (1,)), ((), ())), preferred_element_type=F32)


def _rms(x, g):
    return x * lax.rsqrt(jnp.mean(x * x, axis=-1, keepdims=True) + EPS) * g


def _bias_kernel(rb_ref, out_ref, *, offsets, keys_on_rows):
    nb = N_BUCKETS // 2
    max_exact = nb // 2
    n_r = out_ref.shape[1] if keys_on_rows else out_ref.shape[2]
    n_c = out_ref.shape[2] // B_HEADS if keys_on_rows else out_ref.shape[3]
    row = lax.broadcasted_iota(I32, (n_r, n_c), 0)
    col = lax.broadcasted_iota(I32, (n_r, n_c), 1)
    for p, off in enumerate(offsets):
        rel = off + (row - col if keys_on_rows else col - row)
        ret = jnp.where(rel > 0, nb, 0)
        n = jnp.abs(rel)
        nf = jnp.maximum(n, 1).astype(F32)
        large = max_exact + (jnp.log(nf / max_exact) / math.log(MAX_DISTANCE / max_exact)
                             * (nb - max_exact)).astype(I32)
        large = jnp.minimum(large, nb - 1)
        bucket = ret + jnp.where(n < max_exact, n, large)
        for h in range(B_HEADS):
            table = jnp.broadcast_to(rb_ref[h:h + 1, :], (n_r, LANES))
            acc = jnp.concatenate(
                [jnp.take_along_axis(table, bucket[:, c:c + LANES], axis=1) for c in range(0, n_c, LANES)],
                axis=1)
            if keys_on_rows:
                out_ref[p, :, h * n_c:(h + 1) * n_c] = acc * LOG2E
            else:
                out_ref[p, h] = acc * LOG2E


def _bias_tables(rel_bias, offsets, n_r, n_c, keys_on_rows):
    shape = (len(offsets), n_r, B_HEADS * n_c) if keys_on_rows else (len(offsets), B_HEADS, n_r, n_c)
    return pl.pallas_call(
        functools.partial(_bias_kernel, offsets=tuple(offsets), keys_on_rows=keys_on_rows),
        out_shape=jax.ShapeDtypeStruct(shape, F32),
        in_specs=[pl.BlockSpec(memory_space=pltpu.VMEM)],
        out_specs=pl.BlockSpec(memory_space=pltpu.VMEM),
        compiler_params=pltpu.CompilerParams(vmem_limit_bytes=VMEM_LIMIT_BYTES),
        name="bias_tables",
    )(jnp.pad(rel_bias.T, ((0, 0), (0, LANES - N_BUCKETS))))


def _proj_kernel(x_ref, ng_ref, win_ref, qng_ref, kvng_ref, wq_ref, wuk_ref, pmat_ref,
                 cosq_ref, sinq_ref, rc_ref, rs1_ref, rs2_ref,
                 ckv_ref, kpe_ref, kb_ref, vb_ref, kidx_ref,
                 kcat_ref, kb16_ref, vb16_ref, kidx16_ref,
                 qcat_ref, qb_ref, qidx_ref, widx_ref, sga_ref, sgb_ref):
    x = x_ref[...]
    h = _rms(x, ng_ref[...])
    z = _dot(h.astype(BF16), win_ref[...])

    cq = _rms(z[:, C_CQ:C_CQ + A_Q_LORA], qng_ref[...])
    q = _dot(cq.astype(BF16), wq_ref[...])
    n_nope = A_HEADS * A_NOPE
    x1 = q[:, n_nope:n_nope + LANES]
    x2 = q[:, n_nope + LANES:n_nope + 2 * LANES]
    cos8, sin8 = cosq_ref[...], sinq_ref[...]
    o1 = x1 * cos8 - x2 * sin8
    o2 = x1 * sin8 + x2 * cos8
    mla_scale = (A_NOPE + A_ROPE) ** -0.5 * LOG2E
    q_lat = _dot(q[:, :n_nope].astype(BF16), wuk_ref[...]) * mla_scale
    pe = jnp.concatenate([o1, o2], axis=1) * mla_scale
    q_pe = _dot(pe.astype(BF16), pmat_ref[...])
    for hh in range(A_HEADS):
        qcat_ref[:, hh * QCAT:hh * QCAT + LANES] = q_lat[:, hh * LANES:(hh + 1) * LANES].astype(BF16)
        qcat_ref[:, hh * QCAT + LANES:(hh + 1) * QCAT] = q_pe[:, hh * LANES:(hh + 1) * LANES].astype(BF16)

    ckv = _rms(z[:, C_CKV:C_CKV + A_KV_LORA], kvng_ref[...])
    ckv_ref[...] = ckv
    misc = z[:, C_MISC:C_MISC + LANES]
    rot = (misc * rc_ref[...] + pltpu.roll(misc, A_ROPE // 2, 1) * rs1_ref[...]
           + pltpu.roll(misc, LANES - A_ROPE // 2, 1) * rs2_ref[...])
    kpe_ref[...] = rot[:, :A_ROPE]
    kcat_ref[:, :LANES] = ckv.astype(BF16)
    kcat_ref[:, LANES:] = rot.astype(BF16)

    kidx = misc[:, M_KIDX:M_KIDX + IDX_DIM]
    kidx_ref[...] = kidx
    kidx16_ref[...] = kidx.astype(BF16)
    widx_ref[...] = misc[:, M_WIDX:M_WIDX + IDX_HEADS] * (IDX_HEADS ** -0.5)
    kb = z[:, C_KB:C_KB + LANES]
    vb = z[:, C_VB:C_VB + LANES]
    kb_ref[...] = kb
    vb_ref[...] = vb
    kb16_ref[...] = kb.astype(BF16)
    vb16_ref[...] = vb.astype(BF16)
    qb_ref[...] = (z[:, C_QB:C_QB + B_WIDTH] * (B_HEAD_DIM ** -0.5 * LOG2E)).astype(BF16)
    qidx_ref[...] = (z[:, C_QI:C_QI + IDX_HEADS * IDX_DIM] * (IDX_DIM ** -0.5)).astype(BF16)
    sga_ref[...] = jax.nn.silu(z[:, C_GA:C_GA + A_WIDTH])
    sgb_ref[...] = jax.nn.silu(z[:, C_GB:C_GB + B_WIDTH])


def _project(x2d, rope_tabs, lw, *, period):
    n, d = x2d.shape
    tm = PROJ_TM
    assert n % tm == 0 and period % tm == 0
    n_rep = period // tm
    tok = lambda c: pl.BlockSpec((tm, c), lambda i: (i, 0))
    full = lambda a: pl.BlockSpec(a.shape, lambda i: (0,) * a.ndim)
    tab = pl.BlockSpec((tm, LANES), lambda i: (i % n_rep, 0))
    outs = [
        ("ckv", A_KV_LORA, F32), ("kpe", A_ROPE, F32), ("kb", LANES, F32), ("vb", LANES, F32),
        ("kidx", IDX_DIM, F32),
        ("kcat", QCAT, BF16), ("kb16", LANES, BF16), ("vb16", LANES, BF16), ("kidx16", IDX_DIM, BF16),
        ("qcat", A_HEADS * QCAT, BF16), ("qb", B_WIDTH, BF16), ("qidx", IDX_HEADS * IDX_DIM, BF16),
        ("widx", IDX_HEADS, F32), ("sga", A_WIDTH, F32), ("sgb", B_WIDTH, F32),
    ]
    res = pl.pallas_call(
        _proj_kernel,
        grid=(n // tm,),
        in_specs=[tok(d), full(lw["ng"]), full(lw["win"]), full(lw["qng"]), full(lw["kvng"]),
                  full(lw["wq"]), full(lw["wuk"]), full(lw["pmat"]), tab, tab, tab, tab, tab],
        out_specs=[tok(c) for _, c, _ in outs],
        out_shape=[jax.ShapeDtypeStruct((n, c), dt) for _, c, dt in outs],
        compiler_params=pltpu.CompilerParams(dimension_semantics=("arbitrary",),
                                             vmem_limit_bytes=VMEM_LIMIT_BYTES),
        name="project",
    )(x2d, lw["ng"], lw["win"], lw["qng"], lw["kvng"], lw["wq"], lw["wuk"], lw["pmat"], *rope_tabs)
    return {name: r for (name, _, _), r in zip(outs, res)}


R_CQ = 0
R_CKV = R_CQ + A_Q_LORA
R_VB = R_CKV + A_KV_LORA
R_QB = R_VB + LANES
R_QI = R_QB + B_WIDTH
R_GA = R_QI + IDX_HEADS * IDX_DIM
R_GB = R_GA + A_WIDTH
R_WI = R_GB + B_WIDTH
R_KB = R_WI + 16
R_KI = R_KB + LANES
R_KPE = R_KI + IDX_DIM
ROWS_T = R_KPE + A_ROPE
K_MISC = A_KV_LORA
K_KB = K_MISC + LANES
COLS_K = K_KB + LANES
MK_KPE = IDX_DIM
ONES_ROWS = 16
V_EXT = A_KV_LORA + ONES_ROWS


def _proj_t_kernel(x_ref, ng_ref, wk_ref, wt_ref, qng_ref, kvngc_ref, kvng_ref, wqt_ref, wukt_ref,
                   cost_ref, sint_ref, rc_ref, rs1_ref, rs2_ref,
                   ckv_ref, kcat_ref, kb16_ref, kidx16_ref,
                   kpeT_ref, kbT_ref, vbT_ref, kidxT_ref,
                   qcatT_ref, qbT_ref, qidxT_ref, widxT_ref, vmlaT_ref, vdsaT_ref, sgaT_ref, sgbT_ref):
    x = x_ref[...]
    tm = x.shape[0]
    hb = _rms(x, ng_ref[...]).astype(BF16)

    zk = _dot(hb, wk_ref[...])
    ckv = _rms(zk[:, :A_KV_LORA], kvng_ref[...])
    ckv_ref[...] = ckv
    misc = zk[:, K_MISC:K_MISC + LANES]
    lane = lax.broadcasted_iota(I32, (tm, LANES), 1)
    kidx16_ref[...] = jnp.where(lane < IDX_DIM, misc, 0.0).astype(BF16)
    rot = (misc * rc_ref[...] + pltpu.roll(misc, A_ROPE // 2, 1) * rs1_ref[...]
           + pltpu.roll(misc, LANES - A_ROPE // 2, 1) * rs2_ref[...])
    kcat_ref[:, :LANES] = ckv.astype(BF16)
    kcat_ref[:, LANES:] = pltpu.roll(rot, LANES - MK_KPE, 1).astype(BF16)
    kb16_ref[...] = zk[:, K_KB:K_KB + LANES].astype(BF16)

    zt = _dot_nt(wt_ref[...], hb)

    def rms_t(c, g):
        return c * lax.rsqrt(jnp.mean(c * c, axis=0, keepdims=True) + EPS) * g

    cq = rms_t(zt[R_CQ:R_CQ + A_Q_LORA], qng_ref[...])
    qt = _dot(wqt_ref[...], cq.astype(BF16))
    n_nope = A_HEADS * A_NOPE
    x1 = qt[n_nope:n_nope + LANES]
    x2 = qt[n_nope + LANES:n_nope + 2 * LANES]
    cos8, sin8 = cost_ref[...], sint_ref[...]
    mla_scale = (A_NOPE + A_ROPE) ** -0.5 * LOG2E
    o1 = (x1 * cos8 - x2 * sin8) * mla_scale
    o2 = (x1 * sin8 + x2 * cos8) * mla_scale
    q_lat = _dot(wukt_ref[...], qt[:n_nope].astype(BF16)) * mla_scale
    half = A_ROPE // 2
    for h in range(A_HEADS):
        qcatT_ref[h, :LANES, :] = q_lat[h * LANES:(h + 1) * LANES].astype(BF16)
        qcatT_ref[h, LANES:LANES + half, :] = o1[h * half:(h + 1) * half].astype(BF16)
        qcatT_ref[h, LANES + half:LANES + A_ROPE, :] = o2[h * half:(h + 1) * half].astype(BF16)
        qcatT_ref[h, LANES + A_ROPE:, :] = jnp.zeros((QCAT - LANES - A_ROPE, tm), BF16)
    ones = jnp.ones((ONES_ROWS, tm), BF16)
    vmlaT_ref[:A_KV_LORA, :] = rms_t(zt[R_CKV:R_CKV + A_KV_LORA], kvngc_ref[...]).astype(BF16)
    vmlaT_ref[A_KV_LORA:, :] = ones
    vb_t = zt[R_VB:R_VB + LANES]
    vdsaT_ref[:LANES, :] = vb_t.astype(BF16)
    vdsaT_ref[LANES:, :] = ones
    vbT_ref[...] = vb_t
    kbT_ref[...] = zt[R_KB:R_KB + LANES]
    kidxT_ref[...] = zt[R_KI:R_KI + IDX_DIM]
    k1, k2 = zt[R_KPE:R_KPE + half], zt[R_KPE + half:R_KPE + A_ROPE]
    cos1, sin1 = cos8[:half], sin8[:half]
    kpeT_ref[:half, :] = k1 * cos1 - k2 * sin1
    kpeT_ref[half:, :] = k1 * sin1 + k2 * cos1
    qbT_ref[...] = (zt[R_QB:R_QB + B_WIDTH] * (B_HEAD_DIM ** -0.5 * LOG2E)).astype(BF16)
    qidxT_ref[...] = (zt[R_QI:R_QI + IDX_HEADS * IDX_DIM] * (IDX_DIM ** -0.5)).astype(BF16)
    sgaT_ref[...] = jax.nn.silu(zt[R_GA:R_GA + A_WIDTH])
    sgbT_ref[...] = jax.nn.silu(zt[R_GB:R_GB + B_WIDTH])
    widxT_ref[...] = zt[R_WI:R_WI + IDX_HEADS] * (IDX_HEADS ** -0.5)


def _project_t(x2d, tabs, lw, b, t):
    n, d = x2d.shape
    tm = TQ
    nt = t // tm
    tok = lambda c: pl.BlockSpec((tm, c), lambda i: (i, 0))
    full = lambda a: pl.BlockSpec(a.shape, lambda i: (0,) * a.ndim)
    tab_t = pl.BlockSpec((LANES, tm), lambda i: (0, i % nt))
    tab_k = pl.BlockSpec((tm, LANES), lambda i: (i % nt, 0))
    rows_t = lambda r: pl.BlockSpec((None, r, tm), lambda i: (i // nt, 0, i % nt))
    outs = [
        ("ckv", (n, A_KV_LORA), F32, tok(A_KV_LORA)),
        ("kcat", (n, QCAT), BF16, tok(QCAT)), ("kb16", (n, LANES), BF16, tok(LANES)),
        ("kidx16", (n, LANES), BF16, tok(LANES)),
        ("kpeT", (b, A_ROPE, t), F32, rows_t(A_ROPE)), ("kbT", (b, LANES, t), F32, rows_t(LANES)),
        ("vbT", (b, LANES, t), F32, rows_t(LANES)), ("kidxT", (b, IDX_DIM, t), F32, rows_t(IDX_DIM)),
        ("qcatT", (b, A_HEADS, QCAT, t), BF16,
         pl.BlockSpec((None, A_HEADS, QCAT, tm), lambda i: (i // nt, 0, 0, i % nt))),
        ("qbT", (b, B_WIDTH, t), BF16, rows_t(B_WIDTH)),
        ("qidxT", (b, IDX_HEADS * IDX_DIM, t), BF16, rows_t(IDX_HEADS * IDX_DIM)),
        ("widxT", (b, IDX_HEADS, t), F32, rows_t(IDX_HEADS)),
        ("vmlaT", (b, nt, V_EXT, tm), BF16, pl.BlockSpec((None, None, V_EXT, tm), lambda i: (i // nt, i % nt, 0, 0))),
        ("vdsaT", (b, nt, V_EXT, tm), BF16, pl.BlockSpec((None, None, V_EXT, tm), lambda i: (i // nt, i % nt, 0, 0))),
        ("sgaT", (b, A_WIDTH, t), F32, rows_t(A_WIDTH)), ("sgbT", (b, B_WIDTH, t), F32, rows_t(B_WIDTH)),
    ]
    cost, sint, rc, rs1, rs2 = tabs
    res = pl.pallas_call(
        _proj_t_kernel,
        grid=(n // tm,),
        in_specs=[tok(d), full(lw["ng"]), full(lw["wk"]), full(lw["wt"]), full(lw["qngc"]), full(lw["kvngc"]),
                  full(lw["kvng"]), full(lw["wqt"]), full(lw["wukt"]), tab_t, tab_t, tab_k, tab_k, tab_k],
        out_specs=[o[3] for o in outs],
        out_shape=[jax.ShapeDtypeStruct(o[1], o[2]) for o in outs],
        compiler_params=pltpu.CompilerParams(dimension_semantics=("arbitrary",),
                                             vmem_limit_bytes=VMEM_LIMIT_BYTES),
        name="project_prompt",
    )(x2d, lw["ng"], lw["wk"], lw["wt"], lw["qngc"], lw["kvngc"], lw["kvng"], lw["wqt"], lw["wukt"],
      cost, sint, rc, rs1, rs2)
    return {o[0]: r for o, r in zip(outs, res)}


NEG_FLT_MAX = -3.4028234663852886e38
KEY_NEG_FLT_MAX = INT_MIN + (1 << 23)


def _key_to_float(k):
    k = jnp.maximum(k, KEY_NEG_FLT_MAX)
    return pltpu.bitcast(k ^ ((k >> 31) & 0x7FFFFFFF), F32)


def _count(pred):
    return jnp.sum(jnp.where(pred, 1.0, 0.0), axis=1, keepdims=True)


def _fold8(x):
    parts = [x[i:i + 8] for i in range(0, x.shape[0], 8)]
    while len(parts) > 1:
        parts = [a + b for a, b in zip(parts[::2], parts[1::2])]
    return parts[0]


def _prompt_attn_kernel(qcatT_ref, qbT_ref, qidxT_ref, widxT_ref, kcat_ref, kb_ref, kidx_ref,
                        vmlaT_ref, vdsaT_ref, bias_ref, olatT_ref, obT_ref,
                        qa_ref, qbp_ref, qip_ref, sc_ref, m_ref, acc_ref, mb_ref, accb_ref, *, n_top):
    qi = pl.program_id(1)
    nblk = qi + 1
    tq = TQ
    krow = lax.broadcasted_iota(I32, (tq, tq), 0)
    qcol = lax.broadcasted_iota(I32, (tq, tq), 1)
    shift = CHUNK.bit_length() - 1
    diag_ok = (qcol >> shift) >= (krow >> shift)
    hcols = lambda h: slice(h * tq, (h + 1) * tq)

    def per_head(fn):
        return jnp.concatenate([fn(h) for h in range(A_HEADS)], axis=1)

    qbp_ref[...] = jnp.zeros(qbp_ref.shape, BF16)
    qip_ref[...] = jnp.zeros(qip_ref.shape, BF16)
    for h in range(A_HEADS):
        g = h // B_GROUP
        qa_ref[:, hcols(h)] = qcatT_ref[h]
        qbp_ref[g * B_HEAD_DIM:(g + 1) * B_HEAD_DIM, hcols(h)] = qbT_ref[h * B_HEAD_DIM:(h + 1) * B_HEAD_DIM, :]
        qip_ref[:IDX_DIM, hcols(h)] = qidxT_ref[h * IDX_DIM:(h + 1) * IDX_DIM, :]
    w_all = per_head(lambda h: widxT_ref[h:h + 1, :])

    def flash_update(s_t, v_t, m_r, acc_r):
        m_prev = m_r[0:1, :]
        m_new = jnp.maximum(m_prev, jnp.max(s_t, axis=0, keepdims=True))
        alpha = jnp.exp2(m_prev - m_new)
        p_t = jnp.exp2(s_t - m_new).astype(BF16)
        acc_r[...] = alpha * acc_r[...] + _dot(v_t, p_t)
        m_r[0:1, :] = m_new

    m_ref[...] = jnp.full(m_ref.shape, NEG_BIG, F32)
    acc_ref[...] = jnp.zeros(acc_ref.shape, F32)

    def mla_block(j, masked):
        start = pl.multiple_of(j * tq, tq)
        s_t = _dot(kcat_ref[pl.ds(start, tq), :], qa_ref[...])
        if masked:
            s_t = per_head(lambda h: jnp.where(diag_ok, s_t[:, hcols(h)], NEG_BIG))
        flash_update(s_t, vmlaT_ref[j], m_ref, acc_ref)
        r = jnp.maximum(_dot(kidx_ref[pl.ds(start, tq), :], qip_ref[...]), 0.0) * w_all
        score = r[:, hcols(0)]
        for h in range(1, IDX_HEADS):
            score = score + r[:, hcols(h)]
        if masked:
            score = jnp.where(diag_ok, score, -jnp.inf)
        sc_ref[j] = score

    def mla_body(j, c):
        mla_block(j, False)
        return c

    lax.fori_loop(0, qi, mla_body, 0)
    mla_block(qi, True)
    o_t = acc_ref[:A_KV_LORA, :] * (1.0 / acc_ref[A_KV_LORA:A_KV_LORA + 1, :])
    for h in range(A_HEADS):
        olatT_ref[h * LANES:(h + 1) * LANES, :] = o_t[:, hcols(h)].astype(olatT_ref.dtype)

    def count(pred):
        def body(j, c):
            return c + _fold8(jnp.where(pred(sc_ref[j], j), 1.0, 0.0))
        part = lax.fori_loop(0, nblk, body, jnp.zeros((8, tq), F32))
        return jnp.sum(part, axis=0, keepdims=True)

    kf = float(n_top)

    def bis_body(it, carry):
        lo, cnt_lo = carry
        cand = lo + lax.shift_left(jnp.int32(1), 31 - it)
        cand_f = _key_to_float(cand)
        cnt = count(lambda s, j: s >= cand_f)
        take = cnt >= kf
        return jnp.where(take, cand, lo), jnp.where(take, cnt, cnt_lo)

    lo, cnt_ge = lax.fori_loop(
        0, 32, bis_body, (jnp.full((1, tq), INT_MIN, I32), jnp.full((1, tq), 1e9, F32)))
    few = lo == INT_MIN
    thr = _key_to_float(lo)
    has_tie = jnp.max(jnp.where((cnt_ge > kf) & (~few), 1.0, 0.0))

    @pl.when(has_tie > 0.0)
    def _():
        need = kf - count(lambda s, j: s > thr)
        n_bits = (sc_ref.shape[0] * tq).bit_length()

        def cut_body(it, cpos):
            cand = cpos + lax.shift_left(jnp.int32(1), n_bits - 1 - it)
            cnt = count(lambda s, j: (s == thr) & ((krow + j * tq) < cand))
            return jnp.where(cnt < need, cand, cpos)

        cut = lax.fori_loop(0, n_bits, cut_body, jnp.zeros((1, tq), I32))

        def drop_body(j, c):
            s = sc_ref[j]
            sc_ref[j] = jnp.where((s == thr) & ((krow + j * tq) > cut) & (~few), -jnp.inf, s)
            return c

        lax.fori_loop(0, nblk, drop_body, 0)

    mb_ref[...] = jnp.full(mb_ref.shape, NEG_BIG, F32)
    accb_ref[...] = jnp.zeros(accb_ref.shape, F32)

    def dsa_body(j, c):
        start = pl.multiple_of(j * tq, tq)
        sel = sc_ref[j] >= thr
        pat = jnp.minimum(qi - j, 2)
        s_t = _dot(kb_ref[pl.ds(start, tq), :], qbp_ref[...]) + bias_ref[pat]
        s_t = per_head(lambda h: jnp.where(sel, s_t[:, hcols(h)], NEG_BIG))
        flash_update(s_t, vdsaT_ref[j], mb_ref, accb_ref)
        return c

    lax.fori_loop(0, nblk, dsa_body, 0)
    inv_b = 1.0 / accb_ref[LANES:LANES + 1, :]
    for h in range(B_HEADS):
        g = h // B_GROUP
        obT_ref[h * B_HEAD_DIM:(h + 1) * B_HEAD_DIM, :] = (
            accb_ref[g * B_HEAD_DIM:(g + 1) * B_HEAD_DIM, hcols(h)] * inv_b[:, hcols(h)])


def _prompt_attention(pr, bias_p, b, t, n_top):
    tq = TQ
    assert t % tq == 0
    nq = t // tq
    r3 = lambda a: a.reshape(b, t, a.shape[-1])
    qrows = lambda r: pl.BlockSpec((None, r, tq), lambda bi, qi: (bi, 0, qi))
    kall = lambda c: pl.BlockSpec((None, t, c), lambda bi, qi: (bi, 0, 0))
    vall = pl.BlockSpec((None, nq, V_EXT, tq), lambda bi, qi: (bi, 0, 0, 0))
    olat, ob = pl.pallas_call(
        functools.partial(_prompt_attn_kernel, n_top=n_top),
        grid=(b, nq),
        in_specs=[pl.BlockSpec((None, A_HEADS, QCAT, tq), lambda bi, qi: (bi, 0, 0, qi)),
                  qrows(B_WIDTH), qrows(IDX_HEADS * IDX_DIM), qrows(IDX_HEADS),
                  kall(QCAT), kall(LANES), kall(LANES), vall, vall,
                  pl.BlockSpec(bias_p.shape, lambda bi, qi: (0, 0, 0))],
        out_specs=[qrows(A_HEADS * A_KV_LORA), qrows(B_WIDTH)],
        out_shape=[jax.ShapeDtypeStruct((b, A_HEADS * A_KV_LORA, t), BF16),
                   jax.ShapeDtypeStruct((b, B_WIDTH, t), F32)],
        scratch_shapes=[
            pltpu.VMEM((QCAT, A_HEADS * tq), BF16),
            pltpu.VMEM((LANES, B_HEADS * tq), BF16),
            pltpu.VMEM((LANES, IDX_HEADS * tq), BF16),
            pltpu.VMEM((nq, tq, tq), F32),
            pltpu.VMEM((8, A_HEADS * tq), F32),
            pltpu.VMEM((V_EXT, A_HEADS * tq), F32),
            pltpu.VMEM((8, B_HEADS * tq), F32),
            pltpu.VMEM((V_EXT, B_HEADS * tq), F32),
        ],
        compiler_params=pltpu.CompilerParams(dimension_semantics=("arbitrary", "arbitrary"),
                                             vmem_limit_bytes=VMEM_LIMIT_BYTES),
        name="prompt_attention",
    )(pr["qcatT"], pr["qbT"], pr["qidxT"], pr["widxT"],
      r3(pr["kcat"]), r3(pr["kb16"]), r3(pr["kidx16"]), pr["vmlaT"], pr["vdsaT"], bias_p)
    return olat, ob


def _sample_attn_kernel(qcat_ref, qb_ref, qidx_ref, widx_ref, kcatn_ref, kbn_ref, vbn_ref, kidxn_ref,
                        cckv_ref, ckpeT_ref, ckT_ref, cvT_ref, ckidxT_ref, bias_ref,
                        olat_ref, ob_ref, *, n_top, t_new, past):
    tq = t_new
    pad = LANES
    n_keys = past + pad
    colmask = lax.broadcasted_iota(I32, (tq, n_keys), 1) < past + t_new

    def padrows(a):
        return jnp.concatenate([a, jnp.zeros((pad - t_new, a.shape[1]), a.dtype)], axis=0)

    ckv_c = cckv_ref[...].astype(BF16)
    kpe_t = ckpeT_ref[...].astype(BF16)
    k_t = ckT_ref[...].astype(BF16)
    v_t = cvT_ref[...].astype(BF16)
    kidx_t = ckidxT_ref[...].astype(BF16)
    kcat_n = padrows(kcatn_ref[...])
    k_n = padrows(kbn_ref[...])
    v_n = padrows(vbn_ref[...])
    kidx_n = padrows(kidxn_ref[...])

    def softmax(s):
        m = jnp.max(s, axis=1, keepdims=True)
        p = jnp.exp2(s - m)
        return p.astype(BF16), jnp.sum(p, axis=1, keepdims=True)

    qs = jnp.concatenate([qcat_ref[:, h * QCAT:(h + 1) * QCAT] for h in range(A_HEADS)], axis=0)
    s_c = _dot_nt(qs[:, :A_KV_LORA], ckv_c) + _dot(qs[:, A_KV_LORA:A_KV_LORA + A_ROPE], kpe_t)
    s = jnp.concatenate([s_c, _dot_nt(qs, kcat_n)], axis=1)
    cm8 = lax.broadcasted_iota(I32, (A_HEADS * tq, n_keys), 1) < past + t_new
    pb, l = softmax(jnp.where(cm8, s, NEG_BIG))
    o = (_dot(pb[:, :past], ckv_c) + _dot(pb[:, past:], kcat_n[:, :A_KV_LORA])) / l
    for h in range(A_HEADS):
        olat_ref[:, h * LANES:(h + 1) * LANES] = o[h * tq:(h + 1) * tq].astype(olat_ref.dtype)

    qis = jnp.concatenate([qidx_ref[:, h * IDX_DIM:(h + 1) * IDX_DIM] for h in range(IDX_HEADS)], axis=0)
    dots = jnp.concatenate([_dot(qis, kidx_t), _dot_nt(qis, kidx_n)], axis=1)
    score = jnp.zeros((tq, n_keys), F32)
    for h in range(IDX_HEADS):
        score = score + jnp.maximum(dots[h * tq:(h + 1) * tq], 0.0) * widx_ref[:, h:h + 1]
    score = jnp.where(colmask, score, -jnp.inf)
    kf = float(n_top)

    def bis_body(it, carry):
        lo, cnt_lo = carry
        cand = lo + lax.shift_left(jnp.int32(1), 31 - it)
        cnt = _count(score >= _key_to_float(cand))
        take = cnt >= kf
        return jnp.where(take, cand, lo), jnp.where(take, cnt, cnt_lo)

    lo, cnt_ge = lax.fori_loop(
        0, 32, bis_body, (jnp.full((tq, 1), INT_MIN, I32), jnp.full((tq, 1), 1e9, F32)), unroll=True)
    few = lo == INT_MIN
    thr = _key_to_float(lo)
    cols = lax.broadcasted_iota(I32, (tq, n_keys), 1)
    has_tie = jnp.max(jnp.where((cnt_ge > kf) & (~few), 1.0, 0.0))
    n_bits = n_keys.bit_length()

    def tie_path():
        need = kf - _count(score > thr)
        eq = score == thr

        def cut_body(it, cpos):
            cand = cpos + lax.shift_left(jnp.int32(1), n_bits - 1 - it)
            cnt = _count(eq & (cols < cand))
            return jnp.where(cnt < need, cand, cpos)

        return lax.fori_loop(0, n_bits, cut_body, jnp.zeros((tq, 1), I32))

    cut = lax.cond(has_tie > 0.0, tie_path, lambda: jnp.full((tq, 1), 2 ** 30, I32))
    sel = (score > thr) | ((score == thr) & (cols <= cut))

    n_far = n_keys - bias_ref.shape[-1]
    for g in range(B_KV_HEADS):
        lanes = slice(g * B_HEAD_DIM, (g + 1) * B_HEAD_DIM)
        qg = jnp.concatenate(
            [qb_ref[:, h * B_HEAD_DIM:(h + 1) * B_HEAD_DIM] for h in range(g * B_GROUP, (g + 1) * B_GROUP)],
            axis=0)
        sg = jnp.concatenate([_dot(qg, k_t[lanes]), _dot_nt(qg, k_n[:, lanes])], axis=1)
        for hh in range(B_GROUP):
            h = g * B_GROUP + hh
            near = bias_ref[0, h]
            far = bias_ref[1, h][:, :1]
            sh = sg[hh * tq:(hh + 1) * tq]
            sh = jnp.concatenate([sh[:, :n_far] + far, sh[:, n_far:] + near], axis=1)
            pb, l = softmax(jnp.where(sel, sh, NEG_BIG))
            oh = (_dot_nt(pb[:, :past], v_t[lanes]) + _dot(pb[:, past:], v_n[:, lanes])) / l
            ob_ref[:, h * B_HEAD_DIM:(h + 1) * B_HEAD_DIM] = oh


def _sample_attention(pr, caches, bias_s, b, t_new, past, n_top):
    r3 = lambda a: a.reshape(b, t_new, a.shape[-1])
    per_b = lambda n, c: pl.BlockSpec((None, n, c), lambda bi: (bi, 0, 0))
    news = [pr["qcat"], pr["qb"], pr["qidx"], pr["widx"], pr["kcat"], pr["kb16"], pr["vb16"], pr["kidx16"]]
    olat, ob = pl.pallas_call(
        functools.partial(_sample_attn_kernel, n_top=n_top, t_new=t_new, past=past),
        grid=(b,),
        in_specs=[per_b(t_new, a.shape[-1]) for a in news]
                 + [per_b(c.shape[1], c.shape[2]) for c in caches]
                 + [pl.BlockSpec(bias_s.shape, lambda bi: (0, 0, 0, 0))],
        out_specs=[per_b(t_new, A_HEADS * A_KV_LORA), per_b(t_new, B_WIDTH)],
        out_shape=[jax.ShapeDtypeStruct((b, t_new, A_HEADS * A_KV_LORA), BF16),
                   jax.ShapeDtypeStruct((b, t_new, B_WIDTH), F32)],
        compiler_params=pltpu.CompilerParams(dimension_semantics=("arbitrary",),
                                             vmem_limit_bytes=VMEM_LIMIT_BYTES),
        name="sample_attention",
    )(*[r3(a) for a in news], *caches, bias_s)
    return olat.reshape(b * t_new, -1), ob.reshape(b * t_new, -1)


def _combine_kernel(x_ref, olat_ref, ob_ref, sga_ref, sgb_ref, wuv_ref, wout_ref, fg_ref, y_ref, *, final):
    o_a = _dot(olat_ref[...], wuv_ref[...])
    mix = jnp.concatenate([o_a * sga_ref[...], ob_ref[...] * sgb_ref[...]], axis=1)
    y = x_ref[...] + _dot(mix.astype(BF16), wout_ref[...])
    if final:
        y = _rms(y, fg_ref[...])
    y_ref[...] = y


def _combine(x2d, olat, ob, pr, lw, fg, final):
    n, d = x2d.shape
    tm = PROJ_TM
    tok = lambda c: pl.BlockSpec((tm, c), lambda i: (i, 0))
    full = lambda a: pl.BlockSpec(a.shape, lambda i: (0,) * a.ndim)
    return pl.pallas_call(
        functools.partial(_combine_kernel, final=final),
        grid=(n // tm,),
        in_specs=[tok(d), tok(olat.shape[1]), tok(ob.shape[1]), tok(A_WIDTH), tok(B_WIDTH),
                  full(lw["wuv"]), full(lw["wout"]), full(fg)],
        out_specs=tok(d),
        out_shape=jax.ShapeDtypeStruct((n, d), F32),
        compiler_params=pltpu.CompilerParams(dimension_semantics=("arbitrary",),
                                             vmem_limit_bytes=VMEM_LIMIT_BYTES),
        name="combine",
    )(x2d, olat, ob, pr["sga"], pr["sgb"], lw["wuv"], lw["wout"], fg)


def _combine_t_kernel(x_ref, olatT_ref, obT_ref, sgaT_ref, sgbT_ref, wuvt_ref, wout_ref, fg_ref, y_ref, *, final):
    o_a = _dot(wuvt_ref[...], olatT_ref[...])
    mix_t = jnp.concatenate([o_a * sgaT_ref[...], obT_ref[...] * sgbT_ref[...]], axis=0).astype(BF16)
    y = x_ref[...] + lax.dot_general(mix_t, wout_ref[...], (((0,), (0,)), ((), ())),
                                     preferred_element_type=F32)
    if final:
        y = _rms(y, fg_ref[...])
    y_ref[...] = y


def _combine_t(x3d, olat_t, ob_t, pr, lw, fg, final):
    b, t, d = x3d.shape
    tm = TQ
    rows_t = lambda r: pl.BlockSpec((None, r, tm), lambda bi, ti: (bi, 0, ti))
    full = lambda a: pl.BlockSpec(a.shape, lambda bi, ti: (0,) * a.ndim)
    xblk = pl.BlockSpec((None, tm, d), lambda bi, ti: (bi, ti, 0))
    return pl.pallas_call(
        functools.partial(_combine_t_kernel, final=final),
        grid=(b, t // tm),
        in_specs=[xblk, rows_t(A_HEADS * A_KV_LORA), rows_t(B_WIDTH), rows_t(A_WIDTH), rows_t(B_WIDTH),
                  full(lw["wuvt"]), full(lw["wout"]), full(fg)],
        out_specs=xblk,
        out_shape=jax.ShapeDtypeStruct((b, t, d), F32),
        compiler_params=pltpu.CompilerParams(dimension_semantics=("arbitrary", "arbitrary"),
                                             vmem_limit_bytes=VMEM_LIMIT_BYTES),
        name="combine_prompt",
    )(x3d, olat_t, ob_t, pr["sgaT"], pr["sgbT"], lw["wuvt"], lw["wout"], fg)


def _layer_weights(norm_g, w_in, q_norm_g, kv_norm_g, w_uq, w_uk, w_uv, w_out):
    d = w_in.shape[0]
    o = np.cumsum([0, A_Q_LORA, A_KV_LORA, A_ROPE, A_WIDTH, B_WIDTH, B_KV_HEADS * B_HEAD_DIM,
                   B_KV_HEADS * B_HEAD_DIM, IDX_HEADS * IDX_DIM, IDX_DIM, IDX_HEADS, B_WIDTH])
    seg = lambda i: w_in[:, int(o[i]):int(o[i + 1])]
    misc = jnp.concatenate([seg(2), seg(8), seg(9),
                            jnp.zeros((d, LANES - A_ROPE - IDX_DIM - IDX_HEADS), w_in.dtype)], axis=1)
    win = jnp.concatenate([seg(0), seg(1), misc, seg(3), seg(4), seg(5), seg(6), seg(7), seg(10)], axis=1)
    assert win.shape[1] == IN_PAD
    half = A_ROPE // 2
    wq = jnp.concatenate([
        w_uq[:, :, :A_NOPE].reshape(A_Q_LORA, A_HEADS * A_NOPE),
        w_uq[:, :, A_NOPE:A_NOPE + half].reshape(A_Q_LORA, A_HEADS * half),
        w_uq[:, :, A_NOPE + half:].reshape(A_Q_LORA, A_HEADS * half)], axis=1)
    eye = jnp.eye(A_HEADS, dtype=w_uk.dtype)
    wuk = jnp.einsum('chn,hg->hngc', w_uk, eye).reshape(A_HEADS * A_NOPE, A_HEADS * A_KV_LORA)
    wuv = jnp.einsum('chv,hg->hcgv', w_uv, eye).reshape(A_HEADS * A_KV_LORA, A_HEADS * A_V)
    pm = np.zeros((2 * LANES, A_HEADS * LANES), np.float32)
    for h in range(A_HEADS):
        for i in range(half):
            pm[h * half + i, h * LANES + i] = 1.0
            pm[LANES + h * half + i, h * LANES + half + i] = 1.0
    zpad = lambda c: jnp.zeros((d, c), w_in.dtype)
    wk = jnp.concatenate([seg(1), seg(8), seg(2), zpad(LANES - IDX_DIM - A_ROPE), seg(5)], axis=1)
    wt = jnp.concatenate([seg(0), seg(1), seg(6), seg(4), seg(7), seg(3), seg(10), seg(9),
                          zpad(R_KB - R_WI - IDX_HEADS), seg(5), seg(8), seg(2)], axis=1).T
    assert wk.shape[1] == COLS_K and wt.shape[0] == ROWS_T
    bc = lambda g: jnp.broadcast_to(g.reshape(-1, 1), (g.shape[0], TQ))
    return {
        "wk": wk.astype(BF16), "wt": wt.astype(BF16), "qngc": bc(q_norm_g), "kvngc": bc(kv_norm_g),
        "wqt": wq.T.astype(BF16), "wukt": wuk.T.astype(BF16), "wuvt": wuv.T.astype(BF16),
        "ng": norm_g.reshape(1, -1), "win": win.astype(BF16),
        "qng": q_norm_g.reshape(1, -1), "kvng": kv_norm_g.reshape(1, -1),
        "wq": wq.astype(BF16), "wuk": wuk.astype(BF16), "wuv": wuv.astype(BF16),
        "pmat": jnp.asarray(pm, BF16), "wout": w_out.astype(BF16),
    }


def _rope_tables(pos):
    half = A_ROPE // 2
    inv = ROPE_THETA ** (-jnp.arange(half, dtype=F32) / half)
    ang = pos.astype(F32)[:, None] * inv[None, :]
    cos, sin = jnp.cos(ang), jnp.sin(ang)
    z = jnp.zeros((pos.shape[0], LANES - A_ROPE), F32)
    zh = jnp.zeros_like(cos)
    cosq = jnp.tile(cos, (1, A_HEADS))
    sinq = jnp.tile(sin, (1, A_HEADS))
    rc = jnp.concatenate([cos, cos, z], axis=1)
    rs1 = jnp.concatenate([zh, sin, z], axis=1)
    rs2 = jnp.concatenate([-sin, zh, z], axis=1)
    return cosq, sinq, rc, rs1, rs2


def _rope_tables_t(pos):
    half = A_ROPE // 2
    inv = ROPE_THETA ** (-jnp.arange(half, dtype=F32) / half)
    ang = pos.astype(F32)[:, None] * inv[None, :]
    cos, sin = jnp.cos(ang), jnp.sin(ang)
    cost = jnp.tile(cos.T, (A_HEADS, 1))
    sint = jnp.tile(sin.T, (A_HEADS, 1))
    z0 = jnp.zeros((pos.shape[0], MK_KPE), F32)
    z1 = jnp.zeros((pos.shape[0], LANES - MK_KPE - A_ROPE), F32)
    zh = jnp.zeros_like(cos)
    rc = jnp.concatenate([z0, cos, cos, z1], axis=1)
    rs1 = jnp.concatenate([z0, zh, sin, z1], axis=1)
    rs2 = jnp.concatenate([z0, -sin, zh, z1], axis=1)
    return cost, sint, rc, rs1, rs2


def kernel(x_prompt, x_sample, cache_mla_ckv, cache_mla_kpe, cache_dsa_k, cache_dsa_v, cache_dsa_kidx,
           norm_g, w_in, mla_q_norm_g, mla_kv_norm_g, mla_w_uq, mla_w_uk, mla_w_uv, rel_bias, w_out,
           final_norm_g):
    bp, tp, d = x_prompt.shape
    bs, ts, _ = x_sample.shape
    depth = w_in.shape[0]
    past = cache_mla_ckv.shape[2]
    n_top_p = min(TOP_K_MAX, tp // 4)
    n_top_s = min(TOP_K_MAX, (past + ts) // 4)
    assert ts <= CHUNK and past % CHUNK == 0 and past % LANES == 0

    rope_p = _rope_tables_t(jnp.arange(tp, dtype=jnp.int32))
    reps = PROJ_TM // ts
    rope_s = tuple(jnp.tile(a, (reps, 1)) for a in _rope_tables(past + jnp.arange(ts, dtype=jnp.int32)))

    bias_p = _bias_tables(rel_bias, (0, -TQ, -3 * TQ), TQ, TQ, True)
    win_s = 2 * LANES
    bias_s = _bias_tables(rel_bias, (-(win_s - LANES), -(past + win_s)), ts, win_s, False)
    fg = final_norm_g.reshape(1, -1)

    xp = x_prompt
    xs = x_sample.reshape(bs * ts, d)
    outs_p, outs_s = [], []
    for l in range(depth):
        lw = _layer_weights(norm_g[l], w_in[l], mla_q_norm_g[l], mla_kv_norm_g[l],
                            mla_w_uq[l], mla_w_uk[l], mla_w_uv[l], w_out[l])
        final = l == depth - 1
        pr = _project_t(xp.reshape(bp * tp, d), rope_p, lw, bp, tp)
        olat, ob = _prompt_attention(pr, bias_p, bp, tp, n_top_p)
        xp = _combine_t(xp, olat, ob, pr, lw, fg, final)
        heads_t = lambda a: a.reshape(bp, B_KV_HEADS, B_HEAD_DIM, tp).transpose(0, 3, 1, 2)
        outs_p.append((pr["ckv"].reshape(bp, tp, A_KV_LORA), pr["kpeT"].transpose(0, 2, 1),
                       heads_t(pr["kbT"]), heads_t(pr["vbT"]), pr["kidxT"].transpose(0, 2, 1)))
        ps = _project(xs, rope_s, lw, period=PROJ_TM)
        feat_t = lambda a: a.transpose(0, 2, 3, 1).reshape(bs, B_KV_HEADS * B_HEAD_DIM, past)
        caches = (cache_mla_ckv[l], cache_mla_kpe[l].transpose(0, 2, 1), feat_t(cache_dsa_k[l]),
                  feat_t(cache_dsa_v[l]), cache_dsa_kidx[l].transpose(0, 2, 1))
        olat, ob = _sample_attention(ps, caches, bias_s, bs, ts, past, n_top_s)
        xs = _combine(xs, olat, ob, ps, lw, fg, final)
        outs_s.append((ps["ckv"].reshape(bs, ts, A_KV_LORA), ps["kpe"].reshape(bs, ts, A_ROPE),
                       ps["kb"].reshape(bs, ts, B_KV_HEADS, B_HEAD_DIM),
                       ps["vb"].reshape(bs, ts, B_KV_HEADS, B_HEAD_DIM),
                       ps["kidx"].reshape(bs, ts, IDX_DIM)))

    stack = lambda outs, i: jnp.stack([o[i] for o in outs])
    return ((xp, xs.reshape(bs, ts, d))
            + tuple(stack(outs_p, i) for i in range(5)) + tuple(stack(outs_s, i) for i in range(5)))
```

```python
import functools
import math

import jax
import jax.numpy as jnp
import numpy as np
from jax import lax
from jax.experimental import pallas as pl
from jax.experimental.pallas import tpu as pltpu

F32 = jnp.float32
BF16 = jnp.bfloat16
I32 = jnp.int32

CHUNK = 64
EPS = 1e-6
A_HEADS = 8
A_NOPE = 64
A_ROPE = 32
A_V = 64
A_Q_LORA = 256
A_KV_LORA = 128
ROPE_THETA = 10000.0
A_WIDTH = A_HEADS * A_V
B_HEADS = 8
B_KV_HEADS = 2
B_HEAD_DIM = 64
B_WIDTH = B_HEADS * B_HEAD_DIM
B_GROUP = B_HEADS // B_KV_HEADS
IDX_HEADS = 8
IDX_DIM = 64
TOP_K_MAX = 256
N_BUCKETS = 32
MAX_DISTANCE = 128

LANES = 128
VMEM_LIMIT_BYTES = 56 * 1024 * 1024

LOG2E = 1.4426950408889634
NEG_BIG = -1e30
INT_MIN = -(2 ** 31)

C_CQ = 0
C_CKV = C_CQ + A_Q_LORA
C_MISC = C_CKV + A_KV_LORA
C_GA = C_MISC + LANES
C_QB = C_GA + A_WIDTH
C_KB = C_QB + B_WIDTH
C_VB = C_KB + B_KV_HEADS * B_HEAD_DIM
C_QI = C_VB + B_KV_HEADS * B_HEAD_DIM
C_GB = C_QI + IDX_HEADS * IDX_DIM
IN_PAD = C_GB + B_WIDTH
M_KPE = 0
M_KIDX = A_ROPE
M_WIDX = A_ROPE + IDX_DIM

QCAT = 2 * LANES
TQ = 256
PROJ_TM = 256


def _dot(a, b):
    return jnp.dot(a, b, preferred_element_type=F32)


def _dot_nt(a, b):
    return lax.dot_general(a, b, (((1,), (1,)), ((), ())), preferred_element_type=F32)


def _rms(x, g):
    return x * lax.rsqrt(jnp.mean(x * x, axis=-1, keepdims=True) + EPS) * g


def _bias_kernel(rb_ref, out_ref, *, offsets, keys_on_rows):
    nb = N_BUCKETS // 2
    max_exact = nb // 2
    n_r = out_ref.shape[1] if keys_on_rows else out_ref.shape[2]
    n_c = out_ref.shape[2] // B_HEADS if keys_on_rows else out_ref.shape[3]
    row = lax.broadcasted_iota(I32, (n_r, n_c), 0)
    col = lax.broadcasted_iota(I32, (n_r, n_c), 1)
    for p, off in enumerate(offsets):
        rel = off + (row - col if keys_on_rows else col - row)
        ret = jnp.where(rel > 0, nb, 0)
        n = jnp.abs(rel)
        nf = jnp.maximum(n, 1).astype(F32)
        large = max_exact + (jnp.log(nf / max_exact) / math.log(MAX_DISTANCE / max_exact)
                             * (nb - max_exact)).astype(I32)
        large = jnp.minimum(large, nb - 1)
        bucket = ret + jnp.where(n < max_exact, n, large)
        for h in range(B_HEADS):
            table = jnp.broadcast_to(rb_ref[h:h + 1, :], (n_r, LANES))
            acc = jnp.concatenate(
                [jnp.take_along_axis(table, bucket[:, c:c + LANES], axis=1) for c in range(0, n_c, LANES)],
                axis=1)
            if keys_on_rows:
                out_ref[p, :, h * n_c:(h + 1) * n_c] = acc * LOG2E
            else:
                out_ref[p, h] = acc * LOG2E


def _bias_tables(rel_bias, offsets, n_r, n_c, keys_on_rows):
    shape = (len(offsets), n_r, B_HEADS * n_c) if keys_on_rows else (len(offsets), B_HEADS, n_r, n_c)
    return pl.pallas_call(
        functools.partial(_bias_kernel, offsets=tuple(offsets), keys_on_rows=keys_on_rows),
        out_shape=jax.ShapeDtypeStruct(shape, F32),
        in_specs=[pl.BlockSpec(memory_space=pltpu.VMEM)],
        out_specs=pl.BlockSpec(memory_space=pltpu.VMEM),
        compiler_params=pltpu.CompilerParams(vmem_limit_bytes=VMEM_LIMIT_BYTES),
        name="bias_tables",
    )(jnp.pad(rel_bias.T, ((0, 0), (0, LANES - N_BUCKETS))))


def _proj_kernel(x_ref, ng_ref, win_ref, qng_ref, kvng_ref, wq_ref, wuk_ref, pmat_ref,
                 cosq_ref, sinq_ref, rc_ref, rs1_ref, rs2_ref,
                 ckv_ref, kpe_ref, kb_ref, vb_ref, kidx_ref,
                 kcat_ref, kb16_ref, vb16_ref, kidx16_ref,
                 qcat_ref, qb_ref, qidx_ref, widx_ref, sga_ref, sgb_ref):
    x = x_ref[...]
    h = _rms(x, ng_ref[...])
    z = _dot(h.astype(BF16), win_ref[...])

    cq = _rms(z[:, C_CQ:C_CQ + A_Q_LORA], qng_ref[...])
    q = _dot(cq.astype(BF16), wq_ref[...])
    n_nope = A_HEADS * A_NOPE
    x1 = q[:, n_nope:n_nope + LANES]
    x2 = q[:, n_nope + LANES:n_nope + 2 * LANES]
    cos8, sin8 = cosq_ref[...], sinq_ref[...]
    o1 = x1 * cos8 - x2 * sin8
    o2 = x1 * sin8 + x2 * cos8
    mla_scale = (A_NOPE + A_ROPE) ** -0.5 * LOG2E
    q_lat = _dot(q[:, :n_nope].astype(BF16), wuk_ref[...]) * mla_scale
    pe = jnp.concatenate([o1, o2], axis=1) * mla_scale
    q_pe = _dot(pe.astype(BF16), pmat_ref[...])
    for hh in range(A_HEADS):
        qcat_ref[:, hh * QCAT:hh * QCAT + LANES] = q_lat[:, hh * LANES:(hh + 1) * LANES].astype(BF16)
        qcat_ref[:, hh * QCAT + LANES:(hh + 1) * QCAT] = q_pe[:, hh * LANES:(hh + 1) * LANES].astype(BF16)

    ckv = _rms(z[:, C_CKV:C_CKV + A_KV_LORA], kvng_ref[...])
    ckv_ref[...] = ckv
    misc = z[:, C_MISC:C_MISC + LANES]
    rot = (misc * rc_ref[...] + pltpu.roll(misc, A_ROPE // 2, 1) * rs1_ref[...]
           + pltpu.roll(misc, LANES - A_ROPE // 2, 1) * rs2_ref[...])
    kpe_ref[...] = rot[:, :A_ROPE]
    kcat_ref[:, :LANES] = ckv.astype(BF16)
    kcat_ref[:, LANES:] = rot.astype(BF16)

    kidx = misc[:, M_KIDX:M_KIDX + IDX_DIM]
    kidx_ref[...] = kidx
    kidx16_ref[...] = kidx.astype(BF16)
    widx_ref[...] = misc[:, M_WIDX:M_WIDX + IDX_HEADS] * (IDX_HEADS ** -0.5)
    kb = z[:, C_KB:C_KB + LANES]
    vb = z[:, C_VB:C_VB + LANES]
    kb_ref[...] = kb
    vb_ref[...] = vb
    kb16_ref[...] = kb.astype(BF16)
    vb16_ref[...] = vb.astype(BF16)
    qb_ref[...] = (z[:, C_QB:C_QB + B_WIDTH] * (B_HEAD_DIM ** -0.5 * LOG2E)).astype(BF16)
    qidx_ref[...] = (z[:, C_QI:C_QI + IDX_HEADS * IDX_DIM] * (IDX_DIM ** -0.5)).astype(BF16)
    sga_ref[...] = jax.nn.silu(z[:, C_GA:C_GA + A_WIDTH])
    sgb_ref[...] = jax.nn.silu(z[:, C_GB:C_GB + B_WIDTH])


def _project(x2d, rope_tabs, lw, *, period):
    n, d = x2d.shape
    tm = PROJ_TM
    assert n % tm == 0 and period % tm == 0
    n_rep = period // tm
    tok = lambda c: pl.BlockSpec((tm, c), lambda i: (i, 0))
    full = lambda a: pl.BlockSpec(a.shape, lambda i: (0,) * a.ndim)
    tab = pl.BlockSpec((tm, LANES), lambda i: (i % n_rep, 0))
    outs = [
        ("ckv", A_KV_LORA, F32), ("kpe", A_ROPE, F32), ("kb", LANES, F32), ("vb", LANES, F32),
        ("kidx", IDX_DIM, F32),
        ("kcat", QCAT, BF16), ("kb16", LANES, BF16), ("vb16", LANES, BF16), ("kidx16", IDX_DIM, BF16),
        ("qcat", A_HEADS * QCAT, BF16), ("qb", B_WIDTH, BF16), ("qidx", IDX_HEADS * IDX_DIM, BF16),
        ("widx", IDX_HEADS, F32), ("sga", A_WIDTH, F32), ("sgb", B_WIDTH, F32),
    ]
    res = pl.pallas_call(
        _proj_kernel,
        grid=(n // tm,),
        in_specs=[tok(d), full(lw["ng"]), full(lw["win"]), full(lw["qng"]), full(lw["kvng"]),
                  full(lw["wq"]), full(lw["wuk"]), full(lw["pmat"]), tab, tab, tab, tab, tab],
        out_specs=[tok(c) for _, c, _ in outs],
        out_shape=[jax.ShapeDtypeStruct((n, c), dt) for _, c, dt in outs],
        compiler_params=pltpu.CompilerParams(dimension_semantics=("arbitrary",),
                                             vmem_limit_bytes=VMEM_LIMIT_BYTES),
        name="project",
    )(x2d, lw["ng"], lw["win"], lw["qng"], lw["kvng"], lw["wq"], lw["wuk"], lw["pmat"], *rope_tabs)
    return {name: r for (name, _, _), r in zip(outs, res)}


R_CQ = 0
R_CKV = R_CQ + A_Q_LORA
R_VB = R_CKV + A_KV_LORA
R_QB = R_VB + LANES
R_QI = R_QB + B_WIDTH
R_GA = R_QI + IDX_HEADS * IDX_DIM
R_GB = R_GA + A_WIDTH
R_WI = R_GB + B_WIDTH
R_KB = R_WI + 16
R_KI = R_KB + LANES
R_KPE = R_KI + IDX_DIM
ROWS_T = R_KPE + A_ROPE
K_MISC = A_KV_LORA
K_KB = K_MISC + LANES
COLS_K = K_KB + LANES
MK_KPE = IDX_DIM
ONES_ROWS = 16
V_EXT = A_KV_LORA + ONES_ROWS


def _proj_t_kernel(x_ref, ng_ref, wk_ref, wt_ref, qng_ref, kvngc_ref, kvng_ref, wqt_ref, wukt_ref,
                   cost_ref, sint_ref, rc_ref, rs1_ref, rs2_ref,
                   ckv_ref, kcat_ref, kb16_ref, kidx16_ref,
                   kpeT_ref, kbT_ref, vbT_ref, kidxT_ref,
                   qcatT_ref, qbT_ref, qidxT_ref, widxT_ref, vmlaT_ref, vdsaT_ref, sgaT_ref, sgbT_ref):
    x = x_ref[...]
    tm = x.shape[0]
    hb = _rms(x, ng_ref[...]).astype(BF16)

    zk = _dot(hb, wk_ref[...])
    ckv = _rms(zk[:, :A_KV_LORA], kvng_ref[...])
    ckv_ref[...] = ckv
    misc = zk[:, K_MISC:K_MISC + LANES]
    lane = lax.broadcasted_iota(I32, (tm, LANES), 1)
    kidx16_ref[...] = jnp.where(lane < IDX_DIM, misc, 0.0).astype(BF16)
    rot = (misc * rc_ref[...] + pltpu.roll(misc, A_ROPE // 2, 1) * rs1_ref[...]
           + pltpu.roll(misc, LANES - A_ROPE // 2, 1) * rs2_ref[...])
    kcat_ref[:, :LANES] = ckv.astype(BF16)
    kcat_ref[:, LANES:] = pltpu.roll(rot, LANES - MK_KPE, 1).astype(BF16)
    kb16_ref[...] = zk[:, K_KB:K_KB + LANES].astype(BF16)

    zt = _dot_nt(wt_ref[...], hb)

    def rms_t(c, g):
        return c * lax.rsqrt(jnp.mean(c * c, axis=0, keepdims=True) + EPS) * g

    cq = rms_t(zt[R_CQ:R_CQ + A_Q_LORA], qng_ref[...])
    qt = _dot(wqt_ref[...], cq.astype(BF16))
    n_nope = A_HEADS * A_NOPE
    x1 = qt[n_nope:n_nope + LANES]
    x2 = qt[n_nope + LANES:n_nope + 2 * LANES]
    cos8, sin8 = cost_ref[...], sint_ref[...]
    mla_scale = (A_NOPE + A_ROPE) ** -0.5 * LOG2E
    o1 = (x1 * cos8 - x2 * sin8) * mla_scale
    o2 = (x1 * sin8 + x2 * cos8) * mla_scale
    q_lat = _dot(wukt_ref[...], qt[:n_nope].astype(BF16)) * mla_scale
    half = A_ROPE // 2
    for h in range(A_HEADS):
        qcatT_ref[h, :LANES, :] = q_lat[h * LANES:(h + 1) * LANES].astype(BF16)
        qcatT_ref[h, LANES:LANES + half, :] = o1[h * half:(h + 1) * half].astype(BF16)
        qcatT_ref[h, LANES + half:LANES + A_ROPE, :] = o2[h * half:(h + 1) * half].astype(BF16)
        qcatT_ref[h, LANES + A_ROPE:, :] = jnp.zeros((QCAT - LANES - A_ROPE, tm), BF16)
    ones = jnp.ones((ONES_ROWS, tm), BF16)
    vmlaT_ref[:A_KV_LORA, :] = rms_t(zt[R_CKV:R_CKV + A_KV_LORA], kvngc_ref[...]).astype(BF16)
    vmlaT_ref[A_KV_LORA:, :] = ones
    vb_t = zt[R_VB:R_VB + LANES]
    vdsaT_ref[:LANES, :] = vb_t.astype(BF16)
    vdsaT_ref[LANES:, :] = ones
    vbT_ref[...] = vb_t
    kbT_ref[...] = zt[R_KB:R_KB + LANES]
    kidxT_ref[...] = zt[R_KI:R_KI + IDX_DIM]
    k1, k2 = zt[R_KPE:R_KPE + half], zt[R_KPE + half:R_KPE + A_ROPE]
    cos1, sin1 = cos8[:half], sin8[:half]
    kpeT_ref[:half, :] = k1 * cos1 - k2 * sin1
    kpeT_ref[half:, :] = k1 * sin1 + k2 * cos1
    qbT_ref[...] = (zt[R_QB:R_QB + B_WIDTH] * (B_HEAD_DIM ** -0.5 * LOG2E)).astype(BF16)
    qidxT_ref[...] = (zt[R_QI:R_QI + IDX_HEADS * IDX_DIM] * (IDX_DIM ** -0.5)).astype(BF16)
    sgaT_ref[...] = jax.nn.silu(zt[R_GA:R_GA + A_WIDTH])
    sgbT_ref[...] = jax.nn.silu(zt[R_GB:R_GB + B_WIDTH])
    widxT_ref[...] = zt[R_WI:R_WI + IDX_HEADS] * (IDX_HEADS ** -0.5)


def _project_t(x2d, tabs, lw, b, t):
    n, d = x2d.shape
    tm = TQ
    nt = t // tm
    tok = lambda c: pl.BlockSpec((tm, c), lambda i: (i, 0))
    full = lambda a: pl.BlockSpec(a.shape, lambda i: (0,) * a.ndim)
    tab_t = pl.BlockSpec((LANES, tm), lambda i: (0, i % nt))
    tab_k = pl.BlockSpec((tm, LANES), lambda i: (i % nt, 0))
    rows_t = lambda r: pl.BlockSpec((None, r, tm), lambda i: (i // nt, 0, i % nt))
    outs = [
        ("ckv", (n, A_KV_LORA), F32, tok(A_KV_LORA)),
        ("kcat", (n, QCAT), BF16, tok(QCAT)), ("kb16", (n, LANES), BF16, tok(LANES)),
        ("kidx16", (n, LANES), BF16, tok(LANES)),
        ("kpeT", (b, A_ROPE, t), F32, rows_t(A_ROPE)), ("kbT", (b, LANES, t), F32, rows_t(LANES)),
        ("vbT", (b, LANES, t), F32, rows_t(LANES)), ("kidxT", (b, IDX_DIM, t), F32, rows_t(IDX_DIM)),
        ("qcatT", (b, A_HEADS, QCAT, t), BF16,
         pl.BlockSpec((None, A_HEADS, QCAT, tm), lambda i: (i // nt, 0, 0, i % nt))),
        ("qbT", (b, B_WIDTH, t), BF16, rows_t(B_WIDTH)),
        ("qidxT", (b, IDX_HEADS * IDX_DIM, t), BF16, rows_t(IDX_HEADS * IDX_DIM)),
        ("widxT", (b, IDX_HEADS, t), F32, rows_t(IDX_HEADS)),
        ("vmlaT", (b, nt, V_EXT, tm), BF16, pl.BlockSpec((None, None, V_EXT, tm), lambda i: (i // nt, i % nt, 0, 0))),
        ("vdsaT", (b, nt, V_EXT, tm), BF16, pl.BlockSpec((None, None, V_EXT, tm), lambda i: (i // nt, i % nt, 0, 0))),
        ("sgaT", (b, A_WIDTH, t), F32, rows_t(A_WIDTH)), ("sgbT", (b, B_WIDTH, t), F32, rows_t(B_WIDTH)),
    ]
    cost, sint, rc, rs1, rs2 = tabs
    res = pl.pallas_call(
        _proj_t_kernel,
        grid=(n // tm,),
        in_specs=[tok(d), full(lw["ng"]), full(lw["wk"]), full(lw["wt"]), full(lw["qngc"]), full(lw["kvngc"]),
                  full(lw["kvng"]), full(lw["wqt"]), full(lw["wukt"]), tab_t, tab_t, tab_k, tab_k, tab_k],
        out_specs=[o[3] for o in outs],
        out_shape=[jax.ShapeDtypeStruct(o[1], o[2]) for o in outs],
        compiler_params=pltpu.CompilerParams(dimension_semantics=("arbitrary",),
                                             vmem_limit_bytes=VMEM_LIMIT_BYTES),
        name="project_prompt",
    )(x2d, lw["ng"], lw["wk"], lw["wt"], lw["qngc"], lw["kvngc"], lw["kvng"], lw["wqt"], lw["wukt"],
      cost, sint, rc, rs1, rs2)
    return {o[0]: r for o, r in zip(outs, res)}


NEG_FLT_MAX = -3.4028234663852886e38
KEY_NEG_FLT_MAX = INT_MIN + (1 << 23)


def _key_to_float(k):
    k = jnp.maximum(k, KEY_NEG_FLT_MAX)
    return pltpu.bitcast(k ^ ((k >> 31) & 0x7FFFFFFF), F32)


def _count(pred):
    return jnp.sum(jnp.where(pred, 1.0, 0.0), axis=1, keepdims=True)


MAX_TIE_SWEEPS = 8.0


def _fold8(x, op=jnp.add):
    parts = [x[i:i + 8] for i in range(0, x.shape[0], 8)]
    while len(parts) > 1:
        parts = [op(a, b) for a, b in zip(parts[::2], parts[1::2])]
    return parts[0]


def _prompt_attn_kernel(qcatT_ref, qbT_ref, qidxT_ref, widxT_ref, kcat_ref, kb_ref, kidx_ref,
                        vmlaT_ref, vdsaT_ref, bias_ref, olatT_ref, obT_ref,
                        qa_ref, qbp_ref, qip_ref, sc_ref, m_ref, acc_ref, mb_ref, accb_ref, *, n_top):
    qi = pl.program_id(1)
    nblk = qi + 1
    tq = TQ
    krow = lax.broadcasted_iota(I32, (tq, tq), 0)
    qcol = lax.broadcasted_iota(I32, (tq, tq), 1)
    shift = CHUNK.bit_length() - 1
    diag_ok = (qcol >> shift) >= (krow >> shift)
    hcols = lambda h: slice(h * tq, (h + 1) * tq)

    def per_head(fn):
        return jnp.concatenate([fn(h) for h in range(A_HEADS)], axis=1)

    qbp_ref[...] = jnp.zeros(qbp_ref.shape, BF16)
    qip_ref[...] = jnp.zeros(qip_ref.shape, BF16)
    for h in range(A_HEADS):
        g = h // B_GROUP
        qa_ref[:, hcols(h)] = qcatT_ref[h]
        qbp_ref[g * B_HEAD_DIM:(g + 1) * B_HEAD_DIM, hcols(h)] = qbT_ref[h * B_HEAD_DIM:(h + 1) * B_HEAD_DIM, :]
        qip_ref[:IDX_DIM, hcols(h)] = qidxT_ref[h * IDX_DIM:(h + 1) * IDX_DIM, :]
    w_all = per_head(lambda h: widxT_ref[h:h + 1, :])

    def flash_update(s_t, v_t, m_r, acc_r):
        m_prev = m_r[0:1, :]
        m_new = jnp.maximum(m_prev, jnp.max(s_t, axis=0, keepdims=True))
        alpha = jnp.exp2(m_prev - m_new)
        p_t = jnp.exp2(s_t - m_new).astype(BF16)
        acc_r[...] = alpha * acc_r[...] + _dot(v_t, p_t)
        m_r[0:1, :] = m_new

    m_ref[...] = jnp.full(m_ref.shape, NEG_BIG, F32)
    acc_ref[...] = jnp.zeros(acc_ref.shape, F32)

    def mla_block(j, masked):
        start = pl.multiple_of(j * tq, tq)
        s_t = _dot(kcat_ref[pl.ds(start, tq), :], qa_ref[...])
        if masked:
            s_t = per_head(lambda h: jnp.where(diag_ok, s_t[:, hcols(h)], NEG_BIG))
        flash_update(s_t, vmlaT_ref[j], m_ref, acc_ref)
        r = jnp.maximum(_dot(kidx_ref[pl.ds(start, tq), :], qip_ref[...]), 0.0) * w_all
        score = r[:, hcols(0)]
        for h in range(1, IDX_HEADS):
            score = score + r[:, hcols(h)]
        if masked:
            score = jnp.where(diag_ok, score, -jnp.inf)
        sc_ref[j] = score

    def mla_body(j, c):
        mla_block(j, False)
        return c

    lax.fori_loop(0, qi, mla_body, 0)
    mla_block(qi, True)
    o_t = acc_ref[:A_KV_LORA, :] * (1.0 / acc_ref[A_KV_LORA:A_KV_LORA + 1, :])
    for h in range(A_HEADS):
        olatT_ref[h * LANES:(h + 1) * LANES, :] = o_t[:, hcols(h)].astype(olatT_ref.dtype)

    def count(pred):
        def body(j, c):
            return c + _fold8(jnp.where(pred(sc_ref[j], j), 1.0, 0.0))
        part = lax.fori_loop(0, nblk, body, jnp.zeros((8, tq), F32))
        return jnp.sum(part, axis=0, keepdims=True)

    kf = float(n_top)

    def bis_body(it, carry):
        lo, cnt_lo = carry
        cand = lo + lax.shift_left(jnp.int32(1), 31 - it)
        cand_f = _key_to_float(cand)
        cnt = count(lambda s, j: s >= cand_f)
        take = cnt >= kf
        return jnp.where(take, cand, lo), jnp.where(take, cnt, cnt_lo)

    lo, cnt_ge = lax.fori_loop(
        0, 32, bis_body, (jnp.full((1, tq), INT_MIN, I32), jnp.full((1, tq), 1e9, F32)))
    few = lo == INT_MIN
    thr = _key_to_float(lo)
    excess0 = jnp.where(few, 0.0, cnt_ge - kf)
    max_excess = jnp.max(excess0)

    def drop_from(cut):
        def body(j, c):
            s = sc_ref[j]
            sc_ref[j] = jnp.where((s == thr) & ((krow + j * tq) >= cut), -jnp.inf, s)
            return c
        lax.fori_loop(0, nblk, body, 0)

    @pl.when((max_excess > 0.0) & (max_excess <= MAX_TIE_SWEEPS))
    def _():
        def last_tie_below(cut):
            def body(j, m):
                pos = krow + j * tq
                hit = jnp.where((sc_ref[j] == thr) & (pos < cut), pos, -1)
                return jnp.maximum(m, _fold8(hit, jnp.maximum))
            part = lax.fori_loop(0, nblk, body, jnp.full((8, tq), -1, I32))
            return jnp.max(part, axis=0, keepdims=True)

        def sweep(c):
            excess, cut = c
            last = last_tie_below(cut)
            live = excess > 0.0
            return jnp.where(live, excess - 1.0, excess), jnp.where(live, last, cut)

        _, cut = lax.while_loop(lambda c: jnp.max(c[0]) > 0.0, sweep,
                                (excess0, jnp.full((1, tq), 2 ** 30, I32)))
        drop_from(cut)

    @pl.when(max_excess > MAX_TIE_SWEEPS)
    def _():
        need = kf - count(lambda s, j: s > thr)
        n_bits = (sc_ref.shape[0] * tq).bit_length()

        def cut_body(it, cpos):
            cand = cpos + lax.shift_left(jnp.int32(1), n_bits - 1 - it)
            cnt = count(lambda s, j: (s == thr) & ((krow + j * tq) < cand))
            return jnp.where(cnt < need, cand, cpos)

        keep = lax.fori_loop(0, n_bits, cut_body, jnp.zeros((1, tq), I32))
        drop_from(jnp.where(excess0 > 0.0, keep + 1, 2 ** 30))

    mb_ref[...] = jnp.full(mb_ref.shape, NEG_BIG, F32)
    accb_ref[...] = jnp.zeros(accb_ref.shape, F32)

    def dsa_body(j, c):
        start = pl.multiple_of(j * tq, tq)
        sel = sc_ref[j] >= thr
        pat = jnp.minimum(qi - j, 2)
        s_t = _dot(kb_ref[pl.ds(start, tq), :], qbp_ref[...]) + bias_ref[pat]
        s_t = per_head(lambda h: jnp.where(sel, s_t[:, hcols(h)], NEG_BIG))
        flash_update(s_t, vdsaT_ref[j], mb_ref, accb_ref)
        return c

    lax.fori_loop(0, nblk, dsa_body, 0)
    inv_b = 1.0 / accb_ref[LANES:LANES + 1, :]
    for h in range(B_HEADS):
        g = h // B_GROUP
        obT_ref[h * B_HEAD_DIM:(h + 1) * B_HEAD_DIM, :] = (
            accb_ref[g * B_HEAD_DIM:(g + 1) * B_HEAD_DIM, hcols(h)] * inv_b[:, hcols(h)])


def _prompt_attention(pr, bias_p, b, t, n_top):
    tq = TQ
    assert t % tq == 0
    nq = t // tq
    r3 = lambda a: a.reshape(b, t, a.shape[-1])
    qrows = lambda r: pl.BlockSpec((None, r, tq), lambda bi, qi: (bi, 0, qi))
    kall = lambda c: pl.BlockSpec((None, t, c), lambda bi, qi: (bi, 0, 0))
    vall = pl.BlockSpec((None, nq, V_EXT, tq), lambda bi, qi: (bi, 0, 0, 0))
    olat, ob = pl.pallas_call(
        functools.partial(_prompt_attn_kernel, n_top=n_top),
        grid=(b, nq),
        in_specs=[pl.BlockSpec((None, A_HEADS, QCAT, tq), lambda bi, qi: (bi, 0, 0, qi)),
                  qrows(B_WIDTH), qrows(IDX_HEADS * IDX_DIM), qrows(IDX_HEADS),
                  kall(QCAT), kall(LANES), kall(LANES), vall, vall,
                  pl.BlockSpec(bias_p.shape, lambda bi, qi: (0, 0, 0))],
        out_specs=[qrows(A_HEADS * A_KV_LORA), qrows(B_WIDTH)],
        out_shape=[jax.ShapeDtypeStruct((b, A_HEADS * A_KV_LORA, t), BF16),
                   jax.ShapeDtypeStruct((b, B_WIDTH, t), F32)],
        scratch_shapes=[
            pltpu.VMEM((QCAT, A_HEADS * tq), BF16),
            pltpu.VMEM((LANES, B_HEADS * tq), BF16),
            pltpu.VMEM((LANES, IDX_HEADS * tq), BF16),
            pltpu.VMEM((nq, tq, tq), F32),
            pltpu.VMEM((8, A_HEADS * tq), F32),
            pltpu.VMEM((V_EXT, A_HEADS * tq), F32),
            pltpu.VMEM((8, B_HEADS * tq), F32),
            pltpu.VMEM((V_EXT, B_HEADS * tq), F32),
        ],
        compiler_params=pltpu.CompilerParams(dimension_semantics=("arbitrary", "arbitrary"),
                                             vmem_limit_bytes=VMEM_LIMIT_BYTES),
        name="prompt_attention",
    )(pr["qcatT"], pr["qbT"], pr["qidxT"], pr["widxT"],
      r3(pr["kcat"]), r3(pr["kb16"]), r3(pr["kidx16"]), pr["vmlaT"], pr["vdsaT"], bias_p)
    return olat, ob


def _sample_attn_kernel(qcat_ref, qb_ref, qidx_ref, widx_ref, kcatn_ref, kbn_ref, vbn_ref, kidxn_ref,
                        cckv_ref, ckpeT_ref, ckT_ref, cvT_ref, ckidxT_ref, bias_ref,
                        olat_ref, ob_ref, *, n_top, t_new, past):
    tq = t_new
    pad = LANES
    n_keys = past + pad
    colmask = lax.broadcasted_iota(I32, (tq, n_keys), 1) < past + t_new

    def padrows(a):
        return jnp.concatenate([a, jnp.zeros((pad - t_new, a.shape[1]), a.dtype)], axis=0)

    ckv_c = cckv_ref[...].astype(BF16)
    kpe_t = ckpeT_ref[...].astype(BF16)
    k_t = ckT_ref[...].astype(BF16)
    v_t = cvT_ref[...].astype(BF16)
    kidx_t = ckidxT_ref[...].astype(BF16)
    kcat_n = padrows(kcatn_ref[...])
    k_n = padrows(kbn_ref[...])
    v_n = padrows(vbn_ref[...])
    kidx_n = padrows(kidxn_ref[...])

    def softmax(s):
        m = jnp.max(s, axis=1, keepdims=True)
        p = jnp.exp2(s - m)
        return p.astype(BF16), jnp.sum(p, axis=1, keepdims=True)

    qs = jnp.concatenate([qcat_ref[:, h * QCAT:(h + 1) * QCAT] for h in range(A_HEADS)], axis=0)
    s_c = _dot_nt(qs[:, :A_KV_LORA], ckv_c) + _dot(qs[:, A_KV_LORA:A_KV_LORA + A_ROPE], kpe_t)
    s = jnp.concatenate([s_c, _dot_nt(qs, kcat_n)], axis=1)
    cm8 = lax.broadcasted_iota(I32, (A_HEADS * tq, n_keys), 1) < past + t_new
    pb, l = softmax(jnp.where(cm8, s, NEG_BIG))
    o = (_dot(pb[:, :past], ckv_c) + _dot(pb[:, past:], kcat_n[:, :A_KV_LORA])) / l
    for h in range(A_HEADS):
        olat_ref[:, h * LANES:(h + 1) * LANES] = o[h * tq:(h + 1) * tq].astype(olat_ref.dtype)

    qis = jnp.concatenate([qidx_ref[:, h * IDX_DIM:(h + 1) * IDX_DIM] for h in range(IDX_HEADS)], axis=0)
    dots = jnp.concatenate([_dot(qis, kidx_t), _dot_nt(qis, kidx_n)], axis=1)
    score = jnp.zeros((tq, n_keys), F32)
    for h in range(IDX_HEADS):
        score = score + jnp.maximum(dots[h * tq:(h + 1) * tq], 0.0) * widx_ref[:, h:h + 1]
    score = jnp.where(colmask, score, -jnp.inf)
    kf = float(n_top)

    def bis_body(it, carry):
        lo, cnt_lo = carry
        cand = lo + lax.shift_left(jnp.int32(1), 31 - it)
        cnt = _count(score >= _key_to_float(cand))
        take = cnt >= kf
        return jnp.where(take, cand, lo), jnp.where(take, cnt, cnt_lo)

    lo, cnt_ge = lax.fori_loop(
        0, 32, bis_body, (jnp.full((tq, 1), INT_MIN, I32), jnp.full((tq, 1), 1e9, F32)), unroll=True)
    few = lo == INT_MIN
    thr = _key_to_float(lo)
    cols = lax.broadcasted_iota(I32, (tq, n_keys), 1)
    has_tie = jnp.max(jnp.where((cnt_ge > kf) & (~few), 1.0, 0.0))
    n_bits = n_keys.bit_length()

    def tie_path():
        need = kf - _count(score > thr)
        eq = score == thr

        def cut_body(it, cpos):
            cand = cpos + lax.shift_left(jnp.int32(1), n_bits - 1 - it)
            cnt = _count(eq & (cols < cand))
            return jnp.where(cnt < need, cand, cpos)

        return lax.fori_loop(0, n_bits, cut_body, jnp.zeros((tq, 1), I32))

    cut = lax.cond(has_tie > 0.0, tie_path, lambda: jnp.full((tq, 1), 2 ** 30, I32))
    sel = (score > thr) | ((score == thr) & (cols <= cut))

    n_far = n_keys - bias_ref.shape[-1]
    for g in range(B_KV_HEADS):
        lanes = slice(g * B_HEAD_DIM, (g + 1) * B_HEAD_DIM)
        qg = jnp.concatenate(
            [qb_ref[:, h * B_HEAD_DIM:(h + 1) * B_HEAD_DIM] for h in range(g * B_GROUP, (g + 1) * B_GROUP)],
            axis=0)
        sg = jnp.concatenate([_dot(qg, k_t[lanes]), _dot_nt(qg, k_n[:, lanes])], axis=1)
        for hh in range(B_GROUP):
            h = g * B_GROUP + hh
            near = bias_ref[0, h]
            far = bias_ref[1, h][:, :1]
            sh = sg[hh * tq:(hh + 1) * tq]
            sh = jnp.concatenate([sh[:, :n_far] + far, sh[:, n_far:] + near], axis=1)
            pb, l = softmax(jnp.where(sel, sh, NEG_BIG))
            oh = (_dot_nt(pb[:, :past], v_t[lanes]) + _dot(pb[:, past:], v_n[:, lanes])) / l
            ob_ref[:, h * B_HEAD_DIM:(h + 1) * B_HEAD_DIM] = oh


def _sample_attention(pr, caches, bias_s, b, t_new, past, n_top):
    r3 = lambda a: a.reshape(b, t_new, a.shape[-1])
    per_b = lambda n, c: pl.BlockSpec((None, n, c), lambda bi: (bi, 0, 0))
    news = [pr["qcat"], pr["qb"], pr["qidx"], pr["widx"], pr["kcat"], pr["kb16"], pr["vb16"], pr["kidx16"]]
    olat, ob = pl.pallas_call(
        functools.partial(_sample_attn_kernel, n_top=n_top, t_new=t_new, past=past),
        grid=(b,),
        in_specs=[per_b(t_new, a.shape[-1]) for a in news]
                 + [per_b(c.shape[1], c.shape[2]) for c in caches]
                 + [pl.BlockSpec(bias_s.shape, lambda bi: (0, 0, 0, 0))],
        out_specs=[per_b(t_new, A_HEADS * A_KV_LORA), per_b(t_new, B_WIDTH)],
        out_shape=[jax.ShapeDtypeStruct((b, t_new, A_HEADS * A_KV_LORA), BF16),
                   jax.ShapeDtypeStruct((b, t_new, B_WIDTH), F32)],
        compiler_params=pltpu.CompilerParams(dimension_semantics=("arbitrary",),
                                             vmem_limit_bytes=VMEM_LIMIT_BYTES),
        name="sample_attention",
    )(*[r3(a) for a in news], *caches, bias_s)
    return olat.reshape(b * t_new, -1), ob.reshape(b * t_new, -1)


def _combine_kernel(x_ref, olat_ref, ob_ref, sga_ref, sgb_ref, wuv_ref, wout_ref, fg_ref, y_ref, *, final):
    o_a = _dot(olat_ref[...], wuv_ref[...])
    mix = jnp.concatenate([o_a * sga_ref[...], ob_ref[...] * sgb_ref[...]], axis=1)
    y = x_ref[...] + _dot(mix.astype(BF16), wout_ref[...])
    if final:
        y = _rms(y, fg_ref[...])
    y_ref[...] = y


def _combine(x2d, olat, ob, pr, lw, fg, final):
    n, d = x2d.shape
    tm = PROJ_TM
    tok = lambda c: pl.BlockSpec((tm, c), lambda i: (i, 0))
    full = lambda a: pl.BlockSpec(a.shape, lambda i: (0,) * a.ndim)
    return pl.pallas_call(
        functools.partial(_combine_kernel, final=final),
        grid=(n // tm,),
        in_specs=[tok(d), tok(olat.shape[1]), tok(ob.shape[1]), tok(A_WIDTH), tok(B_WIDTH),
                  full(lw["wuv"]), full(lw["wout"]), full(fg)],
        out_specs=tok(d),
        out_shape=jax.ShapeDtypeStruct((n, d), F32),
        compiler_params=pltpu.CompilerParams(dimension_semantics=("arbitrary",),
                                             vmem_limit_bytes=VMEM_LIMIT_BYTES),
        name="combine",
    )(x2d, olat, ob, pr["sga"], pr["sgb"], lw["wuv"], lw["wout"], fg)


def _combine_t_kernel(x_ref, olatT_ref, obT_ref, sgaT_ref, sgbT_ref, wuvt_ref, wout_ref, fg_ref, y_ref, *, final):
    o_a = _dot(wuvt_ref[...], olatT_ref[...])
    mix_t = jnp.concatenate([o_a * sgaT_ref[...], obT_ref[...] * sgbT_ref[...]], axis=0).astype(BF16)
    y = x_ref[...] + lax.dot_general(mix_t, wout_ref[...], (((0,), (0,)), ((), ())),
                                     preferred_element_type=F32)
    if final:
        y = _rms(y, fg_ref[...])
    y_ref[...] = y


def _combine_t(x3d, olat_t, ob_t, pr, lw, fg, final):
    b, t, d = x3d.shape
    tm = TQ
    rows_t = lambda r: pl.BlockSpec((None, r, tm), lambda bi, ti: (bi, 0, ti))
    full = lambda a: pl.BlockSpec(a.shape, lambda bi, ti: (0,) * a.ndim)
    xblk = pl.BlockSpec((None, tm, d), lambda bi, ti: (bi, ti, 0))
    return pl.pallas_call(
        functools.partial(_combine_t_kernel, final=final),
        grid=(b, t // tm),
        in_specs=[xblk, rows_t(A_HEADS * A_KV_LORA), rows_t(B_WIDTH), rows_t(A_WIDTH), rows_t(B_WIDTH),
                  full(lw["wuvt"]), full(lw["wout"]), full(fg)],
        out_specs=xblk,
        out_shape=jax.ShapeDtypeStruct((b, t, d), F32),
        compiler_params=pltpu.CompilerParams(dimension_semantics=("arbitrary", "arbitrary"),
                                             vmem_limit_bytes=VMEM_LIMIT_BYTES),
        name="combine_prompt",
    )(x3d, olat_t, ob_t, pr["sgaT"], pr["sgbT"], lw["wuvt"], lw["wout"], fg)


def _layer_weights(norm_g, w_in, q_norm_g, kv_norm_g, w_uq, w_uk, w_uv, w_out):
    d = w_in.shape[0]
    o = np.cumsum([0, A_Q_LORA, A_KV_LORA, A_ROPE, A_WIDTH, B_WIDTH, B_KV_HEADS * B_HEAD_DIM,
                   B_KV_HEADS * B_HEAD_DIM, IDX_HEADS * IDX_DIM, IDX_DIM, IDX_HEADS, B_WIDTH])
    seg = lambda i: w_in[:, int(o[i]):int(o[i + 1])]
    misc = jnp.concatenate([seg(2), seg(8), seg(9),
                            jnp.zeros((d, LANES - A_ROPE - IDX_DIM - IDX_HEADS), w_in.dtype)], axis=1)
    win = jnp.concatenate([seg(0), seg(1), misc, seg(3), seg(4), seg(5), seg(6), seg(7), seg(10)], axis=1)
    assert win.shape[1] == IN_PAD
    half = A_ROPE // 2
    wq = jnp.concatenate([
        w_uq[:, :, :A_NOPE].reshape(A_Q_LORA, A_HEADS * A_NOPE),
        w_uq[:, :, A_NOPE:A_NOPE + half].reshape(A_Q_LORA, A_HEADS * half),
        w_uq[:, :, A_NOPE + half:].reshape(A_Q_LORA, A_HEADS * half)], axis=1)
    eye = jnp.eye(A_HEADS, dtype=w_uk.dtype)
    wuk = jnp.einsum('chn,hg->hngc', w_uk, eye).reshape(A_HEADS * A_NOPE, A_HEADS * A_KV_LORA)
    wuv = jnp.einsum('chv,hg->hcgv', w_uv, eye).reshape(A_HEADS * A_KV_LORA, A_HEADS * A_V)
    pm = np.zeros((2 * LANES, A_HEADS * LANES), np.float32)
    for h in range(A_HEADS):
        for i in range(half):
            pm[h * half + i, h * LANES + i] = 1.0
            pm[LANES + h * half + i, h * LANES + half + i] = 1.0
    zpad = lambda c: jnp.zeros((d, c), w_in.dtype)
    wk = jnp.concatenate([seg(1), seg(8), seg(2), zpad(LANES - IDX_DIM - A_ROPE), seg(5)], axis=1)
    wt = jnp.concatenate([seg(0), seg(1), seg(6), seg(4), seg(7), seg(3), seg(10), seg(9),
                          zpad(R_KB - R_WI - IDX_HEADS), seg(5), seg(8), seg(2)], axis=1).T
    assert wk.shape[1] == COLS_K and wt.shape[0] == ROWS_T
    bc = lambda g: jnp.broadcast_to(g.reshape(-1, 1), (g.shape[0], TQ))
    return {
        "wk": wk.astype(BF16), "wt": wt.astype(BF16), "qngc": bc(q_norm_g), "kvngc": bc(kv_norm_g),
        "wqt": wq.T.astype(BF16), "wukt": wuk.T.astype(BF16), "wuvt": wuv.T.astype(BF16),
        "ng": norm_g.reshape(1, -1), "win": win.astype(BF16),
        "qng": q_norm_g.reshape(1, -1), "kvng": kv_norm_g.reshape(1, -1),
        "wq": wq.astype(BF16), "wuk": wuk.astype(BF16), "wuv": wuv.astype(BF16),
        "pmat": jnp.asarray(pm, BF16), "wout": w_out.astype(BF16),
    }


def _rope_tables(pos):
    half = A_ROPE // 2
    inv = ROPE_THETA ** (-jnp.arange(half, dtype=F32) / half)
    ang = pos.astype(F32)[:, None] * inv[None, :]
    cos, sin = jnp.cos(ang), jnp.sin(ang)
    z = jnp.zeros((pos.shape[0], LANES - A_ROPE), F32)
    zh = jnp.zeros_like(cos)
    cosq = jnp.tile(cos, (1, A_HEADS))
    sinq = jnp.tile(sin, (1, A_HEADS))
    rc = jnp.concatenate([cos, cos, z], axis=1)
    rs1 = jnp.concatenate([zh, sin, z], axis=1)
    rs2 = jnp.concatenate([-sin, zh, z], axis=1)
    return cosq, sinq, rc, rs1, rs2


def _rope_tables_t(pos):
    half = A_ROPE // 2
    inv = ROPE_THETA ** (-jnp.arange(half, dtype=F32) / half)
    ang = pos.astype(F32)[:, None] * inv[None, :]
    cos, sin = jnp.cos(ang), jnp.sin(ang)
    cost = jnp.tile(cos.T, (A_HEADS, 1))
    sint = jnp.tile(sin.T, (A_HEADS, 1))
    z0 = jnp.zeros((pos.shape[0], MK_KPE), F32)
    z1 = jnp.zeros((pos.shape[0], LANES - MK_KPE - A_ROPE), F32)
    zh = jnp.zeros_like(cos)
    rc = jnp.concatenate([z0, cos, cos, z1], axis=1)
    rs1 = jnp.concatenate([z0, zh, sin, z1], axis=1)
    rs2 = jnp.concatenate([z0, -sin, zh, z1], axis=1)
    return cost, sint, rc, rs1, rs2


def kernel(x_prompt, x_sample, cache_mla_ckv, cache_mla_kpe, cache_dsa_k, cache_dsa_v, cache_dsa_kidx,
           norm_g, w_in, mla_q_norm_g, mla_kv_norm_g, mla_w_uq, mla_w_uk, mla_w_uv, rel_bias, w_out,
           final_norm_g):
    bp, tp, d = x_prompt.shape
    bs, ts, _ = x_sample.shape
    depth = w_in.shape[0]
    past = cache_mla_ckv.shape[2]
    n_top_p = min(TOP_K_MAX, tp // 4)
    n_top_s = min(TOP_K_MAX, (past + ts) // 4)
    assert ts <= CHUNK and past % CHUNK == 0 and past % LANES == 0

    rope_p = _rope_tables_t(jnp.arange(tp, dtype=jnp.int32))
    reps = PROJ_TM // ts
    rope_s = tuple(jnp.tile(a, (reps, 1)) for a in _rope_tables(past + jnp.arange(ts, dtype=jnp.int32)))

    bias_p = _bias_tables(rel_bias, (0, -TQ, -3 * TQ), TQ, TQ, True)
    win_s = 2 * LANES
    bias_s = _bias_tables(rel_bias, (-(win_s - LANES), -(past + win_s)), ts, win_s, False)
    fg = final_norm_g.reshape(1, -1)

    xp = x_prompt
    xs = x_sample.reshape(bs * ts, d)
    outs_p, outs_s = [], []
    for l in range(depth):
        lw = _layer_weights(norm_g[l], w_in[l], mla_q_norm_g[l], mla_kv_norm_g[l],
                            mla_w_uq[l], mla_w_uk[l], mla_w_uv[l], w_out[l])
        final = l == depth - 1
        pr = _project_t(xp.reshape(bp * tp, d), rope_p, lw, bp, tp)
        olat, ob = _prompt_attention(pr, bias_p, bp, tp, n_top_p)
        xp = _combine_t(xp, olat, ob, pr, lw, fg, final)
        heads_t = lambda a: a.reshape(bp, B_KV_HEADS, B_HEAD_DIM, tp).transpose(0, 3, 1, 2)
        outs_p.append((pr["ckv"].reshape(bp, tp, A_KV_LORA), pr["kpeT"].transpose(0, 2, 1),
                       heads_t(pr["kbT"]), heads_t(pr["vbT"]), pr["kidxT"].transpose(0, 2, 1)))
        ps = _project(xs, rope_s, lw, period=PROJ_TM)
        feat_t = lambda a: a.transpose(0, 2, 3, 1).reshape(bs, B_KV_HEADS * B_HEAD_DIM, past)
        caches = (cache_mla_ckv[l], cache_mla_kpe[l].transpose(0, 2, 1), feat_t(cache_dsa_k[l]),
                  feat_t(cache_dsa_v[l]), cache_dsa_kidx[l].transpose(0, 2, 1))
        olat, ob = _sample_attention(ps, caches, bias_s, bs, ts, past, n_top_s)
        xs = _combine(xs, olat, ob, ps, lw, fg, final)
        outs_s.append((ps["ckv"].reshape(bs, ts, A_KV_LORA), ps["kpe"].reshape(bs, ts, A_ROPE),
                       ps["kb"].reshape(bs, ts, B_KV_HEADS, B_HEAD_DIM),
                       ps["vb"].reshape(bs, ts, B_KV_HEADS, B_HEAD_DIM),
                       ps["kidx"].reshape(bs, ts, IDX_DIM)))

    stack = lambda outs, i: jnp.stack([o[i] for o in outs])
    return ((xp, xs.reshape(bs, ts, d))
            + tuple(stack(outs_p, i) for i in range(5)) + tuple(stack(outs_s, i) for i in range(5)))
```

```python
import functools
import math

import jax
import jax.numpy as jnp
import numpy as np
from jax import lax
from jax.experimental import pallas as pl
from jax.experimental.pallas import tpu as pltpu

F32 = jnp.float32
BF16 = jnp.bfloat16
I32 = jnp.int32

CHUNK = 64
EPS = 1e-6
A_HEADS = 8
A_NOPE = 64
A_ROPE = 32
A_V = 64
A_Q_LORA = 256
A_KV_LORA = 128
ROPE_THETA = 10000.0
A_WIDTH = A_HEADS * A_V
B_HEADS = 8
B_KV_HEADS = 2
B_HEAD_DIM = 64
B_WIDTH = B_HEADS * B_HEAD_DIM
B_GROUP = B_HEADS // B_KV_HEADS
IDX_HEADS = 8
IDX_DIM = 64
TOP_K_MAX = 256
N_BUCKETS = 32
MAX_DISTANCE = 128

LANES = 128
VMEM_LIMIT_BYTES = 56 * 1024 * 1024

LOG2E = 1.4426950408889634
NEG_BIG = -1e30
INT_MIN = -(2 ** 31)

C_CQ = 0
C_CKV = C_CQ + A_Q_LORA
C_MISC = C_CKV + A_KV_LORA
C_GA = C_MISC + LANES
C_QB = C_GA + A_WIDTH
C_KB = C_QB + B_WIDTH
C_VB = C_KB + B_KV_HEADS * B_HEAD_DIM
C_QI = C_VB + B_KV_HEADS * B_HEAD_DIM
C_GB = C_QI + IDX_HEADS * IDX_DIM
IN_PAD = C_GB + B_WIDTH
M_KPE = 0
M_KIDX = A_ROPE
M_WIDX = A_ROPE + IDX_DIM

QCAT = 2 * LANES
TQ = 256
PROJ_TM = 256


def _dot(a, b):
    return jnp.dot(a, b, preferred_element_type=F32)


def _dot_nt(a, b):
    return lax.dot_general(a, b, (((1,), (1,)), ((), ())), preferred_element_type=F32)


def _rms(x, g):
    return x * lax.rsqrt(jnp.mean(x * x, axis=-1, keepdims=True) + EPS) * g


def _bias_kernel(rb_ref, out_ref, *, offsets, keys_on_rows):
    nb = N_BUCKETS // 2
    max_exact = nb // 2
    n_r = out_ref.shape[1] if keys_on_rows else out_ref.shape[2]
    n_c = out_ref.shape[2] // B_HEADS if keys_on_rows else out_ref.shape[3]
    row = lax.broadcasted_iota(I32, (n_r, n_c), 0)
    col = lax.broadcasted_iota(I32, (n_r, n_c), 1)
    for p, off in enumerate(offsets):
        rel = off + (row - col if keys_on_rows else col - row)
        ret = jnp.where(rel > 0, nb, 0)
        n = jnp.abs(rel)
        nf = jnp.maximum(n, 1).astype(F32)
        large = max_exact + (jnp.log(nf / max_exact) / math.log(MAX_DISTANCE / max_exact)
                             * (nb - max_exact)).astype(I32)
        large = jnp.minimum(large, nb - 1)
        bucket = ret + jnp.where(n < max_exact, n, large)
        for h in range(B_HEADS):
            table = jnp.broadcast_to(rb_ref[h:h + 1, :], (n_r, LANES))
            acc = jnp.concatenate(
                [jnp.take_along_axis(table, bucket[:, c:c + LANES], axis=1) for c in range(0, n_c, LANES)],
                axis=1)
            if keys_on_rows:
                out_ref[p, :, h * n_c:(h + 1) * n_c] = acc * LOG2E
            else:
                out_ref[p, h] = acc * LOG2E


def _bias_tables(rel_bias, offsets, n_r, n_c, keys_on_rows):
    shape = (len(offsets), n_r, B_HEADS * n_c) if keys_on_rows else (len(offsets), B_HEADS, n_r, n_c)
    return pl.pallas_call(
        functools.partial(_bias_kernel, offsets=tuple(offsets), keys_on_rows=keys_on_rows),
        out_shape=jax.ShapeDtypeStruct(shape, F32),
        in_specs=[pl.BlockSpec(memory_space=pltpu.VMEM)],
        out_specs=pl.BlockSpec(memory_space=pltpu.VMEM),
        compiler_params=pltpu.CompilerParams(vmem_limit_bytes=VMEM_LIMIT_BYTES),
        name="bias_tables",
    )(jnp.pad(rel_bias.T, ((0, 0), (0, LANES - N_BUCKETS))))


def _proj_kernel(x_ref, ng_ref, win_ref, qng_ref, kvng_ref, wq_ref, wuk_ref, pmat_ref,
                 cosq_ref, sinq_ref, rc_ref, rs1_ref, rs2_ref,
                 ckv_ref, kpe_ref, kb_ref, vb_ref, kidx_ref,
                 kcat_ref, kb16_ref, vb16_ref, kidx16_ref,
                 qcat_ref, qb_ref, qidx_ref, widx_ref, sga_ref, sgb_ref):
    x = x_ref[...]
    h = _rms(x, ng_ref[...])
    z = _dot(h.astype(BF16), win_ref[...])

    cq = _rms(z[:, C_CQ:C_CQ + A_Q_LORA], qng_ref[...])
    q = _dot(cq.astype(BF16), wq_ref[...])
    n_nope = A_HEADS * A_NOPE
    x1 = q[:, n_nope:n_nope + LANES]
    x2 = q[:, n_nope + LANES:n_nope + 2 * LANES]
    cos8, sin8 = cosq_ref[...], sinq_ref[...]
    o1 = x1 * cos8 - x2 * sin8
    o2 = x1 * sin8 + x2 * cos8
    mla_scale = (A_NOPE + A_ROPE) ** -0.5 * LOG2E
    q_lat = _dot(q[:, :n_nope].astype(BF16), wuk_ref[...]) * mla_scale
    pe = jnp.concatenate([o1, o2], axis=1) * mla_scale
    q_pe = _dot(pe.astype(BF16), pmat_ref[...])
    for hh in range(A_HEADS):
        qcat_ref[:, hh * QCAT:hh * QCAT + LANES] = q_lat[:, hh * LANES:(hh + 1) * LANES].astype(BF16)
        qcat_ref[:, hh * QCAT + LANES:(hh + 1) * QCAT] = q_pe[:, hh * LANES:(hh + 1) * LANES].astype(BF16)

    ckv = _rms(z[:, C_CKV:C_CKV + A_KV_LORA], kvng_ref[...])
    ckv_ref[...] = ckv
    misc = z[:, C_MISC:C_MISC + LANES]
    rot = (misc * rc_ref[...] + pltpu.roll(misc, A_ROPE // 2, 1) * rs1_ref[...]
           + pltpu.roll(misc, LANES - A_ROPE // 2, 1) * rs2_ref[...])
    kpe_ref[...] = rot[:, :A_ROPE]
    kcat_ref[:, :LANES] = ckv.astype(BF16)
    kcat_ref[:, LANES:] = rot.astype(BF16)

    kidx = misc[:, M_KIDX:M_KIDX + IDX_DIM]
    kidx_ref[...] = kidx
    kidx16_ref[...] = kidx.astype(BF16)
    widx_ref[...] = misc[:, M_WIDX:M_WIDX + IDX_HEADS] * (IDX_HEADS ** -0.5)
    kb = z[:, C_KB:C_KB + LANES]
    vb = z[:, C_VB:C_VB + LANES]
    kb_ref[...] = kb
    vb_ref[...] = vb
    kb16_ref[...] = kb.astype(BF16)
    vb16_ref[...] = vb.astype(BF16)
    qb_ref[...] = (z[:, C_QB:C_QB + B_WIDTH] * (B_HEAD_DIM ** -0.5 * LOG2E)).astype(BF16)
    qidx_ref[...] = (z[:, C_QI:C_QI + IDX_HEADS * IDX_DIM] * (IDX_DIM ** -0.5)).astype(BF16)
    sga_ref[...] = jax.nn.silu(z[:, C_GA:C_GA + A_WIDTH])
    sgb_ref[...] = jax.nn.silu(z[:, C_GB:C_GB + B_WIDTH])


def _project(x2d, rope_tabs, lw, *, period):
    n, d = x2d.shape
    tm = PROJ_TM
    assert n % tm == 0 and period % tm == 0
    n_rep = period // tm
    tok = lambda c: pl.BlockSpec((tm, c), lambda i: (i, 0))
    full = lambda a: pl.BlockSpec(a.shape, lambda i: (0,) * a.ndim)
    tab = pl.BlockSpec((tm, LANES), lambda i: (i % n_rep, 0))
    outs = [
        ("ckv", A_KV_LORA, F32), ("kpe", A_ROPE, F32), ("kb", LANES, F32), ("vb", LANES, F32),
        ("kidx", IDX_DIM, F32),
        ("kcat", QCAT, BF16), ("kb16", LANES, BF16), ("vb16", LANES, BF16), ("kidx16", IDX_DIM, BF16),
        ("qcat", A_HEADS * QCAT, BF16), ("qb", B_WIDTH, BF16), ("qidx", IDX_HEADS * IDX_DIM, BF16),
        ("widx", IDX_HEADS, F32), ("sga", A_WIDTH, F32), ("sgb", B_WIDTH, F32),
    ]
    res = pl.pallas_call(
        _proj_kernel,
        grid=(n // tm,),
        in_specs=[tok(d), full(lw["ng"]), full(lw["win"]), full(lw["qng"]), full(lw["kvng"]),
                  full(lw["wq"]), full(lw["wuk"]), full(lw["pmat"]), tab, tab, tab, tab, tab],
        out_specs=[tok(c) for _, c, _ in outs],
        out_shape=[jax.ShapeDtypeStruct((n, c), dt) for _, c, dt in outs],
        compiler_params=pltpu.CompilerParams(dimension_semantics=("arbitrary",),
                                             vmem_limit_bytes=VMEM_LIMIT_BYTES),
        name="project",
    )(x2d, lw["ng"], lw["win"], lw["qng"], lw["kvng"], lw["wq"], lw["wuk"], lw["pmat"], *rope_tabs)
    return {name: r for (name, _, _), r in zip(outs, res)}


R_CQ = 0
R_CKV = R_CQ + A_Q_LORA
R_VB = R_CKV + A_KV_LORA
R_QB = R_VB + LANES
R_QI = R_QB + B_WIDTH
R_GA = R_QI + IDX_HEADS * IDX_DIM
R_GB = R_GA + A_WIDTH
R_WI = R_GB + B_WIDTH
R_KB = R_WI + 16
R_KI = R_KB + LANES
R_KPE = R_KI + IDX_DIM
ROWS_T = R_KPE + A_ROPE
K_MISC = A_KV_LORA
K_KB = K_MISC + LANES
COLS_K = K_KB + LANES
MK_KPE = IDX_DIM
ONES_ROWS = 16
V_EXT = A_KV_LORA + ONES_ROWS


def _proj_t_kernel(x_ref, ng_ref, wk_ref, wt_ref, qng_ref, kvngc_ref, kvng_ref, wqt_ref, wukt_ref,
                   cost_ref, sint_ref, rc_ref, rs1_ref, rs2_ref,
                   ckv_ref, kcat_ref, kb16_ref, kidx16_ref,
                   kpeT_ref, kbT_ref, vbT_ref, kidxT_ref,
                   qcatT_ref, qbT_ref, qidxT_ref, widxT_ref, vmlaT_ref, vdsaT_ref, sgaT_ref, sgbT_ref):
    x = x_ref[...]
    tm = x.shape[0]
    hb = _rms(x, ng_ref[...]).astype(BF16)

    zk = _dot(hb, wk_ref[...])
    ckv = _rms(zk[:, :A_KV_LORA], kvng_ref[...])
    ckv_ref[...] = ckv
    misc = zk[:, K_MISC:K_MISC + LANES]
    lane = lax.broadcasted_iota(I32, (tm, LANES), 1)
    kidx16_ref[...] = jnp.where(lane < IDX_DIM, misc, 0.0).astype(BF16)
    rot = (misc * rc_ref[...] + pltpu.roll(misc, A_ROPE // 2, 1) * rs1_ref[...]
           + pltpu.roll(misc, LANES - A_ROPE // 2, 1) * rs2_ref[...])
    kcat_ref[:, :LANES] = ckv.astype(BF16)
    kcat_ref[:, LANES:] = pltpu.roll(rot, LANES - MK_KPE, 1).astype(BF16)
    kb16_ref[...] = zk[:, K_KB:K_KB + LANES].astype(BF16)

    zt = _dot_nt(wt_ref[...], hb)

    def rms_t(c, g):
        return c * lax.rsqrt(jnp.mean(c * c, axis=0, keepdims=True) + EPS) * g

    cq = rms_t(zt[R_CQ:R_CQ + A_Q_LORA], qng_ref[...])
    qt = _dot(wqt_ref[...], cq.astype(BF16))
    n_nope = A_HEADS * A_NOPE
    x1 = qt[n_nope:n_nope + LANES]
    x2 = qt[n_nope + LANES:n_nope + 2 * LANES]
    cos8, sin8 = cost_ref[...], sint_ref[...]
    mla_scale = (A_NOPE + A_ROPE) ** -0.5 * LOG2E
    o1 = (x1 * cos8 - x2 * sin8) * mla_scale
    o2 = (x1 * sin8 + x2 * cos8) * mla_scale
    q_lat = _dot(wukt_ref[...], qt[:n_nope].astype(BF16)) * mla_scale
    half = A_ROPE // 2
    for h in range(A_HEADS):
        qcatT_ref[h, :LANES, :] = q_lat[h * LANES:(h + 1) * LANES].astype(BF16)
        qcatT_ref[h, LANES:LANES + half, :] = o1[h * half:(h + 1) * half].astype(BF16)
        qcatT_ref[h, LANES + half:LANES + A_ROPE, :] = o2[h * half:(h + 1) * half].astype(BF16)
        qcatT_ref[h, LANES + A_ROPE:, :] = jnp.zeros((QCAT - LANES - A_ROPE, tm), BF16)
    ones = jnp.ones((ONES_ROWS, tm), BF16)
    vmlaT_ref[:A_KV_LORA, :] = rms_t(zt[R_CKV:R_CKV + A_KV_LORA], kvngc_ref[...]).astype(BF16)
    vmlaT_ref[A_KV_LORA:, :] = ones
    vb_t = zt[R_VB:R_VB + LANES]
    vdsaT_ref[:LANES, :] = vb_t.astype(BF16)
    vdsaT_ref[LANES:, :] = ones
    vbT_ref[...] = vb_t
    kbT_ref[...] = zt[R_KB:R_KB + LANES]
    kidxT_ref[...] = zt[R_KI:R_KI + IDX_DIM]
    k1, k2 = zt[R_KPE:R_KPE + half], zt[R_KPE + half:R_KPE + A_ROPE]
    cos1, sin1 = cos8[:half], sin8[:half]
    kpeT_ref[:half, :] = k1 * cos1 - k2 * sin1
    kpeT_ref[half:, :] = k1 * sin1 + k2 * cos1
    qbT_ref[...] = (zt[R_QB:R_QB + B_WIDTH] * (B_HEAD_DIM ** -0.5 * LOG2E)).astype(BF16)
    qidxT_ref[...] = (zt[R_QI:R_QI + IDX_HEADS * IDX_DIM] * (IDX_DIM ** -0.5)).astype(BF16)
    sgaT_ref[...] = jax.nn.silu(zt[R_GA:R_GA + A_WIDTH]).astype(BF16)
    sgbT_ref[...] = jax.nn.silu(zt[R_GB:R_GB + B_WIDTH]).astype(BF16)
    widxT_ref[...] = zt[R_WI:R_WI + IDX_HEADS] * (IDX_HEADS ** -0.5)


def _project_t(x2d, tabs, lw, b, t):
    n, d = x2d.shape
    tm = TQ
    nt = t // tm
    tok = lambda c: pl.BlockSpec((tm, c), lambda i: (i, 0))
    full = lambda a: pl.BlockSpec(a.shape, lambda i: (0,) * a.ndim)
    tab_t = pl.BlockSpec((LANES, tm), lambda i: (0, i % nt))
    tab_k = pl.BlockSpec((tm, LANES), lambda i: (i % nt, 0))
    rows_t = lambda r: pl.BlockSpec((None, r, tm), lambda i: (i // nt, 0, i % nt))
    outs = [
        ("ckv", (n, A_KV_LORA), F32, tok(A_KV_LORA)),
        ("kcat", (n, QCAT), BF16, tok(QCAT)), ("kb16", (n, LANES), BF16, tok(LANES)),
        ("kidx16", (n, LANES), BF16, tok(LANES)),
        ("kpeT", (b, A_ROPE, t), F32, rows_t(A_ROPE)), ("kbT", (b, LANES, t), F32, rows_t(LANES)),
        ("vbT", (b, LANES, t), F32, rows_t(LANES)), ("kidxT", (b, IDX_DIM, t), F32, rows_t(IDX_DIM)),
        ("qcatT", (b, A_HEADS, QCAT, t), BF16,
         pl.BlockSpec((None, A_HEADS, QCAT, tm), lambda i: (i // nt, 0, 0, i % nt))),
        ("qbT", (b, B_WIDTH, t), BF16, rows_t(B_WIDTH)),
        ("qidxT", (b, IDX_HEADS * IDX_DIM, t), BF16, rows_t(IDX_HEADS * IDX_DIM)),
        ("widxT", (b, IDX_HEADS, t), F32, rows_t(IDX_HEADS)),
        ("vmlaT", (b, nt, V_EXT, tm), BF16, pl.BlockSpec((None, None, V_EXT, tm), lambda i: (i // nt, i % nt, 0, 0))),
        ("vdsaT", (b, nt, V_EXT, tm), BF16, pl.BlockSpec((None, None, V_EXT, tm), lambda i: (i // nt, i % nt, 0, 0))),
        ("sgaT", (b, A_WIDTH, t), BF16, rows_t(A_WIDTH)), ("sgbT", (b, B_WIDTH, t), BF16, rows_t(B_WIDTH)),
    ]
    cost, sint, rc, rs1, rs2 = tabs
    res = pl.pallas_call(
        _proj_t_kernel,
        grid=(n // tm,),
        in_specs=[tok(d), full(lw["ng"]), full(lw["wk"]), full(lw["wt"]), full(lw["qngc"]), full(lw["kvngc"]),
                  full(lw["kvng"]), full(lw["wqt"]), full(lw["wukt"]), tab_t, tab_t, tab_k, tab_k, tab_k],
        out_specs=[o[3] for o in outs],
        out_shape=[jax.ShapeDtypeStruct(o[1], o[2]) for o in outs],
        compiler_params=pltpu.CompilerParams(dimension_semantics=("arbitrary",),
                                             vmem_limit_bytes=VMEM_LIMIT_BYTES),
        name="project_prompt",
    )(x2d, lw["ng"], lw["wk"], lw["wt"], lw["qngc"], lw["kvngc"], lw["kvng"], lw["wqt"], lw["wukt"],
      cost, sint, rc, rs1, rs2)
    return {o[0]: r for o, r in zip(outs, res)}


NEG_FLT_MAX = -3.4028234663852886e38
KEY_NEG_FLT_MAX = INT_MIN + (1 << 23)


def _key_to_float(k):
    k = jnp.maximum(k, KEY_NEG_FLT_MAX)
    return pltpu.bitcast(k ^ ((k >> 31) & 0x7FFFFFFF), F32)


def _count(pred):
    return jnp.sum(jnp.where(pred, 1.0, 0.0), axis=1, keepdims=True)


MAX_TIE_SWEEPS = 8.0
HEAD_GROUPS = 1


def _fold8(x, op=jnp.add):
    parts = [x[i:i + 8] for i in range(0, x.shape[0], 8)]
    while len(parts) > 1:
        parts = [op(a, b) for a, b in zip(parts[::2], parts[1::2])]
    return parts[0]


def _prompt_attn_kernel(qcatT_ref, qbT_ref, qidxT_ref, widxT_ref, kcat_ref, kb_ref, kidx_ref,
                        vmlaT_ref, vdsaT_ref, bias_ref, olatT_ref, obT_ref,
                        qa_ref, qbp_ref, qip_ref, sc_ref, m_ref, acc_ref, mb_ref, accb_ref, *, n_top):
    qi = pl.program_id(1)
    nblk = qi + 1
    tq = TQ
    krow = lax.broadcasted_iota(I32, (tq, tq), 0)
    qcol = lax.broadcasted_iota(I32, (tq, tq), 1)
    shift = CHUNK.bit_length() - 1
    diag_ok = (qcol >> shift) >= (krow >> shift)
    hcols = lambda h: slice(h * tq, (h + 1) * tq)

    def per_head(fn):
        return jnp.concatenate([fn(h) for h in range(A_HEADS)], axis=1)

    qbp_ref[...] = jnp.zeros(qbp_ref.shape, BF16)
    qip_ref[...] = jnp.zeros(qip_ref.shape, BF16)
    for h in range(A_HEADS):
        g = h // B_GROUP
        qa_ref[:, hcols(h)] = qcatT_ref[h]
        qbp_ref[g * B_HEAD_DIM:(g + 1) * B_HEAD_DIM, hcols(h)] = qbT_ref[h * B_HEAD_DIM:(h + 1) * B_HEAD_DIM, :]
        qip_ref[:IDX_DIM, hcols(h)] = qidxT_ref[h * IDX_DIM:(h + 1) * IDX_DIM, :]
    w_all = per_head(lambda h: widxT_ref[h:h + 1, :])

    def flash_update(s_t, v_t, m_r, acc_r, cols):
        m_prev = m_r[0:1, cols]
        m_new = jnp.maximum(m_prev, jnp.max(s_t, axis=0, keepdims=True))
        alpha = jnp.exp2(m_prev - m_new)
        p_t = jnp.exp2(s_t - m_new).astype(BF16)
        acc_r[:, cols] = alpha * acc_r[:, cols] + _dot(v_t, p_t)
        m_r[0:1, cols] = m_new

    m_ref[...] = jnp.full(m_ref.shape, NEG_BIG, F32)
    acc_ref[...] = jnp.zeros(acc_ref.shape, F32)
    hpg = A_HEADS // HEAD_GROUPS

    def mla_block(j, masked):
        start = pl.multiple_of(j * tq, tq)
        kblk = kcat_ref[pl.ds(start, tq), :]
        for gi in range(HEAD_GROUPS):
            cols = slice(gi * hpg * tq, (gi + 1) * hpg * tq)
            s_t = _dot(kblk, qa_ref[:, cols])
            if masked:
                s_t = jnp.concatenate(
                    [jnp.where(diag_ok, s_t[:, hcols(h)], NEG_BIG) for h in range(hpg)], axis=1)
            flash_update(s_t, vmlaT_ref[j], m_ref, acc_ref, cols)
        r = jnp.maximum(_dot(kidx_ref[pl.ds(start, tq), :], qip_ref[...]), 0.0) * w_all
        score = r[:, hcols(0)]
        for h in range(1, IDX_HEADS):
            score = score + r[:, hcols(h)]
        if masked:
            score = jnp.where(diag_ok, score, -jnp.inf)
        sc_ref[j] = score

    def for_blocks(n, block):
        def pair(p, c):
            block(2 * p)
            block(2 * p + 1)
            return c

        lax.fori_loop(0, lax.shift_right_logical(n, 1), pair, 0)

        @pl.when((n & 1) == 1)
        def _():
            block(n - 1)

    for_blocks(qi, lambda j: mla_block(j, False))
    mla_block(qi, True)
    o_t = acc_ref[:A_KV_LORA, :] * (1.0 / acc_ref[A_KV_LORA:A_KV_LORA + 1, :])
    for h in range(A_HEADS):
        olatT_ref[h * LANES:(h + 1) * LANES, :] = o_t[:, hcols(h)].astype(olatT_ref.dtype)

    def count(pred):
        def body(j, c):
            return c + _fold8(jnp.where(pred(sc_ref[j], j), 1.0, 0.0))
        part = lax.fori_loop(0, nblk, body, jnp.zeros((8, tq), F32))
        return jnp.sum(part, axis=0, keepdims=True)

    kf = float(n_top)

    def bis_body(it, carry):
        lo, cnt_lo = carry
        cand = lo + lax.shift_left(jnp.int32(1), 31 - it)
        cand_f = _key_to_float(cand)
        cnt = count(lambda s, j: s >= cand_f)
        take = cnt >= kf
        return jnp.where(take, cand, lo), jnp.where(take, cnt, cnt_lo)

    lo, cnt_ge = lax.fori_loop(
        0, 32, bis_body, (jnp.full((1, tq), INT_MIN, I32), jnp.full((1, tq), 1e9, F32)))
    few = lo == INT_MIN
    thr = _key_to_float(lo)
    excess0 = jnp.where(few, 0.0, cnt_ge - kf)
    max_excess = jnp.max(excess0)

    def drop_from(cut):
        def body(j, c):
            s = sc_ref[j]
            sc_ref[j] = jnp.where((s == thr) & ((krow + j * tq) >= cut), -jnp.inf, s)
            return c
        lax.fori_loop(0, nblk, body, 0)

    @pl.when((max_excess > 0.0) & (max_excess <= MAX_TIE_SWEEPS))
    def _():
        def last_tie_below(cut):
            def body(j, m):
                pos = krow + j * tq
                hit = jnp.where((sc_ref[j] == thr) & (pos < cut), pos, -1)
                return jnp.maximum(m, _fold8(hit, jnp.maximum))
            part = lax.fori_loop(0, nblk, body, jnp.full((8, tq), -1, I32))
            return jnp.max(part, axis=0, keepdims=True)

        def sweep(c):
            excess, cut = c
            last = last_tie_below(cut)
            live = excess > 0.0
            return jnp.where(live, excess - 1.0, excess), jnp.where(live, last, cut)

        _, cut = lax.while_loop(lambda c: jnp.max(c[0]) > 0.0, sweep,
                                (excess0, jnp.full((1, tq), 2 ** 30, I32)))
        drop_from(cut)

    @pl.when(max_excess > MAX_TIE_SWEEPS)
    def _():
        need = kf - count(lambda s, j: s > thr)
        n_bits = (sc_ref.shape[0] * tq).bit_length()

        def cut_body(it, cpos):
            cand = cpos + lax.shift_left(jnp.int32(1), n_bits - 1 - it)
            cnt = count(lambda s, j: (s == thr) & ((krow + j * tq) < cand))
            return jnp.where(cnt < need, cand, cpos)

        keep = lax.fori_loop(0, n_bits, cut_body, jnp.zeros((1, tq), I32))
        drop_from(jnp.where(excess0 > 0.0, keep + 1, 2 ** 30))

    mb_ref[...] = jnp.full(mb_ref.shape, NEG_BIG, F32)
    accb_ref[...] = jnp.zeros(accb_ref.shape, F32)

    def dsa_block(j):
        start = pl.multiple_of(j * tq, tq)
        sel = sc_ref[j] >= thr
        pat = jnp.minimum(qi - j, 2)
        kblk = kb_ref[pl.ds(start, tq), :]
        for gi in range(HEAD_GROUPS):
            cols = slice(gi * hpg * tq, (gi + 1) * hpg * tq)
            s_t = _dot(kblk, qbp_ref[:, cols]) + bias_ref[pat, :, cols]
            s_t = jnp.concatenate([jnp.where(sel, s_t[:, hcols(h)], NEG_BIG) for h in range(hpg)], axis=1)
            flash_update(s_t, vdsaT_ref[j], mb_ref, accb_ref, cols)

    for_blocks(nblk, dsa_block)
    inv_b = 1.0 / accb_ref[LANES:LANES + 1, :]
    for h in range(B_HEADS):
        g = h // B_GROUP
        obT_ref[h * B_HEAD_DIM:(h + 1) * B_HEAD_DIM, :] = (
            accb_ref[g * B_HEAD_DIM:(g + 1) * B_HEAD_DIM, hcols(h)] * inv_b[:, hcols(h)]).astype(obT_ref.dtype)


def _prompt_attention(pr, bias_p, b, t, n_top):
    tq = TQ
    assert t % tq == 0
    nq = t // tq
    r3 = lambda a: a.reshape(b, t, a.shape[-1])
    qrows = lambda r: pl.BlockSpec((None, r, tq), lambda bi, qi: (bi, 0, qi))
    kall = lambda c: pl.BlockSpec((None, t, c), lambda bi, qi: (bi, 0, 0))
    vall = pl.BlockSpec((None, nq, V_EXT, tq), lambda bi, qi: (bi, 0, 0, 0))
    olat, ob = pl.pallas_call(
        functools.partial(_prompt_attn_kernel, n_top=n_top),
        grid=(b, nq),
        in_specs=[pl.BlockSpec((None, A_HEADS, QCAT, tq), lambda bi, qi: (bi, 0, 0, qi)),
                  qrows(B_WIDTH), qrows(IDX_HEADS * IDX_DIM), qrows(IDX_HEADS),
                  kall(QCAT), kall(LANES), kall(LANES), vall, vall,
                  pl.BlockSpec(bias_p.shape, lambda bi, qi: (0, 0, 0))],
        out_specs=[qrows(A_HEADS * A_KV_LORA), qrows(B_WIDTH)],
        out_shape=[jax.ShapeDtypeStruct((b, A_HEADS * A_KV_LORA, t), BF16),
                   jax.ShapeDtypeStruct((b, B_WIDTH, t), BF16)],
        scratch_shapes=[
            pltpu.VMEM((QCAT, A_HEADS * tq), BF16),
            pltpu.VMEM((LANES, B_HEADS * tq), BF16),
            pltpu.VMEM((LANES, IDX_HEADS * tq), BF16),
            pltpu.VMEM((nq, tq, tq), F32),
            pltpu.VMEM((8, A_HEADS * tq), F32),
            pltpu.VMEM((V_EXT, A_HEADS * tq), F32),
            pltpu.VMEM((8, B_HEADS * tq), F32),
            pltpu.VMEM((V_EXT, B_HEADS * tq), F32),
        ],
        compiler_params=pltpu.CompilerParams(dimension_semantics=("arbitrary", "arbitrary"),
                                             vmem_limit_bytes=VMEM_LIMIT_BYTES),
        name="prompt_attention",
    )(pr["qcatT"], pr["qbT"], pr["qidxT"], pr["widxT"],
      r3(pr["kcat"]), r3(pr["kb16"]), r3(pr["kidx16"]), pr["vmlaT"], pr["vdsaT"], bias_p)
    return olat, ob


def _sample_attn_kernel(qcat_ref, qb_ref, qidx_ref, widx_ref, kcatn_ref, kbn_ref, vbn_ref, kidxn_ref,
                        cckv_ref, ckpeT_ref, ckT_ref, cvT_ref, ckidxT_ref, bias_ref,
                        olat_ref, ob_ref, *, n_top, t_new, past):
    tq = t_new
    pad = LANES
    n_keys = past + pad
    colmask = lax.broadcasted_iota(I32, (tq, n_keys), 1) < past + t_new

    def padrows(a):
        return jnp.concatenate([a, jnp.zeros((pad - t_new, a.shape[1]), a.dtype)], axis=0)

    ckv_c = cckv_ref[...].astype(BF16)
    kpe_t = ckpeT_ref[...].astype(BF16)
    k_t = ckT_ref[...].astype(BF16)
    v_t = cvT_ref[...].astype(BF16)
    kidx_t = ckidxT_ref[...].astype(BF16)
    kcat_n = padrows(kcatn_ref[...])
    k_n = padrows(kbn_ref[...])
    v_n = padrows(vbn_ref[...])
    kidx_n = padrows(kidxn_ref[...])

    def softmax(s):
        m = jnp.max(s, axis=1, keepdims=True)
        p = jnp.exp2(s - m)
        return p.astype(BF16), jnp.sum(p, axis=1, keepdims=True)

    qs = jnp.concatenate([qcat_ref[:, h * QCAT:(h + 1) * QCAT] for h in range(A_HEADS)], axis=0)
    s_c = _dot_nt(qs[:, :A_KV_LORA], ckv_c) + _dot(qs[:, A_KV_LORA:A_KV_LORA + A_ROPE], kpe_t)
    s = jnp.concatenate([s_c, _dot_nt(qs, kcat_n)], axis=1)
    cm8 = lax.broadcasted_iota(I32, (A_HEADS * tq, n_keys), 1) < past + t_new
    pb, l = softmax(jnp.where(cm8, s, NEG_BIG))
    o = (_dot(pb[:, :past], ckv_c) + _dot(pb[:, past:], kcat_n[:, :A_KV_LORA])) / l
    for h in range(A_HEADS):
        olat_ref[:, h * LANES:(h + 1) * LANES] = o[h * tq:(h + 1) * tq].astype(olat_ref.dtype)

    qis = jnp.concatenate([qidx_ref[:, h * IDX_DIM:(h + 1) * IDX_DIM] for h in range(IDX_HEADS)], axis=0)
    dots = jnp.concatenate([_dot(qis, kidx_t), _dot_nt(qis, kidx_n)], axis=1)
    score = jnp.zeros((tq, n_keys), F32)
    for h in range(IDX_HEADS):
        score = score + jnp.maximum(dots[h * tq:(h + 1) * tq], 0.0) * widx_ref[:, h:h + 1]
    score = jnp.where(colmask, score, -jnp.inf)
    kf = float(n_top)

    def bis_body(it, carry):
        lo, cnt_lo = carry
        cand = lo + lax.shift_left(jnp.int32(1), 31 - it)
        cnt = _count(score >= _key_to_float(cand))
        take = cnt >= kf
        return jnp.where(take, cand, lo), jnp.where(take, cnt, cnt_lo)

    lo, cnt_ge = lax.fori_loop(
        0, 32, bis_body, (jnp.full((tq, 1), INT_MIN, I32), jnp.full((tq, 1), 1e9, F32)), unroll=True)
    few = lo == INT_MIN
    thr = _key_to_float(lo)
    cols = lax.broadcasted_iota(I32, (tq, n_keys), 1)
    has_tie = jnp.max(jnp.where((cnt_ge > kf) & (~few), 1.0, 0.0))
    n_bits = n_keys.bit_length()

    def tie_path():
        need = kf - _count(score > thr)
        eq = score == thr

        def cut_body(it, cpos):
            cand = cpos + lax.shift_left(jnp.int32(1), n_bits - 1 - it)
            cnt = _count(eq & (cols < cand))
            return jnp.where(cnt < need, cand, cpos)

        return lax.fori_loop(0, n_bits, cut_body, jnp.zeros((tq, 1), I32))

    cut = lax.cond(has_tie > 0.0, tie_path, lambda: jnp.full((tq, 1), 2 ** 30, I32))
    sel = (score > thr) | ((score == thr) & (cols <= cut))

    n_far = n_keys - bias_ref.shape[-1]
    for g in range(B_KV_HEADS):
        lanes = slice(g * B_HEAD_DIM, (g + 1) * B_HEAD_DIM)
        qg = jnp.concatenate(
            [qb_ref[:, h * B_HEAD_DIM:(h + 1) * B_HEAD_DIM] for h in range(g * B_GROUP, (g + 1) * B_GROUP)],
            axis=0)
        sg = jnp.concatenate([_dot(qg, k_t[lanes]), _dot_nt(qg, k_n[:, lanes])], axis=1)
        for hh in range(B_GROUP):
            h = g * B_GROUP + hh
            near = bias_ref[0, h]
            far = bias_ref[1, h][:, :1]
            sh = sg[hh * tq:(hh + 1) * tq]
            sh = jnp.concatenate([sh[:, :n_far] + far, sh[:, n_far:] + near], axis=1)
            pb, l = softmax(jnp.where(sel, sh, NEG_BIG))
            oh = (_dot_nt(pb[:, :past], v_t[lanes]) + _dot(pb[:, past:], v_n[:, lanes])) / l
            ob_ref[:, h * B_HEAD_DIM:(h + 1) * B_HEAD_DIM] = oh


def _sample_attention(pr, caches, bias_s, b, t_new, past, n_top):
    r3 = lambda a: a.reshape(b, t_new, a.shape[-1])
    per_b = lambda n, c: pl.BlockSpec((None, n, c), lambda bi: (bi, 0, 0))
    news = [pr["qcat"], pr["qb"], pr["qidx"], pr["widx"], pr["kcat"], pr["kb16"], pr["vb16"], pr["kidx16"]]
    olat, ob = pl.pallas_call(
        functools.partial(_sample_attn_kernel, n_top=n_top, t_new=t_new, past=past),
        grid=(b,),
        in_specs=[per_b(t_new, a.shape[-1]) for a in news]
                 + [per_b(c.shape[1], c.shape[2]) for c in caches]
                 + [pl.BlockSpec(bias_s.shape, lambda bi: (0, 0, 0, 0))],
        out_specs=[per_b(t_new, A_HEADS * A_KV_LORA), per_b(t_new, B_WIDTH)],
        out_shape=[jax.ShapeDtypeStruct((b, t_new, A_HEADS * A_KV_LORA), BF16),
                   jax.ShapeDtypeStruct((b, t_new, B_WIDTH), F32)],
        compiler_params=pltpu.CompilerParams(dimension_semantics=("arbitrary",),
                                             vmem_limit_bytes=VMEM_LIMIT_BYTES),
        name="sample_attention",
    )(*[r3(a) for a in news], *caches, bias_s)
    return olat.reshape(b * t_new, -1), ob.reshape(b * t_new, -1)


def _combine_kernel(x_ref, olat_ref, ob_ref, sga_ref, sgb_ref, wuv_ref, wout_ref, fg_ref, y_ref, *, final):
    o_a = _dot(olat_ref[...], wuv_ref[...])
    mix = jnp.concatenate([o_a * sga_ref[...], ob_ref[...] * sgb_ref[...]], axis=1)
    y = x_ref[...] + _dot(mix.astype(BF16), wout_ref[...])
    if final:
        y = _rms(y, fg_ref[...])
    y_ref[...] = y


def _combine(x2d, olat, ob, pr, lw, fg, final):
    n, d = x2d.shape
    tm = PROJ_TM
    tok = lambda c: pl.BlockSpec((tm, c), lambda i: (i, 0))
    full = lambda a: pl.BlockSpec(a.shape, lambda i: (0,) * a.ndim)
    return pl.pallas_call(
        functools.partial(_combine_kernel, final=final),
        grid=(n // tm,),
        in_specs=[tok(d), tok(olat.shape[1]), tok(ob.shape[1]), tok(A_WIDTH), tok(B_WIDTH),
                  full(lw["wuv"]), full(lw["wout"]), full(fg)],
        out_specs=tok(d),
        out_shape=jax.ShapeDtypeStruct((n, d), F32),
        compiler_params=pltpu.CompilerParams(dimension_semantics=("arbitrary",),
                                             vmem_limit_bytes=VMEM_LIMIT_BYTES),
        name="combine",
    )(x2d, olat, ob, pr["sga"], pr["sgb"], lw["wuv"], lw["wout"], fg)


def _combine_t_kernel(x_ref, olatT_ref, obT_ref, sgaT_ref, sgbT_ref, wuvt_ref, wout_ref, fg_ref, y_ref, *, final):
    o_a = _dot(wuvt_ref[...], olatT_ref[...])
    mix_t = jnp.concatenate([o_a * sgaT_ref[...].astype(F32),
                             obT_ref[...].astype(F32) * sgbT_ref[...].astype(F32)], axis=0).astype(BF16)
    y = x_ref[...] + lax.dot_general(mix_t, wout_ref[...], (((0,), (0,)), ((), ())),
                                     preferred_element_type=F32)
    if final:
        y = _rms(y, fg_ref[...])
    y_ref[...] = y


def _combine_t(x3d, olat_t, ob_t, pr, lw, fg, final):
    b, t, d = x3d.shape
    tm = TQ
    rows_t = lambda r: pl.BlockSpec((None, r, tm), lambda bi, ti: (bi, 0, ti))
    full = lambda a: pl.BlockSpec(a.shape, lambda bi, ti: (0,) * a.ndim)
    xblk = pl.BlockSpec((None, tm, d), lambda bi, ti: (bi, ti, 0))
    return pl.pallas_call(
        functools.partial(_combine_t_kernel, final=final),
        grid=(b, t // tm),
        in_specs=[xblk, rows_t(A_HEADS * A_KV_LORA), rows_t(B_WIDTH), rows_t(A_WIDTH), rows_t(B_WIDTH),
                  full(lw["wuvt"]), full(lw["wout"]), full(fg)],
        out_specs=xblk,
        out_shape=jax.ShapeDtypeStruct((b, t, d), F32),
        compiler_params=pltpu.CompilerParams(dimension_semantics=("arbitrary", "arbitrary"),
                                             vmem_limit_bytes=VMEM_LIMIT_BYTES),
        name="combine_prompt",
    )(x3d, olat_t, ob_t, pr["sgaT"], pr["sgbT"], lw["wuvt"], lw["wout"], fg)


def _layer_weights(norm_g, w_in, q_norm_g, kv_norm_g, w_uq, w_uk, w_uv, w_out):
    d = w_in.shape[0]
    o = np.cumsum([0, A_Q_LORA, A_KV_LORA, A_ROPE, A_WIDTH, B_WIDTH, B_KV_HEADS * B_HEAD_DIM,
                   B_KV_HEADS * B_HEAD_DIM, IDX_HEADS * IDX_DIM, IDX_DIM, IDX_HEADS, B_WIDTH])
    seg = lambda i: w_in[:, int(o[i]):int(o[i + 1])]
    misc = jnp.concatenate([seg(2), seg(8), seg(9),
                            jnp.zeros((d, LANES - A_ROPE - IDX_DIM - IDX_HEADS), w_in.dtype)], axis=1)
    win = jnp.concatenate([seg(0), seg(1), misc, seg(3), seg(4), seg(5), seg(6), seg(7), seg(10)], axis=1)
    assert win.shape[1] == IN_PAD
    half = A_ROPE // 2
    wq = jnp.concatenate([
        w_uq[:, :, :A_NOPE].reshape(A_Q_LORA, A_HEADS * A_NOPE),
        w_uq[:, :, A_NOPE:A_NOPE + half].reshape(A_Q_LORA, A_HEADS * half),
        w_uq[:, :, A_NOPE + half:].reshape(A_Q_LORA, A_HEADS * half)], axis=1)
    eye = jnp.eye(A_HEADS, dtype=w_uk.dtype)
    wuk = jnp.einsum('chn,hg->hngc', w_uk, eye).reshape(A_HEADS * A_NOPE, A_HEADS * A_KV_LORA)
    wuv = jnp.einsum('chv,hg->hcgv', w_uv, eye).reshape(A_HEADS * A_KV_LORA, A_HEADS * A_V)
    pm = np.zeros((2 * LANES, A_HEADS * LANES), np.float32)
    for h in range(A_HEADS):
        for i in range(half):
            pm[h * half + i, h * LANES + i] = 1.0
            pm[LANES + h * half + i, h * LANES + half + i] = 1.0
    zpad = lambda c: jnp.zeros((d, c), w_in.dtype)
    wk = jnp.concatenate([seg(1), seg(8), seg(2), zpad(LANES - IDX_DIM - A_ROPE), seg(5)], axis=1)
    wt = jnp.concatenate([seg(0), seg(1), seg(6), seg(4), seg(7), seg(3), seg(10), seg(9),
                          zpad(R_KB - R_WI - IDX_HEADS), seg(5), seg(8), seg(2)], axis=1).T
    assert wk.shape[1] == COLS_K and wt.shape[0] == ROWS_T
    bc = lambda g: jnp.broadcast_to(g.reshape(-1, 1), (g.shape[0], TQ))
    return {
        "wk": wk.astype(BF16), "wt": wt.astype(BF16), "qngc": bc(q_norm_g), "kvngc": bc(kv_norm_g),
        "wqt": wq.T.astype(BF16), "wukt": wuk.T.astype(BF16), "wuvt": wuv.T.astype(BF16),
        "ng": norm_g.reshape(1, -1), "win": win.astype(BF16),
        "qng": q_norm_g.reshape(1, -1), "kvng": kv_norm_g.reshape(1, -1),
        "wq": wq.astype(BF16), "wuk": wuk.astype(BF16), "wuv": wuv.astype(BF16),
        "pmat": jnp.asarray(pm, BF16), "wout": w_out.astype(BF16),
    }


def _rope_tables(pos):
    half = A_ROPE // 2
    inv = ROPE_THETA ** (-jnp.arange(half, dtype=F32) / half)
    ang = pos.astype(F32)[:, None] * inv[None, :]
    cos, sin = jnp.cos(ang), jnp.sin(ang)
    z = jnp.zeros((pos.shape[0], LANES - A_ROPE), F32)
    zh = jnp.zeros_like(cos)
    cosq = jnp.tile(cos, (1, A_HEADS))
    sinq = jnp.tile(sin, (1, A_HEADS))
    rc = jnp.concatenate([cos, cos, z], axis=1)
    rs1 = jnp.concatenate([zh, sin, z], axis=1)
    rs2 = jnp.concatenate([-sin, zh, z], axis=1)
    return cosq, sinq, rc, rs1, rs2


def _rope_tables_t(pos):
    half = A_ROPE // 2
    inv = ROPE_THETA ** (-jnp.arange(half, dtype=F32) / half)
    ang = pos.astype(F32)[:, None] * inv[None, :]
    cos, sin = jnp.cos(ang), jnp.sin(ang)
    cost = jnp.tile(cos.T, (A_HEADS, 1))
    sint = jnp.tile(sin.T, (A_HEADS, 1))
    z0 = jnp.zeros((pos.shape[0], MK_KPE), F32)
    z1 = jnp.zeros((pos.shape[0], LANES - MK_KPE - A_ROPE), F32)
    zh = jnp.zeros_like(cos)
    rc = jnp.concatenate([z0, cos, cos, z1], axis=1)
    rs1 = jnp.concatenate([z0, zh, sin, z1], axis=1)
    rs2 = jnp.concatenate([z0, -sin, zh, z1], axis=1)
    return cost, sint, rc, rs1, rs2


def kernel(x_prompt, x_sample, cache_mla_ckv, cache_mla_kpe, cache_dsa_k, cache_dsa_v, cache_dsa_kidx,
           norm_g, w_in, mla_q_norm_g, mla_kv_norm_g, mla_w_uq, mla_w_uk, mla_w_uv, rel_bias, w_out,
           final_norm_g):
    bp, tp, d = x_prompt.shape
    bs, ts, _ = x_sample.shape
    depth = w_in.shape[0]
    past = cache_mla_ckv.shape[2]
    n_top_p = min(TOP_K_MAX, tp // 4)
    n_top_s = min(TOP_K_MAX, (past + ts) // 4)
    assert ts <= CHUNK and past % CHUNK == 0 and past % LANES == 0

    rope_p = _rope_tables_t(jnp.arange(tp, dtype=jnp.int32))
    reps = PROJ_TM // ts
    rope_s = tuple(jnp.tile(a, (reps, 1)) for a in _rope_tables(past + jnp.arange(ts, dtype=jnp.int32)))

    bias_p = _bias_tables(rel_bias, (0, -TQ, -3 * TQ), TQ, TQ, True)
    win_s = 2 * LANES
    bias_s = _bias_tables(rel_bias, (-(win_s - LANES), -(past + win_s)), ts, win_s, False)
    fg = final_norm_g.reshape(1, -1)

    xp = x_prompt
    xs = x_sample.reshape(bs * ts, d)
    outs_p, outs_s = [], []
    for l in range(depth):
        lw = _layer_weights(norm_g[l], w_in[l], mla_q_norm_g[l], mla_kv_norm_g[l],
                            mla_w_uq[l], mla_w_uk[l], mla_w_uv[l], w_out[l])
        final = l == depth - 1
        pr = _project_t(xp.reshape(bp * tp, d), rope_p, lw, bp, tp)
        olat, ob = _prompt_attention(pr, bias_p, bp, tp, n_top_p)
        xp = _combine_t(xp, olat, ob, pr, lw, fg, final)
        heads_t = lambda a: a.reshape(bp, B_KV_HEADS, B_HEAD_DIM, tp).transpose(0, 3, 1, 2)
        outs_p.append((pr["ckv"].reshape(bp, tp, A_KV_LORA), pr["kpeT"].transpose(0, 2, 1),
                       heads_t(pr["kbT"]), heads_t(pr["vbT"]), pr["kidxT"].transpose(0, 2, 1)))
        ps = _project(xs, rope_s, lw, period=PROJ_TM)
        feat_t = lambda a: a.transpose(0, 2, 3, 1).reshape(bs, B_KV_HEADS * B_HEAD_DIM, past)
        caches = (cache_mla_ckv[l], cache_mla_kpe[l].transpose(0, 2, 1), feat_t(cache_dsa_k[l]),
                  feat_t(cache_dsa_v[l]), cache_dsa_kidx[l].transpose(0, 2, 1))
        olat, ob = _sample_attention(ps, caches, bias_s, bs, ts, past, n_top_s)
        xs = _combine(xs, olat, ob, ps, lw, fg, final)
        outs_s.append((ps["ckv"].reshape(bs, ts, A_KV_LORA), ps["kpe"].reshape(bs, ts, A_ROPE),
                       ps["kb"].reshape(bs, ts, B_KV_HEADS, B_HEAD_DIM),
                       ps["vb"].reshape(bs, ts, B_KV_HEADS, B_HEAD_DIM),
                       ps["kidx"].reshape(bs, ts, IDX_DIM)))

    stack = lambda outs, i: jnp.stack([o[i] for o in outs])
    return ((xp, xs.reshape(bs, ts, d))
            + tuple(stack(outs_p, i) for i in range(5)) + tuple(stack(outs_s, i) for i in range(5)))
```

```python
import functools
import math

import jax
import jax.numpy as jnp
import numpy as np
from jax import lax
from jax.experimental import pallas as pl
from jax.experimental.pallas import tpu as pltpu

F32 = jnp.float32
BF16 = jnp.bfloat16
I32 = jnp.int32

CHUNK = 64
EPS = 1e-6
A_HEADS = 8
A_NOPE = 64
A_ROPE = 32
A_V = 64
A_Q_LORA = 256
A_KV_LORA = 128
ROPE_THETA = 10000.0
A_WIDTH = A_HEADS * A_V
B_HEADS = 8
B_KV_HEADS = 2
B_HEAD_DIM = 64
B_WIDTH = B_HEADS * B_HEAD_DIM
B_GROUP = B_HEADS // B_KV_HEADS
IDX_HEADS = 8
IDX_DIM = 64
TOP_K_MAX = 256
N_BUCKETS = 32
MAX_DISTANCE = 128

LANES = 128
VMEM_LIMIT_BYTES = 56 * 1024 * 1024

LOG2E = 1.4426950408889634
NEG_BIG = -1e30
INT_MIN = -(2 ** 31)

C_CQ = 0
C_CKV = C_CQ + A_Q_LORA
C_MISC = C_CKV + A_KV_LORA
C_GA = C_MISC + LANES
C_QB = C_GA + A_WIDTH
C_KB = C_QB + B_WIDTH
C_VB = C_KB + B_KV_HEADS * B_HEAD_DIM
C_QI = C_VB + B_KV_HEADS * B_HEAD_DIM
C_GB = C_QI + IDX_HEADS * IDX_DIM
IN_PAD = C_GB + B_WIDTH
M_KPE = 0
M_KIDX = A_ROPE
M_WIDX = A_ROPE + IDX_DIM

QCAT = 2 * LANES
TQ = 256
PROJ_TM = 256


def _dot(a, b):
    return jnp.dot(a, b, preferred_element_type=F32)


def _dot_nt(a, b):
    return lax.dot_general(a, b, (((1,), (1,)), ((), ())), preferred_element_type=F32)


def _rms(x, g):
    return x * lax.rsqrt(jnp.mean(x * x, axis=-1, keepdims=True) + EPS) * g


def _bias_kernel(rb_ref, out_ref, *, offsets, keys_on_rows):
    nb = N_BUCKETS // 2
    max_exact = nb // 2
    n_r = out_ref.shape[1] if keys_on_rows else out_ref.shape[2]
    n_c = out_ref.shape[2] // B_HEADS if keys_on_rows else out_ref.shape[3]
    row = lax.broadcasted_iota(I32, (n_r, n_c), 0)
    col = lax.broadcasted_iota(I32, (n_r, n_c), 1)

    def bucket_of(off):
        rel = off + (row - col if keys_on_rows else col - row)
        ret = jnp.where(rel > 0, nb, 0)
        n = jnp.abs(rel)
        nf = jnp.maximum(n, 1).astype(F32)
        large = max_exact + (jnp.log(nf / max_exact) / math.log(MAX_DISTANCE / max_exact)
                             * (nb - max_exact)).astype(I32)
        large = jnp.minimum(large, nb - 1)
        return ret + jnp.where(n < max_exact, n, large)

    def lookup(bucket, h):
        table = jnp.broadcast_to(rb_ref[h:h + 1, :], (n_r, LANES))
        return jnp.concatenate(
            [jnp.take_along_axis(table, bucket[:, c:c + LANES], axis=1) for c in range(0, n_c, LANES)], axis=1)

    buckets = [bucket_of(off) for off in offsets]
    for h in range(B_HEADS):
        if keys_on_rows:
            far = lookup(buckets[-1], h)
            for p in range(len(offsets) - 1):
                out_ref[p, :, h * n_c:(h + 1) * n_c] = (lookup(buckets[p], h) - far) * LOG2E
        else:
            for p in range(len(offsets)):
                out_ref[p, h] = lookup(buckets[p], h) * LOG2E


def _bias_tables(rel_bias, offsets, n_r, n_c, keys_on_rows):
    shape = (len(offsets) - 1, n_r, B_HEADS * n_c) if keys_on_rows else (len(offsets), B_HEADS, n_r, n_c)
    return pl.pallas_call(
        functools.partial(_bias_kernel, offsets=tuple(offsets), keys_on_rows=keys_on_rows),
        out_shape=jax.ShapeDtypeStruct(shape, F32),
        in_specs=[pl.BlockSpec(memory_space=pltpu.VMEM)],
        out_specs=pl.BlockSpec(memory_space=pltpu.VMEM),
        compiler_params=pltpu.CompilerParams(vmem_limit_bytes=VMEM_LIMIT_BYTES),
        name="bias_tables",
    )(jnp.pad(rel_bias.T, ((0, 0), (0, LANES - N_BUCKETS))))


def _proj_kernel(x_ref, ng_ref, win_ref, qng_ref, kvng_ref, wq_ref, wuk_ref, pmat_ref,
                 cosq_ref, sinq_ref, rc_ref, rs1_ref, rs2_ref,
                 ckv_ref, kpe_ref, kb_ref, vb_ref, kidx_ref,
                 kcat_ref, kb16_ref, vb16_ref, kidx16_ref,
                 qcat_ref, qb_ref, qidx_ref, widx_ref, sga_ref, sgb_ref):
    x = x_ref[...]
    h = _rms(x, ng_ref[...])
    z = _dot(h.astype(BF16), win_ref[...])

    cq = _rms(z[:, C_CQ:C_CQ + A_Q_LORA], qng_ref[...])
    q = _dot(cq.astype(BF16), wq_ref[...])
    n_nope = A_HEADS * A_NOPE
    x1 = q[:, n_nope:n_nope + LANES]
    x2 = q[:, n_nope + LANES:n_nope + 2 * LANES]
    cos8, sin8 = cosq_ref[...], sinq_ref[...]
    o1 = x1 * cos8 - x2 * sin8
    o2 = x1 * sin8 + x2 * cos8
    mla_scale = (A_NOPE + A_ROPE) ** -0.5 * LOG2E
    q_lat = _dot(q[:, :n_nope].astype(BF16), wuk_ref[...]) * mla_scale
    pe = jnp.concatenate([o1, o2], axis=1) * mla_scale
    q_pe = _dot(pe.astype(BF16), pmat_ref[...])
    for hh in range(A_HEADS):
        qcat_ref[:, hh * QCAT:hh * QCAT + LANES] = q_lat[:, hh * LANES:(hh + 1) * LANES].astype(BF16)
        qcat_ref[:, hh * QCAT + LANES:(hh + 1) * QCAT] = q_pe[:, hh * LANES:(hh + 1) * LANES].astype(BF16)

    ckv = _rms(z[:, C_CKV:C_CKV + A_KV_LORA], kvng_ref[...])
    ckv_ref[...] = ckv
    misc = z[:, C_MISC:C_MISC + LANES]
    rot = (misc * rc_ref[...] + pltpu.roll(misc, A_ROPE // 2, 1) * rs1_ref[...]
           + pltpu.roll(misc, LANES - A_ROPE // 2, 1) * rs2_ref[...])
    kpe_ref[...] = rot[:, :A_ROPE]
    kcat_ref[:, :LANES] = ckv.astype(BF16)
    kcat_ref[:, LANES:] = rot.astype(BF16)

    kidx = misc[:, M_KIDX:M_KIDX + IDX_DIM]
    kidx_ref[...] = kidx
    kidx16_ref[...] = kidx.astype(BF16)
    widx_ref[...] = misc[:, M_WIDX:M_WIDX + IDX_HEADS] * (IDX_HEADS ** -0.5)
    kb = z[:, C_KB:C_KB + LANES]
    vb = z[:, C_VB:C_VB + LANES]
    kb_ref[...] = kb
    vb_ref[...] = vb
    kb16_ref[...] = kb.astype(BF16)
    vb16_ref[...] = vb.astype(BF16)
    qb_ref[...] = (z[:, C_QB:C_QB + B_WIDTH] * (B_HEAD_DIM ** -0.5 * LOG2E)).astype(BF16)
    qidx_ref[...] = (z[:, C_QI:C_QI + IDX_HEADS * IDX_DIM] * (IDX_DIM ** -0.5)).astype(BF16)
    sga_ref[...] = jax.nn.silu(z[:, C_GA:C_GA + A_WIDTH])
    sgb_ref[...] = jax.nn.silu(z[:, C_GB:C_GB + B_WIDTH])


def _project(x2d, rope_tabs, lw, *, period):
    n, d = x2d.shape
    tm = PROJ_TM
    assert n % tm == 0 and period % tm == 0
    n_rep = period // tm
    tok = lambda c: pl.BlockSpec((tm, c), lambda i: (i, 0))
    full = lambda a: pl.BlockSpec(a.shape, lambda i: (0,) * a.ndim)
    tab = pl.BlockSpec((tm, LANES), lambda i: (i % n_rep, 0))
    outs = [
        ("ckv", A_KV_LORA, F32), ("kpe", A_ROPE, F32), ("kb", LANES, F32), ("vb", LANES, F32),
        ("kidx", IDX_DIM, F32),
        ("kcat", QCAT, BF16), ("kb16", LANES, BF16), ("vb16", LANES, BF16), ("kidx16", IDX_DIM, BF16),
        ("qcat", A_HEADS * QCAT, BF16), ("qb", B_WIDTH, BF16), ("qidx", IDX_HEADS * IDX_DIM, BF16),
        ("widx", IDX_HEADS, F32), ("sga", A_WIDTH, F32), ("sgb", B_WIDTH, F32),
    ]
    res = pl.pallas_call(
        _proj_kernel,
        grid=(n // tm,),
        in_specs=[tok(d), full(lw["ng"]), full(lw["win"]), full(lw["qng"]), full(lw["kvng"]),
                  full(lw["wq"]), full(lw["wuk"]), full(lw["pmat"]), tab, tab, tab, tab, tab],
        out_specs=[tok(c) for _, c, _ in outs],
        out_shape=[jax.ShapeDtypeStruct((n, c), dt) for _, c, dt in outs],
        compiler_params=pltpu.CompilerParams(dimension_semantics=("arbitrary",),
                                             vmem_limit_bytes=VMEM_LIMIT_BYTES),
        name="project",
    )(x2d, lw["ng"], lw["win"], lw["qng"], lw["kvng"], lw["wq"], lw["wuk"], lw["pmat"], *rope_tabs)
    return {name: r for (name, _, _), r in zip(outs, res)}


R_CQ = 0
R_CKV = R_CQ + A_Q_LORA
R_VB = R_CKV + A_KV_LORA
R_QB = R_VB + LANES
R_QI = R_QB + B_WIDTH
R_GA = R_QI + IDX_HEADS * IDX_DIM
R_GB = R_GA + A_WIDTH
R_WI = R_GB + B_WIDTH
R_KB = R_WI + 16
R_KI = R_KB + LANES
R_KPE = R_KI + IDX_DIM
ROWS_T = R_KPE + A_ROPE
K_MISC = A_KV_LORA
K_KB = K_MISC + LANES
COLS_K = K_KB + LANES
MK_KPE = IDX_DIM
ONES_ROWS = 16
V_EXT = A_KV_LORA + ONES_ROWS


def _proj_t_kernel(x_ref, ng_ref, wk_ref, wt_ref, qng_ref, kvngc_ref, kvng_ref, wqt_ref, wukt_ref,
                   cost_ref, sint_ref, rc_ref, rs1_ref, rs2_ref,
                   ckv_ref, kcat_ref, kb16_ref, kidx16_ref,
                   kpeT_ref, kbT_ref, vbT_ref, kidxT_ref,
                   qcatT_ref, qbT_ref, qidxT_ref, widxT_ref, vmlaT_ref, vdsaT_ref, sgaT_ref, sgbT_ref):
    x = x_ref[...]
    tm = x.shape[0]
    hb = _rms(x, ng_ref[...]).astype(BF16)

    zk = _dot(hb, wk_ref[...])
    ckv = _rms(zk[:, :A_KV_LORA], kvng_ref[...])
    ckv_ref[...] = ckv
    misc = zk[:, K_MISC:K_MISC + LANES]
    lane = lax.broadcasted_iota(I32, (tm, LANES), 1)
    kidx16_ref[...] = jnp.where(lane < IDX_DIM, misc, 0.0).astype(BF16)
    rot = (misc * rc_ref[...] + pltpu.roll(misc, A_ROPE // 2, 1) * rs1_ref[...]
           + pltpu.roll(misc, LANES - A_ROPE // 2, 1) * rs2_ref[...])
    kcat_ref[:, :LANES] = ckv.astype(BF16)
    kcat_ref[:, LANES:] = pltpu.roll(rot, LANES - MK_KPE, 1).astype(BF16)
    kb16_ref[...] = zk[:, K_KB:K_KB + LANES].astype(BF16)

    zt = _dot_nt(wt_ref[...], hb)

    def rms_t(c, g):
        return c * lax.rsqrt(jnp.mean(c * c, axis=0, keepdims=True) + EPS) * g

    cq = rms_t(zt[R_CQ:R_CQ + A_Q_LORA], qng_ref[...])
    qt = _dot(wqt_ref[...], cq.astype(BF16))
    n_nope = A_HEADS * A_NOPE
    x1 = qt[n_nope:n_nope + LANES]
    x2 = qt[n_nope + LANES:n_nope + 2 * LANES]
    cos8, sin8 = cost_ref[...], sint_ref[...]
    mla_scale = (A_NOPE + A_ROPE) ** -0.5 * LOG2E
    o1 = (x1 * cos8 - x2 * sin8) * mla_scale
    o2 = (x1 * sin8 + x2 * cos8) * mla_scale
    q_lat = _dot(wukt_ref[...], qt[:n_nope].astype(BF16)) * mla_scale
    half = A_ROPE // 2
    for h in range(A_HEADS):
        qcatT_ref[h, :LANES, :] = q_lat[h * LANES:(h + 1) * LANES].astype(BF16)
        qcatT_ref[h, LANES:LANES + half, :] = o1[h * half:(h + 1) * half].astype(BF16)
        qcatT_ref[h, LANES + half:LANES + A_ROPE, :] = o2[h * half:(h + 1) * half].astype(BF16)
        qcatT_ref[h, LANES + A_ROPE:, :] = jnp.zeros((QCAT - LANES - A_ROPE, tm), BF16)
    ones = jnp.ones((ONES_ROWS, tm), BF16)
    vmlaT_ref[:A_KV_LORA, :] = rms_t(zt[R_CKV:R_CKV + A_KV_LORA], kvngc_ref[...]).astype(BF16)
    vmlaT_ref[A_KV_LORA:, :] = ones
    vb_t = zt[R_VB:R_VB + LANES]
    vdsaT_ref[:LANES, :] = vb_t.astype(BF16)
    vdsaT_ref[LANES:, :] = ones
    vbT_ref[...] = vb_t
    kbT_ref[...] = zt[R_KB:R_KB + LANES]
    kidxT_ref[...] = zt[R_KI:R_KI + IDX_DIM]
    k1, k2 = zt[R_KPE:R_KPE + half], zt[R_KPE + half:R_KPE + A_ROPE]
    cos1, sin1 = cos8[:half], sin8[:half]
    kpeT_ref[:half, :] = k1 * cos1 - k2 * sin1
    kpeT_ref[half:, :] = k1 * sin1 + k2 * cos1
    qbT_ref[...] = (zt[R_QB:R_QB + B_WIDTH] * (B_HEAD_DIM ** -0.5 * LOG2E)).astype(BF16)
    qidxT_ref[...] = (zt[R_QI:R_QI + IDX_HEADS * IDX_DIM] * (IDX_DIM ** -0.5)).astype(BF16)
    sgaT_ref[...] = jax.nn.silu(zt[R_GA:R_GA + A_WIDTH]).astype(BF16)
    sgbT_ref[...] = jax.nn.silu(zt[R_GB:R_GB + B_WIDTH]).astype(BF16)
    widxT_ref[...] = zt[R_WI:R_WI + IDX_HEADS] * (IDX_HEADS ** -0.5)


def _project_t(x2d, tabs, lw, b, t):
    n, d = x2d.shape
    tm = TQ
    nt = t // tm
    tok = lambda c: pl.BlockSpec((tm, c), lambda i: (i, 0))
    full = lambda a: pl.BlockSpec(a.shape, lambda i: (0,) * a.ndim)
    tab_t = pl.BlockSpec((LANES, tm), lambda i: (0, i % nt))
    tab_k = pl.BlockSpec((tm, LANES), lambda i: (i % nt, 0))
    rows_t = lambda r: pl.BlockSpec((None, r, tm), lambda i: (i // nt, 0, i % nt))
    outs = [
        ("ckv", (n, A_KV_LORA), F32, tok(A_KV_LORA)),
        ("kcat", (n, QCAT), BF16, tok(QCAT)), ("kb16", (n, LANES), BF16, tok(LANES)),
        ("kidx16", (n, LANES), BF16, tok(LANES)),
        ("kpeT", (b, A_ROPE, t), F32, rows_t(A_ROPE)), ("kbT", (b, LANES, t), F32, rows_t(LANES)),
        ("vbT", (b, LANES, t), F32, rows_t(LANES)), ("kidxT", (b, IDX_DIM, t), F32, rows_t(IDX_DIM)),
        ("qcatT", (b, A_HEADS, QCAT, t), BF16,
         pl.BlockSpec((None, A_HEADS, QCAT, tm), lambda i: (i // nt, 0, 0, i % nt))),
        ("qbT", (b, B_WIDTH, t), BF16, rows_t(B_WIDTH)),
        ("qidxT", (b, IDX_HEADS * IDX_DIM, t), BF16, rows_t(IDX_HEADS * IDX_DIM)),
        ("widxT", (b, IDX_HEADS, t), F32, rows_t(IDX_HEADS)),
        ("vmlaT", (b, nt, V_EXT, tm), BF16, pl.BlockSpec((None, None, V_EXT, tm), lambda i: (i // nt, i % nt, 0, 0))),
        ("vdsaT", (b, nt, V_EXT, tm), BF16, pl.BlockSpec((None, None, V_EXT, tm), lambda i: (i // nt, i % nt, 0, 0))),
        ("sgaT", (b, A_WIDTH, t), BF16, rows_t(A_WIDTH)), ("sgbT", (b, B_WIDTH, t), BF16, rows_t(B_WIDTH)),
    ]
    cost, sint, rc, rs1, rs2 = tabs
    res = pl.pallas_call(
        _proj_t_kernel,
        grid=(n // tm,),
        in_specs=[tok(d), full(lw["ng"]), full(lw["wk"]), full(lw["wt"]), full(lw["qngc"]), full(lw["kvngc"]),
                  full(lw["kvng"]), full(lw["wqt"]), full(lw["wukt"]), tab_t, tab_t, tab_k, tab_k, tab_k],
        out_specs=[o[3] for o in outs],
        out_shape=[jax.ShapeDtypeStruct(o[1], o[2]) for o in outs],
        compiler_params=pltpu.CompilerParams(dimension_semantics=("arbitrary",),
                                             vmem_limit_bytes=VMEM_LIMIT_BYTES),
        name="project_prompt",
    )(x2d, lw["ng"], lw["wk"], lw["wt"], lw["qngc"], lw["kvngc"], lw["kvng"], lw["wqt"], lw["wukt"],
      cost, sint, rc, rs1, rs2)
    return {o[0]: r for o, r in zip(outs, res)}


NEG_FLT_MAX = -3.4028234663852886e38
KEY_NEG_FLT_MAX = INT_MIN + (1 << 23)


def _key_to_float(k):
    k = jnp.maximum(k, KEY_NEG_FLT_MAX)
    return pltpu.bitcast(k ^ ((k >> 31) & 0x7FFFFFFF), F32)


def _count(pred):
    return jnp.sum(jnp.where(pred, 1.0, 0.0), axis=1, keepdims=True)


MAX_TIE_SWEEPS = 8.0


def _fold8(x, op=jnp.add):
    parts = [x[i:i + 8] for i in range(0, x.shape[0], 8)]
    while len(parts) > 1:
        parts = [op(a, b) for a, b in zip(parts[::2], parts[1::2])]
    return parts[0]


def _prompt_attn_kernel(qcatT_ref, qbT_ref, qidxT_ref, widxT_ref, kcat_ref, kb_ref, kidx_ref,
                        vmlaT_ref, vdsaT_ref, bias_ref, olatT_ref, obT_ref,
                        qa_ref, qbp_ref, qip_ref, sc_ref, m_ref, acc_ref, mb_ref, accb_ref, *, n_top):
    qi = pl.program_id(1)
    nblk = qi + 1
    tq = TQ
    krow = lax.broadcasted_iota(I32, (tq, tq), 0)
    qcol = lax.broadcasted_iota(I32, (tq, tq), 1)
    shift = CHUNK.bit_length() - 1
    diag_ok = (qcol >> shift) >= (krow >> shift)
    hcols = lambda h: slice(h * tq, (h + 1) * tq)

    def per_head(fn):
        return jnp.concatenate([fn(h) for h in range(A_HEADS)], axis=1)

    qbp_ref[...] = jnp.zeros(qbp_ref.shape, BF16)
    qip_ref[...] = jnp.zeros(qip_ref.shape, BF16)
    for h in range(A_HEADS):
        g = h // B_GROUP
        qa_ref[:, hcols(h)] = qcatT_ref[h]
        qbp_ref[g * B_HEAD_DIM:(g + 1) * B_HEAD_DIM, hcols(h)] = qbT_ref[h * B_HEAD_DIM:(h + 1) * B_HEAD_DIM, :]
        qip_ref[:IDX_DIM, hcols(h)] = qidxT_ref[h * IDX_DIM:(h + 1) * IDX_DIM, :]
    w_all = per_head(lambda h: widxT_ref[h:h + 1, :])

    def flash_update(s_t, values, m_r, acc_r):
        m_prev = m_r[0:1, :]
        m_new = jnp.maximum(m_prev, jnp.max(s_t, axis=0, keepdims=True))
        alpha = jnp.exp2(m_prev - m_new)
        p_t = jnp.exp2(s_t - m_new).astype(BF16)
        for v_t, lanes in values:
            acc_r[:, lanes] = alpha[:, lanes] * acc_r[:, lanes] + _dot(v_t, p_t[:, lanes])
        m_r[0:1, :] = m_new

    m_ref[...] = jnp.full(m_ref.shape, NEG_BIG, F32)
    acc_ref[...] = jnp.zeros(acc_ref.shape, F32)
    all_lanes = slice(0, A_HEADS * tq)

    def mla_block(j, masked):
        start = pl.multiple_of(j * tq, tq)
        s_t = _dot(kcat_ref[pl.ds(start, tq), :], qa_ref[...])
        if masked:
            s_t = per_head(lambda h: jnp.where(diag_ok, s_t[:, hcols(h)], NEG_BIG))
        flash_update(s_t, [(vmlaT_ref[j], all_lanes)], m_ref, acc_ref)
        r = jnp.maximum(_dot(kidx_ref[pl.ds(start, tq), :], qip_ref[...]), 0.0) * w_all
        score = r[:, hcols(0)]
        for h in range(1, IDX_HEADS):
            score = score + r[:, hcols(h)]
        if masked:
            score = jnp.where(diag_ok, score, -jnp.inf)
        sc_ref[j] = score

    def for_blocks(n, block):
        def pair(p, c):
            block(2 * p)
            block(2 * p + 1)
            return c

        lax.fori_loop(0, lax.shift_right_logical(n, 1), pair, 0)

        @pl.when((n & 1) == 1)
        def _():
            block(n - 1)

    for_blocks(qi, lambda j: mla_block(j, False))
    mla_block(qi, True)
    o_t = acc_ref[:A_KV_LORA, :] * (1.0 / acc_ref[A_KV_LORA:A_KV_LORA + 1, :])
    for h in range(A_HEADS):
        olatT_ref[h * LANES:(h + 1) * LANES, :] = o_t[:, hcols(h)].astype(olatT_ref.dtype)

    def count(pred):
        def body(j, c):
            return c + _fold8(jnp.where(pred(sc_ref[j], j), 1.0, 0.0))
        part = lax.fori_loop(0, nblk, body, jnp.zeros((8, tq), F32))
        return jnp.sum(part, axis=0, keepdims=True)

    kf = float(n_top)

    def bis_body(it, carry):
        lo, cnt_lo = carry
        cand = lo + lax.shift_left(jnp.int32(1), 31 - it)
        cand_f = _key_to_float(cand)
        cnt = count(lambda s, j: s >= cand_f)
        take = cnt >= kf
        return jnp.where(take, cand, lo), jnp.where(take, cnt, cnt_lo)

    n_steps = jnp.where(nblk * tq <= n_top, 0, 32)
    lo, cnt_ge = lax.fori_loop(
        0, n_steps, bis_body, (jnp.full((1, tq), INT_MIN, I32), jnp.full((1, tq), 1e9, F32)))
    few = lo == INT_MIN
    thr = _key_to_float(lo)
    excess0 = jnp.where(few, 0.0, cnt_ge - kf)
    max_excess = jnp.max(excess0)

    def drop_from(cut):
        def body(j, c):
            s = sc_ref[j]
            sc_ref[j] = jnp.where((s == thr) & ((krow + j * tq) >= cut), -jnp.inf, s)
            return c
        lax.fori_loop(0, nblk, body, 0)

    @pl.when((max_excess > 0.0) & (max_excess <= MAX_TIE_SWEEPS))
    def _():
        def last_tie_below(cut):
            def body(j, m):
                pos = krow + j * tq
                hit = jnp.where((sc_ref[j] == thr) & (pos < cut), pos, -1)
                return jnp.maximum(m, _fold8(hit, jnp.maximum))
            part = lax.fori_loop(0, nblk, body, jnp.full((8, tq), -1, I32))
            return jnp.max(part, axis=0, keepdims=True)

        def sweep(c):
            excess, cut = c
            last = last_tie_below(cut)
            live = excess > 0.0
            return jnp.where(live, excess - 1.0, excess), jnp.where(live, last, cut)

        _, cut = lax.while_loop(lambda c: jnp.max(c[0]) > 0.0, sweep,
                                (excess0, jnp.full((1, tq), 2 ** 30, I32)))
        drop_from(cut)

    @pl.when(max_excess > MAX_TIE_SWEEPS)
    def _():
        need = kf - count(lambda s, j: s > thr)
        n_bits = (sc_ref.shape[0] * tq).bit_length()

        def cut_body(it, cpos):
            cand = cpos + lax.shift_left(jnp.int32(1), n_bits - 1 - it)
            cnt = count(lambda s, j: (s == thr) & ((krow + j * tq) < cand))
            return jnp.where(cnt < need, cand, cpos)

        keep = lax.fori_loop(0, n_bits, cut_body, jnp.zeros((1, tq), I32))
        drop_from(jnp.where(excess0 > 0.0, keep + 1, 2 ** 30))

    mb_ref[...] = jnp.full(mb_ref.shape, NEG_BIG, F32)
    accb_ref[...] = jnp.zeros(accb_ref.shape, F32)

    group_lanes = [slice(g * B_GROUP * tq, (g + 1) * B_GROUP * tq) for g in range(B_KV_HEADS)]

    def dsa_block(j, near):
        start = pl.multiple_of(j * tq, tq)
        sel = sc_ref[j] >= thr
        s_t = _dot(kb_ref[pl.ds(start, tq), :], qbp_ref[...])
        if near:
            s_t = s_t + bias_ref[qi - j]
        s_t = per_head(lambda h: jnp.where(sel, s_t[:, hcols(h)], NEG_BIG))
        v_all = vdsaT_ref[j]
        values = [(jnp.concatenate([v_all[g * B_HEAD_DIM:(g + 1) * B_HEAD_DIM], v_all[LANES:]], axis=0),
                   group_lanes[g]) for g in range(B_KV_HEADS)]
        flash_update(s_t, values, mb_ref, accb_ref)

    for_blocks(jnp.maximum(qi - 1, 0), lambda j: dsa_block(j, False))

    @pl.when(qi >= 1)
    def _():
        dsa_block(qi - 1, True)

    dsa_block(qi, True)
    inv_b = 1.0 / accb_ref[B_HEAD_DIM:B_HEAD_DIM + 1, :]
    for h in range(B_HEADS):
        obT_ref[h * B_HEAD_DIM:(h + 1) * B_HEAD_DIM, :] = (
            accb_ref[:B_HEAD_DIM, hcols(h)] * inv_b[:, hcols(h)]).astype(obT_ref.dtype)


def _prompt_attention(pr, bias_p, b, t, n_top):
    tq = TQ
    assert t % tq == 0
    nq = t // tq
    r3 = lambda a: a.reshape(b, t, a.shape[-1])
    qrows = lambda r: pl.BlockSpec((None, r, tq), lambda bi, qi: (bi, 0, qi))
    kall = lambda c: pl.BlockSpec((None, t, c), lambda bi, qi: (bi, 0, 0))
    vall = pl.BlockSpec((None, nq, V_EXT, tq), lambda bi, qi: (bi, 0, 0, 0))
    olat, ob = pl.pallas_call(
        functools.partial(_prompt_attn_kernel, n_top=n_top),
        grid=(b, nq),
        in_specs=[pl.BlockSpec((None, A_HEADS, QCAT, tq), lambda bi, qi: (bi, 0, 0, qi)),
                  qrows(B_WIDTH), qrows(IDX_HEADS * IDX_DIM), qrows(IDX_HEADS),
                  kall(QCAT), kall(LANES), kall(LANES), vall, vall,
                  pl.BlockSpec(bias_p.shape, lambda bi, qi: (0, 0, 0))],
        out_specs=[qrows(A_HEADS * A_KV_LORA), qrows(B_WIDTH)],
        out_shape=[jax.ShapeDtypeStruct((b, A_HEADS * A_KV_LORA, t), BF16),
                   jax.ShapeDtypeStruct((b, B_WIDTH, t), BF16)],
        scratch_shapes=[
            pltpu.VMEM((QCAT, A_HEADS * tq), BF16),
            pltpu.VMEM((LANES, B_HEADS * tq), BF16),
            pltpu.VMEM((LANES, IDX_HEADS * tq), BF16),
            pltpu.VMEM((nq, tq, tq), F32),
            pltpu.VMEM((8, A_HEADS * tq), F32),
            pltpu.VMEM((V_EXT, A_HEADS * tq), F32),
            pltpu.VMEM((8, B_HEADS * tq), F32),
            pltpu.VMEM((B_HEAD_DIM + ONES_ROWS, B_HEADS * tq), F32),
        ],
        compiler_params=pltpu.CompilerParams(dimension_semantics=("arbitrary", "arbitrary"),
                                             vmem_limit_bytes=VMEM_LIMIT_BYTES),
        name="prompt_attention",
    )(pr["qcatT"], pr["qbT"], pr["qidxT"], pr["widxT"],
      r3(pr["kcat"]), r3(pr["kb16"]), r3(pr["kidx16"]), pr["vmlaT"], pr["vdsaT"], bias_p)
    return olat, ob


def _sample_attn_kernel(qcat_ref, qb_ref, qidx_ref, widx_ref, kcatn_ref, kbn_ref, vbn_ref, kidxn_ref,
                        cckv_ref, ckpeT_ref, ckT_ref, cvT_ref, ckidxT_ref, bias_ref,
                        olat_ref, ob_ref, *, n_top, t_new, past):
    tq = t_new
    pad = LANES
    n_keys = past + pad

    def padrows(a):
        return jnp.concatenate([a, jnp.zeros((pad - t_new, a.shape[1]), a.dtype)], axis=0)

    ckv_c = cckv_ref[...].astype(BF16)
    kpe_t = ckpeT_ref[...].astype(BF16)
    k_t = ckT_ref[...].astype(BF16)
    v_t = cvT_ref[...].astype(BF16)
    kidx_t = ckidxT_ref[...].astype(BF16)
    kcat_n = padrows(kcatn_ref[...])
    k_n = padrows(kbn_ref[...])
    v_n = padrows(vbn_ref[...])
    kidx_n = padrows(kidxn_ref[...])

    def softmax(s):
        m = jnp.max(s, axis=1, keepdims=True)
        p = jnp.exp2(s - m)
        return p.astype(BF16), jnp.sum(p, axis=1, keepdims=True)

    def new_cols(rows):
        return lax.broadcasted_iota(I32, (rows, pad), 1) < t_new

    qs = jnp.concatenate([qcat_ref[:, h * QCAT:(h + 1) * QCAT] for h in range(A_HEADS)], axis=0)
    s_c = _dot_nt(qs[:, :A_KV_LORA], ckv_c) + _dot(qs[:, A_KV_LORA:A_KV_LORA + A_ROPE], kpe_t)
    s_n = jnp.where(new_cols(A_HEADS * tq), _dot_nt(qs, kcat_n), NEG_BIG)
    pb, l = softmax(jnp.concatenate([s_c, s_n], axis=1))
    o = (_dot(pb[:, :past], ckv_c) + _dot(pb[:, past:], kcat_n[:, :A_KV_LORA])) / l
    for h in range(A_HEADS):
        olat_ref[:, h * LANES:(h + 1) * LANES] = o[h * tq:(h + 1) * tq].astype(olat_ref.dtype)

    qis = jnp.concatenate([qidx_ref[:, h * IDX_DIM:(h + 1) * IDX_DIM] for h in range(IDX_HEADS)], axis=0)

    def head_sum(dots):
        acc = jnp.maximum(dots[:tq], 0.0) * widx_ref[:, 0:1]
        for h in range(1, IDX_HEADS):
            acc = acc + jnp.maximum(dots[h * tq:(h + 1) * tq], 0.0) * widx_ref[:, h:h + 1]
        return acc

    score = jnp.concatenate(
        [head_sum(_dot(qis, kidx_t)), jnp.where(new_cols(tq), head_sum(_dot_nt(qis, kidx_n)), -jnp.inf)], axis=1)
    kf = float(n_top)

    def bis_body(it, carry):
        lo, cnt_lo = carry
        cand = lo + lax.shift_left(jnp.int32(1), 31 - it)
        cnt = _count(score >= _key_to_float(cand))
        take = cnt >= kf
        return jnp.where(take, cand, lo), jnp.where(take, cnt, cnt_lo)

    lo, cnt_ge = lax.fori_loop(
        0, 32, bis_body, (jnp.full((tq, 1), INT_MIN, I32), jnp.full((tq, 1), 1e9, F32)), unroll=True)
    few = lo == INT_MIN
    thr = _key_to_float(lo)
    has_tie = jnp.max(jnp.where((cnt_ge > kf) & (~few), 1.0, 0.0))
    n_bits = n_keys.bit_length()

    def drop_ties():
        cols = lax.broadcasted_iota(I32, (tq, n_keys), 1)
        need = kf - _count(score > thr)
        eq = score == thr

        def cut_body(it, cpos):
            cand = cpos + lax.shift_left(jnp.int32(1), n_bits - 1 - it)
            cnt = _count(eq & (cols < cand))
            return jnp.where(cnt < need, cand, cpos)

        keep = lax.fori_loop(0, n_bits, cut_body, jnp.zeros((tq, 1), I32))
        return jnp.where(eq & (cols > keep) & (cnt_ge > kf) & (~few), -jnp.inf, score)

    sel = lax.cond(has_tie > 0.0, drop_ties, lambda: score) >= thr

    n_far = n_keys - bias_ref.shape[-1]
    for g in range(B_KV_HEADS):
        feats = slice(g * B_HEAD_DIM, (g + 1) * B_HEAD_DIM)
        qg = jnp.concatenate(
            [qb_ref[:, h * B_HEAD_DIM:(h + 1) * B_HEAD_DIM] for h in range(g * B_GROUP, (g + 1) * B_GROUP)],
            axis=0)
        sg = jnp.concatenate([_dot(qg, k_t[feats]), _dot_nt(qg, k_n[:, feats])], axis=1)
        for hh in range(B_GROUP):
            h = g * B_GROUP + hh
            near = bias_ref[0, h] - bias_ref[1, h]
            sh = sg[hh * tq:(hh + 1) * tq]
            sh = jnp.concatenate([sh[:, :n_far], sh[:, n_far:] + near], axis=1)
            pb, l = softmax(jnp.where(sel, sh, NEG_BIG))
            oh = (_dot_nt(pb[:, :past], v_t[feats]) + _dot(pb[:, past:], v_n[:, feats])) / l
            ob_ref[:, h * B_HEAD_DIM:(h + 1) * B_HEAD_DIM] = oh


def _sample_attention(pr, caches, bias_s, b, t_new, past, n_top):
    r3 = lambda a: a.reshape(b, t_new, a.shape[-1])
    per_b = lambda n, c: pl.BlockSpec((None, n, c), lambda bi: (bi, 0, 0))
    news = [pr["qcat"], pr["qb"], pr["qidx"], pr["widx"], pr["kcat"], pr["kb16"], pr["vb16"], pr["kidx16"]]
    olat, ob = pl.pallas_call(
        functools.partial(_sample_attn_kernel, n_top=n_top, t_new=t_new, past=past),
        grid=(b,),
        in_specs=[per_b(t_new, a.shape[-1]) for a in news]
                 + [per_b(c.shape[1], c.shape[2]) for c in caches]
                 + [pl.BlockSpec(bias_s.shape, lambda bi: (0, 0, 0, 0))],
        out_specs=[per_b(t_new, A_HEADS * A_KV_LORA), per_b(t_new, B_WIDTH)],
        out_shape=[jax.ShapeDtypeStruct((b, t_new, A_HEADS * A_KV_LORA), BF16),
                   jax.ShapeDtypeStruct((b, t_new, B_WIDTH), F32)],
        compiler_params=pltpu.CompilerParams(dimension_semantics=("arbitrary",),
                                             vmem_limit_bytes=VMEM_LIMIT_BYTES),
        name="sample_attention",
    )(*[r3(a) for a in news], *caches, bias_s)
    return olat.reshape(b * t_new, -1), ob.reshape(b * t_new, -1)


def _combine_kernel(x_ref, olat_ref, ob_ref, sga_ref, sgb_ref, wuv_ref, wout_ref, fg_ref, y_ref, *, final):
    o_a = _dot(olat_ref[...], wuv_ref[...])
    mix = jnp.concatenate([o_a * sga_ref[...], ob_ref[...] * sgb_ref[...]], axis=1)
    y = x_ref[...] + _dot(mix.astype(BF16), wout_ref[...])
    if final:
        y = _rms(y, fg_ref[...])
    y_ref[...] = y


def _combine(x2d, olat, ob, pr, lw, fg, final):
    n, d = x2d.shape
    tm = PROJ_TM
    tok = lambda c: pl.BlockSpec((tm, c), lambda i: (i, 0))
    full = lambda a: pl.BlockSpec(a.shape, lambda i: (0,) * a.ndim)
    return pl.pallas_call(
        functools.partial(_combine_kernel, final=final),
        grid=(n // tm,),
        in_specs=[tok(d), tok(olat.shape[1]), tok(ob.shape[1]), tok(A_WIDTH), tok(B_WIDTH),
                  full(lw["wuv"]), full(lw["wout"]), full(fg)],
        out_specs=tok(d),
        out_shape=jax.ShapeDtypeStruct((n, d), F32),
        compiler_params=pltpu.CompilerParams(dimension_semantics=("arbitrary",),
                                             vmem_limit_bytes=VMEM_LIMIT_BYTES),
        name="combine",
    )(x2d, olat, ob, pr["sga"], pr["sgb"], lw["wuv"], lw["wout"], fg)


def _combine_t_kernel(x_ref, olatT_ref, obT_ref, sgaT_ref, sgbT_ref, wuvt_ref, wout_ref, fg_ref, y_ref, *, final):
    o_a = _dot(wuvt_ref[...], olatT_ref[...])
    mix_t = jnp.concatenate([o_a * sgaT_ref[...].astype(F32),
                             obT_ref[...].astype(F32) * sgbT_ref[...].astype(F32)], axis=0).astype(BF16)
    y = x_ref[...] + lax.dot_general(mix_t, wout_ref[...], (((0,), (0,)), ((), ())),
                                     preferred_element_type=F32)
    if final:
        y = _rms(y, fg_ref[...])
    y_ref[...] = y


def _combine_t(x3d, olat_t, ob_t, pr, lw, fg, final):
    b, t, d = x3d.shape
    tm = TQ
    rows_t = lambda r: pl.BlockSpec((None, r, tm), lambda bi, ti: (bi, 0, ti))
    full = lambda a: pl.BlockSpec(a.shape, lambda bi, ti: (0,) * a.ndim)
    xblk = pl.BlockSpec((None, tm, d), lambda bi, ti: (bi, ti, 0))
    return pl.pallas_call(
        functools.partial(_combine_t_kernel, final=final),
        grid=(b, t // tm),
        in_specs=[xblk, rows_t(A_HEADS * A_KV_LORA), rows_t(B_WIDTH), rows_t(A_WIDTH), rows_t(B_WIDTH),
                  full(lw["wuvt"]), full(lw["wout"]), full(fg)],
        out_specs=xblk,
        out_shape=jax.ShapeDtypeStruct((b, t, d), F32),
        compiler_params=pltpu.CompilerParams(dimension_semantics=("arbitrary", "arbitrary"),
                                             vmem_limit_bytes=VMEM_LIMIT_BYTES),
        name="combine_prompt",
    )(x3d, olat_t, ob_t, pr["sgaT"], pr["sgbT"], lw["wuvt"], lw["wout"], fg)


def _layer_weights(norm_g, w_in, q_norm_g, kv_norm_g, w_uq, w_uk, w_uv, w_out):
    d = w_in.shape[0]
    o = np.cumsum([0, A_Q_LORA, A_KV_LORA, A_ROPE, A_WIDTH, B_WIDTH, B_KV_HEADS * B_HEAD_DIM,
                   B_KV_HEADS * B_HEAD_DIM, IDX_HEADS * IDX_DIM, IDX_DIM, IDX_HEADS, B_WIDTH])
    seg = lambda i: w_in[:, int(o[i]):int(o[i + 1])]
    misc = jnp.concatenate([seg(2), seg(8), seg(9),
                            jnp.zeros((d, LANES - A_ROPE - IDX_DIM - IDX_HEADS), w_in.dtype)], axis=1)
    win = jnp.concatenate([seg(0), seg(1), misc, seg(3), seg(4), seg(5), seg(6), seg(7), seg(10)], axis=1)
    assert win.shape[1] == IN_PAD
    half = A_ROPE // 2
    wq = jnp.concatenate([
        w_uq[:, :, :A_NOPE].reshape(A_Q_LORA, A_HEADS * A_NOPE),
        w_uq[:, :, A_NOPE:A_NOPE + half].reshape(A_Q_LORA, A_HEADS * half),
        w_uq[:, :, A_NOPE + half:].reshape(A_Q_LORA, A_HEADS * half)], axis=1)
    eye = jnp.eye(A_HEADS, dtype=w_uk.dtype)
    wuk = jnp.einsum('chn,hg->hngc', w_uk, eye).reshape(A_HEADS * A_NOPE, A_HEADS * A_KV_LORA)
    wuv = jnp.einsum('chv,hg->hcgv', w_uv, eye).reshape(A_HEADS * A_KV_LORA, A_HEADS * A_V)
    pm = np.zeros((2 * LANES, A_HEADS * LANES), np.float32)
    for h in range(A_HEADS):
        for i in range(half):
            pm[h * half + i, h * LANES + i] = 1.0
            pm[LANES + h * half + i, h * LANES + half + i] = 1.0
    zpad = lambda c: jnp.zeros((d, c), w_in.dtype)
    wk = jnp.concatenate([seg(1), seg(8), seg(2), zpad(LANES - IDX_DIM - A_ROPE), seg(5)], axis=1)
    wt = jnp.concatenate([seg(0), seg(1), seg(6), seg(4), seg(7), seg(3), seg(10), seg(9),
                          zpad(R_KB - R_WI - IDX_HEADS), seg(5), seg(8), seg(2)], axis=1).T
    assert wk.shape[1] == COLS_K and wt.shape[0] == ROWS_T
    bc = lambda g: jnp.broadcast_to(g.reshape(-1, 1), (g.shape[0], TQ))
    return {
        "wk": wk.astype(BF16), "wt": wt.astype(BF16), "qngc": bc(q_norm_g), "kvngc": bc(kv_norm_g),
        "wqt": wq.T.astype(BF16), "wukt": wuk.T.astype(BF16), "wuvt": wuv.T.astype(BF16),
        "ng": norm_g.reshape(1, -1), "win": win.astype(BF16),
        "qng": q_norm_g.reshape(1, -1), "kvng": kv_norm_g.reshape(1, -1),
        "wq": wq.astype(BF16), "wuk": wuk.astype(BF16), "wuv": wuv.astype(BF16),
        "pmat": jnp.asarray(pm, BF16), "wout": w_out.astype(BF16),
    }


def _rope_tables(pos):
    half = A_ROPE // 2
    inv = ROPE_THETA ** (-jnp.arange(half, dtype=F32) / half)
    ang = pos.astype(F32)[:, None] * inv[None, :]
    cos, sin = jnp.cos(ang), jnp.sin(ang)
    z = jnp.zeros((pos.shape[0], LANES - A_ROPE), F32)
    zh = jnp.zeros_like(cos)
    cosq = jnp.tile(cos, (1, A_HEADS))
    sinq = jnp.tile(sin, (1, A_HEADS))
    rc = jnp.concatenate([cos, cos, z], axis=1)
    rs1 = jnp.concatenate([zh, sin, z], axis=1)
    rs2 = jnp.concatenate([-sin, zh, z], axis=1)
    return cosq, sinq, rc, rs1, rs2


def _rope_tables_t(pos):
    half = A_ROPE // 2
    inv = ROPE_THETA ** (-jnp.arange(half, dtype=F32) / half)
    ang = pos.astype(F32)[:, None] * inv[None, :]
    cos, sin = jnp.cos(ang), jnp.sin(ang)
    cost = jnp.tile(cos.T, (A_HEADS, 1))
    sint = jnp.tile(sin.T, (A_HEADS, 1))
    z0 = jnp.zeros((pos.shape[0], MK_KPE), F32)
    z1 = jnp.zeros((pos.shape[0], LANES - MK_KPE - A_ROPE), F32)
    zh = jnp.zeros_like(cos)
    rc = jnp.concatenate([z0, cos, cos, z1], axis=1)
    rs1 = jnp.concatenate([z0, zh, sin, z1], axis=1)
    rs2 = jnp.concatenate([z0, -sin, zh, z1], axis=1)
    return cost, sint, rc, rs1, rs2


def kernel(x_prompt, x_sample, cache_mla_ckv, cache_mla_kpe, cache_dsa_k, cache_dsa_v, cache_dsa_kidx,
           norm_g, w_in, mla_q_norm_g, mla_kv_norm_g, mla_w_uq, mla_w_uk, mla_w_uv, rel_bias, w_out,
           final_norm_g):
    bp, tp, d = x_prompt.shape
    bs, ts, _ = x_sample.shape
    depth = w_in.shape[0]
    past = cache_mla_ckv.shape[2]
    n_top_p = min(TOP_K_MAX, tp // 4)
    n_top_s = min(TOP_K_MAX, (past + ts) // 4)
    assert ts <= CHUNK and past % CHUNK == 0 and past % LANES == 0

    rope_p = _rope_tables_t(jnp.arange(tp, dtype=jnp.int32))
    reps = PROJ_TM // ts
    rope_s = tuple(jnp.tile(a, (reps, 1)) for a in _rope_tables(past + jnp.arange(ts, dtype=jnp.int32)))

    bias_p = _bias_tables(rel_bias, (0, -TQ, -3 * TQ), TQ, TQ, True)
    win_s = 2 * LANES
    bias_s = _bias_tables(rel_bias, (-(win_s - LANES), -(past + win_s)), ts, win_s, False)
    fg = final_norm_g.reshape(1, -1)

    xp = x_prompt
    xs = x_sample.reshape(bs * ts, d)
    outs_p, outs_s = [], []
    for l in range(depth):
        lw = _layer_weights(norm_g[l], w_in[l], mla_q_norm_g[l], mla_kv_norm_g[l],
                            mla_w_uq[l], mla_w_uk[l], mla_w_uv[l], w_out[l])
        final = l == depth - 1
        pr = _project_t(xp.reshape(bp * tp, d), rope_p, lw, bp, tp)
        olat, ob = _prompt_attention(pr, bias_p, bp, tp, n_top_p)
        xp = _combine_t(xp, olat, ob, pr, lw, fg, final)
        heads_t = lambda a: a.reshape(bp, B_KV_HEADS, B_HEAD_DIM, tp).transpose(0, 3, 1, 2)
        outs_p.append((pr["ckv"].reshape(bp, tp, A_KV_LORA), pr["kpeT"].transpose(0, 2, 1),
                       heads_t(pr["kbT"]), heads_t(pr["vbT"]), pr["kidxT"].transpose(0, 2, 1)))
        ps = _project(xs, rope_s, lw, period=PROJ_TM)
        feat_t = lambda a: a.transpose(0, 2, 3, 1).reshape(bs, B_KV_HEADS * B_HEAD_DIM, past)
        caches = (cache_mla_ckv[l], cache_mla_kpe[l].transpose(0, 2, 1), feat_t(cache_dsa_k[l]),
                  feat_t(cache_dsa_v[l]), cache_dsa_kidx[l].transpose(0, 2, 1))
        olat, ob = _sample_attention(ps, caches, bias_s, bs, ts, past, n_top_s)
        xs = _combine(xs, olat, ob, ps, lw, fg, final)
        outs_s.append((ps["ckv"].reshape(bs, ts, A_KV_LORA), ps["kpe"].reshape(bs, ts, A_ROPE),
                       ps["kb"].reshape(bs, ts, B_KV_HEADS, B_HEAD_DIM),
                       ps["vb"].reshape(bs, ts, B_KV_HEADS, B_HEAD_DIM),
                       ps["kidx"].reshape(bs, ts, IDX_DIM)))

    stack = lambda outs, i: jnp.stack([o[i] for o in outs])
    return ((xp, xs.reshape(bs, ts, d))
            + tuple(stack(outs_p, i) for i in range(5)) + tuple(stack(outs_s, i) for i in range(5)))
```

```python
import functools
import math

import jax
import jax.numpy as jnp
import numpy as np
from jax import lax
from jax.experimental import pallas as pl
from jax.experimental.pallas import tpu as pltpu

F32 = jnp.float32
BF16 = jnp.bfloat16
I32 = jnp.int32

CHUNK = 64
EPS = 1e-6
A_HEADS = 8
A_NOPE = 64
A_ROPE = 32
A_V = 64
A_Q_LORA = 256
A_KV_LORA = 128
ROPE_THETA = 10000.0
A_WIDTH = A_HEADS * A_V
B_HEADS = 8
B_KV_HEADS = 2
B_HEAD_DIM = 64
B_WIDTH = B_HEADS * B_HEAD_DIM
B_GROUP = B_HEADS // B_KV_HEADS
IDX_HEADS = 8
IDX_DIM = 64
TOP_K_MAX = 256
N_BUCKETS = 32
MAX_DISTANCE = 128

LANES = 128
VMEM_LIMIT_BYTES = 56 * 1024 * 1024

LOG2E = 1.4426950408889634
NEG_BIG = -1e30
INT_MIN = -(2 ** 31)

C_CQ = 0
C_CKV = C_CQ + A_Q_LORA
C_MISC = C_CKV + A_KV_LORA
C_GA = C_MISC + LANES
C_QB = C_GA + A_WIDTH
C_KB = C_QB + B_WIDTH
C_VB = C_KB + B_KV_HEADS * B_HEAD_DIM
C_QI = C_VB + B_KV_HEADS * B_HEAD_DIM
C_GB = C_QI + IDX_HEADS * IDX_DIM
IN_PAD = C_GB + B_WIDTH
M_KPE = 0
M_KIDX = A_ROPE
M_WIDX = A_ROPE + IDX_DIM

QCAT = 2 * LANES
TQ = 256
PROJ_TM = 256


def _dot(a, b):
    return jnp.dot(a, b, preferred_element_type=F32)


def _dot_nt(a, b):
    return lax.dot_general(a, b, (((1,), (1,)), ((), ())), preferred_element_type=F32)


def _rms(x, g):
    return x * lax.rsqrt(jnp.mean(x * x, axis=-1, keepdims=True) + EPS) * g


def _bias_kernel(rb_ref, out_ref, *, offsets, keys_on_rows):
    nb = N_BUCKETS // 2
    max_exact = nb // 2
    n_r = out_ref.shape[1] if keys_on_rows else out_ref.shape[2]
    n_c = out_ref.shape[2] // B_HEADS if keys_on_rows else out_ref.shape[3]
    row = lax.broadcasted_iota(I32, (n_r, n_c), 0)
    col = lax.broadcasted_iota(I32, (n_r, n_c), 1)

    def bucket_of(off):
        rel = off + (row - col if keys_on_rows else col - row)
        ret = jnp.where(rel > 0, nb, 0)
        n = jnp.abs(rel)
        nf = jnp.maximum(n, 1).astype(F32)
        large = max_exact + (jnp.log(nf / max_exact) / math.log(MAX_DISTANCE / max_exact)
                             * (nb - max_exact)).astype(I32)
        large = jnp.minimum(large, nb - 1)
        return ret + jnp.where(n < max_exact, n, large)

    def lookup(bucket, h):
        table = jnp.broadcast_to(rb_ref[h:h + 1, :], (n_r, LANES))
        return jnp.concatenate(
            [jnp.take_along_axis(table, bucket[:, c:c + LANES], axis=1) for c in range(0, n_c, LANES)], axis=1)

    buckets = [bucket_of(off) for off in offsets]
    for h in range(B_HEADS):
        if keys_on_rows:
            far = lookup(buckets[-1], h)
            for p in range(len(offsets) - 1):
                out_ref[p, :, h * n_c:(h + 1) * n_c] = (lookup(buckets[p], h) - far) * LOG2E
        else:
            for p in range(len(offsets)):
                out_ref[p, h] = lookup(buckets[p], h) * LOG2E


def _bias_tables(rel_bias, offsets, n_r, n_c, keys_on_rows):
    shape = (len(offsets) - 1, n_r, B_HEADS * n_c) if keys_on_rows else (len(offsets), B_HEADS, n_r, n_c)
    return pl.pallas_call(
        functools.partial(_bias_kernel, offsets=tuple(offsets), keys_on_rows=keys_on_rows),
        out_shape=jax.ShapeDtypeStruct(shape, F32),
        in_specs=[pl.BlockSpec(memory_space=pltpu.VMEM)],
        out_specs=pl.BlockSpec(memory_space=pltpu.VMEM),
        compiler_params=pltpu.CompilerParams(vmem_limit_bytes=VMEM_LIMIT_BYTES),
        name="bias_tables",
    )(jnp.pad(rel_bias.T, ((0, 0), (0, LANES - N_BUCKETS))))


def _proj_kernel(x_ref, ng_ref, win_ref, qng_ref, kvng_ref, wq_ref, wuk_ref, pmat_ref,
                 cosq_ref, sinq_ref, rc_ref, rs1_ref, rs2_ref,
                 ckv_ref, kpe_ref, kb_ref, vb_ref, kidx_ref,
                 kcat_ref, kb16_ref, vb16_ref, kidx16_ref,
                 qcat_ref, qb_ref, qidx_ref, widx_ref, sga_ref, sgb_ref):
    x = x_ref[...]
    h = _rms(x, ng_ref[...])
    z = _dot(h.astype(BF16), win_ref[...])

    cq = _rms(z[:, C_CQ:C_CQ + A_Q_LORA], qng_ref[...])
    q = _dot(cq.astype(BF16), wq_ref[...])
    n_nope = A_HEADS * A_NOPE
    x1 = q[:, n_nope:n_nope + LANES]
    x2 = q[:, n_nope + LANES:n_nope + 2 * LANES]
    cos8, sin8 = cosq_ref[...], sinq_ref[...]
    o1 = x1 * cos8 - x2 * sin8
    o2 = x1 * sin8 + x2 * cos8
    mla_scale = (A_NOPE + A_ROPE) ** -0.5 * LOG2E
    q_lat = _dot(q[:, :n_nope].astype(BF16), wuk_ref[...]) * mla_scale
    pe = jnp.concatenate([o1, o2], axis=1) * mla_scale
    q_pe = _dot(pe.astype(BF16), pmat_ref[...])
    for hh in range(A_HEADS):
        qcat_ref[:, hh * QCAT:hh * QCAT + LANES] = q_lat[:, hh * LANES:(hh + 1) * LANES].astype(BF16)
        qcat_ref[:, hh * QCAT + LANES:(hh + 1) * QCAT] = q_pe[:, hh * LANES:(hh + 1) * LANES].astype(BF16)

    ckv = _rms(z[:, C_CKV:C_CKV + A_KV_LORA], kvng_ref[...])
    ckv_ref[...] = ckv
    misc = z[:, C_MISC:C_MISC + LANES]
    rot = (misc * rc_ref[...] + pltpu.roll(misc, A_ROPE // 2, 1) * rs1_ref[...]
           + pltpu.roll(misc, LANES - A_ROPE // 2, 1) * rs2_ref[...])
    kpe_ref[...] = rot[:, :A_ROPE]
    kcat_ref[:, :LANES] = ckv.astype(BF16)
    kcat_ref[:, LANES:] = rot.astype(BF16)

    kidx = misc[:, M_KIDX:M_KIDX + IDX_DIM]
    kidx_ref[...] = kidx
    kidx16_ref[...] = kidx.astype(BF16)
    widx_ref[...] = misc[:, M_WIDX:M_WIDX + IDX_HEADS] * (IDX_HEADS ** -0.5)
    kb = z[:, C_KB:C_KB + LANES]
    vb = z[:, C_VB:C_VB + LANES]
    kb_ref[...] = kb
    vb_ref[...] = vb
    kb16_ref[...] = kb.astype(BF16)
    vb16_ref[...] = vb.astype(BF16)
    qb_ref[...] = (z[:, C_QB:C_QB + B_WIDTH] * (B_HEAD_DIM ** -0.5 * LOG2E)).astype(BF16)
    qidx_ref[...] = (z[:, C_QI:C_QI + IDX_HEADS * IDX_DIM] * (IDX_DIM ** -0.5)).astype(BF16)
    sga_ref[...] = jax.nn.silu(z[:, C_GA:C_GA + A_WIDTH])
    sgb_ref[...] = jax.nn.silu(z[:, C_GB:C_GB + B_WIDTH])


def _project(x2d, rope_tabs, lw, *, period):
    n, d = x2d.shape
    tm = PROJ_TM
    assert n % tm == 0 and period % tm == 0
    n_rep = period // tm
    tok = lambda c: pl.BlockSpec((tm, c), lambda i: (i, 0))
    full = lambda a: pl.BlockSpec(a.shape, lambda i: (0,) * a.ndim)
    tab = pl.BlockSpec((tm, LANES), lambda i: (i % n_rep, 0))
    outs = [
        ("ckv", A_KV_LORA, F32), ("kpe", A_ROPE, F32), ("kb", LANES, F32), ("vb", LANES, F32),
        ("kidx", IDX_DIM, F32),
        ("kcat", QCAT, BF16), ("kb16", LANES, BF16), ("vb16", LANES, BF16), ("kidx16", IDX_DIM, BF16),
        ("qcat", A_HEADS * QCAT, BF16), ("qb", B_WIDTH, BF16), ("qidx", IDX_HEADS * IDX_DIM, BF16),
        ("widx", IDX_HEADS, F32), ("sga", A_WIDTH, F32), ("sgb", B_WIDTH, F32),
    ]
    res = pl.pallas_call(
        _proj_kernel,
        grid=(n // tm,),
        in_specs=[tok(d), full(lw["ng"]), full(lw["win"]), full(lw["qng"]), full(lw["kvng"]),
                  full(lw["wq"]), full(lw["wuk"]), full(lw["pmat"]), tab, tab, tab, tab, tab],
        out_specs=[tok(c) for _, c, _ in outs],
        out_shape=[jax.ShapeDtypeStruct((n, c), dt) for _, c, dt in outs],
        compiler_params=pltpu.CompilerParams(dimension_semantics=("arbitrary",),
                                             vmem_limit_bytes=VMEM_LIMIT_BYTES),
        name="project",
    )(x2d, lw["ng"], lw["win"], lw["qng"], lw["kvng"], lw["wq"], lw["wuk"], lw["pmat"], *rope_tabs)
    return {name: r for (name, _, _), r in zip(outs, res)}


R_CQ = 0
R_CKV = R_CQ + A_Q_LORA
R_VB = R_CKV + A_KV_LORA
R_QB = R_VB + LANES
R_QI = R_QB + B_WIDTH
R_GA = R_QI + IDX_HEADS * IDX_DIM
R_GB = R_GA + A_WIDTH
R_WI = R_GB + B_WIDTH
R_KB = R_WI + 16
R_KI = R_KB + LANES
R_KPE = R_KI + IDX_DIM
ROWS_T = R_KPE + A_ROPE
K_MISC = A_KV_LORA
K_KB = K_MISC + LANES
COLS_K = K_KB + LANES
MK_KPE = IDX_DIM
ONES_ROWS = 16
V_EXT = A_KV_LORA + ONES_ROWS


def _proj_t_kernel(x_ref, ng_ref, wk_ref, wt_ref, qng_ref, kvngc_ref, kvng_ref, wqt_ref, wukt_ref,
                   cost_ref, sint_ref, rc_ref, rs1_ref, rs2_ref,
                   ckv_ref, kcat_ref, kb16_ref, kidx16_ref,
                   kpeT_ref, kbT_ref, vbT_ref, kidxT_ref,
                   qcatT_ref, qbT_ref, qidxT_ref, widxT_ref, vmlaT_ref, vdsaT_ref, sgaT_ref, sgbT_ref):
    x = x_ref[...]
    tm = x.shape[0]
    hb = _rms(x, ng_ref[...]).astype(BF16)

    zk = _dot(hb, wk_ref[...])
    ckv = _rms(zk[:, :A_KV_LORA], kvng_ref[...])
    ckv_ref[...] = ckv
    misc = zk[:, K_MISC:K_MISC + LANES]
    lane = lax.broadcasted_iota(I32, (tm, LANES), 1)
    kidx16_ref[...] = jnp.where(lane < IDX_DIM, misc, 0.0).astype(BF16)
    rot = (misc * rc_ref[...] + pltpu.roll(misc, A_ROPE // 2, 1) * rs1_ref[...]
           + pltpu.roll(misc, LANES - A_ROPE // 2, 1) * rs2_ref[...])
    kcat_ref[:, :LANES] = ckv.astype(BF16)
    kcat_ref[:, LANES:] = pltpu.roll(rot, LANES - MK_KPE, 1).astype(BF16)
    kb16_ref[...] = zk[:, K_KB:K_KB + LANES].astype(BF16)

    zt = _dot_nt(wt_ref[...], hb)

    def rms_t(c, g):
        return c * lax.rsqrt(jnp.mean(c * c, axis=0, keepdims=True) + EPS) * g

    cq = rms_t(zt[R_CQ:R_CQ + A_Q_LORA], qng_ref[...])
    qt = _dot(wqt_ref[...], cq.astype(BF16))
    n_nope = A_HEADS * A_NOPE
    x1 = qt[n_nope:n_nope + LANES]
    x2 = qt[n_nope + LANES:n_nope + 2 * LANES]
    cos8, sin8 = cost_ref[...], sint_ref[...]
    mla_scale = (A_NOPE + A_ROPE) ** -0.5 * LOG2E
    o1 = (x1 * cos8 - x2 * sin8) * mla_scale
    o2 = (x1 * sin8 + x2 * cos8) * mla_scale
    q_lat = _dot(wukt_ref[...], qt[:n_nope].astype(BF16)) * mla_scale
    half = A_ROPE // 2
    for h in range(A_HEADS):
        qcatT_ref[h, :LANES, :] = q_lat[h * LANES:(h + 1) * LANES].astype(BF16)
        qcatT_ref[h, LANES:LANES + half, :] = o1[h * half:(h + 1) * half].astype(BF16)
        qcatT_ref[h, LANES + half:LANES + A_ROPE, :] = o2[h * half:(h + 1) * half].astype(BF16)
        qcatT_ref[h, LANES + A_ROPE:, :] = jnp.zeros((QCAT - LANES - A_ROPE, tm), BF16)
    ones = jnp.ones((ONES_ROWS, tm), BF16)
    vmlaT_ref[:A_KV_LORA, :] = rms_t(zt[R_CKV:R_CKV + A_KV_LORA], kvngc_ref[...]).astype(BF16)
    vmlaT_ref[A_KV_LORA:, :] = ones
    vb_t = zt[R_VB:R_VB + LANES]
    vdsaT_ref[:LANES, :] = vb_t.astype(BF16)
    vdsaT_ref[LANES:, :] = ones
    vbT_ref[...] = vb_t
    kbT_ref[...] = zt[R_KB:R_KB + LANES]
    kidxT_ref[...] = zt[R_KI:R_KI + IDX_DIM]
    k1, k2 = zt[R_KPE:R_KPE + half], zt[R_KPE + half:R_KPE + A_ROPE]
    cos1, sin1 = cos8[:half], sin8[:half]
    kpeT_ref[:half, :] = k1 * cos1 - k2 * sin1
    kpeT_ref[half:, :] = k1 * sin1 + k2 * cos1
    qbT_ref[...] = (zt[R_QB:R_QB + B_WIDTH] * (B_HEAD_DIM ** -0.5 * LOG2E)).astype(BF16)
    qidxT_ref[...] = (zt[R_QI:R_QI + IDX_HEADS * IDX_DIM] * (IDX_DIM ** -0.5)).astype(BF16)
    sgaT_ref[...] = jax.nn.silu(zt[R_GA:R_GA + A_WIDTH]).astype(BF16)
    sgbT_ref[...] = jax.nn.silu(zt[R_GB:R_GB + B_WIDTH]).astype(BF16)
    widxT_ref[...] = zt[R_WI:R_WI + IDX_HEADS] * (IDX_HEADS ** -0.5)


def _project_t(x2d, tabs, lw, b, t):
    n, d = x2d.shape
    tm = TQ
    nt = t // tm
    tok = lambda c: pl.BlockSpec((tm, c), lambda i: (i, 0))
    full = lambda a: pl.BlockSpec(a.shape, lambda i: (0,) * a.ndim)
    tab_t = pl.BlockSpec((LANES, tm), lambda i: (0, i % nt))
    tab_k = pl.BlockSpec((tm, LANES), lambda i: (i % nt, 0))
    rows_t = lambda r: pl.BlockSpec((None, r, tm), lambda i: (i // nt, 0, i % nt))
    outs = [
        ("ckv", (n, A_KV_LORA), F32, tok(A_KV_LORA)),
        ("kcat", (n, QCAT), BF16, tok(QCAT)), ("kb16", (n, LANES), BF16, tok(LANES)),
        ("kidx16", (n, LANES), BF16, tok(LANES)),
        ("kpeT", (b, A_ROPE, t), F32, rows_t(A_ROPE)), ("kbT", (b, LANES, t), F32, rows_t(LANES)),
        ("vbT", (b, LANES, t), F32, rows_t(LANES)), ("kidxT", (b, IDX_DIM, t), F32, rows_t(IDX_DIM)),
        ("qcatT", (b, A_HEADS, QCAT, t), BF16,
         pl.BlockSpec((None, A_HEADS, QCAT, tm), lambda i: (i // nt, 0, 0, i % nt))),
        ("qbT", (b, B_WIDTH, t), BF16, rows_t(B_WIDTH)),
        ("qidxT", (b, IDX_HEADS * IDX_DIM, t), BF16, rows_t(IDX_HEADS * IDX_DIM)),
        ("widxT", (b, IDX_HEADS, t), F32, rows_t(IDX_HEADS)),
        ("vmlaT", (b, nt, V_EXT, tm), BF16, pl.BlockSpec((None, None, V_EXT, tm), lambda i: (i // nt, i % nt, 0, 0))),
        ("vdsaT", (b, nt, V_EXT, tm), BF16, pl.BlockSpec((None, None, V_EXT, tm), lambda i: (i // nt, i % nt, 0, 0))),
        ("sgaT", (b, A_WIDTH, t), BF16, rows_t(A_WIDTH)), ("sgbT", (b, B_WIDTH, t), BF16, rows_t(B_WIDTH)),
    ]
    cost, sint, rc, rs1, rs2 = tabs
    res = pl.pallas_call(
        _proj_t_kernel,
        grid=(n // tm,),
        in_specs=[tok(d), full(lw["ng"]), full(lw["wk"]), full(lw["wt"]), full(lw["qngc"]), full(lw["kvngc"]),
                  full(lw["kvng"]), full(lw["wqt"]), full(lw["wukt"]), tab_t, tab_t, tab_k, tab_k, tab_k],
        out_specs=[o[3] for o in outs],
        out_shape=[jax.ShapeDtypeStruct(o[1], o[2]) for o in outs],
        compiler_params=pltpu.CompilerParams(dimension_semantics=("arbitrary",),
                                             vmem_limit_bytes=VMEM_LIMIT_BYTES),
        name="project_prompt",
    )(x2d, lw["ng"], lw["wk"], lw["wt"], lw["qngc"], lw["kvngc"], lw["kvng"], lw["wqt"], lw["wukt"],
      cost, sint, rc, rs1, rs2)
    return {o[0]: r for o, r in zip(outs, res)}


NEG_FLT_MAX = -3.4028234663852886e38
KEY_NEG_FLT_MAX = INT_MIN + (1 << 23)


def _key_to_float(k):
    k = jnp.maximum(k, KEY_NEG_FLT_MAX)
    return pltpu.bitcast(k ^ ((k >> 31) & 0x7FFFFFFF), F32)


def _count(pred):
    return jnp.sum(jnp.where(pred, 1.0, 0.0), axis=1, keepdims=True)


MAX_TIE_SWEEPS = 8.0
SAFE_DENOM_MIN = 2.0 ** -90
SAFE_DENOM_MAX = 2.0 ** 40


def _fold8(x, op=jnp.add):
    parts = [x[i:i + 8] for i in range(0, x.shape[0], 8)]
    while len(parts) > 1:
        parts = [op(a, b) for a, b in zip(parts[::2], parts[1::2])]
    return parts[0]


def _prompt_attn_kernel(qcatT_ref, qbT_ref, qidxT_ref, widxT_ref, kcat_ref, kb_ref, kidx_ref,
                        vmlaT_ref, vdsaT_ref, bias_ref, olatT_ref, obT_ref,
                        qa_ref, qbp_ref, qip_ref, sc_ref, m_ref, acc_ref, mb_ref, accb_ref, kn_ref, *, n_top):
    qi = pl.program_id(1)
    nblk = qi + 1
    tq = TQ
    krow = lax.broadcasted_iota(I32, (tq, tq), 0)
    qcol = lax.broadcasted_iota(I32, (tq, tq), 1)
    shift = CHUNK.bit_length() - 1
    diag_ok = (qcol >> shift) >= (krow >> shift)
    hcols = lambda h: slice(h * tq, (h + 1) * tq)

    def per_head(fn):
        return jnp.concatenate([fn(h) for h in range(A_HEADS)], axis=1)

    qbp_ref[...] = jnp.zeros(qbp_ref.shape, BF16)
    qip_ref[...] = jnp.zeros(qip_ref.shape, BF16)
    for h in range(A_HEADS):
        g = h // B_GROUP
        qa_ref[:, hcols(h)] = qcatT_ref[h]
        qbp_ref[g * B_HEAD_DIM:(g + 1) * B_HEAD_DIM, hcols(h)] = qbT_ref[h * B_HEAD_DIM:(h + 1) * B_HEAD_DIM, :]
        qip_ref[:IDX_DIM, hcols(h)] = qidxT_ref[h * IDX_DIM:(h + 1) * IDX_DIM, :]
    w_all = per_head(lambda h: widxT_ref[h:h + 1, :])

    @pl.when(qi == 0)
    def _():
        def max_row_norm2(k_ref):
            k = k_ref[...].astype(F32)
            return jnp.max(jnp.sum(k * k, axis=1, keepdims=True))
        kn_ref[0:1, :] = jnp.full((1, LANES), max_row_norm2(kcat_ref), F32)
        kn_ref[1:2, :] = jnp.full((1, LANES), max_row_norm2(kb_ref), F32)
        kn_ref[2:3, :] = jnp.full((1, LANES), jnp.max(bias_ref[...]), F32)

    def col_norm(q_ref):
        q = q_ref[...].astype(F32)
        return jnp.sqrt(jnp.sum(q * q, axis=0, keepdims=True))

    shift_a = jnp.sqrt(kn_ref[0:1, 0:1]) * col_norm(qa_ref)
    shift_b = jnp.sqrt(kn_ref[1:2, 0:1]) * col_norm(qbp_ref) + jnp.maximum(kn_ref[2:3, 0:1], 0.0)

    def accumulate(s_t, values, shift_or_m, acc_r, exact):
        if exact:
            m_prev = shift_or_m[0:1, :]
            m_new = jnp.maximum(m_prev, jnp.max(s_t, axis=0, keepdims=True))
            alpha = jnp.exp2(m_prev - m_new)
            shift_or_m[0:1, :] = m_new
        else:
            m_new = shift_or_m
        p_t = jnp.exp2(s_t - m_new).astype(BF16)
        for v_t, lanes in values:
            pv = _dot(v_t, p_t[:, lanes])
            acc_r[:, lanes] = (alpha[:, lanes] * acc_r[:, lanes] if exact else acc_r[:, lanes]) + pv

    def unsafe(l):
        return jnp.max(jnp.where((l >= SAFE_DENOM_MIN) & (l <= SAFE_DENOM_MAX), 0.0, 1.0)) > 0.0

    all_lanes = slice(0, A_HEADS * tq)

    def for_blocks(n, block):
        def pair(p, c):
            block(2 * p)
            block(2 * p + 1)
            return c

        lax.fori_loop(0, lax.shift_right_logical(n, 1), pair, 0)

        @pl.when((n & 1) == 1)
        def _():
            block(n - 1)

    def mla_pass(exact):
        acc_ref[...] = jnp.zeros(acc_ref.shape, F32)
        if exact:
            m_ref[...] = jnp.full(m_ref.shape, NEG_BIG, F32)

        def block(j, masked):
            start = pl.multiple_of(j * tq, tq)
            s_t = _dot(kcat_ref[pl.ds(start, tq), :], qa_ref[...])
            if masked:
                s_t = per_head(lambda h: jnp.where(diag_ok, s_t[:, hcols(h)], NEG_BIG))
            accumulate(s_t, [(vmlaT_ref[j], all_lanes)], m_ref if exact else shift_a, acc_ref, exact)
            if exact:
                return
            r = jnp.maximum(_dot(kidx_ref[pl.ds(start, tq), :], qip_ref[...]), 0.0) * w_all
            score = r[:, hcols(0)]
            for h in range(1, IDX_HEADS):
                score = score + r[:, hcols(h)]
            if masked:
                score = jnp.where(diag_ok, score, -jnp.inf)
            sc_ref[j] = score

        for_blocks(qi, lambda j: block(j, False))
        block(qi, True)

    mla_pass(False)

    @pl.when(unsafe(acc_ref[A_KV_LORA:A_KV_LORA + 1, :]))
    def _():
        mla_pass(True)

    o_t = acc_ref[:A_KV_LORA, :] * (1.0 / acc_ref[A_KV_LORA:A_KV_LORA + 1, :])
    for h in range(A_HEADS):
        olatT_ref[h * LANES:(h + 1) * LANES, :] = o_t[:, hcols(h)].astype(olatT_ref.dtype)

    def count(pred):
        def body(j, c):
            return c + _fold8(jnp.where(pred(sc_ref[j], j), 1.0, 0.0))
        part = lax.fori_loop(0, nblk, body, jnp.zeros((8, tq), F32))
        return jnp.sum(part, axis=0, keepdims=True)

    kf = float(n_top)

    def bis_body(it, carry):
        lo, cnt_lo = carry
        cand = lo + lax.shift_left(jnp.int32(1), 31 - it)
        cand_f = _key_to_float(cand)
        cnt = count(lambda s, j: s >= cand_f)
        take = cnt >= kf
        return jnp.where(take, cand, lo), jnp.where(take, cnt, cnt_lo)

    n_steps = jnp.where(nblk * tq <= n_top, 0, 32)
    lo, cnt_ge = lax.fori_loop(
        0, n_steps, bis_body, (jnp.full((1, tq), INT_MIN, I32), jnp.full((1, tq), 1e9, F32)))
    few = lo == INT_MIN
    thr = _key_to_float(lo)
    excess0 = jnp.where(few, 0.0, cnt_ge - kf)
    max_excess = jnp.max(excess0)

    def drop_from(cut):
        def body(j, c):
            s = sc_ref[j]
            sc_ref[j] = jnp.where((s == thr) & ((krow + j * tq) >= cut), -jnp.inf, s)
            return c
        lax.fori_loop(0, nblk, body, 0)

    @pl.when((max_excess > 0.0) & (max_excess <= MAX_TIE_SWEEPS))
    def _():
        def last_tie_below(cut):
            def body(j, m):
                pos = krow + j * tq
                hit = jnp.where((sc_ref[j] == thr) & (pos < cut), pos, -1)
                return jnp.maximum(m, _fold8(hit, jnp.maximum))
            part = lax.fori_loop(0, nblk, body, jnp.full((8, tq), -1, I32))
            return jnp.max(part, axis=0, keepdims=True)

        def sweep(c):
            excess, cut = c
            last = last_tie_below(cut)
            live = excess > 0.0
            return jnp.where(live, excess - 1.0, excess), jnp.where(live, last, cut)

        _, cut = lax.while_loop(lambda c: jnp.max(c[0]) > 0.0, sweep,
                                (excess0, jnp.full((1, tq), 2 ** 30, I32)))
        drop_from(cut)

    @pl.when(max_excess > MAX_TIE_SWEEPS)
    def _():
        need = kf - count(lambda s, j: s > thr)
        n_bits = (sc_ref.shape[0] * tq).bit_length()

        def cut_body(it, cpos):
            cand = cpos + lax.shift_left(jnp.int32(1), n_bits - 1 - it)
            cnt = count(lambda s, j: (s == thr) & ((krow + j * tq) < cand))
            return jnp.where(cnt < need, cand, cpos)

        keep = lax.fori_loop(0, n_bits, cut_body, jnp.zeros((1, tq), I32))
        drop_from(jnp.where(excess0 > 0.0, keep + 1, 2 ** 30))

    group_lanes = [slice(g * B_GROUP * tq, (g + 1) * B_GROUP * tq) for g in range(B_KV_HEADS)]

    def dsa_pass(exact):
        accb_ref[...] = jnp.zeros(accb_ref.shape, F32)
        if exact:
            mb_ref[...] = jnp.full(mb_ref.shape, NEG_BIG, F32)

        def block(j, near):
            start = pl.multiple_of(j * tq, tq)
            sel = sc_ref[j] >= thr
            s_t = _dot(kb_ref[pl.ds(start, tq), :], qbp_ref[...])
            if near:
                s_t = s_t + bias_ref[qi - j]
            s_t = per_head(lambda h: jnp.where(sel, s_t[:, hcols(h)], NEG_BIG))
            v_all = vdsaT_ref[j]
            values = [(jnp.concatenate([v_all[g * B_HEAD_DIM:(g + 1) * B_HEAD_DIM], v_all[LANES:]], axis=0),
                       group_lanes[g]) for g in range(B_KV_HEADS)]
            accumulate(s_t, values, mb_ref if exact else shift_b, accb_ref, exact)

        for_blocks(jnp.maximum(qi - 1, 0), lambda j: block(j, False))

        @pl.when(qi >= 1)
        def _():
            block(qi - 1, True)

        block(qi, True)

    dsa_pass(False)

    @pl.when(unsafe(accb_ref[B_HEAD_DIM:B_HEAD_DIM + 1, :]))
    def _():
        dsa_pass(True)

    inv_b = 1.0 / accb_ref[B_HEAD_DIM:B_HEAD_DIM + 1, :]
    for h in range(B_HEADS):
        obT_ref[h * B_HEAD_DIM:(h + 1) * B_HEAD_DIM, :] = (
            accb_ref[:B_HEAD_DIM, hcols(h)] * inv_b[:, hcols(h)]).astype(obT_ref.dtype)


def _prompt_attention(pr, bias_p, b, t, n_top):
    tq = TQ
    assert t % tq == 0
    nq = t // tq
    r3 = lambda a: a.reshape(b, t, a.shape[-1])
    qrows = lambda r: pl.BlockSpec((None, r, tq), lambda bi, qi: (bi, 0, qi))
    kall = lambda c: pl.BlockSpec((None, t, c), lambda bi, qi: (bi, 0, 0))
    vall = pl.BlockSpec((None, nq, V_EXT, tq), lambda bi, qi: (bi, 0, 0, 0))
    olat, ob = pl.pallas_call(
        functools.partial(_prompt_attn_kernel, n_top=n_top),
        grid=(b, nq),
        in_specs=[pl.BlockSpec((None, A_HEADS, QCAT, tq), lambda bi, qi: (bi, 0, 0, qi)),
                  qrows(B_WIDTH), qrows(IDX_HEADS * IDX_DIM), qrows(IDX_HEADS),
                  kall(QCAT), kall(LANES), kall(LANES), vall, vall,
                  pl.BlockSpec(bias_p.shape, lambda bi, qi: (0, 0, 0))],
        out_specs=[qrows(A_HEADS * A_KV_LORA), qrows(B_WIDTH)],
        out_shape=[jax.ShapeDtypeStruct((b, A_HEADS * A_KV_LORA, t), BF16),
                   jax.ShapeDtypeStruct((b, B_WIDTH, t), BF16)],
        scratch_shapes=[
            pltpu.VMEM((QCAT, A_HEADS * tq), BF16),
            pltpu.VMEM((LANES, B_HEADS * tq), BF16),
            pltpu.VMEM((LANES, IDX_HEADS * tq), BF16),
            pltpu.VMEM((nq, tq, tq), F32),
            pltpu.VMEM((8, A_HEADS * tq), F32),
            pltpu.VMEM((V_EXT, A_HEADS * tq), F32),
            pltpu.VMEM((8, B_HEADS * tq), F32),
            pltpu.VMEM((B_HEAD_DIM + ONES_ROWS, B_HEADS * tq), F32),
            pltpu.VMEM((8, LANES), F32),
        ],
        compiler_params=pltpu.CompilerParams(dimension_semantics=("arbitrary", "arbitrary"),
                                             vmem_limit_bytes=VMEM_LIMIT_BYTES),
        name="prompt_attention",
    )(pr["qcatT"], pr["qbT"], pr["qidxT"], pr["widxT"],
      r3(pr["kcat"]), r3(pr["kb16"]), r3(pr["kidx16"]), pr["vmlaT"], pr["vdsaT"], bias_p)
    return olat, ob


def _sample_attn_kernel(qcat_ref, qb_ref, qidx_ref, widx_ref, kcatn_ref, kbn_ref, vbn_ref, kidxn_ref,
                        cckv_ref, ckpeT_ref, ckT_ref, cvT_ref, ckidxT_ref, bias_ref,
                        olat_ref, ob_ref, *, n_top, t_new, past):
    tq = t_new
    pad = LANES
    n_keys = past + pad

    def padrows(a):
        return jnp.concatenate([a, jnp.zeros((pad - t_new, a.shape[1]), a.dtype)], axis=0)

    ckv_c = cckv_ref[...].astype(BF16)
    kpe_t = ckpeT_ref[...].astype(BF16)
    k_t = ckT_ref[...].astype(BF16)
    v_t = cvT_ref[...].astype(BF16)
    kidx_t = ckidxT_ref[...].astype(BF16)
    kcat_n = padrows(kcatn_ref[...])
    k_n = padrows(kbn_ref[...])
    v_n = padrows(vbn_ref[...])
    kidx_n = padrows(kidxn_ref[...])

    def softmax(s):
        m = jnp.max(s, axis=1, keepdims=True)
        p = jnp.exp2(s - m)
        return p.astype(BF16), jnp.sum(p, axis=1, keepdims=True)

    def new_cols(rows):
        return lax.broadcasted_iota(I32, (rows, pad), 1) < t_new

    qs = jnp.concatenate([qcat_ref[:, h * QCAT:(h + 1) * QCAT] for h in range(A_HEADS)], axis=0)
    s_c = _dot_nt(qs[:, :A_KV_LORA], ckv_c) + _dot(qs[:, A_KV_LORA:A_KV_LORA + A_ROPE], kpe_t)
    s_n = jnp.where(new_cols(A_HEADS * tq), _dot_nt(qs, kcat_n), NEG_BIG)
    pb, l = softmax(jnp.concatenate([s_c, s_n], axis=1))
    o = (_dot(pb[:, :past], ckv_c) + _dot(pb[:, past:], kcat_n[:, :A_KV_LORA])) / l
    for h in range(A_HEADS):
        olat_ref[:, h * LANES:(h + 1) * LANES] = o[h * tq:(h + 1) * tq].astype(olat_ref.dtype)

    qis = jnp.concatenate([qidx_ref[:, h * IDX_DIM:(h + 1) * IDX_DIM] for h in range(IDX_HEADS)], axis=0)

    def head_sum(dots):
        acc = jnp.maximum(dots[:tq], 0.0) * widx_ref[:, 0:1]
        for h in range(1, IDX_HEADS):
            acc = acc + jnp.maximum(dots[h * tq:(h + 1) * tq], 0.0) * widx_ref[:, h:h + 1]
        return acc

    score = jnp.concatenate(
        [head_sum(_dot(qis, kidx_t)), jnp.where(new_cols(tq), head_sum(_dot_nt(qis, kidx_n)), -jnp.inf)], axis=1)
    kf = float(n_top)

    def bis_body(it, carry):
        lo, cnt_lo = carry
        cand = lo + lax.shift_left(jnp.int32(1), 31 - it)
        cnt = _count(score >= _key_to_float(cand))
        take = cnt >= kf
        return jnp.where(take, cand, lo), jnp.where(take, cnt, cnt_lo)

    lo, cnt_ge = lax.fori_loop(
        0, 32, bis_body, (jnp.full((tq, 1), INT_MIN, I32), jnp.full((tq, 1), 1e9, F32)), unroll=True)
    few = lo == INT_MIN
    thr = _key_to_float(lo)
    has_tie = jnp.max(jnp.where((cnt_ge > kf) & (~few), 1.0, 0.0))
    n_bits = n_keys.bit_length()

    def drop_ties():
        cols = lax.broadcasted_iota(I32, (tq, n_keys), 1)
        need = kf - _count(score > thr)
        eq = score == thr

        def cut_body(it, cpos):
            cand = cpos + lax.shift_left(jnp.int32(1), n_bits - 1 - it)
            cnt = _count(eq & (cols < cand))
            return jnp.where(cnt < need, cand, cpos)

        keep = lax.fori_loop(0, n_bits, cut_body, jnp.zeros((tq, 1), I32))
        return jnp.where(eq & (cols > keep) & (cnt_ge > kf) & (~few), -jnp.inf, score)

    sel = lax.cond(has_tie > 0.0, drop_ties, lambda: score) >= thr

    n_far = n_keys - bias_ref.shape[-1]
    for g in range(B_KV_HEADS):
        feats = slice(g * B_HEAD_DIM, (g + 1) * B_HEAD_DIM)
        qg = jnp.concatenate(
            [qb_ref[:, h * B_HEAD_DIM:(h + 1) * B_HEAD_DIM] for h in range(g * B_GROUP, (g + 1) * B_GROUP)],
            axis=0)
        sg = jnp.concatenate([_dot(qg, k_t[feats]), _dot_nt(qg, k_n[:, feats])], axis=1)
        for hh in range(B_GROUP):
            h = g * B_GROUP + hh
            near = bias_ref[0, h] - bias_ref[1, h]
            sh = sg[hh * tq:(hh + 1) * tq]
            sh = jnp.concatenate([sh[:, :n_far], sh[:, n_far:] + near], axis=1)
            pb, l = softmax(jnp.where(sel, sh, NEG_BIG))
            oh = (_dot_nt(pb[:, :past], v_t[feats]) + _dot(pb[:, past:], v_n[:, feats])) / l
            ob_ref[:, h * B_HEAD_DIM:(h + 1) * B_HEAD_DIM] = oh


def _sample_attention(pr, caches, bias_s, b, t_new, past, n_top):
    r3 = lambda a: a.reshape(b, t_new, a.shape[-1])
    per_b = lambda n, c: pl.BlockSpec((None, n, c), lambda bi: (bi, 0, 0))
    news = [pr["qcat"], pr["qb"], pr["qidx"], pr["widx"], pr["kcat"], pr["kb16"], pr["vb16"], pr["kidx16"]]
    olat, ob = pl.pallas_call(
        functools.partial(_sample_attn_kernel, n_top=n_top, t_new=t_new, past=past),
        grid=(b,),
        in_specs=[per_b(t_new, a.shape[-1]) for a in news]
                 + [per_b(c.shape[1], c.shape[2]) for c in caches]
                 + [pl.BlockSpec(bias_s.shape, lambda bi: (0, 0, 0, 0))],
        out_specs=[per_b(t_new, A_HEADS * A_KV_LORA), per_b(t_new, B_WIDTH)],
        out_shape=[jax.ShapeDtypeStruct((b, t_new, A_HEADS * A_KV_LORA), BF16),
                   jax.ShapeDtypeStruct((b, t_new, B_WIDTH), F32)],
        compiler_params=pltpu.CompilerParams(dimension_semantics=("arbitrary",),
                                             vmem_limit_bytes=VMEM_LIMIT_BYTES),
        name="sample_attention",
    )(*[r3(a) for a in news], *caches, bias_s)
    return olat.reshape(b * t_new, -1), ob.reshape(b * t_new, -1)


def _combine_kernel(x_ref, olat_ref, ob_ref, sga_ref, sgb_ref, wuv_ref, wout_ref, fg_ref, y_ref, *, final):
    o_a = _dot(olat_ref[...], wuv_ref[...])
    mix = jnp.concatenate([o_a * sga_ref[...], ob_ref[...] * sgb_ref[...]], axis=1)
    y = x_ref[...] + _dot(mix.astype(BF16), wout_ref[...])
    if final:
        y = _rms(y, fg_ref[...])
    y_ref[...] = y


def _combine(x2d, olat, ob, pr, lw, fg, final):
    n, d = x2d.shape
    tm = PROJ_TM
    tok = lambda c: pl.BlockSpec((tm, c), lambda i: (i, 0))
    full = lambda a: pl.BlockSpec(a.shape, lambda i: (0,) * a.ndim)
    return pl.pallas_call(
        functools.partial(_combine_kernel, final=final),
        grid=(n // tm,),
        in_specs=[tok(d), tok(olat.shape[1]), tok(ob.shape[1]), tok(A_WIDTH), tok(B_WIDTH),
                  full(lw["wuv"]), full(lw["wout"]), full(fg)],
        out_specs=tok(d),
        out_shape=jax.ShapeDtypeStruct((n, d), F32),
        compiler_params=pltpu.CompilerParams(dimension_semantics=("arbitrary",),
                                             vmem_limit_bytes=VMEM_LIMIT_BYTES),
        name="combine",
    )(x2d, olat, ob, pr["sga"], pr["sgb"], lw["wuv"], lw["wout"], fg)


def _combine_t_kernel(x_ref, olatT_ref, obT_ref, sgaT_ref, sgbT_ref, wuvt_ref, wout_ref, fg_ref, y_ref, *, final):
    o_a = _dot(wuvt_ref[...], olatT_ref[...])
    mix_t = jnp.concatenate([o_a * sgaT_ref[...].astype(F32),
                             obT_ref[...].astype(F32) * sgbT_ref[...].astype(F32)], axis=0).astype(BF16)
    y = x_ref[...] + lax.dot_general(mix_t, wout_ref[...], (((0,), (0,)), ((), ())),
                                     preferred_element_type=F32)
    if final:
        y = _rms(y, fg_ref[...])
    y_ref[...] = y


def _combine_t(x3d, olat_t, ob_t, pr, lw, fg, final):
    b, t, d = x3d.shape
    tm = TQ
    rows_t = lambda r: pl.BlockSpec((None, r, tm), lambda bi, ti: (bi, 0, ti))
    full = lambda a: pl.BlockSpec(a.shape, lambda bi, ti: (0,) * a.ndim)
    xblk = pl.BlockSpec((None, tm, d), lambda bi, ti: (bi, ti, 0))
    return pl.pallas_call(
        functools.partial(_combine_t_kernel, final=final),
        grid=(b, t // tm),
        in_specs=[xblk, rows_t(A_HEADS * A_KV_LORA), rows_t(B_WIDTH), rows_t(A_WIDTH), rows_t(B_WIDTH),
                  full(lw["wuvt"]), full(lw["wout"]), full(fg)],
        out_specs=xblk,
        out_shape=jax.ShapeDtypeStruct((b, t, d), F32),
        compiler_params=pltpu.CompilerParams(dimension_semantics=("arbitrary", "arbitrary"),
                                             vmem_limit_bytes=VMEM_LIMIT_BYTES),
        name="combine_prompt",
    )(x3d, olat_t, ob_t, pr["sgaT"], pr["sgbT"], lw["wuvt"], lw["wout"], fg)


def _layer_weights(norm_g, w_in, q_norm_g, kv_norm_g, w_uq, w_uk, w_uv, w_out):
    d = w_in.shape[0]
    o = np.cumsum([0, A_Q_LORA, A_KV_LORA, A_ROPE, A_WIDTH, B_WIDTH, B_KV_HEADS * B_HEAD_DIM,
                   B_KV_HEADS * B_HEAD_DIM, IDX_HEADS * IDX_DIM, IDX_DIM, IDX_HEADS, B_WIDTH])
    seg = lambda i: w_in[:, int(o[i]):int(o[i + 1])]
    misc = jnp.concatenate([seg(2), seg(8), seg(9),
                            jnp.zeros((d, LANES - A_ROPE - IDX_DIM - IDX_HEADS), w_in.dtype)], axis=1)
    win = jnp.concatenate([seg(0), seg(1), misc, seg(3), seg(4), seg(5), seg(6), seg(7), seg(10)], axis=1)
    assert win.shape[1] == IN_PAD
    half = A_ROPE // 2
    wq = jnp.concatenate([
        w_uq[:, :, :A_NOPE].reshape(A_Q_LORA, A_HEADS * A_NOPE),
        w_uq[:, :, A_NOPE:A_NOPE + half].reshape(A_Q_LORA, A_HEADS * half),
        w_uq[:, :, A_NOPE + half:].reshape(A_Q_LORA, A_HEADS * half)], axis=1)
    eye = jnp.eye(A_HEADS, dtype=w_uk.dtype)
    wuk = jnp.einsum('chn,hg->hngc', w_uk, eye).reshape(A_HEADS * A_NOPE, A_HEADS * A_KV_LORA)
    wuv = jnp.einsum('chv,hg->hcgv', w_uv, eye).reshape(A_HEADS * A_KV_LORA, A_HEADS * A_V)
    pm = np.zeros((2 * LANES, A_HEADS * LANES), np.float32)
    for h in range(A_HEADS):
        for i in range(half):
            pm[h * half + i, h * LANES + i] = 1.0
            pm[LANES + h * half + i, h * LANES + half + i] = 1.0
    zpad = lambda c: jnp.zeros((d, c), w_in.dtype)
    wk = jnp.concatenate([seg(1), seg(8), seg(2), zpad(LANES - IDX_DIM - A_ROPE), seg(5)], axis=1)
    wt = jnp.concatenate([seg(0), seg(1), seg(6), seg(4), seg(7), seg(3), seg(10), seg(9),
                          zpad(R_KB - R_WI - IDX_HEADS), seg(5), seg(8), seg(2)], axis=1).T
    assert wk.shape[1] == COLS_K and wt.shape[0] == ROWS_T
    bc = lambda g: jnp.broadcast_to(g.reshape(-1, 1), (g.shape[0], TQ))
    return {
        "wk": wk.astype(BF16), "wt": wt.astype(BF16), "qngc": bc(q_norm_g), "kvngc": bc(kv_norm_g),
        "wqt": wq.T.astype(BF16), "wukt": wuk.T.astype(BF16), "wuvt": wuv.T.astype(BF16),
        "ng": norm_g.reshape(1, -1), "win": win.astype(BF16),
        "qng": q_norm_g.reshape(1, -1), "kvng": kv_norm_g.reshape(1, -1),
        "wq": wq.astype(BF16), "wuk": wuk.astype(BF16), "wuv": wuv.astype(BF16),
        "pmat": jnp.asarray(pm, BF16), "wout": w_out.astype(BF16),
    }


def _rope_tables(pos):
    half = A_ROPE // 2
    inv = ROPE_THETA ** (-jnp.arange(half, dtype=F32) / half)
    ang = pos.astype(F32)[:, None] * inv[None, :]
    cos, sin = jnp.cos(ang), jnp.sin(ang)
    z = jnp.zeros((pos.shape[0], LANES - A_ROPE), F32)
    zh = jnp.zeros_like(cos)
    cosq = jnp.tile(cos, (1, A_HEADS))
    sinq = jnp.tile(sin, (1, A_HEADS))
    rc = jnp.concatenate([cos, cos, z], axis=1)
    rs1 = jnp.concatenate([zh, sin, z], axis=1)
    rs2 = jnp.concatenate([-sin, zh, z], axis=1)
    return cosq, sinq, rc, rs1, rs2


def _rope_tables_t(pos):
    half = A_ROPE // 2
    inv = ROPE_THETA ** (-jnp.arange(half, dtype=F32) / half)
    ang = pos.astype(F32)[:, None] * inv[None, :]
    cos, sin = jnp.cos(ang), jnp.sin(ang)
    cost = jnp.tile(cos.T, (A_HEADS, 1))
    sint = jnp.tile(sin.T, (A_HEADS, 1))
    z0 = jnp.zeros((pos.shape[0], MK_KPE), F32)
    z1 = jnp.zeros((pos.shape[0], LANES - MK_KPE - A_ROPE), F32)
    zh = jnp.zeros_like(cos)
    rc = jnp.concatenate([z0, cos, cos, z1], axis=1)
    rs1 = jnp.concatenate([z0, zh, sin, z1], axis=1)
    rs2 = jnp.concatenate([z0, -sin, zh, z1], axis=1)
    return cost, sint, rc, rs1, rs2


def kernel(x_prompt, x_sample, cache_mla_ckv, cache_mla_kpe, cache_dsa_k, cache_dsa_v, cache_dsa_kidx,
           norm_g, w_in, mla_q_norm_g, mla_kv_norm_g, mla_w_uq, mla_w_uk, mla_w_uv, rel_bias, w_out,
           final_norm_g):
    bp, tp, d = x_prompt.shape
    bs, ts, _ = x_sample.shape
    depth = w_in.shape[0]
    past = cache_mla_ckv.shape[2]
    n_top_p = min(TOP_K_MAX, tp // 4)
    n_top_s = min(TOP_K_MAX, (past + ts) // 4)
    assert ts <= CHUNK and past % CHUNK == 0 and past % LANES == 0

    rope_p = _rope_tables_t(jnp.arange(tp, dtype=jnp.int32))
    reps = PROJ_TM // ts
    rope_s = tuple(jnp.tile(a, (reps, 1)) for a in _rope_tables(past + jnp.arange(ts, dtype=jnp.int32)))

    bias_p = _bias_tables(rel_bias, (0, -TQ, -3 * TQ), TQ, TQ, True)
    win_s = 2 * LANES
    bias_s = _bias_tables(rel_bias, (-(win_s - LANES), -(past + win_s)), ts, win_s, False)
    fg = final_norm_g.reshape(1, -1)

    xp = x_prompt
    xs = x_sample.reshape(bs * ts, d)
    outs_p, outs_s = [], []
    for l in range(depth):
        lw = _layer_weights(norm_g[l], w_in[l], mla_q_norm_g[l], mla_kv_norm_g[l],
                            mla_w_uq[l], mla_w_uk[l], mla_w_uv[l], w_out[l])
        final = l == depth - 1
        pr = _project_t(xp.reshape(bp * tp, d), rope_p, lw, bp, tp)
        olat, ob = _prompt_attention(pr, bias_p, bp, tp, n_top_p)
        xp = _combine_t(xp, olat, ob, pr, lw, fg, final)
        heads_t = lambda a: a.reshape(bp, B_KV_HEADS, B_HEAD_DIM, tp).transpose(0, 3, 1, 2)
        outs_p.append((pr["ckv"].reshape(bp, tp, A_KV_LORA), pr["kpeT"].transpose(0, 2, 1),
                       heads_t(pr["kbT"]), heads_t(pr["vbT"]), pr["kidxT"].transpose(0, 2, 1)))
        ps = _project(xs, rope_s, lw, period=PROJ_TM)
        feat_t = lambda a: a.transpose(0, 2, 3, 1).reshape(bs, B_KV_HEADS * B_HEAD_DIM, past)
        caches = (cache_mla_ckv[l], cache_mla_kpe[l].transpose(0, 2, 1), feat_t(cache_dsa_k[l]),
                  feat_t(cache_dsa_v[l]), cache_dsa_kidx[l].transpose(0, 2, 1))
        olat, ob = _sample_attention(ps, caches, bias_s, bs, ts, past, n_top_s)
        xs = _combine(xs, olat, ob, ps, lw, fg, final)
        outs_s.append((ps["ckv"].reshape(bs, ts, A_KV_LORA), ps["kpe"].reshape(bs, ts, A_ROPE),
                       ps["kb"].reshape(bs, ts, B_KV_HEADS, B_HEAD_DIM),
                       ps["vb"].reshape(bs, ts, B_KV_HEADS, B_HEAD_DIM),
                       ps["kidx"].reshape(bs, ts, IDX_DIM)))

    stack = lambda outs, i: jnp.stack([o[i] for o in outs])
    return ((xp, xs.reshape(bs, ts, d))
            + tuple(stack(outs_p, i) for i in range(5)) + tuple(stack(outs_s, i) for i in range(5)))
```

```python
import functools
import math

import jax
import jax.numpy as jnp
import numpy as np
from jax import lax
from jax.experimental import pallas as pl
from jax.experimental.pallas import tpu as pltpu

F32 = jnp.float32
BF16 = jnp.bfloat16
I32 = jnp.int32

CHUNK = 64
EPS = 1e-6
A_HEADS = 8
A_NOPE = 64
A_ROPE = 32
A_V = 64
A_Q_LORA = 256
A_KV_LORA = 128
ROPE_THETA = 10000.0
A_WIDTH = A_HEADS * A_V
B_HEADS = 8
B_KV_HEADS = 2
B_HEAD_DIM = 64
B_WIDTH = B_HEADS * B_HEAD_DIM
B_GROUP = B_HEADS // B_KV_HEADS
IDX_HEADS = 8
IDX_DIM = 64
TOP_K_MAX = 256
N_BUCKETS = 32
MAX_DISTANCE = 128

LANES = 128
VMEM_LIMIT_BYTES = 56 * 1024 * 1024

LOG2E = 1.4426950408889634
NEG_BIG = -1e30
INT_MIN = -(2 ** 31)

C_CQ = 0
C_CKV = C_CQ + A_Q_LORA
C_MISC = C_CKV + A_KV_LORA
C_GA = C_MISC + LANES
C_QB = C_GA + A_WIDTH
C_KB = C_QB + B_WIDTH
C_VB = C_KB + B_KV_HEADS * B_HEAD_DIM
C_QI = C_VB + B_KV_HEADS * B_HEAD_DIM
C_GB = C_QI + IDX_HEADS * IDX_DIM
IN_PAD = C_GB + B_WIDTH
M_KPE = 0
M_KIDX = A_ROPE
M_WIDX = A_ROPE + IDX_DIM

QCAT = 2 * LANES
TQ = 256
PROJ_TM = 256


def _dot(a, b):
    return jnp.dot(a, b, preferred_element_type=F32)


def _dot_nt(a, b):
    return lax.dot_general(a, b, (((1,), (1,)), ((), ())), preferred_element_type=F32)


def _rms(x, g):
    return x * lax.rsqrt(jnp.mean(x * x, axis=-1, keepdims=True) + EPS) * g


def _bias_kernel(rb_ref, out_ref, *, offsets, keys_on_rows):
    nb = N_BUCKETS // 2
    max_exact = nb // 2
    n_r = out_ref.shape[1] if keys_on_rows else out_ref.shape[2]
    n_c = out_ref.shape[2] // B_HEADS if keys_on_rows else out_ref.shape[3]
    row = lax.broadcasted_iota(I32, (n_r, n_c), 0)
    col = lax.broadcasted_iota(I32, (n_r, n_c), 1)

    def bucket_of(off):
        rel = off + (row - col if keys_on_rows else col - row)
        ret = jnp.where(rel > 0, nb, 0)
        n = jnp.abs(rel)
        nf = jnp.maximum(n, 1).astype(F32)
        large = max_exact + (jnp.log(nf / max_exact) / math.log(MAX_DISTANCE / max_exact)
                             * (nb - max_exact)).astype(I32)
        large = jnp.minimum(large, nb - 1)
        return ret + jnp.where(n < max_exact, n, large)

    def lookup(bucket, h):
        table = jnp.broadcast_to(rb_ref[h:h + 1, :], (n_r, LANES))
        return jnp.concatenate(
            [jnp.take_along_axis(table, bucket[:, c:c + LANES], axis=1) for c in range(0, n_c, LANES)], axis=1)

    buckets = [bucket_of(off) for off in offsets]
    for h in range(B_HEADS):
        if keys_on_rows:
            far = lookup(buckets[-1], h)
            for p in range(len(offsets) - 1):
                out_ref[p, :, h * n_c:(h + 1) * n_c] = (lookup(buckets[p], h) - far) * LOG2E
        else:
            for p in range(len(offsets)):
                out_ref[p, h] = lookup(buckets[p], h) * LOG2E


def _bias_tables(rel_bias, offsets, n_r, n_c, keys_on_rows):
    shape = (len(offsets) - 1, n_r, B_HEADS * n_c) if keys_on_rows else (len(offsets), B_HEADS, n_r, n_c)
    return pl.pallas_call(
        functools.partial(_bias_kernel, offsets=tuple(offsets), keys_on_rows=keys_on_rows),
        out_shape=jax.ShapeDtypeStruct(shape, F32),
        in_specs=[pl.BlockSpec(memory_space=pltpu.VMEM)],
        out_specs=pl.BlockSpec(memory_space=pltpu.VMEM),
        compiler_params=pltpu.CompilerParams(vmem_limit_bytes=VMEM_LIMIT_BYTES),
        name="bias_tables",
    )(jnp.pad(rel_bias.T, ((0, 0), (0, LANES - N_BUCKETS))))


def _proj_kernel(x_ref, ng_ref, win_ref, qng_ref, kvng_ref, wq_ref, wuk_ref, pmat_ref,
                 cosq_ref, sinq_ref, rc_ref, rs1_ref, rs2_ref,
                 ckv_ref, kpe_ref, kb_ref, vb_ref, kidx_ref,
                 kcat_ref, kb16_ref, vb16_ref, kidx16_ref,
                 qcat_ref, qb_ref, qidx_ref, widx_ref, sga_ref, sgb_ref):
    x = x_ref[...]
    h = _rms(x, ng_ref[...])
    z = _dot(h.astype(BF16), win_ref[...])

    cq = _rms(z[:, C_CQ:C_CQ + A_Q_LORA], qng_ref[...])
    q = _dot(cq.astype(BF16), wq_ref[...])
    n_nope = A_HEADS * A_NOPE
    x1 = q[:, n_nope:n_nope + LANES]
    x2 = q[:, n_nope + LANES:n_nope + 2 * LANES]
    cos8, sin8 = cosq_ref[...], sinq_ref[...]
    o1 = x1 * cos8 - x2 * sin8
    o2 = x1 * sin8 + x2 * cos8
    mla_scale = (A_NOPE + A_ROPE) ** -0.5 * LOG2E
    q_lat = _dot(q[:, :n_nope].astype(BF16), wuk_ref[...]) * mla_scale
    pe = jnp.concatenate([o1, o2], axis=1) * mla_scale
    q_pe = _dot(pe.astype(BF16), pmat_ref[...])
    for hh in range(A_HEADS):
        qcat_ref[:, hh * QCAT:hh * QCAT + LANES] = q_lat[:, hh * LANES:(hh + 1) * LANES].astype(BF16)
        qcat_ref[:, hh * QCAT + LANES:(hh + 1) * QCAT] = q_pe[:, hh * LANES:(hh + 1) * LANES].astype(BF16)

    ckv = _rms(z[:, C_CKV:C_CKV + A_KV_LORA], kvng_ref[...])
    ckv_ref[...] = ckv
    misc = z[:, C_MISC:C_MISC + LANES]
    rot = (misc * rc_ref[...] + pltpu.roll(misc, A_ROPE // 2, 1) * rs1_ref[...]
           + pltpu.roll(misc, LANES - A_ROPE // 2, 1) * rs2_ref[...])
    kpe_ref[...] = rot[:, :A_ROPE]
    kcat_ref[:, :LANES] = ckv.astype(BF16)
    kcat_ref[:, LANES:] = rot.astype(BF16)

    kidx = misc[:, M_KIDX:M_KIDX + IDX_DIM]
    kidx_ref[...] = kidx
    kidx16_ref[...] = kidx.astype(BF16)
    widx_ref[...] = misc[:, M_WIDX:M_WIDX + IDX_HEADS] * (IDX_HEADS ** -0.5)
    kb = z[:, C_KB:C_KB + LANES]
    vb = z[:, C_VB:C_VB + LANES]
    kb_ref[...] = kb
    vb_ref[...] = vb
    kb16_ref[...] = kb.astype(BF16)
    vb16_ref[...] = vb.astype(BF16)
    qb_ref[...] = (z[:, C_QB:C_QB + B_WIDTH] * (B_HEAD_DIM ** -0.5 * LOG2E)).astype(BF16)
    qidx_ref[...] = (z[:, C_QI:C_QI + IDX_HEADS * IDX_DIM] * (IDX_DIM ** -0.5)).astype(BF16)
    sga_ref[...] = jax.nn.silu(z[:, C_GA:C_GA + A_WIDTH])
    sgb_ref[...] = jax.nn.silu(z[:, C_GB:C_GB + B_WIDTH])


def _project(x2d, rope_tabs, lw, *, period):
    n, d = x2d.shape
    tm = PROJ_TM
    assert n % tm == 0 and period % tm == 0
    n_rep = period // tm
    tok = lambda c: pl.BlockSpec((tm, c), lambda i: (i, 0))
    full = lambda a: pl.BlockSpec(a.shape, lambda i: (0,) * a.ndim)
    tab = pl.BlockSpec((tm, LANES), lambda i: (i % n_rep, 0))
    outs = [
        ("ckv", A_KV_LORA, F32), ("kpe", A_ROPE, F32), ("kb", LANES, F32), ("vb", LANES, F32),
        ("kidx", IDX_DIM, F32),
        ("kcat", QCAT, BF16), ("kb16", LANES, BF16), ("vb16", LANES, BF16), ("kidx16", IDX_DIM, BF16),
        ("qcat", A_HEADS * QCAT, BF16), ("qb", B_WIDTH, BF16), ("qidx", IDX_HEADS * IDX_DIM, BF16),
        ("widx", IDX_HEADS, F32), ("sga", A_WIDTH, F32), ("sgb", B_WIDTH, F32),
    ]
    res = pl.pallas_call(
        _proj_kernel,
        grid=(n // tm,),
        in_specs=[tok(d), full(lw["ng"]), full(lw["win"]), full(lw["qng"]), full(lw["kvng"]),
                  full(lw["wq"]), full(lw["wuk"]), full(lw["pmat"]), tab, tab, tab, tab, tab],
        out_specs=[tok(c) for _, c, _ in outs],
        out_shape=[jax.ShapeDtypeStruct((n, c), dt) for _, c, dt in outs],
        compiler_params=pltpu.CompilerParams(dimension_semantics=("arbitrary",),
                                             vmem_limit_bytes=VMEM_LIMIT_BYTES),
        name="project",
    )(x2d, lw["ng"], lw["win"], lw["qng"], lw["kvng"], lw["wq"], lw["wuk"], lw["pmat"], *rope_tabs)
    return {name: r for (name, _, _), r in zip(outs, res)}


R_CQ = 0
R_CKV = R_CQ + A_Q_LORA
R_VB = R_CKV + A_KV_LORA
R_QB = R_VB + LANES
R_QI = R_QB + B_WIDTH
R_GA = R_QI + IDX_HEADS * IDX_DIM
R_GB = R_GA + A_WIDTH
R_WI = R_GB + B_WIDTH
R_KB = R_WI + 16
R_KI = R_KB + LANES
R_KPE = R_KI + IDX_DIM
ROWS_T = R_KPE + A_ROPE
K_MISC = A_KV_LORA
K_KB = K_MISC + LANES
COLS_K = K_KB + LANES
MK_KPE = IDX_DIM
ONES_ROWS = 16
V_EXT = A_KV_LORA + ONES_ROWS


def _proj_t_kernel(x_ref, ng_ref, wk_ref, wt_ref, qng_ref, kvngc_ref, kvng_ref, wqt_ref, wukt_ref,
                   cost_ref, sint_ref, rc_ref, rs1_ref, rs2_ref,
                   ckv_ref, kcat_ref, kb16_ref, kidx16_ref,
                   kpeT_ref, kbT_ref, vbT_ref, kidxT_ref,
                   qcatT_ref, qbT_ref, qidxT_ref, widxT_ref, vmlaT_ref, vdsaT_ref, sgaT_ref, sgbT_ref):
    x = x_ref[...]
    tm = x.shape[0]
    hb = _rms(x, ng_ref[...]).astype(BF16)

    zk = _dot(hb, wk_ref[...])
    ckv = _rms(zk[:, :A_KV_LORA], kvng_ref[...])
    ckv_ref[...] = ckv
    misc = zk[:, K_MISC:K_MISC + LANES]
    lane = lax.broadcasted_iota(I32, (tm, LANES), 1)
    kidx16_ref[...] = jnp.where(lane < IDX_DIM, misc, 0.0).astype(BF16)
    rot = (misc * rc_ref[...] + pltpu.roll(misc, A_ROPE // 2, 1) * rs1_ref[...]
           + pltpu.roll(misc, LANES - A_ROPE // 2, 1) * rs2_ref[...])
    kcat_ref[:, :LANES] = ckv.astype(BF16)
    kcat_ref[:, LANES:] = pltpu.roll(rot, LANES - MK_KPE, 1).astype(BF16)
    kb16_ref[...] = zk[:, K_KB:K_KB + LANES].astype(BF16)

    zt = _dot_nt(wt_ref[...], hb)

    def rms_t(c, g):
        return c * lax.rsqrt(jnp.mean(c * c, axis=0, keepdims=True) + EPS) * g

    cq = rms_t(zt[R_CQ:R_CQ + A_Q_LORA], qng_ref[...])
    qt = _dot(wqt_ref[...], cq.astype(BF16))
    n_nope = A_HEADS * A_NOPE
    x1 = qt[n_nope:n_nope + LANES]
    x2 = qt[n_nope + LANES:n_nope + 2 * LANES]
    cos8, sin8 = cost_ref[...], sint_ref[...]
    mla_scale = (A_NOPE + A_ROPE) ** -0.5 * LOG2E
    o1 = (x1 * cos8 - x2 * sin8) * mla_scale
    o2 = (x1 * sin8 + x2 * cos8) * mla_scale
    q_lat = _dot(wukt_ref[...], qt[:n_nope].astype(BF16)) * mla_scale
    half = A_ROPE // 2
    for h in range(A_HEADS):
        qcatT_ref[h, :LANES, :] = q_lat[h * LANES:(h + 1) * LANES].astype(BF16)
        qcatT_ref[h, LANES:LANES + half, :] = o1[h * half:(h + 1) * half].astype(BF16)
        qcatT_ref[h, LANES + half:LANES + A_ROPE, :] = o2[h * half:(h + 1) * half].astype(BF16)
        qcatT_ref[h, LANES + A_ROPE:, :] = jnp.zeros((QCAT - LANES - A_ROPE, tm), BF16)
    ones = jnp.ones((ONES_ROWS, tm), BF16)
    vmlaT_ref[:A_KV_LORA, :] = rms_t(zt[R_CKV:R_CKV + A_KV_LORA], kvngc_ref[...]).astype(BF16)
    vmlaT_ref[A_KV_LORA:, :] = ones
    vb_t = zt[R_VB:R_VB + LANES]
    vdsaT_ref[:LANES, :] = vb_t.astype(BF16)
    vdsaT_ref[LANES:, :] = ones
    vbT_ref[...] = vb_t
    kbT_ref[...] = zt[R_KB:R_KB + LANES]
    kidxT_ref[...] = zt[R_KI:R_KI + IDX_DIM]
    k1, k2 = zt[R_KPE:R_KPE + half], zt[R_KPE + half:R_KPE + A_ROPE]
    cos1, sin1 = cos8[:half], sin8[:half]
    kpeT_ref[:half, :] = k1 * cos1 - k2 * sin1
    kpeT_ref[half:, :] = k1 * sin1 + k2 * cos1
    qbT_ref[...] = (zt[R_QB:R_QB + B_WIDTH] * (B_HEAD_DIM ** -0.5 * LOG2E)).astype(BF16)
    qidxT_ref[...] = (zt[R_QI:R_QI + IDX_HEADS * IDX_DIM] * (IDX_DIM ** -0.5)).astype(BF16)
    sgaT_ref[...] = jax.nn.silu(zt[R_GA:R_GA + A_WIDTH]).astype(BF16)
    sgbT_ref[...] = jax.nn.silu(zt[R_GB:R_GB + B_WIDTH]).astype(BF16)
    widxT_ref[...] = zt[R_WI:R_WI + IDX_HEADS] * (IDX_HEADS ** -0.5)


def _project_t(x2d, tabs, lw, b, t):
    n, d = x2d.shape
    tm = TQ
    nt = t // tm
    tok = lambda c: pl.BlockSpec((tm, c), lambda i: (i, 0))
    full = lambda a: pl.BlockSpec(a.shape, lambda i: (0,) * a.ndim)
    tab_t = pl.BlockSpec((LANES, tm), lambda i: (0, i % nt))
    tab_k = pl.BlockSpec((tm, LANES), lambda i: (i % nt, 0))
    rows_t = lambda r: pl.BlockSpec((None, r, tm), lambda i: (i // nt, 0, i % nt))
    outs = [
        ("ckv", (n, A_KV_LORA), F32, tok(A_KV_LORA)),
        ("kcat", (n, QCAT), BF16, tok(QCAT)), ("kb16", (n, LANES), BF16, tok(LANES)),
        ("kidx16", (n, LANES), BF16, tok(LANES)),
        ("kpeT", (b, A_ROPE, t), F32, rows_t(A_ROPE)), ("kbT", (b, LANES, t), F32, rows_t(LANES)),
        ("vbT", (b, LANES, t), F32, rows_t(LANES)), ("kidxT", (b, IDX_DIM, t), F32, rows_t(IDX_DIM)),
        ("qcatT", (b, A_HEADS, QCAT, t), BF16,
         pl.BlockSpec((None, A_HEADS, QCAT, tm), lambda i: (i // nt, 0, 0, i % nt))),
        ("qbT", (b, B_WIDTH, t), BF16, rows_t(B_WIDTH)),
        ("qidxT", (b, IDX_HEADS * IDX_DIM, t), BF16, rows_t(IDX_HEADS * IDX_DIM)),
        ("widxT", (b, IDX_HEADS, t), F32, rows_t(IDX_HEADS)),
        ("vmlaT", (b, nt, V_EXT, tm), BF16, pl.BlockSpec((None, None, V_EXT, tm), lambda i: (i // nt, i % nt, 0, 0))),
        ("vdsaT", (b, nt, V_EXT, tm), BF16, pl.BlockSpec((None, None, V_EXT, tm), lambda i: (i // nt, i % nt, 0, 0))),
        ("sgaT", (b, A_WIDTH, t), BF16, rows_t(A_WIDTH)), ("sgbT", (b, B_WIDTH, t), BF16, rows_t(B_WIDTH)),
    ]
    cost, sint, rc, rs1, rs2 = tabs
    res = pl.pallas_call(
        _proj_t_kernel,
        grid=(n // tm,),
        in_specs=[tok(d), full(lw["ng"]), full(lw["wk"]), full(lw["wt"]), full(lw["qngc"]), full(lw["kvngc"]),
                  full(lw["kvng"]), full(lw["wqt"]), full(lw["wukt"]), tab_t, tab_t, tab_k, tab_k, tab_k],
        out_specs=[o[3] for o in outs],
        out_shape=[jax.ShapeDtypeStruct(o[1], o[2]) for o in outs],
        compiler_params=pltpu.CompilerParams(dimension_semantics=("arbitrary",),
                                             vmem_limit_bytes=VMEM_LIMIT_BYTES),
        name="project_prompt",
    )(x2d, lw["ng"], lw["wk"], lw["wt"], lw["qngc"], lw["kvngc"], lw["kvng"], lw["wqt"], lw["wukt"],
      cost, sint, rc, rs1, rs2)
    return {o[0]: r for o, r in zip(outs, res)}


NEG_FLT_MAX = -3.4028234663852886e38
KEY_NEG_FLT_MAX = INT_MIN + (1 << 23)


def _key_to_float(k):
    k = jnp.maximum(k, KEY_NEG_FLT_MAX)
    return pltpu.bitcast(k ^ ((k >> 31) & 0x7FFFFFFF), F32)


def _count(pred):
    return jnp.sum(jnp.where(pred, 1.0, 0.0), axis=1, keepdims=True)


SAMPLE_STREAMS = 2
MAX_TIE_SWEEPS = 8.0
SAFE_DENOM_MIN = 2.0 ** -90
SAFE_DENOM_MAX = 2.0 ** 40


def _fold8(x, op=jnp.add):
    parts = [x[i:i + 8] for i in range(0, x.shape[0], 8)]
    while len(parts) > 1:
        parts = [op(a, b) for a, b in zip(parts[::2], parts[1::2])]
    return parts[0]


def _prompt_attn_kernel(qcatT_ref, qbT_ref, qidxT_ref, widxT_ref, kcat_ref, kb_ref, kidx_ref,
                        vmlaT_ref, vdsaT_ref, bias_ref, olatT_ref, obT_ref,
                        qa_ref, qbp_ref, qip_ref, sc_ref, m_ref, acc_ref, mb_ref, accb_ref, kn_ref, *, n_top):
    qi = pl.program_id(1)
    nblk = qi + 1
    tq = TQ
    krow = lax.broadcasted_iota(I32, (tq, tq), 0)
    qcol = lax.broadcasted_iota(I32, (tq, tq), 1)
    shift = CHUNK.bit_length() - 1
    diag_ok = (qcol >> shift) >= (krow >> shift)
    hcols = lambda h: slice(h * tq, (h + 1) * tq)

    def per_head(fn):
        return jnp.concatenate([fn(h) for h in range(A_HEADS)], axis=1)

    qbp_ref[...] = jnp.zeros(qbp_ref.shape, BF16)
    qip_ref[...] = jnp.zeros(qip_ref.shape, BF16)
    for h in range(A_HEADS):
        g = h // B_GROUP
        qa_ref[:, hcols(h)] = qcatT_ref[h]
        qbp_ref[g * B_HEAD_DIM:(g + 1) * B_HEAD_DIM, hcols(h)] = qbT_ref[h * B_HEAD_DIM:(h + 1) * B_HEAD_DIM, :]
        qip_ref[:IDX_DIM, hcols(h)] = qidxT_ref[h * IDX_DIM:(h + 1) * IDX_DIM, :]
    w_all = per_head(lambda h: widxT_ref[h:h + 1, :])

    @pl.when(qi == 0)
    def _():
        def max_row_norm2(k_ref):
            k = k_ref[...].astype(F32)
            return jnp.max(jnp.sum(k * k, axis=1, keepdims=True))
        kn_ref[0:1, :] = jnp.full((1, LANES), max_row_norm2(kcat_ref), F32)
        kn_ref[1:2, :] = jnp.full((1, LANES), max_row_norm2(kb_ref), F32)
        kn_ref[2:3, :] = jnp.full((1, LANES), jnp.max(bias_ref[...]), F32)

    def col_norm(q_ref):
        q = q_ref[...].astype(F32)
        return jnp.sqrt(jnp.sum(q * q, axis=0, keepdims=True))

    shift_a = jnp.sqrt(kn_ref[0:1, 0:1]) * col_norm(qa_ref)
    shift_b = jnp.sqrt(kn_ref[1:2, 0:1]) * col_norm(qbp_ref) + jnp.maximum(kn_ref[2:3, 0:1], 0.0)

    def accumulate(s_t, values, shift_or_m, acc_r, exact):
        if exact:
            m_prev = shift_or_m[0:1, :]
            m_new = jnp.maximum(m_prev, jnp.max(s_t, axis=0, keepdims=True))
            alpha = jnp.exp2(m_prev - m_new)
            shift_or_m[0:1, :] = m_new
        else:
            m_new = shift_or_m
        p_t = jnp.exp2(s_t - m_new).astype(BF16)
        for v_t, lanes in values:
            pv = _dot(v_t, p_t[:, lanes])
            acc_r[:, lanes] = (alpha[:, lanes] * acc_r[:, lanes] if exact else acc_r[:, lanes]) + pv

    def unsafe(l):
        return jnp.max(jnp.where((l >= SAFE_DENOM_MIN) & (l <= SAFE_DENOM_MAX), 0.0, 1.0)) > 0.0

    all_lanes = slice(0, A_HEADS * tq)

    def for_blocks(n, block):
        def pair(p, c):
            block(2 * p)
            block(2 * p + 1)
            return c

        lax.fori_loop(0, lax.shift_right_logical(n, 1), pair, 0)

        @pl.when((n & 1) == 1)
        def _():
            block(n - 1)

    def mla_pass(exact):
        acc_ref[...] = jnp.zeros(acc_ref.shape, F32)
        if exact:
            m_ref[...] = jnp.full(m_ref.shape, NEG_BIG, F32)

        def block(j, masked):
            start = pl.multiple_of(j * tq, tq)
            s_t = _dot(kcat_ref[pl.ds(start, tq), :], qa_ref[...])
            if masked:
                s_t = per_head(lambda h: jnp.where(diag_ok, s_t[:, hcols(h)], NEG_BIG))
            accumulate(s_t, [(vmlaT_ref[j], all_lanes)], m_ref if exact else shift_a, acc_ref, exact)
            if exact:
                return
            r = jnp.maximum(_dot(kidx_ref[pl.ds(start, tq), :], qip_ref[...]), 0.0) * w_all
            score = r[:, hcols(0)]
            for h in range(1, IDX_HEADS):
                score = score + r[:, hcols(h)]
            if masked:
                score = jnp.where(diag_ok, score, -jnp.inf)
            sc_ref[j] = score

        for_blocks(qi, lambda j: block(j, False))
        block(qi, True)

    mla_pass(False)

    @pl.when(unsafe(acc_ref[A_KV_LORA:A_KV_LORA + 1, :]))
    def _():
        mla_pass(True)

    o_t = acc_ref[:A_KV_LORA, :] * (1.0 / acc_ref[A_KV_LORA:A_KV_LORA + 1, :])
    for h in range(A_HEADS):
        olatT_ref[h * LANES:(h + 1) * LANES, :] = o_t[:, hcols(h)].astype(olatT_ref.dtype)

    def count(pred):
        def body(j, c):
            return c + _fold8(jnp.where(pred(sc_ref[j], j), 1.0, 0.0))
        part = lax.fori_loop(0, nblk, body, jnp.zeros((8, tq), F32))
        return jnp.sum(part, axis=0, keepdims=True)

    kf = float(n_top)

    def bis_body(it, carry):
        lo, cnt_lo = carry
        cand = lo + lax.shift_left(jnp.int32(1), 31 - it)
        cand_f = _key_to_float(cand)
        cnt = count(lambda s, j: s >= cand_f)
        take = cnt >= kf
        return jnp.where(take, cand, lo), jnp.where(take, cnt, cnt_lo)

    n_steps = jnp.where(nblk * tq <= n_top, 0, 32)
    lo, cnt_ge = lax.fori_loop(
        0, n_steps, bis_body, (jnp.full((1, tq), INT_MIN, I32), jnp.full((1, tq), 1e9, F32)))
    few = lo == INT_MIN
    thr = _key_to_float(lo)
    excess0 = jnp.where(few, 0.0, cnt_ge - kf)
    max_excess = jnp.max(excess0)

    def drop_from(cut):
        def body(j, c):
            s = sc_ref[j]
            sc_ref[j] = jnp.where((s == thr) & ((krow + j * tq) >= cut), -jnp.inf, s)
            return c
        lax.fori_loop(0, nblk, body, 0)

    @pl.when((max_excess > 0.0) & (max_excess <= MAX_TIE_SWEEPS))
    def _():
        def last_tie_below(cut):
            def body(j, m):
                pos = krow + j * tq
                hit = jnp.where((sc_ref[j] == thr) & (pos < cut), pos, -1)
                return jnp.maximum(m, _fold8(hit, jnp.maximum))
            part = lax.fori_loop(0, nblk, body, jnp.full((8, tq), -1, I32))
            return jnp.max(part, axis=0, keepdims=True)

        def sweep(c):
            excess, cut = c
            last = last_tie_below(cut)
            live = excess > 0.0
            return jnp.where(live, excess - 1.0, excess), jnp.where(live, last, cut)

        _, cut = lax.while_loop(lambda c: jnp.max(c[0]) > 0.0, sweep,
                                (excess0, jnp.full((1, tq), 2 ** 30, I32)))
        drop_from(cut)

    @pl.when(max_excess > MAX_TIE_SWEEPS)
    def _():
        need = kf - count(lambda s, j: s > thr)
        n_bits = (sc_ref.shape[0] * tq).bit_length()

        def cut_body(it, cpos):
            cand = cpos + lax.shift_left(jnp.int32(1), n_bits - 1 - it)
            cnt = count(lambda s, j: (s == thr) & ((krow + j * tq) < cand))
            return jnp.where(cnt < need, cand, cpos)

        keep = lax.fori_loop(0, n_bits, cut_body, jnp.zeros((1, tq), I32))
        drop_from(jnp.where(excess0 > 0.0, keep + 1, 2 ** 30))

    group_lanes = [slice(g * B_GROUP * tq, (g + 1) * B_GROUP * tq) for g in range(B_KV_HEADS)]

    def dsa_pass(exact):
        accb_ref[...] = jnp.zeros(accb_ref.shape, F32)
        if exact:
            mb_ref[...] = jnp.full(mb_ref.shape, NEG_BIG, F32)

        def block(j, near):
            start = pl.multiple_of(j * tq, tq)
            sel = sc_ref[j] >= thr
            s_t = _dot(kb_ref[pl.ds(start, tq), :], qbp_ref[...])
            if near:
                s_t = s_t + bias_ref[qi - j]
            s_t = per_head(lambda h: jnp.where(sel, s_t[:, hcols(h)], NEG_BIG))
            v_all = vdsaT_ref[j]
            values = [(jnp.concatenate([v_all[g * B_HEAD_DIM:(g + 1) * B_HEAD_DIM], v_all[LANES:]], axis=0),
                       group_lanes[g]) for g in range(B_KV_HEADS)]
            accumulate(s_t, values, mb_ref if exact else shift_b, accb_ref, exact)

        for_blocks(jnp.maximum(qi - 1, 0), lambda j: block(j, False))

        @pl.when(qi >= 1)
        def _():
            block(qi - 1, True)

        block(qi, True)

    dsa_pass(False)

    @pl.when(unsafe(accb_ref[B_HEAD_DIM:B_HEAD_DIM + 1, :]))
    def _():
        dsa_pass(True)

    inv_b = 1.0 / accb_ref[B_HEAD_DIM:B_HEAD_DIM + 1, :]
    for h in range(B_HEADS):
        obT_ref[h * B_HEAD_DIM:(h + 1) * B_HEAD_DIM, :] = (
            accb_ref[:B_HEAD_DIM, hcols(h)] * inv_b[:, hcols(h)]).astype(obT_ref.dtype)


def _prompt_attention(pr, bias_p, b, t, n_top):
    tq = TQ
    assert t % tq == 0
    nq = t // tq
    r3 = lambda a: a.reshape(b, t, a.shape[-1])
    qrows = lambda r: pl.BlockSpec((None, r, tq), lambda bi, qi: (bi, 0, qi))
    kall = lambda c: pl.BlockSpec((None, t, c), lambda bi, qi: (bi, 0, 0))
    vall = pl.BlockSpec((None, nq, V_EXT, tq), lambda bi, qi: (bi, 0, 0, 0))
    olat, ob = pl.pallas_call(
        functools.partial(_prompt_attn_kernel, n_top=n_top),
        grid=(b, nq),
        in_specs=[pl.BlockSpec((None, A_HEADS, QCAT, tq), lambda bi, qi: (bi, 0, 0, qi)),
                  qrows(B_WIDTH), qrows(IDX_HEADS * IDX_DIM), qrows(IDX_HEADS),
                  kall(QCAT), kall(LANES), kall(LANES), vall, vall,
                  pl.BlockSpec(bias_p.shape, lambda bi, qi: (0, 0, 0))],
        out_specs=[qrows(A_HEADS * A_KV_LORA), qrows(B_WIDTH)],
        out_shape=[jax.ShapeDtypeStruct((b, A_HEADS * A_KV_LORA, t), BF16),
                   jax.ShapeDtypeStruct((b, B_WIDTH, t), BF16)],
        scratch_shapes=[
            pltpu.VMEM((QCAT, A_HEADS * tq), BF16),
            pltpu.VMEM((LANES, B_HEADS * tq), BF16),
            pltpu.VMEM((LANES, IDX_HEADS * tq), BF16),
            pltpu.VMEM((nq, tq, tq), F32),
            pltpu.VMEM((8, A_HEADS * tq), F32),
            pltpu.VMEM((V_EXT, A_HEADS * tq), F32),
            pltpu.VMEM((8, B_HEADS * tq), F32),
            pltpu.VMEM((B_HEAD_DIM + ONES_ROWS, B_HEADS * tq), F32),
            pltpu.VMEM((8, LANES), F32),
        ],
        compiler_params=pltpu.CompilerParams(dimension_semantics=("arbitrary", "arbitrary"),
                                             vmem_limit_bytes=VMEM_LIMIT_BYTES),
        name="prompt_attention",
    )(pr["qcatT"], pr["qbT"], pr["qidxT"], pr["widxT"],
      r3(pr["kcat"]), r3(pr["kb16"]), r3(pr["kidx16"]), pr["vmlaT"], pr["vdsaT"], bias_p)
    return olat, ob


def _sample_attn_kernel(qcat_ref, qb_ref, qidx_ref, widx_ref, kcatn_ref, kbn_ref, vbn_ref, kidxn_ref,
                        cckv_ref, ckpeT_ref, ckT_ref, cvT_ref, ckidxT_ref, bias_ref,
                        olat_ref, ob_ref, *, n_top, t_new, past):
    tq = t_new
    pad = LANES
    n_keys = past + pad
    kf = float(n_top)
    n_bits = n_keys.bit_length()
    n_far = n_keys - bias_ref.shape[-1]

    def padrows(a):
        return jnp.concatenate([a, jnp.zeros((pad - t_new, a.shape[1]), a.dtype)], axis=0)

    def softmax(s):
        m = jnp.max(s, axis=1, keepdims=True)
        p = jnp.exp2(s - m)
        return p.astype(BF16), jnp.sum(p, axis=1, keepdims=True)

    def new_cols(rows):
        return lax.broadcasted_iota(I32, (rows, pad), 1) < t_new

    def mla_and_scores(i):
        ckv_c = cckv_ref[i].astype(BF16)
        kpe_t = ckpeT_ref[i].astype(BF16)
        kcat_n = padrows(kcatn_ref[i])
        qs = jnp.concatenate([qcat_ref[i, :, h * QCAT:(h + 1) * QCAT] for h in range(A_HEADS)], axis=0)
        s_c = _dot_nt(qs[:, :A_KV_LORA], ckv_c) + _dot(qs[:, A_KV_LORA:A_KV_LORA + A_ROPE], kpe_t)
        s_n = jnp.where(new_cols(A_HEADS * tq), _dot_nt(qs, kcat_n), NEG_BIG)
        pb, l = softmax(jnp.concatenate([s_c, s_n], axis=1))
        o = (_dot(pb[:, :past], ckv_c) + _dot(pb[:, past:], kcat_n[:, :A_KV_LORA])) / l
        for h in range(A_HEADS):
            olat_ref[i, :, h * LANES:(h + 1) * LANES] = o[h * tq:(h + 1) * tq].astype(olat_ref.dtype)

        kidx_t = ckidxT_ref[i].astype(BF16)
        kidx_n = padrows(kidxn_ref[i])
        qis = jnp.concatenate(
            [qidx_ref[i, :, h * IDX_DIM:(h + 1) * IDX_DIM] for h in range(IDX_HEADS)], axis=0)

        def head_sum(dots):
            acc = jnp.maximum(dots[:tq], 0.0) * widx_ref[i, :, 0:1]
            for h in range(1, IDX_HEADS):
                acc = acc + jnp.maximum(dots[h * tq:(h + 1) * tq], 0.0) * widx_ref[i, :, h:h + 1]
            return acc

        return jnp.concatenate(
            [head_sum(_dot(qis, kidx_t)),
             jnp.where(new_cols(tq), head_sum(_dot_nt(qis, kidx_n)), -jnp.inf)], axis=1)

    def search_step(it, score, lo, cnt_lo):
        cand = lo + jnp.int32(INT_MIN if it == 0 else 1 << (31 - it))
        cnt = _count(score >= _key_to_float(cand))
        take = cnt >= kf
        return jnp.where(take, cand, lo), jnp.where(take, cnt, cnt_lo)

    def attend_selected(i, score, lo, cnt_ge):
        few = lo == INT_MIN
        thr = _key_to_float(lo)
        has_tie = jnp.max(jnp.where((cnt_ge > kf) & (~few), 1.0, 0.0))

        def drop_ties():
            cols = lax.broadcasted_iota(I32, (tq, n_keys), 1)
            need = kf - _count(score > thr)
            eq = score == thr

            def cut_body(it, cpos):
                cand = cpos + lax.shift_left(jnp.int32(1), n_bits - 1 - it)
                cnt = _count(eq & (cols < cand))
                return jnp.where(cnt < need, cand, cpos)

            keep = lax.fori_loop(0, n_bits, cut_body, jnp.zeros((tq, 1), I32))
            return jnp.where(eq & (cols > keep) & (cnt_ge > kf) & (~few), -jnp.inf, score)

        kept = lax.cond(has_tie > 0.0, drop_ties, lambda: score)
        sel_g = jnp.concatenate([kept] * B_GROUP, axis=0) >= jnp.concatenate([thr] * B_GROUP, axis=0)

        k_t = ckT_ref[i].astype(BF16)
        v_t = cvT_ref[i].astype(BF16)
        k_n = padrows(kbn_ref[i])
        v_n = padrows(vbn_ref[i])
        for g in range(B_KV_HEADS):
            feats = slice(g * B_HEAD_DIM, (g + 1) * B_HEAD_DIM)
            heads = range(g * B_GROUP, (g + 1) * B_GROUP)
            qg = jnp.concatenate([qb_ref[i, :, h * B_HEAD_DIM:(h + 1) * B_HEAD_DIM] for h in heads], axis=0)
            near = jnp.concatenate([bias_ref[0, h] - bias_ref[1, h] for h in heads], axis=0)
            sg = _dot(qg, k_t[feats])
            sg = jnp.concatenate([sg[:, :n_far], sg[:, n_far:] + near[:, :past - n_far],
                                  _dot_nt(qg, k_n[:, feats]) + near[:, past - n_far:]], axis=1)
            pb, l = softmax(jnp.where(sel_g, sg, NEG_BIG))
            og = (_dot_nt(pb[:, :past], v_t[feats]) + _dot(pb[:, past:], v_n[:, feats])) / l
            for hh, h in enumerate(heads):
                ob_ref[i, :, h * B_HEAD_DIM:(h + 1) * B_HEAD_DIM] = og[hh * tq:(hh + 1) * tq]

    streams = range(SAMPLE_STREAMS)
    scores = [mla_and_scores(i) for i in streams]
    state = [(jnp.full((tq, 1), INT_MIN, I32), jnp.full((tq, 1), 1e9, F32)) for _ in streams]
    for it in range(32):
        state = [search_step(it, scores[i], *state[i]) for i in streams]
    for i in streams:
        attend_selected(i, scores[i], *state[i])


def _sample_attention(pr, caches, bias_s, b, t_new, past, n_top):
    r3 = lambda a: a.reshape(b, t_new, a.shape[-1])
    ns = SAMPLE_STREAMS
    assert b % ns == 0
    per_b = lambda n, c: pl.BlockSpec((ns, n, c), lambda bi: (bi, 0, 0))
    news = [pr["qcat"], pr["qb"], pr["qidx"], pr["widx"], pr["kcat"], pr["kb16"], pr["vb16"], pr["kidx16"]]
    olat, ob = pl.pallas_call(
        functools.partial(_sample_attn_kernel, n_top=n_top, t_new=t_new, past=past),
        grid=(b // ns,),
        in_specs=[per_b(t_new, a.shape[-1]) for a in news]
                 + [per_b(c.shape[1], c.shape[2]) for c in caches]
                 + [pl.BlockSpec(bias_s.shape, lambda bi: (0, 0, 0, 0))],
        out_specs=[per_b(t_new, A_HEADS * A_KV_LORA), per_b(t_new, B_WIDTH)],
        out_shape=[jax.ShapeDtypeStruct((b, t_new, A_HEADS * A_KV_LORA), BF16),
                   jax.ShapeDtypeStruct((b, t_new, B_WIDTH), F32)],
        compiler_params=pltpu.CompilerParams(dimension_semantics=("arbitrary",),
                                             vmem_limit_bytes=VMEM_LIMIT_BYTES),
        name="sample_attention",
    )(*[r3(a) for a in news], *caches, bias_s)
    return olat.reshape(b * t_new, -1), ob.reshape(b * t_new, -1)


def _combine_kernel(x_ref, olat_ref, ob_ref, sga_ref, sgb_ref, wuv_ref, wout_ref, fg_ref, y_ref, *, final):
    o_a = _dot(olat_ref[...], wuv_ref[...])
    mix = jnp.concatenate([o_a * sga_ref[...], ob_ref[...] * sgb_ref[...]], axis=1)
    y = x_ref[...] + _dot(mix.astype(BF16), wout_ref[...])
    if final:
        y = _rms(y, fg_ref[...])
    y_ref[...] = y


def _combine(x2d, olat, ob, pr, lw, fg, final):
    n, d = x2d.shape
    tm = PROJ_TM
    tok = lambda c: pl.BlockSpec((tm, c), lambda i: (i, 0))
    full = lambda a: pl.BlockSpec(a.shape, lambda i: (0,) * a.ndim)
    return pl.pallas_call(
        functools.partial(_combine_kernel, final=final),
        grid=(n // tm,),
        in_specs=[tok(d), tok(olat.shape[1]), tok(ob.shape[1]), tok(A_WIDTH), tok(B_WIDTH),
                  full(lw["wuv"]), full(lw["wout"]), full(fg)],
        out_specs=tok(d),
        out_shape=jax.ShapeDtypeStruct((n, d), F32),
        compiler_params=pltpu.CompilerParams(dimension_semantics=("arbitrary",),
                                             vmem_limit_bytes=VMEM_LIMIT_BYTES),
        name="combine",
    )(x2d, olat, ob, pr["sga"], pr["sgb"], lw["wuv"], lw["wout"], fg)


def _combine_t_kernel(x_ref, olatT_ref, obT_ref, sgaT_ref, sgbT_ref, wuvt_ref, wout_ref, fg_ref, y_ref, *, final):
    o_a = _dot(wuvt_ref[...], olatT_ref[...])
    mix_t = jnp.concatenate([o_a * sgaT_ref[...].astype(F32),
                             obT_ref[...].astype(F32) * sgbT_ref[...].astype(F32)], axis=0).astype(BF16)
    y = x_ref[...] + lax.dot_general(mix_t, wout_ref[...], (((0,), (0,)), ((), ())),
                                     preferred_element_type=F32)
    if final:
        y = _rms(y, fg_ref[...])
    y_ref[...] = y


def _combine_t(x3d, olat_t, ob_t, pr, lw, fg, final):
    b, t, d = x3d.shape
    tm = TQ
    rows_t = lambda r: pl.BlockSpec((None, r, tm), lambda bi, ti: (bi, 0, ti))
    full = lambda a: pl.BlockSpec(a.shape, lambda bi, ti: (0,) * a.ndim)
    xblk = pl.BlockSpec((None, tm, d), lambda bi, ti: (bi, ti, 0))
    return pl.pallas_call(
        functools.partial(_combine_t_kernel, final=final),
        grid=(b, t // tm),
        in_specs=[xblk, rows_t(A_HEADS * A_KV_LORA), rows_t(B_WIDTH), rows_t(A_WIDTH), rows_t(B_WIDTH),
                  full(lw["wuvt"]), full(lw["wout"]), full(fg)],
        out_specs=xblk,
        out_shape=jax.ShapeDtypeStruct((b, t, d), F32),
        compiler_params=pltpu.CompilerParams(dimension_semantics=("arbitrary", "arbitrary"),
                                             vmem_limit_bytes=VMEM_LIMIT_BYTES),
        name="combine_prompt",
    )(x3d, olat_t, ob_t, pr["sgaT"], pr["sgbT"], lw["wuvt"], lw["wout"], fg)


def _layer_weights(norm_g, w_in, q_norm_g, kv_norm_g, w_uq, w_uk, w_uv, w_out):
    d = w_in.shape[0]
    o = np.cumsum([0, A_Q_LORA, A_KV_LORA, A_ROPE, A_WIDTH, B_WIDTH, B_KV_HEADS * B_HEAD_DIM,
                   B_KV_HEADS * B_HEAD_DIM, IDX_HEADS * IDX_DIM, IDX_DIM, IDX_HEADS, B_WIDTH])
    seg = lambda i: w_in[:, int(o[i]):int(o[i + 1])]
    misc = jnp.concatenate([seg(2), seg(8), seg(9),
                            jnp.zeros((d, LANES - A_ROPE - IDX_DIM - IDX_HEADS), w_in.dtype)], axis=1)
    win = jnp.concatenate([seg(0), seg(1), misc, seg(3), seg(4), seg(5), seg(6), seg(7), seg(10)], axis=1)
    assert win.shape[1] == IN_PAD
    half = A_ROPE // 2
    wq = jnp.concatenate([
        w_uq[:, :, :A_NOPE].reshape(A_Q_LORA, A_HEADS * A_NOPE),
        w_uq[:, :, A_NOPE:A_NOPE + half].reshape(A_Q_LORA, A_HEADS * half),
        w_uq[:, :, A_NOPE + half:].reshape(A_Q_LORA, A_HEADS * half)], axis=1)
    eye = jnp.eye(A_HEADS, dtype=w_uk.dtype)
    wuk = jnp.einsum('chn,hg->hngc', w_uk, eye).reshape(A_HEADS * A_NOPE, A_HEADS * A_KV_LORA)
    wuv = jnp.einsum('chv,hg->hcgv', w_uv, eye).reshape(A_HEADS * A_KV_LORA, A_HEADS * A_V)
    pm = np.zeros((2 * LANES, A_HEADS * LANES), np.float32)
    for h in range(A_HEADS):
        for i in range(half):
            pm[h * half + i, h * LANES + i] = 1.0
            pm[LANES + h * half + i, h * LANES + half + i] = 1.0
    zpad = lambda c: jnp.zeros((d, c), w_in.dtype)
    wk = jnp.concatenate([seg(1), seg(8), seg(2), zpad(LANES - IDX_DIM - A_ROPE), seg(5)], axis=1)
    wt = jnp.concatenate([seg(0), seg(1), seg(6), seg(4), seg(7), seg(3), seg(10), seg(9),
                          zpad(R_KB - R_WI - IDX_HEADS), seg(5), seg(8), seg(2)], axis=1).T
    assert wk.shape[1] == COLS_K and wt.shape[0] == ROWS_T
    bc = lambda g: jnp.broadcast_to(g.reshape(-1, 1), (g.shape[0], TQ))
    return {
        "wk": wk.astype(BF16), "wt": wt.astype(BF16), "qngc": bc(q_norm_g), "kvngc": bc(kv_norm_g),
        "wqt": wq.T.astype(BF16), "wukt": wuk.T.astype(BF16), "wuvt": wuv.T.astype(BF16),
        "ng": norm_g.reshape(1, -1), "win": win.astype(BF16),
        "qng": q_norm_g.reshape(1, -1), "kvng": kv_norm_g.reshape(1, -1),
        "wq": wq.astype(BF16), "wuk": wuk.astype(BF16), "wuv": wuv.astype(BF16),
        "pmat": jnp.asarray(pm, BF16), "wout": w_out.astype(BF16),
    }


def _rope_tables(pos):
    half = A_ROPE // 2
    inv = ROPE_THETA ** (-jnp.arange(half, dtype=F32) / half)
    ang = pos.astype(F32)[:, None] * inv[None, :]
    cos, sin = jnp.cos(ang), jnp.sin(ang)
    z = jnp.zeros((pos.shape[0], LANES - A_ROPE), F32)
    zh = jnp.zeros_like(cos)
    cosq = jnp.tile(cos, (1, A_HEADS))
    sinq = jnp.tile(sin, (1, A_HEADS))
    rc = jnp.concatenate([cos, cos, z], axis=1)
    rs1 = jnp.concatenate([zh, sin, z], axis=1)
    rs2 = jnp.concatenate([-sin, zh, z], axis=1)
    return cosq, sinq, rc, rs1, rs2


def _rope_tables_t(pos):
    half = A_ROPE // 2
    inv = ROPE_THETA ** (-jnp.arange(half, dtype=F32) / half)
    ang = pos.astype(F32)[:, None] * inv[None, :]
    cos, sin = jnp.cos(ang), jnp.sin(ang)
    cost = jnp.tile(cos.T, (A_HEADS, 1))
    sint = jnp.tile(sin.T, (A_HEADS, 1))
    z0 = jnp.zeros((pos.shape[0], MK_KPE), F32)
    z1 = jnp.zeros((pos.shape[0], LANES - MK_KPE - A_ROPE), F32)
    zh = jnp.zeros_like(cos)
    rc = jnp.concatenate([z0, cos, cos, z1], axis=1)
    rs1 = jnp.concatenate([z0, zh, sin, z1], axis=1)
    rs2 = jnp.concatenate([z0, -sin, zh, z1], axis=1)
    return cost, sint, rc, rs1, rs2


def kernel(x_prompt, x_sample, cache_mla_ckv, cache_mla_kpe, cache_dsa_k, cache_dsa_v, cache_dsa_kidx,
           norm_g, w_in, mla_q_norm_g, mla_kv_norm_g, mla_w_uq, mla_w_uk, mla_w_uv, rel_bias, w_out,
           final_norm_g):
    bp, tp, d = x_prompt.shape
    bs, ts, _ = x_sample.shape
    depth = w_in.shape[0]
    past = cache_mla_ckv.shape[2]
    n_top_p = min(TOP_K_MAX, tp // 4)
    n_top_s = min(TOP_K_MAX, (past + ts) // 4)
    assert ts <= CHUNK and past % CHUNK == 0 and past % LANES == 0

    rope_p = _rope_tables_t(jnp.arange(tp, dtype=jnp.int32))
    reps = PROJ_TM // ts
    rope_s = tuple(jnp.tile(a, (reps, 1)) for a in _rope_tables(past + jnp.arange(ts, dtype=jnp.int32)))

    bias_p = _bias_tables(rel_bias, (0, -TQ, -3 * TQ), TQ, TQ, True)
    win_s = 2 * LANES
    bias_s = _bias_tables(rel_bias, (-(win_s - LANES), -(past + win_s)), ts, win_s, False)
    fg = final_norm_g.reshape(1, -1)

    xp = x_prompt
    xs = x_sample.reshape(bs * ts, d)
    outs_p, outs_s = [], []
    for l in range(depth):
        lw = _layer_weights(norm_g[l], w_in[l], mla_q_norm_g[l], mla_kv_norm_g[l],
                            mla_w_uq[l], mla_w_uk[l], mla_w_uv[l], w_out[l])
        final = l == depth - 1
        pr = _project_t(xp.reshape(bp * tp, d), rope_p, lw, bp, tp)
        olat, ob = _prompt_attention(pr, bias_p, bp, tp, n_top_p)
        xp = _combine_t(xp, olat, ob, pr, lw, fg, final)
        heads_t = lambda a: a.reshape(bp, B_KV_HEADS, B_HEAD_DIM, tp).transpose(0, 3, 1, 2)
        outs_p.append((pr["ckv"].reshape(bp, tp, A_KV_LORA), pr["kpeT"].transpose(0, 2, 1),
                       heads_t(pr["kbT"]), heads_t(pr["vbT"]), pr["kidxT"].transpose(0, 2, 1)))
        ps = _project(xs, rope_s, lw, period=PROJ_TM)
        feat_t = lambda a: a.transpose(0, 2, 3, 1).reshape(bs, B_KV_HEADS * B_HEAD_DIM, past)
        caches = (cache_mla_ckv[l], cache_mla_kpe[l].transpose(0, 2, 1), feat_t(cache_dsa_k[l]),
                  feat_t(cache_dsa_v[l]), cache_dsa_kidx[l].transpose(0, 2, 1))
        olat, ob = _sample_attention(ps, caches, bias_s, bs, ts, past, n_top_s)
        xs = _combine(xs, olat, ob, ps, lw, fg, final)
        outs_s.append((ps["ckv"].reshape(bs, ts, A_KV_LORA), ps["kpe"].reshape(bs, ts, A_ROPE),
                       ps["kb"].reshape(bs, ts, B_KV_HEADS, B_HEAD_DIM),
                       ps["vb"].reshape(bs, ts, B_KV_HEADS, B_HEAD_DIM),
                       ps["kidx"].reshape(bs, ts, IDX_DIM)))

    stack = lambda outs, i: jnp.stack([o[i] for o in outs])
    return ((xp, xs.reshape(bs, ts, d))
            + tuple(stack(outs_p, i) for i in range(5)) + tuple(stack(outs_s, i) for i in range(5)))
```

```python
import functools
import math

import jax
import jax.numpy as jnp
import numpy as np
from jax import lax
from jax.experimental import pallas as pl
from jax.experimental.pallas import tpu as pltpu

F32 = jnp.float32
BF16 = jnp.bfloat16
I32 = jnp.int32

CHUNK = 64
EPS = 1e-6
A_HEADS = 8
A_NOPE = 64
A_ROPE = 32
A_V = 64
A_Q_LORA = 256
A_KV_LORA = 128
ROPE_THETA = 10000.0
A_WIDTH = A_HEADS * A_V
B_HEADS = 8
B_KV_HEADS = 2
B_HEAD_DIM = 64
B_WIDTH = B_HEADS * B_HEAD_DIM
B_GROUP = B_HEADS // B_KV_HEADS
IDX_HEADS = 8
IDX_DIM = 64
TOP_K_MAX = 256
N_BUCKETS = 32
MAX_DISTANCE = 128

LANES = 128
VMEM_LIMIT_BYTES = 56 * 1024 * 1024

LOG2E = 1.4426950408889634
NEG_BIG = -1e30
INT_MIN = -(2 ** 31)

C_CQ = 0
C_CKV = C_CQ + A_Q_LORA
C_MISC = C_CKV + A_KV_LORA
C_GA = C_MISC + LANES
C_QB = C_GA + A_WIDTH
C_KB = C_QB + B_WIDTH
C_VB = C_KB + B_KV_HEADS * B_HEAD_DIM
C_QI = C_VB + B_KV_HEADS * B_HEAD_DIM
C_GB = C_QI + IDX_HEADS * IDX_DIM
IN_PAD = C_GB + B_WIDTH
M_KPE = 0
M_KIDX = A_ROPE
M_WIDX = A_ROPE + IDX_DIM

QCAT = 2 * LANES
TQ = 256
PROJ_TM = 256
COMBINE_TM = 1024
PROJ_T_TM = 1024


def _dot(a, b):
    return jnp.dot(a, b, preferred_element_type=F32)


def _dot_nt(a, b):
    return lax.dot_general(a, b, (((1,), (1,)), ((), ())), preferred_element_type=F32)


def _rms(x, g):
    return x * lax.rsqrt(jnp.mean(x * x, axis=-1, keepdims=True) + EPS) * g


def _bias_kernel(rb_ref, out_ref, *, offsets, keys_on_rows):
    nb = N_BUCKETS // 2
    max_exact = nb // 2
    n_r = out_ref.shape[1] if keys_on_rows else out_ref.shape[2]
    n_c = out_ref.shape[2] // B_HEADS if keys_on_rows else out_ref.shape[3]
    row = lax.broadcasted_iota(I32, (n_r, n_c), 0)
    col = lax.broadcasted_iota(I32, (n_r, n_c), 1)

    def bucket_of(off):
        rel = off + (row - col if keys_on_rows else col - row)
        ret = jnp.where(rel > 0, nb, 0)
        n = jnp.abs(rel)
        nf = jnp.maximum(n, 1).astype(F32)
        large = max_exact + (jnp.log(nf / max_exact) / math.log(MAX_DISTANCE / max_exact)
                             * (nb - max_exact)).astype(I32)
        large = jnp.minimum(large, nb - 1)
        return ret + jnp.where(n < max_exact, n, large)

    def lookup(bucket, h):
        table = jnp.broadcast_to(rb_ref[h:h + 1, :], (n_r, LANES))
        return jnp.concatenate(
            [jnp.take_along_axis(table, bucket[:, c:c + LANES], axis=1) for c in range(0, n_c, LANES)], axis=1)

    buckets = [bucket_of(off) for off in offsets]
    for h in range(B_HEADS):
        if keys_on_rows:
            far = lookup(buckets[-1], h)
            for p in range(len(offsets) - 1):
                out_ref[p, :, h * n_c:(h + 1) * n_c] = (lookup(buckets[p], h) - far) * LOG2E
        else:
            for p in range(len(offsets)):
                out_ref[p, h] = lookup(buckets[p], h) * LOG2E


def _bias_tables(rel_bias, offsets, n_r, n_c, keys_on_rows):
    shape = (len(offsets) - 1, n_r, B_HEADS * n_c) if keys_on_rows else (len(offsets), B_HEADS, n_r, n_c)
    return pl.pallas_call(
        functools.partial(_bias_kernel, offsets=tuple(offsets), keys_on_rows=keys_on_rows),
        out_shape=jax.ShapeDtypeStruct(shape, F32),
        in_specs=[pl.BlockSpec(memory_space=pltpu.VMEM)],
        out_specs=pl.BlockSpec(memory_space=pltpu.VMEM),
        compiler_params=pltpu.CompilerParams(vmem_limit_bytes=VMEM_LIMIT_BYTES),
        name="bias_tables",
    )(jnp.pad(rel_bias.T, ((0, 0), (0, LANES - N_BUCKETS))))


def _proj_kernel(x_ref, ng_ref, win_ref, qng_ref, kvng_ref, wq_ref, wuk_ref, pmat_ref,
                 cosq_ref, sinq_ref, rc_ref, rs1_ref, rs2_ref,
                 ckv_ref, kpe_ref, kb_ref, vb_ref, kidx_ref,
                 kcat_ref, kb16_ref, vb16_ref, kidx16_ref,
                 qcat_ref, qb_ref, qidx_ref, widx_ref, sga_ref, sgb_ref):
    x = x_ref[...]
    h = _rms(x, ng_ref[...])
    z = _dot(h.astype(BF16), win_ref[...])

    cq = _rms(z[:, C_CQ:C_CQ + A_Q_LORA], qng_ref[...])
    q = _dot(cq.astype(BF16), wq_ref[...])
    n_nope = A_HEADS * A_NOPE
    x1 = q[:, n_nope:n_nope + LANES]
    x2 = q[:, n_nope + LANES:n_nope + 2 * LANES]
    cos8, sin8 = cosq_ref[...], sinq_ref[...]
    o1 = x1 * cos8 - x2 * sin8
    o2 = x1 * sin8 + x2 * cos8
    mla_scale = (A_NOPE + A_ROPE) ** -0.5 * LOG2E
    q_lat = _dot(q[:, :n_nope].astype(BF16), wuk_ref[...]) * mla_scale
    pe = jnp.concatenate([o1, o2], axis=1) * mla_scale
    q_pe = _dot(pe.astype(BF16), pmat_ref[...])
    for hh in range(A_HEADS):
        qcat_ref[:, hh * QCAT:hh * QCAT + LANES] = q_lat[:, hh * LANES:(hh + 1) * LANES].astype(BF16)
        qcat_ref[:, hh * QCAT + LANES:(hh + 1) * QCAT] = q_pe[:, hh * LANES:(hh + 1) * LANES].astype(BF16)

    ckv = _rms(z[:, C_CKV:C_CKV + A_KV_LORA], kvng_ref[...])
    ckv_ref[...] = ckv
    misc = z[:, C_MISC:C_MISC + LANES]
    rot = (misc * rc_ref[...] + pltpu.roll(misc, A_ROPE // 2, 1) * rs1_ref[...]
           + pltpu.roll(misc, LANES - A_ROPE // 2, 1) * rs2_ref[...])
    kpe_ref[...] = rot[:, :A_ROPE]
    kcat_ref[:, :LANES] = ckv.astype(BF16)
    kcat_ref[:, LANES:] = rot.astype(BF16)

    kidx = misc[:, M_KIDX:M_KIDX + IDX_DIM]
    kidx_ref[...] = kidx
    kidx16_ref[...] = kidx.astype(BF16)
    widx_ref[...] = misc[:, M_WIDX:M_WIDX + IDX_HEADS] * (IDX_HEADS ** -0.5)
    kb = z[:, C_KB:C_KB + LANES]
    vb = z[:, C_VB:C_VB + LANES]
    kb_ref[...] = kb
    vb_ref[...] = vb
    kb16_ref[...] = kb.astype(BF16)
    vb16_ref[...] = vb.astype(BF16)
    qb_ref[...] = (z[:, C_QB:C_QB + B_WIDTH] * (B_HEAD_DIM ** -0.5 * LOG2E)).astype(BF16)
    qidx_ref[...] = (z[:, C_QI:C_QI + IDX_HEADS * IDX_DIM] * (IDX_DIM ** -0.5)).astype(BF16)
    sga_ref[...] = jax.nn.silu(z[:, C_GA:C_GA + A_WIDTH])
    sgb_ref[...] = jax.nn.silu(z[:, C_GB:C_GB + B_WIDTH])


def _project(x2d, rope_tabs, lw, *, period):
    n, d = x2d.shape
    tm = PROJ_TM
    assert n % tm == 0 and period % tm == 0
    n_rep = period // tm
    tok = lambda c: pl.BlockSpec((tm, c), lambda i: (i, 0))
    full = lambda a: pl.BlockSpec(a.shape, lambda i: (0,) * a.ndim)
    tab = pl.BlockSpec((tm, LANES), lambda i: (i % n_rep, 0))
    outs = [
        ("ckv", A_KV_LORA, F32), ("kpe", A_ROPE, F32), ("kb", LANES, F32), ("vb", LANES, F32),
        ("kidx", IDX_DIM, F32),
        ("kcat", QCAT, BF16), ("kb16", LANES, BF16), ("vb16", LANES, BF16), ("kidx16", IDX_DIM, BF16),
        ("qcat", A_HEADS * QCAT, BF16), ("qb", B_WIDTH, BF16), ("qidx", IDX_HEADS * IDX_DIM, BF16),
        ("widx", IDX_HEADS, F32), ("sga", A_WIDTH, F32), ("sgb", B_WIDTH, F32),
    ]
    res = pl.pallas_call(
        _proj_kernel,
        grid=(n // tm,),
        in_specs=[tok(d), full(lw["ng"]), full(lw["win"]), full(lw["qng"]), full(lw["kvng"]),
                  full(lw["wq"]), full(lw["wuk"]), full(lw["pmat"]), tab, tab, tab, tab, tab],
        out_specs=[tok(c) for _, c, _ in outs],
        out_shape=[jax.ShapeDtypeStruct((n, c), dt) for _, c, dt in outs],
        compiler_params=pltpu.CompilerParams(dimension_semantics=("arbitrary",),
                                             vmem_limit_bytes=VMEM_LIMIT_BYTES),
        name="project",
    )(x2d, lw["ng"], lw["win"], lw["qng"], lw["kvng"], lw["wq"], lw["wuk"], lw["pmat"], *rope_tabs)
    return {name: r for (name, _, _), r in zip(outs, res)}


R_CQ = 0
R_CKV = R_CQ + A_Q_LORA
R_VB = R_CKV + A_KV_LORA
R_QB = R_VB + LANES
R_QI = R_QB + B_WIDTH
R_GA = R_QI + IDX_HEADS * IDX_DIM
R_GB = R_GA + A_WIDTH
R_WI = R_GB + B_WIDTH
R_KB = R_WI + 16
R_KI = R_KB + LANES
R_KPE = R_KI + IDX_DIM
ROWS_T = R_KPE + A_ROPE
K_MISC = A_KV_LORA
K_KB = K_MISC + LANES
COLS_K = K_KB + LANES
MK_KPE = IDX_DIM
ONES_ROWS = 16
V_EXT = A_KV_LORA + ONES_ROWS


def _proj_t_kernel(x_ref, ng_ref, wk_ref, wt_ref, qng_ref, kvngc_ref, kvng_ref, wqt_ref, wukt_ref,
                   cost_ref, sint_ref, rc_ref, rs1_ref, rs2_ref,
                   ckv_ref, kcat_ref, kb16_ref, kidx16_ref,
                   kpeT_ref, kbT_ref, vbT_ref, kidxT_ref,
                   qcatT_ref, qbT_ref, qidxT_ref, widxT_ref, vmlaT_ref, vdsaT_ref, sgaT_ref, sgbT_ref):
    x = x_ref[...]
    tm = x.shape[0]
    hb = _rms(x, ng_ref[...]).astype(BF16)

    zk = _dot(hb, wk_ref[...])
    ckv = _rms(zk[:, :A_KV_LORA], kvng_ref[...])
    ckv_ref[...] = ckv
    misc = zk[:, K_MISC:K_MISC + LANES]
    lane = lax.broadcasted_iota(I32, (tm, LANES), 1)
    kidx16_ref[...] = jnp.where(lane < IDX_DIM, misc, 0.0).astype(BF16)
    rot = (misc * rc_ref[...] + pltpu.roll(misc, A_ROPE // 2, 1) * rs1_ref[...]
           + pltpu.roll(misc, LANES - A_ROPE // 2, 1) * rs2_ref[...])
    kcat_ref[:, :LANES] = ckv.astype(BF16)
    kcat_ref[:, LANES:] = pltpu.roll(rot, LANES - MK_KPE, 1).astype(BF16)
    kb16_ref[...] = zk[:, K_KB:K_KB + LANES].astype(BF16)

    zt = _dot_nt(wt_ref[...], hb)

    def rms_t(c, g):
        return c * lax.rsqrt(jnp.mean(c * c, axis=0, keepdims=True) + EPS) * g

    cq = rms_t(zt[R_CQ:R_CQ + A_Q_LORA], qng_ref[...])
    qt = _dot(wqt_ref[...], cq.astype(BF16))
    n_nope = A_HEADS * A_NOPE
    x1 = qt[n_nope:n_nope + LANES]
    x2 = qt[n_nope + LANES:n_nope + 2 * LANES]
    cos8, sin8 = cost_ref[...], sint_ref[...]
    mla_scale = (A_NOPE + A_ROPE) ** -0.5 * LOG2E
    o1 = (x1 * cos8 - x2 * sin8) * mla_scale
    o2 = (x1 * sin8 + x2 * cos8) * mla_scale
    q_lat = _dot(wukt_ref[...], qt[:n_nope].astype(BF16)) * mla_scale
    half = A_ROPE // 2
    for h in range(A_HEADS):
        qcatT_ref[h, :LANES, :] = q_lat[h * LANES:(h + 1) * LANES].astype(BF16)
        qcatT_ref[h, LANES:LANES + half, :] = o1[h * half:(h + 1) * half].astype(BF16)
        qcatT_ref[h, LANES + half:LANES + A_ROPE, :] = o2[h * half:(h + 1) * half].astype(BF16)
        qcatT_ref[h, LANES + A_ROPE:, :] = jnp.zeros((QCAT - LANES - A_ROPE, tm), BF16)
    ones = jnp.ones((ONES_ROWS, TQ), BF16)
    ckv_t = rms_t(zt[R_CKV:R_CKV + A_KV_LORA], kvngc_ref[...]).astype(BF16)
    vb_t = zt[R_VB:R_VB + LANES]
    for c in range(tm // TQ):
        blk = slice(c * TQ, (c + 1) * TQ)
        vmlaT_ref[c, :A_KV_LORA, :] = ckv_t[:, blk]
        vmlaT_ref[c, A_KV_LORA:, :] = ones
        vdsaT_ref[c, :LANES, :] = vb_t[:, blk].astype(BF16)
        vdsaT_ref[c, LANES:, :] = ones
    vbT_ref[...] = vb_t
    kbT_ref[...] = zt[R_KB:R_KB + LANES]
    kidxT_ref[...] = zt[R_KI:R_KI + IDX_DIM]
    k1, k2 = zt[R_KPE:R_KPE + half], zt[R_KPE + half:R_KPE + A_ROPE]
    cos1, sin1 = cos8[:half], sin8[:half]
    kpeT_ref[:half, :] = k1 * cos1 - k2 * sin1
    kpeT_ref[half:, :] = k1 * sin1 + k2 * cos1
    qbT_ref[...] = (zt[R_QB:R_QB + B_WIDTH] * (B_HEAD_DIM ** -0.5 * LOG2E)).astype(BF16)
    qidxT_ref[...] = (zt[R_QI:R_QI + IDX_HEADS * IDX_DIM] * (IDX_DIM ** -0.5)).astype(BF16)
    sgaT_ref[...] = jax.nn.silu(zt[R_GA:R_GA + A_WIDTH]).astype(BF16)
    sgbT_ref[...] = jax.nn.silu(zt[R_GB:R_GB + B_WIDTH]).astype(BF16)
    widxT_ref[...] = zt[R_WI:R_WI + IDX_HEADS] * (IDX_HEADS ** -0.5)


def _project_t(x2d, tabs, lw, b, t):
    n, d = x2d.shape
    tm = PROJ_T_TM
    assert t % tm == 0 and tm % TQ == 0
    nt = t // tm
    tok = lambda c: pl.BlockSpec((tm, c), lambda i: (i, 0))
    full = lambda a: pl.BlockSpec(a.shape, lambda i: (0,) * a.ndim)
    vblk = pl.BlockSpec((None, tm // TQ, V_EXT, TQ), lambda i: (i // nt, i % nt, 0, 0))
    tab_t = pl.BlockSpec((LANES, tm), lambda i: (0, i % nt))
    tab_k = pl.BlockSpec((tm, LANES), lambda i: (i % nt, 0))
    rows_t = lambda r: pl.BlockSpec((None, r, tm), lambda i: (i // nt, 0, i % nt))
    outs = [
        ("ckv", (n, A_KV_LORA), F32, tok(A_KV_LORA)),
        ("kcat", (n, QCAT), BF16, tok(QCAT)), ("kb16", (n, LANES), BF16, tok(LANES)),
        ("kidx16", (n, LANES), BF16, tok(LANES)),
        ("kpeT", (b, A_ROPE, t), F32, rows_t(A_ROPE)), ("kbT", (b, LANES, t), F32, rows_t(LANES)),
        ("vbT", (b, LANES, t), F32, rows_t(LANES)), ("kidxT", (b, IDX_DIM, t), F32, rows_t(IDX_DIM)),
        ("qcatT", (b, A_HEADS, QCAT, t), BF16,
         pl.BlockSpec((None, A_HEADS, QCAT, tm), lambda i: (i // nt, 0, 0, i % nt))),
        ("qbT", (b, B_WIDTH, t), BF16, rows_t(B_WIDTH)),
        ("qidxT", (b, IDX_HEADS * IDX_DIM, t), BF16, rows_t(IDX_HEADS * IDX_DIM)),
        ("widxT", (b, IDX_HEADS, t), F32, rows_t(IDX_HEADS)),
        ("vmlaT", (b, t // TQ, V_EXT, TQ), BF16, vblk), ("vdsaT", (b, t // TQ, V_EXT, TQ), BF16, vblk),
        ("sgaT", (b, A_WIDTH, t), BF16, rows_t(A_WIDTH)), ("sgbT", (b, B_WIDTH, t), BF16, rows_t(B_WIDTH)),
    ]
    cost, sint, rc, rs1, rs2 = tabs
    res = pl.pallas_call(
        _proj_t_kernel,
        grid=(n // tm,),
        in_specs=[tok(d), full(lw["ng"]), full(lw["wk"]), full(lw["wt"]), full(lw["qngc"]), full(lw["kvngc"]),
                  full(lw["kvng"]), full(lw["wqt"]), full(lw["wukt"]), tab_t, tab_t, tab_k, tab_k, tab_k],
        out_specs=[o[3] for o in outs],
        out_shape=[jax.ShapeDtypeStruct(o[1], o[2]) for o in outs],
        compiler_params=pltpu.CompilerParams(dimension_semantics=("arbitrary",),
                                             vmem_limit_bytes=VMEM_LIMIT_BYTES),
        name="project_prompt",
    )(x2d, lw["ng"], lw["wk"], lw["wt"], lw["qngc"], lw["kvngc"], lw["kvng"], lw["wqt"], lw["wukt"],
      cost, sint, rc, rs1, rs2)
    return {o[0]: r for o, r in zip(outs, res)}


NEG_FLT_MAX = -3.4028234663852886e38
KEY_NEG_FLT_MAX = INT_MIN + (1 << 23)


def _key_to_float(k):
    k = jnp.maximum(k, KEY_NEG_FLT_MAX)
    return pltpu.bitcast(k ^ ((k >> 31) & 0x7FFFFFFF), F32)


def _count(pred):
    return jnp.sum(jnp.where(pred, 1.0, 0.0), axis=1, keepdims=True)


SAMPLE_STREAMS = 2
MAX_TIE_SWEEPS = 8.0
SAFE_DENOM_MIN = 2.0 ** -90
SAFE_DENOM_MAX = 2.0 ** 40


def _fold8(x, op=jnp.add):
    parts = [x[i:i + 8] for i in range(0, x.shape[0], 8)]
    while len(parts) > 1:
        parts = [op(a, b) for a, b in zip(parts[::2], parts[1::2])]
    return parts[0]


def _prompt_attn_kernel(qcatT_ref, qbT_ref, qidxT_ref, widxT_ref, kcat_ref, kb_ref, kidx_ref,
                        vmlaT_ref, vdsaT_ref, bias_ref, olatT_ref, obT_ref,
                        qa_ref, qbp_ref, qip_ref, sc_ref, m_ref, acc_ref, mb_ref, accb_ref, kn_ref, *, n_top):
    qi = pl.program_id(1)
    nblk = qi + 1
    tq = TQ
    krow = lax.broadcasted_iota(I32, (tq, tq), 0)
    qcol = lax.broadcasted_iota(I32, (tq, tq), 1)
    shift = CHUNK.bit_length() - 1
    diag_ok = (qcol >> shift) >= (krow >> shift)
    hcols = lambda h: slice(h * tq, (h + 1) * tq)

    def per_head(fn):
        return jnp.concatenate([fn(h) for h in range(A_HEADS)], axis=1)

    qbp_ref[...] = jnp.zeros(qbp_ref.shape, BF16)
    qip_ref[...] = jnp.zeros(qip_ref.shape, BF16)
    for h in range(A_HEADS):
        g = h // B_GROUP
        qa_ref[:, hcols(h)] = qcatT_ref[h]
        qbp_ref[g * B_HEAD_DIM:(g + 1) * B_HEAD_DIM, hcols(h)] = qbT_ref[h * B_HEAD_DIM:(h + 1) * B_HEAD_DIM, :]
        qip_ref[:IDX_DIM, hcols(h)] = qidxT_ref[h * IDX_DIM:(h + 1) * IDX_DIM, :]
    w_all = per_head(lambda h: widxT_ref[h:h + 1, :])

    @pl.when(qi == 0)
    def _():
        def max_row_norm2(k_ref):
            k = k_ref[...].astype(F32)
            return jnp.max(jnp.sum(k * k, axis=1, keepdims=True))
        kn_ref[0:1, :] = jnp.full((1, LANES), max_row_norm2(kcat_ref), F32)
        kn_ref[1:2, :] = jnp.full((1, LANES), max_row_norm2(kb_ref), F32)
        kn_ref[2:3, :] = jnp.full((1, LANES), jnp.max(bias_ref[...]), F32)

    def col_norm(q_ref):
        q = q_ref[...].astype(F32)
        return jnp.sqrt(jnp.sum(q * q, axis=0, keepdims=True))

    shift_a = jnp.sqrt(kn_ref[0:1, 0:1]) * col_norm(qa_ref)
    shift_b = jnp.sqrt(kn_ref[1:2, 0:1]) * col_norm(qbp_ref) + jnp.maximum(kn_ref[2:3, 0:1], 0.0)

    def accumulate(s_t, values, shift_or_m, acc_r, exact):
        if exact:
            m_prev = shift_or_m[0:1, :]
            m_new = jnp.maximum(m_prev, jnp.max(s_t, axis=0, keepdims=True))
            alpha = jnp.exp2(m_prev - m_new)
            shift_or_m[0:1, :] = m_new
        else:
            m_new = shift_or_m
        p_t = jnp.exp2(s_t - m_new).astype(BF16)
        for v_t, lanes in values:
            pv = _dot(v_t, p_t[:, lanes])
            acc_r[:, lanes] = (alpha[:, lanes] * acc_r[:, lanes] if exact else acc_r[:, lanes]) + pv

    def unsafe(l):
        return jnp.max(jnp.where((l >= SAFE_DENOM_MIN) & (l <= SAFE_DENOM_MAX), 0.0, 1.0)) > 0.0

    all_lanes = slice(0, A_HEADS * tq)

    def for_blocks(n, block):
        def pair(p, c):
            block(2 * p)
            block(2 * p + 1)
            return c

        lax.fori_loop(0, lax.shift_right_logical(n, 1), pair, 0)

        @pl.when((n & 1) == 1)
        def _():
            block(n - 1)

    def mla_pass(exact):
        acc_ref[...] = jnp.zeros(acc_ref.shape, F32)
        if exact:
            m_ref[...] = jnp.full(m_ref.shape, NEG_BIG, F32)

        def block(j, masked):
            start = pl.multiple_of(j * tq, tq)
            s_t = _dot(kcat_ref[pl.ds(start, tq), :], qa_ref[...])
            if masked:
                s_t = per_head(lambda h: jnp.where(diag_ok, s_t[:, hcols(h)], NEG_BIG))
            accumulate(s_t, [(vmlaT_ref[j], all_lanes)], m_ref if exact else shift_a, acc_ref, exact)
            if exact:
                return
            r = jnp.maximum(_dot(kidx_ref[pl.ds(start, tq), :], qip_ref[...]), 0.0) * w_all
            score = r[:, hcols(0)]
            for h in range(1, IDX_HEADS):
                score = score + r[:, hcols(h)]
            if masked:
                score = jnp.where(diag_ok, score, -jnp.inf)
            sc_ref[j] = score

        for_blocks(qi, lambda j: block(j, False))
        block(qi, True)

    mla_pass(False)

    @pl.when(unsafe(acc_ref[A_KV_LORA:A_KV_LORA + 1, :]))
    def _():
        mla_pass(True)

    o_t = acc_ref[:A_KV_LORA, :] * (1.0 / acc_ref[A_KV_LORA:A_KV_LORA + 1, :])
    for h in range(A_HEADS):
        olatT_ref[h * LANES:(h + 1) * LANES, :] = o_t[:, hcols(h)].astype(olatT_ref.dtype)

    def count(pred):
        def body(j, c):
            return c + _fold8(jnp.where(pred(sc_ref[j], j), 1.0, 0.0))
        part = lax.fori_loop(0, nblk, body, jnp.zeros((8, tq), F32))
        return jnp.sum(part, axis=0, keepdims=True)

    kf = float(n_top)

    def bis_body(it, carry):
        lo, cnt_lo = carry
        cand = lo + lax.shift_left(jnp.int32(1), 31 - it)
        cand_f = _key_to_float(cand)
        cnt = count(lambda s, j: s >= cand_f)
        take = cnt >= kf
        return jnp.where(take, cand, lo), jnp.where(take, cnt, cnt_lo)

    n_steps = jnp.where(nblk * tq <= n_top, 0, 32)
    lo, cnt_ge = lax.fori_loop(
        0, n_steps, bis_body, (jnp.full((1, tq), INT_MIN, I32), jnp.full((1, tq), 1e9, F32)))
    few = lo == INT_MIN
    thr = _key_to_float(lo)
    excess0 = jnp.where(few, 0.0, cnt_ge - kf)
    max_excess = jnp.max(excess0)

    def drop_from(cut):
        def body(j, c):
            s = sc_ref[j]
            sc_ref[j] = jnp.where((s == thr) & ((krow + j * tq) >= cut), -jnp.inf, s)
            return c
        lax.fori_loop(0, nblk, body, 0)

    @pl.when((max_excess > 0.0) & (max_excess <= MAX_TIE_SWEEPS))
    def _():
        def last_tie_below(cut):
            def body(j, m):
                pos = krow + j * tq
                hit = jnp.where((sc_ref[j] == thr) & (pos < cut), pos, -1)
                return jnp.maximum(m, _fold8(hit, jnp.maximum))
            part = lax.fori_loop(0, nblk, body, jnp.full((8, tq), -1, I32))
            return jnp.max(part, axis=0, keepdims=True)

        def sweep(c):
            excess, cut = c
            last = last_tie_below(cut)
            live = excess > 0.0
            return jnp.where(live, excess - 1.0, excess), jnp.where(live, last, cut)

        _, cut = lax.while_loop(lambda c: jnp.max(c[0]) > 0.0, sweep,
                                (excess0, jnp.full((1, tq), 2 ** 30, I32)))
        drop_from(cut)

    @pl.when(max_excess > MAX_TIE_SWEEPS)
    def _():
        need = kf - count(lambda s, j: s > thr)
        n_bits = (sc_ref.shape[0] * tq).bit_length()

        def cut_body(it, cpos):
            cand = cpos + lax.shift_left(jnp.int32(1), n_bits - 1 - it)
            cnt = count(lambda s, j: (s == thr) & ((krow + j * tq) < cand))
            return jnp.where(cnt < need, cand, cpos)

        keep = lax.fori_loop(0, n_bits, cut_body, jnp.zeros((1, tq), I32))
        drop_from(jnp.where(excess0 > 0.0, keep + 1, 2 ** 30))

    group_lanes = [slice(g * B_GROUP * tq, (g + 1) * B_GROUP * tq) for g in range(B_KV_HEADS)]

    def dsa_pass(exact):
        accb_ref[...] = jnp.zeros(accb_ref.shape, F32)
        if exact:
            mb_ref[...] = jnp.full(mb_ref.shape, NEG_BIG, F32)

        def block(j, near):
            start = pl.multiple_of(j * tq, tq)
            sel = sc_ref[j] >= thr
            s_t = _dot(kb_ref[pl.ds(start, tq), :], qbp_ref[...])
            if near:
                s_t = s_t + bias_ref[qi - j]
            s_t = per_head(lambda h: jnp.where(sel, s_t[:, hcols(h)], NEG_BIG))
            v_all = vdsaT_ref[j]
            values = [(jnp.concatenate([v_all[g * B_HEAD_DIM:(g + 1) * B_HEAD_DIM], v_all[LANES:]], axis=0),
                       group_lanes[g]) for g in range(B_KV_HEADS)]
            accumulate(s_t, values, mb_ref if exact else shift_b, accb_ref, exact)

        for_blocks(jnp.maximum(qi - 1, 0), lambda j: block(j, False))

        @pl.when(qi >= 1)
        def _():
            block(qi - 1, True)

        block(qi, True)

    dsa_pass(False)

    @pl.when(unsafe(accb_ref[B_HEAD_DIM:B_HEAD_DIM + 1, :]))
    def _():
        dsa_pass(True)

    inv_b = 1.0 / accb_ref[B_HEAD_DIM:B_HEAD_DIM + 1, :]
    for h in range(B_HEADS):
        obT_ref[h * B_HEAD_DIM:(h + 1) * B_HEAD_DIM, :] = (
            accb_ref[:B_HEAD_DIM, hcols(h)] * inv_b[:, hcols(h)]).astype(obT_ref.dtype)


def _prompt_attention(pr, bias_p, b, t, n_top):
    tq = TQ
    assert t % tq == 0
    nq = t // tq
    r3 = lambda a: a.reshape(b, t, a.shape[-1])
    qrows = lambda r: pl.BlockSpec((None, r, tq), lambda bi, qi: (bi, 0, qi))
    kall = lambda c: pl.BlockSpec((None, t, c), lambda bi, qi: (bi, 0, 0))
    vall = pl.BlockSpec((None, nq, V_EXT, tq), lambda bi, qi: (bi, 0, 0, 0))
    olat, ob = pl.pallas_call(
        functools.partial(_prompt_attn_kernel, n_top=n_top),
        grid=(b, nq),
        in_specs=[pl.BlockSpec((None, A_HEADS, QCAT, tq), lambda bi, qi: (bi, 0, 0, qi)),
                  qrows(B_WIDTH), qrows(IDX_HEADS * IDX_DIM), qrows(IDX_HEADS),
                  kall(QCAT), kall(LANES), kall(LANES), vall, vall,
                  pl.BlockSpec(bias_p.shape, lambda bi, qi: (0, 0, 0))],
        out_specs=[qrows(A_HEADS * A_KV_LORA), qrows(B_WIDTH)],
        out_shape=[jax.ShapeDtypeStruct((b, A_HEADS * A_KV_LORA, t), BF16),
                   jax.ShapeDtypeStruct((b, B_WIDTH, t), BF16)],
        scratch_shapes=[
            pltpu.VMEM((QCAT, A_HEADS * tq), BF16),
            pltpu.VMEM((LANES, B_HEADS * tq), BF16),
            pltpu.VMEM((LANES, IDX_HEADS * tq), BF16),
            pltpu.VMEM((nq, tq, tq), F32),
            pltpu.VMEM((8, A_HEADS * tq), F32),
            pltpu.VMEM((V_EXT, A_HEADS * tq), F32),
            pltpu.VMEM((8, B_HEADS * tq), F32),
            pltpu.VMEM((B_HEAD_DIM + ONES_ROWS, B_HEADS * tq), F32),
            pltpu.VMEM((8, LANES), F32),
        ],
        compiler_params=pltpu.CompilerParams(dimension_semantics=("arbitrary", "arbitrary"),
                                             vmem_limit_bytes=VMEM_LIMIT_BYTES),
        name="prompt_attention",
    )(pr["qcatT"], pr["qbT"], pr["qidxT"], pr["widxT"],
      r3(pr["kcat"]), r3(pr["kb16"]), r3(pr["kidx16"]), pr["vmlaT"], pr["vdsaT"], bias_p)
    return olat, ob


def _sample_attn_kernel(qcat_ref, qb_ref, qidx_ref, widx_ref, kcatn_ref, kbn_ref, vbn_ref, kidxn_ref,
                        cckv_ref, ckpeT_ref, ckT_ref, cvT_ref, ckidxT_ref, bias_ref,
                        olat_ref, ob_ref, *, n_top, t_new, past):
    tq = t_new
    pad = LANES
    n_keys = past + pad
    kf = float(n_top)
    n_bits = n_keys.bit_length()
    n_far = n_keys - bias_ref.shape[-1]

    def padrows(a):
        return jnp.concatenate([a, jnp.zeros((pad - t_new, a.shape[1]), a.dtype)], axis=0)

    def softmax(s):
        m = jnp.max(s, axis=1, keepdims=True)
        p = jnp.exp2(s - m)
        return p.astype(BF16), jnp.sum(p, axis=1, keepdims=True)

    def new_cols(rows):
        return lax.broadcasted_iota(I32, (rows, pad), 1) < t_new

    def mla_and_scores(i):
        ckv_c = cckv_ref[i].astype(BF16)
        kpe_t = ckpeT_ref[i].astype(BF16)
        kcat_n = padrows(kcatn_ref[i])
        qs = jnp.concatenate([qcat_ref[i, :, h * QCAT:(h + 1) * QCAT] for h in range(A_HEADS)], axis=0)
        s_c = _dot_nt(qs[:, :A_KV_LORA], ckv_c) + _dot(qs[:, A_KV_LORA:A_KV_LORA + A_ROPE], kpe_t)
        s_n = jnp.where(new_cols(A_HEADS * tq), _dot_nt(qs, kcat_n), NEG_BIG)
        pb, l = softmax(jnp.concatenate([s_c, s_n], axis=1))
        o = (_dot(pb[:, :past], ckv_c) + _dot(pb[:, past:], kcat_n[:, :A_KV_LORA])) / l
        for h in range(A_HEADS):
            olat_ref[i, :, h * LANES:(h + 1) * LANES] = o[h * tq:(h + 1) * tq].astype(olat_ref.dtype)

        kidx_t = ckidxT_ref[i].astype(BF16)
        kidx_n = padrows(kidxn_ref[i])
        qis = jnp.concatenate(
            [qidx_ref[i, :, h * IDX_DIM:(h + 1) * IDX_DIM] for h in range(IDX_HEADS)], axis=0)

        def head_sum(dots):
            acc = jnp.maximum(dots[:tq], 0.0) * widx_ref[i, :, 0:1]
            for h in range(1, IDX_HEADS):
                acc = acc + jnp.maximum(dots[h * tq:(h + 1) * tq], 0.0) * widx_ref[i, :, h:h + 1]
            return acc

        return jnp.concatenate(
            [head_sum(_dot(qis, kidx_t)),
             jnp.where(new_cols(tq), head_sum(_dot_nt(qis, kidx_n)), -jnp.inf)], axis=1)

    def search_step(it, score, lo, cnt_lo):
        cand = lo + jnp.int32(INT_MIN if it == 0 else 1 << (31 - it))
        cnt = _count(score >= _key_to_float(cand))
        take = cnt >= kf
        return jnp.where(take, cand, lo), jnp.where(take, cnt, cnt_lo)

    def attend_selected(i, score, lo, cnt_ge):
        few = lo == INT_MIN
        thr = _key_to_float(lo)
        has_tie = jnp.max(jnp.where((cnt_ge > kf) & (~few), 1.0, 0.0))

        def drop_ties():
            cols = lax.broadcasted_iota(I32, (tq, n_keys), 1)
            need = kf - _count(score > thr)
            eq = score == thr

            def cut_body(it, cpos):
                cand = cpos + lax.shift_left(jnp.int32(1), n_bits - 1 - it)
                cnt = _count(eq & (cols < cand))
                return jnp.where(cnt < need, cand, cpos)

            keep = lax.fori_loop(0, n_bits, cut_body, jnp.zeros((tq, 1), I32))
            return jnp.where(eq & (cols > keep) & (cnt_ge > kf) & (~few), -jnp.inf, score)

        kept = lax.cond(has_tie > 0.0, drop_ties, lambda: score)
        sel_g = jnp.concatenate([kept] * B_GROUP, axis=0) >= jnp.concatenate([thr] * B_GROUP, axis=0)

        k_t = ckT_ref[i].astype(BF16)
        v_t = cvT_ref[i].astype(BF16)
        k_n = padrows(kbn_ref[i])
        v_n = padrows(vbn_ref[i])
        for g in range(B_KV_HEADS):
            feats = slice(g * B_HEAD_DIM, (g + 1) * B_HEAD_DIM)
            heads = range(g * B_GROUP, (g + 1) * B_GROUP)
            qg = jnp.concatenate([qb_ref[i, :, h * B_HEAD_DIM:(h + 1) * B_HEAD_DIM] for h in heads], axis=0)
            near = jnp.concatenate([bias_ref[0, h] - bias_ref[1, h] for h in heads], axis=0)
            sg = _dot(qg, k_t[feats])
            sg = jnp.concatenate([sg[:, :n_far], sg[:, n_far:] + near[:, :past - n_far],
                                  _dot_nt(qg, k_n[:, feats]) + near[:, past - n_far:]], axis=1)
            pb, l = softmax(jnp.where(sel_g, sg, NEG_BIG))
            og = (_dot_nt(pb[:, :past], v_t[feats]) + _dot(pb[:, past:], v_n[:, feats])) / l
            for hh, h in enumerate(heads):
                ob_ref[i, :, h * B_HEAD_DIM:(h + 1) * B_HEAD_DIM] = og[hh * tq:(hh + 1) * tq]

    streams = range(SAMPLE_STREAMS)
    scores = [mla_and_scores(i) for i in streams]
    state = [(jnp.full((tq, 1), INT_MIN, I32), jnp.full((tq, 1), 1e9, F32)) for _ in streams]
    for it in range(32):
        state = [search_step(it, scores[i], *state[i]) for i in streams]
    for i in streams:
        attend_selected(i, scores[i], *state[i])


def _sample_attention(pr, caches, bias_s, b, t_new, past, n_top):
    r3 = lambda a: a.reshape(b, t_new, a.shape[-1])
    ns = SAMPLE_STREAMS
    assert b % ns == 0
    per_b = lambda n, c: pl.BlockSpec((ns, n, c), lambda bi: (bi, 0, 0))
    news = [pr["qcat"], pr["qb"], pr["qidx"], pr["widx"], pr["kcat"], pr["kb16"], pr["vb16"], pr["kidx16"]]
    olat, ob = pl.pallas_call(
        functools.partial(_sample_attn_kernel, n_top=n_top, t_new=t_new, past=past),
        grid=(b // ns,),
        in_specs=[per_b(t_new, a.shape[-1]) for a in news]
                 + [per_b(c.shape[1], c.shape[2]) for c in caches]
                 + [pl.BlockSpec(bias_s.shape, lambda bi: (0, 0, 0, 0))],
        out_specs=[per_b(t_new, A_HEADS * A_KV_LORA), per_b(t_new, B_WIDTH)],
        out_shape=[jax.ShapeDtypeStruct((b, t_new, A_HEADS * A_KV_LORA), BF16),
                   jax.ShapeDtypeStruct((b, t_new, B_WIDTH), F32)],
        compiler_params=pltpu.CompilerParams(dimension_semantics=("arbitrary",),
                                             vmem_limit_bytes=VMEM_LIMIT_BYTES),
        name="sample_attention",
    )(*[r3(a) for a in news], *caches, bias_s)
    return olat.reshape(b * t_new, -1), ob.reshape(b * t_new, -1)


def _combine_kernel(x_ref, olat_ref, ob_ref, sga_ref, sgb_ref, wuv_ref, wout_ref, fg_ref, y_ref, *, final):
    o_a = _dot(olat_ref[...], wuv_ref[...])
    mix = jnp.concatenate([o_a * sga_ref[...], ob_ref[...] * sgb_ref[...]], axis=1)
    y = x_ref[...] + _dot(mix.astype(BF16), wout_ref[...])
    if final:
        y = _rms(y, fg_ref[...])
    y_ref[...] = y


def _combine(x2d, olat, ob, pr, lw, fg, final):
    n, d = x2d.shape
    tm = PROJ_TM
    tok = lambda c: pl.BlockSpec((tm, c), lambda i: (i, 0))
    full = lambda a: pl.BlockSpec(a.shape, lambda i: (0,) * a.ndim)
    return pl.pallas_call(
        functools.partial(_combine_kernel, final=final),
        grid=(n // tm,),
        in_specs=[tok(d), tok(olat.shape[1]), tok(ob.shape[1]), tok(A_WIDTH), tok(B_WIDTH),
                  full(lw["wuv"]), full(lw["wout"]), full(fg)],
        out_specs=tok(d),
        out_shape=jax.ShapeDtypeStruct((n, d), F32),
        compiler_params=pltpu.CompilerParams(dimension_semantics=("arbitrary",),
                                             vmem_limit_bytes=VMEM_LIMIT_BYTES),
        name="combine",
    )(x2d, olat, ob, pr["sga"], pr["sgb"], lw["wuv"], lw["wout"], fg)


def _combine_t_kernel(x_ref, olatT_ref, obT_ref, sgaT_ref, sgbT_ref, wuvt_ref, wout_ref, fg_ref, y_ref, *, final):
    o_a = jnp.concatenate([_dot(wuvt_ref[h], olatT_ref[h * A_KV_LORA:(h + 1) * A_KV_LORA, :])
                           for h in range(A_HEADS)], axis=0)
    mix_t = jnp.concatenate([o_a * sgaT_ref[...].astype(F32),
                             obT_ref[...].astype(F32) * sgbT_ref[...].astype(F32)], axis=0).astype(BF16)
    y = x_ref[...] + lax.dot_general(mix_t, wout_ref[...], (((0,), (0,)), ((), ())),
                                     preferred_element_type=F32)
    if final:
        y = _rms(y, fg_ref[...])
    y_ref[...] = y


def _combine_t(x3d, olat_t, ob_t, pr, lw, fg, final):
    b, t, d = x3d.shape
    tm = COMBINE_TM
    rows_t = lambda r: pl.BlockSpec((None, r, tm), lambda bi, ti: (bi, 0, ti))
    full = lambda a: pl.BlockSpec(a.shape, lambda bi, ti: (0,) * a.ndim)
    xblk = pl.BlockSpec((None, tm, d), lambda bi, ti: (bi, ti, 0))
    return pl.pallas_call(
        functools.partial(_combine_t_kernel, final=final),
        grid=(b, t // tm),
        in_specs=[xblk, rows_t(A_HEADS * A_KV_LORA), rows_t(B_WIDTH), rows_t(A_WIDTH), rows_t(B_WIDTH),
                  full(lw["wuvt"]), full(lw["wout"]), full(fg)],
        out_specs=xblk,
        out_shape=jax.ShapeDtypeStruct((b, t, d), F32),
        compiler_params=pltpu.CompilerParams(dimension_semantics=("arbitrary", "arbitrary"),
                                             vmem_limit_bytes=VMEM_LIMIT_BYTES),
        name="combine_prompt",
    )(x3d, olat_t, ob_t, pr["sgaT"], pr["sgbT"], lw["wuvt"], lw["wout"], fg)


def _layer_weights(norm_g, w_in, q_norm_g, kv_norm_g, w_uq, w_uk, w_uv, w_out):
    d = w_in.shape[0]
    o = np.cumsum([0, A_Q_LORA, A_KV_LORA, A_ROPE, A_WIDTH, B_WIDTH, B_KV_HEADS * B_HEAD_DIM,
                   B_KV_HEADS * B_HEAD_DIM, IDX_HEADS * IDX_DIM, IDX_DIM, IDX_HEADS, B_WIDTH])
    seg = lambda i: w_in[:, int(o[i]):int(o[i + 1])]
    misc = jnp.concatenate([seg(2), seg(8), seg(9),
                            jnp.zeros((d, LANES - A_ROPE - IDX_DIM - IDX_HEADS), w_in.dtype)], axis=1)
    win = jnp.concatenate([seg(0), seg(1), misc, seg(3), seg(4), seg(5), seg(6), seg(7), seg(10)], axis=1)
    assert win.shape[1] == IN_PAD
    half = A_ROPE // 2
    wq = jnp.concatenate([
        w_uq[:, :, :A_NOPE].reshape(A_Q_LORA, A_HEADS * A_NOPE),
        w_uq[:, :, A_NOPE:A_NOPE + half].reshape(A_Q_LORA, A_HEADS * half),
        w_uq[:, :, A_NOPE + half:].reshape(A_Q_LORA, A_HEADS * half)], axis=1)
    eye = jnp.eye(A_HEADS, dtype=w_uk.dtype)
    wuk = jnp.einsum('chn,hg->hngc', w_uk, eye).reshape(A_HEADS * A_NOPE, A_HEADS * A_KV_LORA)
    wuv = jnp.einsum('chv,hg->hcgv', w_uv, eye).reshape(A_HEADS * A_KV_LORA, A_HEADS * A_V)
    pm = np.zeros((2 * LANES, A_HEADS * LANES), np.float32)
    for h in range(A_HEADS):
        for i in range(half):
            pm[h * half + i, h * LANES + i] = 1.0
            pm[LANES + h * half + i, h * LANES + half + i] = 1.0
    zpad = lambda c: jnp.zeros((d, c), w_in.dtype)
    wk = jnp.concatenate([seg(1), seg(8), seg(2), zpad(LANES - IDX_DIM - A_ROPE), seg(5)], axis=1)
    wt = jnp.concatenate([seg(0), seg(1), seg(6), seg(4), seg(7), seg(3), seg(10), seg(9),
                          zpad(R_KB - R_WI - IDX_HEADS), seg(5), seg(8), seg(2)], axis=1).T
    assert wk.shape[1] == COLS_K and wt.shape[0] == ROWS_T
    bc = lambda g: jnp.broadcast_to(g.reshape(-1, 1), (g.shape[0], PROJ_T_TM))
    return {
        "wk": wk.astype(BF16), "wt": wt.astype(BF16), "qngc": bc(q_norm_g), "kvngc": bc(kv_norm_g),
        "wqt": wq.T.astype(BF16), "wukt": wuk.T.astype(BF16),
        "wuvt": w_uv.transpose(1, 2, 0).astype(BF16),
        "ng": norm_g.reshape(1, -1), "win": win.astype(BF16),
        "qng": q_norm_g.reshape(1, -1), "kvng": kv_norm_g.reshape(1, -1),
        "wq": wq.astype(BF16), "wuk": wuk.astype(BF16), "wuv": wuv.astype(BF16),
        "pmat": jnp.asarray(pm, BF16), "wout": w_out.astype(BF16),
    }


def _rope_tables(pos):
    half = A_ROPE // 2
    inv = ROPE_THETA ** (-jnp.arange(half, dtype=F32) / half)
    ang = pos.astype(F32)[:, None] * inv[None, :]
    cos, sin = jnp.cos(ang), jnp.sin(ang)
    z = jnp.zeros((pos.shape[0], LANES - A_ROPE), F32)
    zh = jnp.zeros_like(cos)
    cosq = jnp.tile(cos, (1, A_HEADS))
    sinq = jnp.tile(sin, (1, A_HEADS))
    rc = jnp.concatenate([cos, cos, z], axis=1)
    rs1 = jnp.concatenate([zh, sin, z], axis=1)
    rs2 = jnp.concatenate([-sin, zh, z], axis=1)
    return cosq, sinq, rc, rs1, rs2


def _rope_tables_t(pos):
    half = A_ROPE // 2
    inv = ROPE_THETA ** (-jnp.arange(half, dtype=F32) / half)
    ang = pos.astype(F32)[:, None] * inv[None, :]
    cos, sin = jnp.cos(ang), jnp.sin(ang)
    cost = jnp.tile(cos.T, (A_HEADS, 1))
    sint = jnp.tile(sin.T, (A_HEADS, 1))
    z0 = jnp.zeros((pos.shape[0], MK_KPE), F32)
    z1 = jnp.zeros((pos.shape[0], LANES - MK_KPE - A_ROPE), F32)
    zh = jnp.zeros_like(cos)
    rc = jnp.concatenate([z0, cos, cos, z1], axis=1)
    rs1 = jnp.concatenate([z0, zh, sin, z1], axis=1)
    rs2 = jnp.concatenate([z0, -sin, zh, z1], axis=1)
    return cost, sint, rc, rs1, rs2


def kernel(x_prompt, x_sample, cache_mla_ckv, cache_mla_kpe, cache_dsa_k, cache_dsa_v, cache_dsa_kidx,
           norm_g, w_in, mla_q_norm_g, mla_kv_norm_g, mla_w_uq, mla_w_uk, mla_w_uv, rel_bias, w_out,
           final_norm_g):
    bp, tp, d = x_prompt.shape
    bs, ts, _ = x_sample.shape
    depth = w_in.shape[0]
    past = cache_mla_ckv.shape[2]
    n_top_p = min(TOP_K_MAX, tp // 4)
    n_top_s = min(TOP_K_MAX, (past + ts) // 4)
    assert ts <= CHUNK and past % CHUNK == 0 and past % LANES == 0

    rope_p = _rope_tables_t(jnp.arange(tp, dtype=jnp.int32))
    reps = PROJ_TM // ts
    rope_s = tuple(jnp.tile(a, (reps, 1)) for a in _rope_tables(past + jnp.arange(ts, dtype=jnp.int32)))

    bias_p = _bias_tables(rel_bias, (0, -TQ, -3 * TQ), TQ, TQ, True)
    win_s = 2 * LANES
    bias_s = _bias_tables(rel_bias, (-(win_s - LANES), -(past + win_s)), ts, win_s, False)
    fg = final_norm_g.reshape(1, -1)

    xp = x_prompt
    xs = x_sample.reshape(bs * ts, d)
    outs_p, outs_s = [], []
    for l in range(depth):
        lw = _layer_weights(norm_g[l], w_in[l], mla_q_norm_g[l], mla_kv_norm_g[l],
                            mla_w_uq[l], mla_w_uk[l], mla_w_uv[l], w_out[l])
        final = l == depth - 1
        pr = _project_t(xp.reshape(bp * tp, d), rope_p, lw, bp, tp)
        olat, ob = _prompt_attention(pr, bias_p, bp, tp, n_top_p)
        xp = _combine_t(xp, olat, ob, pr, lw, fg, final)
        heads_t = lambda a: a.reshape(bp, B_KV_HEADS, B_HEAD_DIM, tp).transpose(0, 3, 1, 2)
        outs_p.append((pr["ckv"].reshape(bp, tp, A_KV_LORA), pr["kpeT"].transpose(0, 2, 1),
                       heads_t(pr["kbT"]), heads_t(pr["vbT"]), pr["kidxT"].transpose(0, 2, 1)))
        ps = _project(xs, rope_s, lw, period=PROJ_TM)
        feat_t = lambda a: a.transpose(0, 2, 3, 1).reshape(bs, B_KV_HEADS * B_HEAD_DIM, past)
        caches = (cache_mla_ckv[l], cache_mla_kpe[l].transpose(0, 2, 1), feat_t(cache_dsa_k[l]),
                  feat_t(cache_dsa_v[l]), cache_dsa_kidx[l].transpose(0, 2, 1))
        olat, ob = _sample_attention(ps, caches, bias_s, bs, ts, past, n_top_s)
        xs = _combine(xs, olat, ob, ps, lw, fg, final)
        outs_s.append((ps["ckv"].reshape(bs, ts, A_KV_LORA), ps["kpe"].reshape(bs, ts, A_ROPE),
                       ps["kb"].reshape(bs, ts, B_KV_HEADS, B_HEAD_DIM),
                       ps["vb"].reshape(bs, ts, B_KV_HEADS, B_HEAD_DIM),
                       ps["kidx"].reshape(bs, ts, IDX_DIM)))

    stack = lambda outs, i: jnp.stack([o[i] for o in outs])
    return ((xp, xs.reshape(bs, ts, d))
            + tuple(stack(outs_p, i) for i in range(5)) + tuple(stack(outs_s, i) for i in range(5)))
```

```python
import functools
import math

import jax
import jax.numpy as jnp
import numpy as np
from jax import lax
from jax.experimental import pallas as pl
from jax.experimental.pallas import tpu as pltpu

F32 = jnp.float32
BF16 = jnp.bfloat16
I32 = jnp.int32

CHUNK = 64
EPS = 1e-6
A_HEADS = 8
A_NOPE = 64
A_ROPE = 32
A_V = 64
A_Q_LORA = 256
A_KV_LORA = 128
ROPE_THETA = 10000.0
A_WIDTH = A_HEADS * A_V
B_HEADS = 8
B_KV_HEADS = 2
B_HEAD_DIM = 64
B_WIDTH = B_HEADS * B_HEAD_DIM
B_GROUP = B_HEADS // B_KV_HEADS
IDX_HEADS = 8
IDX_DIM = 64
TOP_K_MAX = 256
N_BUCKETS = 32
MAX_DISTANCE = 128

LANES = 128
VMEM_LIMIT_BYTES = 56 * 1024 * 1024

LOG2E = 1.4426950408889634
NEG_BIG = -1e30
INT_MIN = -(2 ** 31)

C_CQ = 0
C_CKV = C_CQ + A_Q_LORA
C_MISC = C_CKV + A_KV_LORA
C_GA = C_MISC + LANES
C_QB = C_GA + A_WIDTH
C_KB = C_QB + B_WIDTH
C_VB = C_KB + B_KV_HEADS * B_HEAD_DIM
C_QI = C_VB + B_KV_HEADS * B_HEAD_DIM
C_GB = C_QI + IDX_HEADS * IDX_DIM
IN_PAD = C_GB + B_WIDTH
M_KPE = 0
M_KIDX = A_ROPE
M_WIDX = A_ROPE + IDX_DIM

QCAT = 2 * LANES
TQ = 256
PROJ_TM = 256
COMBINE_TM = 1024
PROJ_T_TM = 1024


def _dot(a, b):
    return jnp.dot(a, b, preferred_element_type=F32)


def _dot_nt(a, b):
    return lax.dot_general(a, b, (((1,), (1,)), ((), ())), preferred_element_type=F32)


def _rms(x, g):
    return x * lax.rsqrt(jnp.mean(x * x, axis=-1, keepdims=True) + EPS) * g


def _bias_kernel(rb_ref, out_ref, *, offsets, keys_on_rows):
    nb = N_BUCKETS // 2
    max_exact = nb // 2
    n_r = out_ref.shape[1] if keys_on_rows else out_ref.shape[2]
    n_c = out_ref.shape[2] // B_HEADS if keys_on_rows else out_ref.shape[3]
    row = lax.broadcasted_iota(I32, (n_r, n_c), 0)
    col = lax.broadcasted_iota(I32, (n_r, n_c), 1)

    def bucket_of(off):
        rel = off + (row - col if keys_on_rows else col - row)
        ret = jnp.where(rel > 0, nb, 0)
        n = jnp.abs(rel)
        nf = jnp.maximum(n, 1).astype(F32)
        large = max_exact + (jnp.log(nf / max_exact) / math.log(MAX_DISTANCE / max_exact)
                             * (nb - max_exact)).astype(I32)
        large = jnp.minimum(large, nb - 1)
        return ret + jnp.where(n < max_exact, n, large)

    def lookup(bucket, h):
        table = jnp.broadcast_to(rb_ref[h:h + 1, :], (n_r, LANES))
        return jnp.concatenate(
            [jnp.take_along_axis(table, bucket[:, c:c + LANES], axis=1) for c in range(0, n_c, LANES)], axis=1)

    buckets = [bucket_of(off) for off in offsets]
    for h in range(B_HEADS):
        if keys_on_rows:
            far = lookup(buckets[-1], h)
            for p in range(len(offsets) - 1):
                out_ref[p, :, h * n_c:(h + 1) * n_c] = (lookup(buckets[p], h) - far) * LOG2E
        else:
            for p in range(len(offsets)):
                out_ref[p, h] = lookup(buckets[p], h) * LOG2E


def _bias_tables(rel_bias, offsets, n_r, n_c, keys_on_rows):
    shape = (len(offsets) - 1, n_r, B_HEADS * n_c) if keys_on_rows else (len(offsets), B_HEADS, n_r, n_c)
    return pl.pallas_call(
        functools.partial(_bias_kernel, offsets=tuple(offsets), keys_on_rows=keys_on_rows),
        out_shape=jax.ShapeDtypeStruct(shape, F32),
        in_specs=[pl.BlockSpec(memory_space=pltpu.VMEM)],
        out_specs=pl.BlockSpec(memory_space=pltpu.VMEM),
        compiler_params=pltpu.CompilerParams(vmem_limit_bytes=VMEM_LIMIT_BYTES),
        name="bias_tables",
    )(jnp.pad(rel_bias.T, ((0, 0), (0, LANES - N_BUCKETS))))


def _proj_kernel(x_ref, ng_ref, win_ref, qng_ref, kvng_ref, wq_ref, wuk_ref, pmat_ref,
                 cosq_ref, sinq_ref, rc_ref, rs1_ref, rs2_ref,
                 ckv_ref, kpe_ref, kb_ref, vb_ref, kidx_ref,
                 kcat_ref, kb16_ref, vb16_ref, kidx16_ref,
                 qcat_ref, qb_ref, qidx_ref, widx_ref, sga_ref, sgb_ref):
    x = x_ref[...]
    h = _rms(x, ng_ref[...])
    z = _dot_nt(h.astype(BF16), win_ref[...])

    cq = _rms(z[:, C_CQ:C_CQ + A_Q_LORA], qng_ref[...])
    q = _dot(cq.astype(BF16), wq_ref[...])
    n_nope = A_HEADS * A_NOPE
    x1 = q[:, n_nope:n_nope + LANES]
    x2 = q[:, n_nope + LANES:n_nope + 2 * LANES]
    cos8, sin8 = cosq_ref[...], sinq_ref[...]
    o1 = x1 * cos8 - x2 * sin8
    o2 = x1 * sin8 + x2 * cos8
    mla_scale = (A_NOPE + A_ROPE) ** -0.5 * LOG2E
    q_lat = _dot(q[:, :n_nope].astype(BF16), wuk_ref[...]) * mla_scale
    pe = jnp.concatenate([o1, o2], axis=1) * mla_scale
    q_pe = _dot(pe.astype(BF16), pmat_ref[...])
    for hh in range(A_HEADS):
        qcat_ref[:, hh * QCAT:hh * QCAT + LANES] = q_lat[:, hh * LANES:(hh + 1) * LANES].astype(BF16)
        qcat_ref[:, hh * QCAT + LANES:(hh + 1) * QCAT] = q_pe[:, hh * LANES:(hh + 1) * LANES].astype(BF16)

    ckv = _rms(z[:, C_CKV:C_CKV + A_KV_LORA], kvng_ref[...])
    ckv_ref[...] = ckv
    misc = z[:, C_MISC:C_MISC + LANES]
    rot = (misc * rc_ref[...] + pltpu.roll(misc, A_ROPE // 2, 1) * rs1_ref[...]
           + pltpu.roll(misc, LANES - A_ROPE // 2, 1) * rs2_ref[...])
    kpe_ref[...] = rot[:, :A_ROPE]
    kcat_ref[:, :LANES] = ckv.astype(BF16)
    kcat_ref[:, LANES:] = rot.astype(BF16)

    kidx = misc[:, M_KIDX:M_KIDX + IDX_DIM]
    kidx_ref[...] = kidx
    kidx16_ref[...] = kidx.astype(BF16)
    widx_ref[...] = misc[:, M_WIDX:M_WIDX + IDX_HEADS] * (IDX_HEADS ** -0.5)
    kb = z[:, C_KB:C_KB + LANES]
    vb = z[:, C_VB:C_VB + LANES]
    kb_ref[...] = kb
    vb_ref[...] = vb
    kb16_ref[...] = kb.astype(BF16)
    vb16_ref[...] = vb.astype(BF16)
    qb_ref[...] = (z[:, C_QB:C_QB + B_WIDTH] * (B_HEAD_DIM ** -0.5 * LOG2E)).astype(BF16)
    qidx_ref[...] = (z[:, C_QI:C_QI + IDX_HEADS * IDX_DIM] * (IDX_DIM ** -0.5)).astype(BF16)
    sga_ref[...] = jax.nn.silu(z[:, C_GA:C_GA + A_WIDTH])
    sgb_ref[...] = jax.nn.silu(z[:, C_GB:C_GB + B_WIDTH])


def _project(x2d, rope_tabs, lw, *, period):
    n, d = x2d.shape
    tm = PROJ_TM
    assert n % tm == 0 and period % tm == 0
    n_rep = period // tm
    tok = lambda c: pl.BlockSpec((tm, c), lambda i: (i, 0))
    full = lambda a: pl.BlockSpec(a.shape, lambda i: (0,) * a.ndim)
    tab = pl.BlockSpec((tm, LANES), lambda i: (i % n_rep, 0))
    outs = [
        ("ckv", A_KV_LORA, F32), ("kpe", A_ROPE, F32), ("kb", LANES, F32), ("vb", LANES, F32),
        ("kidx", IDX_DIM, F32),
        ("kcat", QCAT, BF16), ("kb16", LANES, BF16), ("vb16", LANES, BF16), ("kidx16", IDX_DIM, BF16),
        ("qcat", A_HEADS * QCAT, BF16), ("qb", B_WIDTH, BF16), ("qidx", IDX_HEADS * IDX_DIM, BF16),
        ("widx", IDX_HEADS, F32), ("sga", A_WIDTH, F32), ("sgb", B_WIDTH, F32),
    ]
    res = pl.pallas_call(
        _proj_kernel,
        grid=(n // tm,),
        in_specs=[tok(d), full(lw["ng"]), full(lw["win"]), full(lw["qng"]), full(lw["kvng"]),
                  full(lw["wq"]), full(lw["wuk"]), full(lw["pmat"]), tab, tab, tab, tab, tab],
        out_specs=[tok(c) for _, c, _ in outs],
        out_shape=[jax.ShapeDtypeStruct((n, c), dt) for _, c, dt in outs],
        compiler_params=pltpu.CompilerParams(dimension_semantics=("arbitrary",),
                                             vmem_limit_bytes=VMEM_LIMIT_BYTES),
        name="project",
    )(x2d, lw["ng"], lw["win"], lw["qng"], lw["kvng"], lw["wq"], lw["wuk"], lw["pmat"], *rope_tabs)
    return {name: r for (name, _, _), r in zip(outs, res)}


R_CQ = 0
R_CKV = R_CQ + A_Q_LORA
R_VB = R_CKV + A_KV_LORA
R_QB = R_VB + LANES
R_QI = R_QB + B_WIDTH
R_GA = R_QI + IDX_HEADS * IDX_DIM
R_GB = R_GA + A_WIDTH
R_WI = R_GB + B_WIDTH
R_KB = R_WI + 16
R_KI = R_KB + LANES
R_KPE = R_KI + IDX_DIM
ROWS_T = R_KPE + A_ROPE
K_MISC = A_KV_LORA
K_KB = K_MISC + LANES
COLS_K = K_KB + LANES
MK_KPE = IDX_DIM
ONES_ROWS = 16
V_EXT = A_KV_LORA + ONES_ROWS


def _proj_t_kernel(x_ref, ng_ref, wk_ref, wt_ref, qng_ref, kvngc_ref, kvng_ref, wqt_ref, wukt_ref,
                   cost_ref, sint_ref, rc_ref, rs1_ref, rs2_ref,
                   ckv_ref, kcat_ref, kb16_ref, kidx16_ref,
                   kpeT_ref, kbT_ref, vbT_ref, kidxT_ref,
                   qcatT_ref, qbT_ref, qidxT_ref, widxT_ref, vmlaT_ref, vdsaT_ref, sgaT_ref, sgbT_ref):
    x = x_ref[...]
    tm = x.shape[0]
    hb = _rms(x, ng_ref[...]).astype(BF16)

    zk = _dot_nt(hb, wk_ref[...])
    ckv = _rms(zk[:, :A_KV_LORA], kvng_ref[...])
    ckv_ref[...] = ckv
    misc = zk[:, K_MISC:K_MISC + LANES]
    lane = lax.broadcasted_iota(I32, (tm, LANES), 1)
    kidx16_ref[...] = jnp.where(lane < IDX_DIM, misc, 0.0).astype(BF16)
    rot = (misc * rc_ref[...] + pltpu.roll(misc, A_ROPE // 2, 1) * rs1_ref[...]
           + pltpu.roll(misc, LANES - A_ROPE // 2, 1) * rs2_ref[...])
    kcat_ref[:, :LANES] = ckv.astype(BF16)
    kcat_ref[:, LANES:] = pltpu.roll(rot, LANES - MK_KPE, 1).astype(BF16)
    kb16_ref[...] = zk[:, K_KB:K_KB + LANES].astype(BF16)

    zt = _dot_nt(wt_ref[...], hb)

    def rms_t(c, g):
        return c * lax.rsqrt(jnp.mean(c * c, axis=0, keepdims=True) + EPS) * g

    cq = rms_t(zt[R_CQ:R_CQ + A_Q_LORA], qng_ref[...])
    qt = _dot(wqt_ref[...], cq.astype(BF16))
    n_nope = A_HEADS * A_NOPE
    x1 = qt[n_nope:n_nope + LANES]
    x2 = qt[n_nope + LANES:n_nope + 2 * LANES]
    cos8, sin8 = cost_ref[...], sint_ref[...]
    mla_scale = (A_NOPE + A_ROPE) ** -0.5 * LOG2E
    o1 = (x1 * cos8 - x2 * sin8) * mla_scale
    o2 = (x1 * sin8 + x2 * cos8) * mla_scale
    q_lat = _dot(wukt_ref[...], qt[:n_nope].astype(BF16)) * mla_scale
    half = A_ROPE // 2
    for h in range(A_HEADS):
        qcatT_ref[h, :LANES, :] = q_lat[h * LANES:(h + 1) * LANES].astype(BF16)
        qcatT_ref[h, LANES:LANES + half, :] = o1[h * half:(h + 1) * half].astype(BF16)
        qcatT_ref[h, LANES + half:LANES + A_ROPE, :] = o2[h * half:(h + 1) * half].astype(BF16)
        qcatT_ref[h, LANES + A_ROPE:, :] = jnp.zeros((QCAT - LANES - A_ROPE, tm), BF16)
    ones = jnp.ones((ONES_ROWS, TQ), BF16)
    ckv_t = rms_t(zt[R_CKV:R_CKV + A_KV_LORA], kvngc_ref[...]).astype(BF16)
    vb_t = zt[R_VB:R_VB + LANES]
    for c in range(tm // TQ):
        blk = slice(c * TQ, (c + 1) * TQ)
        vmlaT_ref[c, :A_KV_LORA, :] = ckv_t[:, blk]
        vmlaT_ref[c, A_KV_LORA:, :] = ones
        vdsaT_ref[c, :LANES, :] = vb_t[:, blk].astype(BF16)
        vdsaT_ref[c, LANES:, :] = ones
    vbT_ref[...] = vb_t
    kbT_ref[...] = zt[R_KB:R_KB + LANES]
    kidxT_ref[...] = zt[R_KI:R_KI + IDX_DIM]
    k1, k2 = zt[R_KPE:R_KPE + half], zt[R_KPE + half:R_KPE + A_ROPE]
    cos1, sin1 = cos8[:half], sin8[:half]
    kpeT_ref[:half, :] = k1 * cos1 - k2 * sin1
    kpeT_ref[half:, :] = k1 * sin1 + k2 * cos1
    qbT_ref[...] = (zt[R_QB:R_QB + B_WIDTH] * (B_HEAD_DIM ** -0.5 * LOG2E)).astype(BF16)
    qidxT_ref[...] = (zt[R_QI:R_QI + IDX_HEADS * IDX_DIM] * (IDX_DIM ** -0.5)).astype(BF16)
    sgaT_ref[...] = jax.nn.silu(zt[R_GA:R_GA + A_WIDTH]).astype(BF16)
    sgbT_ref[...] = jax.nn.silu(zt[R_GB:R_GB + B_WIDTH]).astype(BF16)
    widxT_ref[...] = zt[R_WI:R_WI + IDX_HEADS] * (IDX_HEADS ** -0.5)


def _project_t(x2d, tabs, lw, b, t):
    n, d = x2d.shape
    tm = PROJ_T_TM
    assert t % tm == 0 and tm % TQ == 0
    nt = t // tm
    tok = lambda c: pl.BlockSpec((tm, c), lambda i: (i, 0))
    full = lambda a: pl.BlockSpec(a.shape, lambda i: (0,) * a.ndim)
    vblk = pl.BlockSpec((None, tm // TQ, V_EXT, TQ), lambda i: (i // nt, i % nt, 0, 0))
    tab_t = pl.BlockSpec((LANES, tm), lambda i: (0, i % nt))
    tab_k = pl.BlockSpec((tm, LANES), lambda i: (i % nt, 0))
    rows_t = lambda r: pl.BlockSpec((None, r, tm), lambda i: (i // nt, 0, i % nt))
    outs = [
        ("ckv", (n, A_KV_LORA), F32, tok(A_KV_LORA)),
        ("kcat", (n, QCAT), BF16, tok(QCAT)), ("kb16", (n, LANES), BF16, tok(LANES)),
        ("kidx16", (n, LANES), BF16, tok(LANES)),
        ("kpeT", (b, A_ROPE, t), F32, rows_t(A_ROPE)), ("kbT", (b, LANES, t), F32, rows_t(LANES)),
        ("vbT", (b, LANES, t), F32, rows_t(LANES)), ("kidxT", (b, IDX_DIM, t), F32, rows_t(IDX_DIM)),
        ("qcatT", (b, A_HEADS, QCAT, t), BF16,
         pl.BlockSpec((None, A_HEADS, QCAT, tm), lambda i: (i // nt, 0, 0, i % nt))),
        ("qbT", (b, B_WIDTH, t), BF16, rows_t(B_WIDTH)),
        ("qidxT", (b, IDX_HEADS * IDX_DIM, t), BF16, rows_t(IDX_HEADS * IDX_DIM)),
        ("widxT", (b, IDX_HEADS, t), F32, rows_t(IDX_HEADS)),
        ("vmlaT", (b, t // TQ, V_EXT, TQ), BF16, vblk), ("vdsaT", (b, t // TQ, V_EXT, TQ), BF16, vblk),
        ("sgaT", (b, A_WIDTH, t), BF16, rows_t(A_WIDTH)), ("sgbT", (b, B_WIDTH, t), BF16, rows_t(B_WIDTH)),
    ]
    cost, sint, rc, rs1, rs2 = tabs
    res = pl.pallas_call(
        _proj_t_kernel,
        grid=(n // tm,),
        in_specs=[tok(d), full(lw["ng"]), full(lw["wk"]), full(lw["wt"]), full(lw["qngc"]), full(lw["kvngc"]),
                  full(lw["kvng"]), full(lw["wqt"]), full(lw["wukt"]), tab_t, tab_t, tab_k, tab_k, tab_k],
        out_specs=[o[3] for o in outs],
        out_shape=[jax.ShapeDtypeStruct(o[1], o[2]) for o in outs],
        compiler_params=pltpu.CompilerParams(dimension_semantics=("arbitrary",),
                                             vmem_limit_bytes=VMEM_LIMIT_BYTES),
        name="project_prompt",
    )(x2d, lw["ng"], lw["wk"], lw["wt"], lw["qngc"], lw["kvngc"], lw["kvng"], lw["wqt"], lw["wukt"],
      cost, sint, rc, rs1, rs2)
    return {o[0]: r for o, r in zip(outs, res)}


NEG_FLT_MAX = -3.4028234663852886e38
KEY_NEG_FLT_MAX = INT_MIN + (1 << 23)


def _key_to_float(k):
    k = jnp.maximum(k, KEY_NEG_FLT_MAX)
    return pltpu.bitcast(k ^ ((k >> 31) & 0x7FFFFFFF), F32)


def _count(pred):
    return jnp.sum(jnp.where(pred, 1.0, 0.0), axis=1, keepdims=True)


SAMPLE_STREAMS = 2
MAX_TIE_SWEEPS = 8.0
SAFE_DENOM_MIN = 2.0 ** -90
SAFE_DENOM_MAX = 2.0 ** 40


def _fold8(x, op=jnp.add):
    parts = [x[i:i + 8] for i in range(0, x.shape[0], 8)]
    while len(parts) > 1:
        parts = [op(a, b) for a, b in zip(parts[::2], parts[1::2])]
    return parts[0]


def _prompt_attn_kernel(qcatT_ref, qbT_ref, qidxT_ref, widxT_ref, kcat_ref, kb_ref, kidx_ref,
                        vmlaT_ref, vdsaT_ref, bias_ref, olatT_ref, obT_ref,
                        qa_ref, qbp_ref, qip_ref, sc_ref, m_ref, acc_ref, mb_ref, accb_ref, kn_ref, *, n_top):
    qi = pl.program_id(1)
    nblk = qi + 1
    tq = TQ
    krow = lax.broadcasted_iota(I32, (tq, tq), 0)
    qcol = lax.broadcasted_iota(I32, (tq, tq), 1)
    shift = CHUNK.bit_length() - 1
    diag_ok = (qcol >> shift) >= (krow >> shift)
    hcols = lambda h: slice(h * tq, (h + 1) * tq)

    def per_head(fn):
        return jnp.concatenate([fn(h) for h in range(A_HEADS)], axis=1)

    qbp_ref[...] = jnp.zeros(qbp_ref.shape, BF16)
    qip_ref[...] = jnp.zeros(qip_ref.shape, BF16)
    for h in range(A_HEADS):
        g = h // B_GROUP
        qa_ref[:, hcols(h)] = qcatT_ref[h]
        qbp_ref[g * B_HEAD_DIM:(g + 1) * B_HEAD_DIM, hcols(h)] = qbT_ref[h * B_HEAD_DIM:(h + 1) * B_HEAD_DIM, :]
        qip_ref[:IDX_DIM, hcols(h)] = qidxT_ref[h * IDX_DIM:(h + 1) * IDX_DIM, :]
    w_all = per_head(lambda h: widxT_ref[h:h + 1, :])

    @pl.when(qi == 0)
    def _():
        def max_row_norm2(k_ref):
            k = k_ref[...].astype(F32)
            return jnp.max(jnp.sum(k * k, axis=1, keepdims=True))
        kn_ref[0:1, :] = jnp.full((1, LANES), max_row_norm2(kcat_ref), F32)
        kn_ref[1:2, :] = jnp.full((1, LANES), max_row_norm2(kb_ref), F32)
        kn_ref[2:3, :] = jnp.full((1, LANES), jnp.max(bias_ref[...]), F32)

    def col_norm(q_ref):
        q = q_ref[...].astype(F32)
        return jnp.sqrt(jnp.sum(q * q, axis=0, keepdims=True))

    shift_a = jnp.sqrt(kn_ref[0:1, 0:1]) * col_norm(qa_ref)
    shift_b = jnp.sqrt(kn_ref[1:2, 0:1]) * col_norm(qbp_ref) + jnp.maximum(kn_ref[2:3, 0:1], 0.0)

    def accumulate(s_t, values, shift_or_m, acc_r, exact):
        if exact:
            m_prev = shift_or_m[0:1, :]
            m_new = jnp.maximum(m_prev, jnp.max(s_t, axis=0, keepdims=True))
            alpha = jnp.exp2(m_prev - m_new)
            shift_or_m[0:1, :] = m_new
        else:
            m_new = shift_or_m
        p_t = jnp.exp2(s_t - m_new).astype(BF16)
        for v_t, lanes in values:
            pv = _dot(v_t, p_t[:, lanes])
            acc_r[:, lanes] = (alpha[:, lanes] * acc_r[:, lanes] if exact else acc_r[:, lanes]) + pv

    def unsafe(l):
        return jnp.max(jnp.where((l >= SAFE_DENOM_MIN) & (l <= SAFE_DENOM_MAX), 0.0, 1.0)) > 0.0

    all_lanes = slice(0, A_HEADS * tq)

    def for_blocks(n, block):
        def pair(p, c):
            block(2 * p)
            block(2 * p + 1)
            return c

        lax.fori_loop(0, lax.shift_right_logical(n, 1), pair, 0)

        @pl.when((n & 1) == 1)
        def _():
            block(n - 1)

    def mla_pass(exact):
        acc_ref[...] = jnp.zeros(acc_ref.shape, F32)
        if exact:
            m_ref[...] = jnp.full(m_ref.shape, NEG_BIG, F32)

        def block(j, masked):
            start = pl.multiple_of(j * tq, tq)
            s_t = _dot(kcat_ref[pl.ds(start, tq), :], qa_ref[...])
            if masked:
                s_t = per_head(lambda h: jnp.where(diag_ok, s_t[:, hcols(h)], NEG_BIG))
            accumulate(s_t, [(vmlaT_ref[j], all_lanes)], m_ref if exact else shift_a, acc_ref, exact)
            if exact:
                return
            r = jnp.maximum(_dot(kidx_ref[pl.ds(start, tq), :], qip_ref[...]), 0.0) * w_all
            score = r[:, hcols(0)]
            for h in range(1, IDX_HEADS):
                score = score + r[:, hcols(h)]
            if masked:
                score = jnp.where(diag_ok, score, -jnp.inf)
            sc_ref[j] = score

        for_blocks(qi, lambda j: block(j, False))
        block(qi, True)

    mla_pass(False)

    @pl.when(unsafe(acc_ref[A_KV_LORA:A_KV_LORA + 1, :]))
    def _():
        mla_pass(True)

    o_t = acc_ref[:A_KV_LORA, :] * (1.0 / acc_ref[A_KV_LORA:A_KV_LORA + 1, :])
    for h in range(A_HEADS):
        olatT_ref[h * LANES:(h + 1) * LANES, :] = o_t[:, hcols(h)].astype(olatT_ref.dtype)

    def count(pred):
        def one(j, c):
            return c + _fold8(jnp.where(pred(sc_ref[j], j), 1.0, 0.0))
        part = lax.fori_loop(0, lax.shift_right_logical(nblk, 1), lambda p, c: one(2 * p + 1, one(2 * p, c)),
                             jnp.zeros((8, tq), F32))
        part = lax.cond((nblk & 1) == 1, lambda c: one(nblk - 1, c), lambda c: c, part)
        return jnp.sum(part, axis=0, keepdims=True)

    kf = float(n_top)

    def bis_body(it, carry):
        lo, cnt_lo = carry
        cand = lo + lax.shift_left(jnp.int32(1), 31 - it)
        cand_f = _key_to_float(cand)
        cnt = count(lambda s, j: s >= cand_f)
        take = cnt >= kf
        return jnp.where(take, cand, lo), jnp.where(take, cnt, cnt_lo)

    n_steps = jnp.where(nblk * tq <= n_top, 0, 32)
    lo, cnt_ge = lax.fori_loop(
        0, n_steps, bis_body, (jnp.full((1, tq), INT_MIN, I32), jnp.full((1, tq), 1e9, F32)))
    few = lo == INT_MIN
    thr = _key_to_float(lo)
    excess0 = jnp.where(few, 0.0, cnt_ge - kf)
    max_excess = jnp.max(excess0)

    def drop_from(cut):
        def body(j, c):
            s = sc_ref[j]
            sc_ref[j] = jnp.where((s == thr) & ((krow + j * tq) >= cut), -jnp.inf, s)
            return c
        lax.fori_loop(0, nblk, body, 0)

    @pl.when((max_excess > 0.0) & (max_excess <= MAX_TIE_SWEEPS))
    def _():
        def last_tie_below(cut):
            def body(j, m):
                pos = krow + j * tq
                hit = jnp.where((sc_ref[j] == thr) & (pos < cut), pos, -1)
                return jnp.maximum(m, _fold8(hit, jnp.maximum))
            part = lax.fori_loop(0, nblk, body, jnp.full((8, tq), -1, I32))
            return jnp.max(part, axis=0, keepdims=True)

        def sweep(c):
            excess, cut = c
            last = last_tie_below(cut)
            live = excess > 0.0
            return jnp.where(live, excess - 1.0, excess), jnp.where(live, last, cut)

        _, cut = lax.while_loop(lambda c: jnp.max(c[0]) > 0.0, sweep,
                                (excess0, jnp.full((1, tq), 2 ** 30, I32)))
        drop_from(cut)

    @pl.when(max_excess > MAX_TIE_SWEEPS)
    def _():
        need = kf - count(lambda s, j: s > thr)
        n_bits = (sc_ref.shape[0] * tq).bit_length()

        def cut_body(it, cpos):
            cand = cpos + lax.shift_left(jnp.int32(1), n_bits - 1 - it)
            cnt = count(lambda s, j: (s == thr) & ((krow + j * tq) < cand))
            return jnp.where(cnt < need, cand, cpos)

        keep = lax.fori_loop(0, n_bits, cut_body, jnp.zeros((1, tq), I32))
        drop_from(jnp.where(excess0 > 0.0, keep + 1, 2 ** 30))

    group_lanes = [slice(g * B_GROUP * tq, (g + 1) * B_GROUP * tq) for g in range(B_KV_HEADS)]

    def dsa_pass(exact):
        accb_ref[...] = jnp.zeros(accb_ref.shape, F32)
        if exact:
            mb_ref[...] = jnp.full(mb_ref.shape, NEG_BIG, F32)

        def block(j, near):
            start = pl.multiple_of(j * tq, tq)
            sel = sc_ref[j] >= thr
            s_t = _dot(kb_ref[pl.ds(start, tq), :], qbp_ref[...])
            if near:
                s_t = s_t + bias_ref[qi - j]
            s_t = per_head(lambda h: jnp.where(sel, s_t[:, hcols(h)], NEG_BIG))
            v_all = vdsaT_ref[j]
            values = [(jnp.concatenate([v_all[g * B_HEAD_DIM:(g + 1) * B_HEAD_DIM], v_all[LANES:]], axis=0),
                       group_lanes[g]) for g in range(B_KV_HEADS)]
            accumulate(s_t, values, mb_ref if exact else shift_b, accb_ref, exact)

        for_blocks(jnp.maximum(qi - 1, 0), lambda j: block(j, False))

        @pl.when(qi >= 1)
        def _():
            block(qi - 1, True)

        block(qi, True)

    dsa_pass(False)

    @pl.when(unsafe(accb_ref[B_HEAD_DIM:B_HEAD_DIM + 1, :]))
    def _():
        dsa_pass(True)

    inv_b = 1.0 / accb_ref[B_HEAD_DIM:B_HEAD_DIM + 1, :]
    for h in range(B_HEADS):
        obT_ref[h * B_HEAD_DIM:(h + 1) * B_HEAD_DIM, :] = (
            accb_ref[:B_HEAD_DIM, hcols(h)] * inv_b[:, hcols(h)]).astype(obT_ref.dtype)


def _prompt_attention(pr, bias_p, b, t, n_top):
    tq = TQ
    assert t % tq == 0
    nq = t // tq
    r3 = lambda a: a.reshape(b, t, a.shape[-1])
    qrows = lambda r: pl.BlockSpec((None, r, tq), lambda bi, qi: (bi, 0, qi))
    kall = lambda c: pl.BlockSpec((None, t, c), lambda bi, qi: (bi, 0, 0))
    vall = pl.BlockSpec((None, nq, V_EXT, tq), lambda bi, qi: (bi, 0, 0, 0))
    olat, ob = pl.pallas_call(
        functools.partial(_prompt_attn_kernel, n_top=n_top),
        grid=(b, nq),
        in_specs=[pl.BlockSpec((None, A_HEADS, QCAT, tq), lambda bi, qi: (bi, 0, 0, qi)),
                  qrows(B_WIDTH), qrows(IDX_HEADS * IDX_DIM), qrows(IDX_HEADS),
                  kall(QCAT), kall(LANES), kall(LANES), vall, vall,
                  pl.BlockSpec(bias_p.shape, lambda bi, qi: (0, 0, 0))],
        out_specs=[qrows(A_HEADS * A_KV_LORA), qrows(B_WIDTH)],
        out_shape=[jax.ShapeDtypeStruct((b, A_HEADS * A_KV_LORA, t), BF16),
                   jax.ShapeDtypeStruct((b, B_WIDTH, t), BF16)],
        scratch_shapes=[
            pltpu.VMEM((QCAT, A_HEADS * tq), BF16),
            pltpu.VMEM((LANES, B_HEADS * tq), BF16),
            pltpu.VMEM((LANES, IDX_HEADS * tq), BF16),
            pltpu.VMEM((nq, tq, tq), F32),
            pltpu.VMEM((8, A_HEADS * tq), F32),
            pltpu.VMEM((V_EXT, A_HEADS * tq), F32),
            pltpu.VMEM((8, B_HEADS * tq), F32),
            pltpu.VMEM((B_HEAD_DIM + ONES_ROWS, B_HEADS * tq), F32),
            pltpu.VMEM((8, LANES), F32),
        ],
        compiler_params=pltpu.CompilerParams(dimension_semantics=("arbitrary", "arbitrary"),
                                             vmem_limit_bytes=VMEM_LIMIT_BYTES),
        name="prompt_attention",
    )(pr["qcatT"], pr["qbT"], pr["qidxT"], pr["widxT"],
      r3(pr["kcat"]), r3(pr["kb16"]), r3(pr["kidx16"]), pr["vmlaT"], pr["vdsaT"], bias_p)
    return olat, ob


def _sample_attn_kernel(qcat_ref, qb_ref, qidx_ref, widx_ref, kcatn_ref, kbn_ref, vbn_ref, kidxn_ref,
                        cckv_ref, ckpeT_ref, ckT_ref, cvT_ref, ckidxT_ref, bias_ref,
                        olat_ref, ob_ref, *, n_top, t_new, past):
    tq = t_new
    pad = LANES
    n_keys = past + pad
    kf = float(n_top)
    n_bits = n_keys.bit_length()
    n_far = n_keys - bias_ref.shape[-1]

    def padrows(a):
        return jnp.concatenate([a, jnp.zeros((pad - t_new, a.shape[1]), a.dtype)], axis=0)

    def softmax(s):
        m = jnp.max(s, axis=1, keepdims=True)
        p = jnp.exp2(s - m)
        return p.astype(BF16), jnp.sum(p, axis=1, keepdims=True)

    def new_cols(rows):
        return lax.broadcasted_iota(I32, (rows, pad), 1) < t_new

    def mla_and_scores(i):
        ckv_c = cckv_ref[i].astype(BF16)
        kpe_t = ckpeT_ref[i].astype(BF16)
        kcat_n = padrows(kcatn_ref[i])
        qs = jnp.concatenate([qcat_ref[i, :, h * QCAT:(h + 1) * QCAT] for h in range(A_HEADS)], axis=0)
        s_c = _dot_nt(qs[:, :A_KV_LORA], ckv_c) + _dot(qs[:, A_KV_LORA:A_KV_LORA + A_ROPE], kpe_t)
        s_n = jnp.where(new_cols(A_HEADS * tq), _dot_nt(qs, kcat_n), NEG_BIG)
        pb, l = softmax(jnp.concatenate([s_c, s_n], axis=1))
        o = (_dot(pb[:, :past], ckv_c) + _dot(pb[:, past:], kcat_n[:, :A_KV_LORA])) / l
        for h in range(A_HEADS):
            olat_ref[i, :, h * LANES:(h + 1) * LANES] = o[h * tq:(h + 1) * tq].astype(olat_ref.dtype)

        kidx_t = ckidxT_ref[i].astype(BF16)
        kidx_n = padrows(kidxn_ref[i])
        qis = jnp.concatenate(
            [qidx_ref[i, :, h * IDX_DIM:(h + 1) * IDX_DIM] for h in range(IDX_HEADS)], axis=0)

        def head_sum(dots):
            acc = jnp.maximum(dots[:tq], 0.0) * widx_ref[i, :, 0:1]
            for h in range(1, IDX_HEADS):
                acc = acc + jnp.maximum(dots[h * tq:(h + 1) * tq], 0.0) * widx_ref[i, :, h:h + 1]
            return acc

        return jnp.concatenate(
            [head_sum(_dot(qis, kidx_t)),
             jnp.where(new_cols(tq), head_sum(_dot_nt(qis, kidx_n)), -jnp.inf)], axis=1)

    def search_step(it, score, lo, cnt_lo):
        cand = lo + jnp.int32(INT_MIN if it == 0 else 1 << (31 - it))
        cnt = _count(score >= _key_to_float(cand))
        take = cnt >= kf
        return jnp.where(take, cand, lo), jnp.where(take, cnt, cnt_lo)

    def attend_selected(i, score, lo, cnt_ge):
        few = lo == INT_MIN
        thr = _key_to_float(lo)
        has_tie = jnp.max(jnp.where((cnt_ge > kf) & (~few), 1.0, 0.0))

        def drop_ties():
            cols = lax.broadcasted_iota(I32, (tq, n_keys), 1)
            need = kf - _count(score > thr)
            eq = score == thr

            def cut_body(it, cpos):
                cand = cpos + lax.shift_left(jnp.int32(1), n_bits - 1 - it)
                cnt = _count(eq & (cols < cand))
                return jnp.where(cnt < need, cand, cpos)

            keep = lax.fori_loop(0, n_bits, cut_body, jnp.zeros((tq, 1), I32))
            return jnp.where(eq & (cols > keep) & (cnt_ge > kf) & (~few), -jnp.inf, score)

        kept = lax.cond(has_tie > 0.0, drop_ties, lambda: score)
        sel_g = jnp.concatenate([kept] * B_GROUP, axis=0) >= jnp.concatenate([thr] * B_GROUP, axis=0)

        k_t = ckT_ref[i].astype(BF16)
        v_t = cvT_ref[i].astype(BF16)
        k_n = padrows(kbn_ref[i])
        v_n = padrows(vbn_ref[i])
        for g in range(B_KV_HEADS):
            feats = slice(g * B_HEAD_DIM, (g + 1) * B_HEAD_DIM)
            heads = range(g * B_GROUP, (g + 1) * B_GROUP)
            qg = jnp.concatenate([qb_ref[i, :, h * B_HEAD_DIM:(h + 1) * B_HEAD_DIM] for h in heads], axis=0)
            near = jnp.concatenate([bias_ref[0, h] - bias_ref[1, h] for h in heads], axis=0)
            sg = _dot(qg, k_t[feats])
            sg = jnp.concatenate([sg[:, :n_far], sg[:, n_far:] + near[:, :past - n_far],
                                  _dot_nt(qg, k_n[:, feats]) + near[:, past - n_far:]], axis=1)
            pb, l = softmax(jnp.where(sel_g, sg, NEG_BIG))
            og = (_dot_nt(pb[:, :past], v_t[feats]) + _dot(pb[:, past:], v_n[:, feats])) / l
            for hh, h in enumerate(heads):
                ob_ref[i, :, h * B_HEAD_DIM:(h + 1) * B_HEAD_DIM] = og[hh * tq:(hh + 1) * tq]

    streams = range(SAMPLE_STREAMS)
    scores = [mla_and_scores(i) for i in streams]
    state = [(jnp.full((tq, 1), INT_MIN, I32), jnp.full((tq, 1), 1e9, F32)) for _ in streams]
    for it in range(32):
        state = [search_step(it, scores[i], *state[i]) for i in streams]
    for i in streams:
        attend_selected(i, scores[i], *state[i])


def _sample_attention(pr, caches, bias_s, b, t_new, past, n_top):
    r3 = lambda a: a.reshape(b, t_new, a.shape[-1])
    ns = SAMPLE_STREAMS
    assert b % ns == 0
    per_b = lambda n, c: pl.BlockSpec((ns, n, c), lambda bi: (bi, 0, 0))
    news = [pr["qcat"], pr["qb"], pr["qidx"], pr["widx"], pr["kcat"], pr["kb16"], pr["vb16"], pr["kidx16"]]
    olat, ob = pl.pallas_call(
        functools.partial(_sample_attn_kernel, n_top=n_top, t_new=t_new, past=past),
        grid=(b // ns,),
        in_specs=[per_b(t_new, a.shape[-1]) for a in news]
                 + [per_b(c.shape[1], c.shape[2]) for c in caches]
                 + [pl.BlockSpec(bias_s.shape, lambda bi: (0, 0, 0, 0))],
        out_specs=[per_b(t_new, A_HEADS * A_KV_LORA), per_b(t_new, B_WIDTH)],
        out_shape=[jax.ShapeDtypeStruct((b, t_new, A_HEADS * A_KV_LORA), BF16),
                   jax.ShapeDtypeStruct((b, t_new, B_WIDTH), F32)],
        compiler_params=pltpu.CompilerParams(dimension_semantics=("arbitrary",),
                                             vmem_limit_bytes=VMEM_LIMIT_BYTES),
        name="sample_attention",
    )(*[r3(a) for a in news], *caches, bias_s)
    return olat.reshape(b * t_new, -1), ob.reshape(b * t_new, -1)


def _combine_kernel(x_ref, olat_ref, ob_ref, sga_ref, sgb_ref, wuv_ref, wout_ref, fg_ref, y_ref, *, final):
    o_a = _dot(olat_ref[...], wuv_ref[...])
    mix = jnp.concatenate([o_a * sga_ref[...], ob_ref[...] * sgb_ref[...]], axis=1)
    y = x_ref[...] + _dot(mix.astype(BF16), wout_ref[...])
    if final:
        y = _rms(y, fg_ref[...])
    y_ref[...] = y


def _combine(x2d, olat, ob, pr, lw, fg, final):
    n, d = x2d.shape
    tm = PROJ_TM
    tok = lambda c: pl.BlockSpec((tm, c), lambda i: (i, 0))
    full = lambda a: pl.BlockSpec(a.shape, lambda i: (0,) * a.ndim)
    return pl.pallas_call(
        functools.partial(_combine_kernel, final=final),
        grid=(n // tm,),
        in_specs=[tok(d), tok(olat.shape[1]), tok(ob.shape[1]), tok(A_WIDTH), tok(B_WIDTH),
                  full(lw["wuv"]), full(lw["wout"]), full(fg)],
        out_specs=tok(d),
        out_shape=jax.ShapeDtypeStruct((n, d), F32),
        compiler_params=pltpu.CompilerParams(dimension_semantics=("arbitrary",),
                                             vmem_limit_bytes=VMEM_LIMIT_BYTES),
        name="combine",
    )(x2d, olat, ob, pr["sga"], pr["sgb"], lw["wuv"], lw["wout"], fg)


def _combine_t_kernel(x_ref, olatT_ref, obT_ref, sgaT_ref, sgbT_ref, wuvt_ref, wout_ref, fg_ref, y_ref, *, final):
    o_a = jnp.concatenate([_dot(wuvt_ref[h], olatT_ref[h * A_KV_LORA:(h + 1) * A_KV_LORA, :])
                           for h in range(A_HEADS)], axis=0)
    mix_t = jnp.concatenate([o_a * sgaT_ref[...].astype(F32),
                             obT_ref[...].astype(F32) * sgbT_ref[...].astype(F32)], axis=0).astype(BF16)
    y = x_ref[...] + lax.dot_general(mix_t, wout_ref[...], (((0,), (0,)), ((), ())),
                                     preferred_element_type=F32)
    if final:
        y = _rms(y, fg_ref[...])
    y_ref[...] = y


def _combine_t(x3d, olat_t, ob_t, pr, lw, fg, final):
    b, t, d = x3d.shape
    tm = COMBINE_TM
    rows_t = lambda r: pl.BlockSpec((None, r, tm), lambda bi, ti: (bi, 0, ti))
    full = lambda a: pl.BlockSpec(a.shape, lambda bi, ti: (0,) * a.ndim)
    xblk = pl.BlockSpec((None, tm, d), lambda bi, ti: (bi, ti, 0))
    return pl.pallas_call(
        functools.partial(_combine_t_kernel, final=final),
        grid=(b, t // tm),
        in_specs=[xblk, rows_t(A_HEADS * A_KV_LORA), rows_t(B_WIDTH), rows_t(A_WIDTH), rows_t(B_WIDTH),
                  full(lw["wuvt"]), full(lw["wout"]), full(fg)],
        out_specs=xblk,
        out_shape=jax.ShapeDtypeStruct((b, t, d), F32),
        compiler_params=pltpu.CompilerParams(dimension_semantics=("arbitrary", "arbitrary"),
                                             vmem_limit_bytes=VMEM_LIMIT_BYTES),
        name="combine_prompt",
    )(x3d, olat_t, ob_t, pr["sgaT"], pr["sgbT"], lw["wuvt"], lw["wout"], fg)


def _layer_weights(norm_g, w_in, q_norm_g, kv_norm_g, w_uq, w_uk, w_uv, w_out):
    d = w_in.shape[0]
    o = np.cumsum([0, A_Q_LORA, A_KV_LORA, A_ROPE, A_WIDTH, B_WIDTH, B_KV_HEADS * B_HEAD_DIM,
                   B_KV_HEADS * B_HEAD_DIM, IDX_HEADS * IDX_DIM, IDX_DIM, IDX_HEADS, B_WIDTH])
    w_t = w_in.T.astype(BF16)
    seg = lambda i: w_t[int(o[i]):int(o[i + 1])]
    zpad = lambda r: jnp.zeros((r, d), BF16)
    win = jnp.concatenate([seg(0), seg(1), seg(2), seg(8), seg(9), zpad(LANES - A_ROPE - IDX_DIM - IDX_HEADS),
                           seg(3), seg(4), seg(5), seg(6), seg(7), seg(10)], axis=0)
    assert win.shape[0] == IN_PAD
    half = A_ROPE // 2
    wq = jnp.concatenate([
        w_uq[:, :, :A_NOPE].reshape(A_Q_LORA, A_HEADS * A_NOPE),
        w_uq[:, :, A_NOPE:A_NOPE + half].reshape(A_Q_LORA, A_HEADS * half),
        w_uq[:, :, A_NOPE + half:].reshape(A_Q_LORA, A_HEADS * half)], axis=1)
    eye = jnp.eye(A_HEADS, dtype=w_uk.dtype)
    wuk = jnp.einsum('chn,hg->hngc', w_uk, eye).reshape(A_HEADS * A_NOPE, A_HEADS * A_KV_LORA)
    wuv = jnp.einsum('chv,hg->hcgv', w_uv, eye).reshape(A_HEADS * A_KV_LORA, A_HEADS * A_V)
    pm = np.zeros((2 * LANES, A_HEADS * LANES), np.float32)
    for h in range(A_HEADS):
        for i in range(half):
            pm[h * half + i, h * LANES + i] = 1.0
            pm[LANES + h * half + i, h * LANES + half + i] = 1.0
    wk = jnp.concatenate([seg(1), seg(8), seg(2), zpad(LANES - IDX_DIM - A_ROPE), seg(5)], axis=0)
    wt = jnp.concatenate([seg(0), seg(1), seg(6), seg(4), seg(7), seg(3), seg(10), seg(9),
                          zpad(R_KB - R_WI - IDX_HEADS), seg(5), seg(8), seg(2)], axis=0)
    assert wk.shape[0] == COLS_K and wt.shape[0] == ROWS_T
    bc = lambda g: jnp.broadcast_to(g.reshape(-1, 1), (g.shape[0], PROJ_T_TM))
    return {
        "wk": wk, "wt": wt, "qngc": bc(q_norm_g), "kvngc": bc(kv_norm_g),
        "wqt": wq.T.astype(BF16), "wukt": wuk.T.astype(BF16),
        "wuvt": w_uv.transpose(1, 2, 0).astype(BF16),
        "ng": norm_g.reshape(1, -1), "win": win,
        "qng": q_norm_g.reshape(1, -1), "kvng": kv_norm_g.reshape(1, -1),
        "wq": wq.astype(BF16), "wuk": wuk.astype(BF16), "wuv": wuv.astype(BF16),
        "pmat": jnp.asarray(pm, BF16), "wout": w_out.astype(BF16),
    }


def _rope_tables(pos):
    half = A_ROPE // 2
    inv = ROPE_THETA ** (-jnp.arange(half, dtype=F32) / half)
    ang = pos.astype(F32)[:, None] * inv[None, :]
    cos, sin = jnp.cos(ang), jnp.sin(ang)
    z = jnp.zeros((pos.shape[0], LANES - A_ROPE), F32)
    zh = jnp.zeros_like(cos)
    cosq = jnp.tile(cos, (1, A_HEADS))
    sinq = jnp.tile(sin, (1, A_HEADS))
    rc = jnp.concatenate([cos, cos, z], axis=1)
    rs1 = jnp.concatenate([zh, sin, z], axis=1)
    rs2 = jnp.concatenate([-sin, zh, z], axis=1)
    return cosq, sinq, rc, rs1, rs2


def _rope_tables_t(pos):
    half = A_ROPE // 2
    inv = ROPE_THETA ** (-jnp.arange(half, dtype=F32) / half)
    ang = pos.astype(F32)[:, None] * inv[None, :]
    cos, sin = jnp.cos(ang), jnp.sin(ang)
    cost = jnp.tile(cos.T, (A_HEADS, 1))
    sint = jnp.tile(sin.T, (A_HEADS, 1))
    z0 = jnp.zeros((pos.shape[0], MK_KPE), F32)
    z1 = jnp.zeros((pos.shape[0], LANES - MK_KPE - A_ROPE), F32)
    zh = jnp.zeros_like(cos)
    rc = jnp.concatenate([z0, cos, cos, z1], axis=1)
    rs1 = jnp.concatenate([z0, zh, sin, z1], axis=1)
    rs2 = jnp.concatenate([z0, -sin, zh, z1], axis=1)
    return cost, sint, rc, rs1, rs2


def kernel(x_prompt, x_sample, cache_mla_ckv, cache_mla_kpe, cache_dsa_k, cache_dsa_v, cache_dsa_kidx,
           norm_g, w_in, mla_q_norm_g, mla_kv_norm_g, mla_w_uq, mla_w_uk, mla_w_uv, rel_bias, w_out,
           final_norm_g):
    bp, tp, d = x_prompt.shape
    bs, ts, _ = x_sample.shape
    depth = w_in.shape[0]
    past = cache_mla_ckv.shape[2]
    n_top_p = min(TOP_K_MAX, tp // 4)
    n_top_s = min(TOP_K_MAX, (past + ts) // 4)
    assert ts <= CHUNK and past % CHUNK == 0 and past % LANES == 0

    rope_p = _rope_tables_t(jnp.arange(tp, dtype=jnp.int32))
    reps = PROJ_TM // ts
    rope_s = tuple(jnp.tile(a, (reps, 1)) for a in _rope_tables(past + jnp.arange(ts, dtype=jnp.int32)))

    bias_p = _bias_tables(rel_bias, (0, -TQ, -3 * TQ), TQ, TQ, True)
    win_s = 2 * LANES
    bias_s = _bias_tables(rel_bias, (-(win_s - LANES), -(past + win_s)), ts, win_s, False)
    fg = final_norm_g.reshape(1, -1)

    xp = x_prompt
    xs = x_sample.reshape(bs * ts, d)
    outs_p, outs_s = [], []
    for l in range(depth):
        lw = _layer_weights(norm_g[l], w_in[l], mla_q_norm_g[l], mla_kv_norm_g[l],
                            mla_w_uq[l], mla_w_uk[l], mla_w_uv[l], w_out[l])
        final = l == depth - 1
        pr = _project_t(xp.reshape(bp * tp, d), rope_p, lw, bp, tp)
        olat, ob = _prompt_attention(pr, bias_p, bp, tp, n_top_p)
        xp = _combine_t(xp, olat, ob, pr, lw, fg, final)
        heads_t = lambda a: a.reshape(bp, B_KV_HEADS, B_HEAD_DIM, tp).transpose(0, 3, 1, 2)
        outs_p.append((pr["ckv"].reshape(bp, tp, A_KV_LORA), pr["kpeT"].transpose(0, 2, 1),
                       heads_t(pr["kbT"]), heads_t(pr["vbT"]), pr["kidxT"].transpose(0, 2, 1)))
        ps = _project(xs, rope_s, lw, period=PROJ_TM)
        feat_t = lambda a: a.transpose(0, 2, 3, 1).reshape(bs, B_KV_HEADS * B_HEAD_DIM, past)
        caches = (cache_mla_ckv[l], cache_mla_kpe[l].transpose(0, 2, 1), feat_t(cache_dsa_k[l]),
                  feat_t(cache_dsa_v[l]), cache_dsa_kidx[l].transpose(0, 2, 1))
        olat, ob = _sample_attention(ps, caches, bias_s, bs, ts, past, n_top_s)
        xs = _combine(xs, olat, ob, ps, lw, fg, final)
        outs_s.append((ps["ckv"].reshape(bs, ts, A_KV_LORA), ps["kpe"].reshape(bs, ts, A_ROPE),
                       ps["kb"].reshape(bs, ts, B_KV_HEADS, B_HEAD_DIM),
                       ps["vb"].reshape(bs, ts, B_KV_HEADS, B_HEAD_DIM),
                       ps["kidx"].reshape(bs, ts, IDX_DIM)))

    stack = lambda outs, i: jnp.stack([o[i] for o in outs])
    return ((xp, xs.reshape(bs, ts, d))
            + tuple(stack(outs_p, i) for i in range(5)) + tuple(stack(outs_s, i) for i in range(5)))
```

```python
import functools
import math

import jax
import jax.numpy as jnp
import numpy as np
from jax import lax
from jax.experimental import pallas as pl
from jax.experimental.pallas import tpu as pltpu

F32 = jnp.float32
BF16 = jnp.bfloat16
I32 = jnp.int32

CHUNK = 64
EPS = 1e-6
A_HEADS = 8
A_NOPE = 64
A_ROPE = 32
A_V = 64
A_Q_LORA = 256
A_KV_LORA = 128
ROPE_THETA = 10000.0
A_WIDTH = A_HEADS * A_V
B_HEADS = 8
B_KV_HEADS = 2
B_HEAD_DIM = 64
B_WIDTH = B_HEADS * B_HEAD_DIM
B_GROUP = B_HEADS // B_KV_HEADS
IDX_HEADS = 8
IDX_DIM = 64
TOP_K_MAX = 256
N_BUCKETS = 32
MAX_DISTANCE = 128

LANES = 128
VMEM_LIMIT_BYTES = 56 * 1024 * 1024

LOG2E = 1.4426950408889634
NEG_BIG = -1e30
INT_MIN = -(2 ** 31)

C_CQ = 0
C_CKV = C_CQ + A_Q_LORA
C_MISC = C_CKV + A_KV_LORA
C_GA = C_MISC + LANES
C_QB = C_GA + A_WIDTH
C_KB = C_QB + B_WIDTH
C_VB = C_KB + B_KV_HEADS * B_HEAD_DIM
C_QI = C_VB + B_KV_HEADS * B_HEAD_DIM
C_GB = C_QI + IDX_HEADS * IDX_DIM
IN_PAD = C_GB + B_WIDTH
M_KPE = 0
M_KIDX = A_ROPE
M_WIDX = A_ROPE + IDX_DIM

QCAT = 2 * LANES
TQ = 256
PROJ_TM = 256
COMBINE_TM = 1024
PROJ_T_TM = 1024


def _dot(a, b):
    return jnp.dot(a, b, preferred_element_type=F32)


def _dot_nt(a, b):
    return lax.dot_general(a, b, (((1,), (1,)), ((), ())), preferred_element_type=F32)


def _rms(x, g):
    return x * lax.rsqrt(jnp.mean(x * x, axis=-1, keepdims=True) + EPS) * g


def _bias_kernel(rb_ref, out_ref, *, offsets, keys_on_rows):
    nb = N_BUCKETS // 2
    max_exact = nb // 2
    n_r = out_ref.shape[1] if keys_on_rows else out_ref.shape[2]
    n_c = out_ref.shape[2] // B_HEADS if keys_on_rows else out_ref.shape[3]
    row = lax.broadcasted_iota(I32, (n_r, n_c), 0)
    col = lax.broadcasted_iota(I32, (n_r, n_c), 1)

    def bucket_of(off):
        rel = off + (row - col if keys_on_rows else col - row)
        ret = jnp.where(rel > 0, nb, 0)
        n = jnp.abs(rel)
        nf = jnp.maximum(n, 1).astype(F32)
        large = max_exact + (jnp.log(nf / max_exact) / math.log(MAX_DISTANCE / max_exact)
                             * (nb - max_exact)).astype(I32)
        large = jnp.minimum(large, nb - 1)
        return ret + jnp.where(n < max_exact, n, large)

    def lookup(bucket, h):
        table = jnp.broadcast_to(rb_ref[h:h + 1, :], (n_r, LANES))
        return jnp.concatenate(
            [jnp.take_along_axis(table, bucket[:, c:c + LANES], axis=1) for c in range(0, n_c, LANES)], axis=1)

    buckets = [bucket_of(off) for off in offsets]
    for h in range(B_HEADS):
        if keys_on_rows:
            far = lookup(buckets[-1], h)
            for p in range(len(offsets) - 1):
                out_ref[p, :, h * n_c:(h + 1) * n_c] = (lookup(buckets[p], h) - far) * LOG2E
        else:
            for p in range(len(offsets)):
                out_ref[p, h] = lookup(buckets[p], h) * LOG2E


def _bias_tables(rel_bias, offsets, n_r, n_c, keys_on_rows):
    shape = (len(offsets) - 1, n_r, B_HEADS * n_c) if keys_on_rows else (len(offsets), B_HEADS, n_r, n_c)
    return pl.pallas_call(
        functools.partial(_bias_kernel, offsets=tuple(offsets), keys_on_rows=keys_on_rows),
        out_shape=jax.ShapeDtypeStruct(shape, F32),
        in_specs=[pl.BlockSpec(memory_space=pltpu.VMEM)],
        out_specs=pl.BlockSpec(memory_space=pltpu.VMEM),
        compiler_params=pltpu.CompilerParams(vmem_limit_bytes=VMEM_LIMIT_BYTES),
        name="bias_tables",
    )(jnp.pad(rel_bias.T, ((0, 0), (0, LANES - N_BUCKETS))))


def _proj_kernel(x_ref, ng_ref, win_ref, qng_ref, kvng_ref, wq_ref, wuk_ref, pmat_ref,
                 cosq_ref, sinq_ref, rc_ref, rs1_ref, rs2_ref,
                 ckv_ref, kpe_ref, kb_ref, vb_ref, kidx_ref,
                 kcat_ref, kb16_ref, vb16_ref, kidx16_ref,
                 qcat_ref, qb_ref, qidx_ref, widx_ref, sga_ref, sgb_ref):
    x = x_ref[...]
    h = _rms(x, ng_ref[...])
    z = _dot_nt(h.astype(BF16), win_ref[...])

    cq = _rms(z[:, C_CQ:C_CQ + A_Q_LORA], qng_ref[...])
    q = _dot(cq.astype(BF16), wq_ref[...])
    n_nope = A_HEADS * A_NOPE
    x1 = q[:, n_nope:n_nope + LANES]
    x2 = q[:, n_nope + LANES:n_nope + 2 * LANES]
    cos8, sin8 = cosq_ref[...], sinq_ref[...]
    o1 = x1 * cos8 - x2 * sin8
    o2 = x1 * sin8 + x2 * cos8
    mla_scale = (A_NOPE + A_ROPE) ** -0.5 * LOG2E
    q_lat = _dot(q[:, :n_nope].astype(BF16), wuk_ref[...]) * mla_scale
    pe = jnp.concatenate([o1, o2], axis=1) * mla_scale
    q_pe = _dot(pe.astype(BF16), pmat_ref[...])
    for hh in range(A_HEADS):
        qcat_ref[:, hh * QCAT:hh * QCAT + LANES] = q_lat[:, hh * LANES:(hh + 1) * LANES].astype(BF16)
        qcat_ref[:, hh * QCAT + LANES:(hh + 1) * QCAT] = q_pe[:, hh * LANES:(hh + 1) * LANES].astype(BF16)

    ckv = _rms(z[:, C_CKV:C_CKV + A_KV_LORA], kvng_ref[...])
    ckv_ref[...] = ckv
    misc = z[:, C_MISC:C_MISC + LANES]
    rot = (misc * rc_ref[...] + pltpu.roll(misc, A_ROPE // 2, 1) * rs1_ref[...]
           + pltpu.roll(misc, LANES - A_ROPE // 2, 1) * rs2_ref[...])
    kpe_ref[...] = rot[:, :A_ROPE]
    kcat_ref[:, :LANES] = ckv.astype(BF16)
    kcat_ref[:, LANES:] = rot.astype(BF16)

    kidx = misc[:, M_KIDX:M_KIDX + IDX_DIM]
    kidx_ref[...] = kidx
    kidx16_ref[...] = kidx.astype(BF16)
    widx_ref[...] = misc[:, M_WIDX:M_WIDX + IDX_HEADS] * (IDX_HEADS ** -0.5)
    kb = z[:, C_KB:C_KB + LANES]
    vb = z[:, C_VB:C_VB + LANES]
    kb_ref[...] = kb
    vb_ref[...] = vb
    kb16_ref[...] = kb.astype(BF16)
    vb16_ref[...] = vb.astype(BF16)
    qb_ref[...] = (z[:, C_QB:C_QB + B_WIDTH] * (B_HEAD_DIM ** -0.5 * LOG2E)).astype(BF16)
    qidx_ref[...] = (z[:, C_QI:C_QI + IDX_HEADS * IDX_DIM] * (IDX_DIM ** -0.5)).astype(BF16)
    sga_ref[...] = jax.nn.silu(z[:, C_GA:C_GA + A_WIDTH])
    sgb_ref[...] = jax.nn.silu(z[:, C_GB:C_GB + B_WIDTH])


def _project(x2d, rope_tabs, lw, *, period):
    n, d = x2d.shape
    tm = PROJ_TM
    assert n % tm == 0 and period % tm == 0
    n_rep = period // tm
    tok = lambda c: pl.BlockSpec((tm, c), lambda i: (i, 0))
    full = lambda a: pl.BlockSpec(a.shape, lambda i: (0,) * a.ndim)
    tab = pl.BlockSpec((tm, LANES), lambda i: (i % n_rep, 0))
    outs = [
        ("ckv", A_KV_LORA, F32), ("kpe", A_ROPE, F32), ("kb", LANES, F32), ("vb", LANES, F32),
        ("kidx", IDX_DIM, F32),
        ("kcat", QCAT, BF16), ("kb16", LANES, BF16), ("vb16", LANES, BF16), ("kidx16", IDX_DIM, BF16),
        ("qcat", A_HEADS * QCAT, BF16), ("qb", B_WIDTH, BF16), ("qidx", IDX_HEADS * IDX_DIM, BF16),
        ("widx", IDX_HEADS, F32), ("sga", A_WIDTH, F32), ("sgb", B_WIDTH, F32),
    ]
    res = pl.pallas_call(
        _proj_kernel,
        grid=(n // tm,),
        in_specs=[tok(d), full(lw["ng"]), full(lw["win"]), full(lw["qng"]), full(lw["kvng"]),
                  full(lw["wq"]), full(lw["wuk"]), full(lw["pmat"]), tab, tab, tab, tab, tab],
        out_specs=[tok(c) for _, c, _ in outs],
        out_shape=[jax.ShapeDtypeStruct((n, c), dt) for _, c, dt in outs],
        compiler_params=pltpu.CompilerParams(dimension_semantics=("arbitrary",),
                                             vmem_limit_bytes=VMEM_LIMIT_BYTES),
        name="project",
    )(x2d, lw["ng"], lw["win"], lw["qng"], lw["kvng"], lw["wq"], lw["wuk"], lw["pmat"], *rope_tabs)
    return {name: r for (name, _, _), r in zip(outs, res)}


R_CQ = 0
R_CKV = R_CQ + A_Q_LORA
R_VB = R_CKV + A_KV_LORA
R_QB = R_VB + LANES
R_QI = R_QB + B_WIDTH
R_GA = R_QI + IDX_HEADS * IDX_DIM
R_GB = R_GA + A_WIDTH
R_WI = R_GB + B_WIDTH
R_KB = R_WI + 16
R_KI = R_KB + LANES
R_KPE = R_KI + IDX_DIM
ROWS_T = R_KPE + A_ROPE
MK_KPE = IDX_DIM
ONES_ROWS = 16
V_EXT = A_KV_LORA + ONES_ROWS


def _proj_t_kernel(x_ref, ng_ref, wt_ref, qng_ref, kvngc_ref, wqt_ref, wukt_ref, cost_ref, sint_ref,
                   ckv_ref, kcat_ref, kb16_ref, kidx16_ref,
                   kpeT_ref, kbT_ref, vbT_ref, kidxT_ref,
                   qcatT_ref, qbT_ref, qidxT_ref, widxT_ref, vmlaT_ref, vdsaT_ref, sgaT_ref, sgbT_ref):
    x = x_ref[...]
    tm = x.shape[0]
    hb = _rms(x, ng_ref[...]).astype(BF16)

    zt = _dot_nt(wt_ref[...], hb)

    def rms_t(c, g):
        return c * lax.rsqrt(jnp.mean(c * c, axis=0, keepdims=True) + EPS) * g

    cq = rms_t(zt[R_CQ:R_CQ + A_Q_LORA], qng_ref[...])
    qt = _dot(wqt_ref[...], cq.astype(BF16))
    n_nope = A_HEADS * A_NOPE
    x1 = qt[n_nope:n_nope + LANES]
    x2 = qt[n_nope + LANES:n_nope + 2 * LANES]
    cos8, sin8 = cost_ref[...], sint_ref[...]
    mla_scale = (A_NOPE + A_ROPE) ** -0.5 * LOG2E
    o1 = (x1 * cos8 - x2 * sin8) * mla_scale
    o2 = (x1 * sin8 + x2 * cos8) * mla_scale
    q_lat = _dot(wukt_ref[...], qt[:n_nope].astype(BF16)) * mla_scale
    half = A_ROPE // 2
    for h in range(A_HEADS):
        qcatT_ref[h, :LANES, :] = q_lat[h * LANES:(h + 1) * LANES].astype(BF16)
        qcatT_ref[h, LANES:LANES + half, :] = o1[h * half:(h + 1) * half].astype(BF16)
        qcatT_ref[h, LANES + half:LANES + A_ROPE, :] = o2[h * half:(h + 1) * half].astype(BF16)
        qcatT_ref[h, LANES + A_ROPE:, :] = jnp.zeros((QCAT - LANES - A_ROPE, tm), BF16)
    ones = jnp.ones((ONES_ROWS, TQ), BF16)
    ckv_t = rms_t(zt[R_CKV:R_CKV + A_KV_LORA], kvngc_ref[...])
    vb_t = zt[R_VB:R_VB + LANES]
    for c in range(tm // TQ):
        blk = slice(c * TQ, (c + 1) * TQ)
        vmlaT_ref[c, :A_KV_LORA, :] = ckv_t[:, blk].astype(BF16)
        vmlaT_ref[c, A_KV_LORA:, :] = ones
        vdsaT_ref[c, :LANES, :] = vb_t[:, blk].astype(BF16)
        vdsaT_ref[c, LANES:, :] = ones
    kb_t = zt[R_KB:R_KB + LANES]
    kidx_t = zt[R_KI:R_KI + IDX_DIM]
    k1, k2 = zt[R_KPE:R_KPE + half], zt[R_KPE + half:R_KPE + A_ROPE]
    cos1, sin1 = cos8[:half], sin8[:half]
    kpe_t = jnp.concatenate([k1 * cos1 - k2 * sin1, k1 * sin1 + k2 * cos1], axis=0)
    vbT_ref[...] = vb_t
    kbT_ref[...] = kb_t
    kidxT_ref[...] = kidx_t
    kpeT_ref[...] = kpe_t
    ckv = ckv_t.T
    misc = jnp.concatenate([kidx_t, kpe_t, jnp.zeros((LANES - IDX_DIM - A_ROPE, tm), F32)], axis=0).T
    lane = lax.broadcasted_iota(I32, (tm, LANES), 1)
    ckv_ref[...] = ckv
    kcat_ref[:, :LANES] = ckv.astype(BF16)
    kcat_ref[:, LANES:] = pltpu.roll(
        jnp.where((lane >= MK_KPE) & (lane < MK_KPE + A_ROPE), misc, 0.0), LANES - MK_KPE, 1).astype(BF16)
    kidx16_ref[...] = jnp.where(lane < IDX_DIM, misc, 0.0).astype(BF16)
    kb16_ref[...] = kb_t.T.astype(BF16)
    qbT_ref[...] = (zt[R_QB:R_QB + B_WIDTH] * (B_HEAD_DIM ** -0.5 * LOG2E)).astype(BF16)
    qidxT_ref[...] = (zt[R_QI:R_QI + IDX_HEADS * IDX_DIM] * (IDX_DIM ** -0.5)).astype(BF16)
    sgaT_ref[...] = jax.nn.silu(zt[R_GA:R_GA + A_WIDTH]).astype(BF16)
    sgbT_ref[...] = jax.nn.silu(zt[R_GB:R_GB + B_WIDTH]).astype(BF16)
    widxT_ref[...] = zt[R_WI:R_WI + IDX_HEADS] * (IDX_HEADS ** -0.5)


def _project_t(x2d, tabs, lw, b, t):
    n, d = x2d.shape
    tm = PROJ_T_TM
    assert t % tm == 0 and tm % TQ == 0
    nt = t // tm
    tok = lambda c: pl.BlockSpec((tm, c), lambda i: (i, 0))
    full = lambda a: pl.BlockSpec(a.shape, lambda i: (0,) * a.ndim)
    vblk = pl.BlockSpec((None, tm // TQ, V_EXT, TQ), lambda i: (i // nt, i % nt, 0, 0))
    tab_t = pl.BlockSpec((LANES, tm), lambda i: (0, i % nt))
    rows_t = lambda r: pl.BlockSpec((None, r, tm), lambda i: (i // nt, 0, i % nt))
    outs = [
        ("ckv", (n, A_KV_LORA), F32, tok(A_KV_LORA)),
        ("kcat", (n, QCAT), BF16, tok(QCAT)), ("kb16", (n, LANES), BF16, tok(LANES)),
        ("kidx16", (n, LANES), BF16, tok(LANES)),
        ("kpeT", (b, A_ROPE, t), F32, rows_t(A_ROPE)), ("kbT", (b, LANES, t), F32, rows_t(LANES)),
        ("vbT", (b, LANES, t), F32, rows_t(LANES)), ("kidxT", (b, IDX_DIM, t), F32, rows_t(IDX_DIM)),
        ("qcatT", (b, A_HEADS, QCAT, t), BF16,
         pl.BlockSpec((None, A_HEADS, QCAT, tm), lambda i: (i // nt, 0, 0, i % nt))),
        ("qbT", (b, B_WIDTH, t), BF16, rows_t(B_WIDTH)),
        ("qidxT", (b, IDX_HEADS * IDX_DIM, t), BF16, rows_t(IDX_HEADS * IDX_DIM)),
        ("widxT", (b, IDX_HEADS, t), F32, rows_t(IDX_HEADS)),
        ("vmlaT", (b, t // TQ, V_EXT, TQ), BF16, vblk), ("vdsaT", (b, t // TQ, V_EXT, TQ), BF16, vblk),
        ("sgaT", (b, A_WIDTH, t), BF16, rows_t(A_WIDTH)), ("sgbT", (b, B_WIDTH, t), BF16, rows_t(B_WIDTH)),
    ]
    cost, sint = tabs
    res = pl.pallas_call(
        _proj_t_kernel,
        grid=(n // tm,),
        in_specs=[tok(d), full(lw["ng"]), full(lw["wt"]), full(lw["qngc"]), full(lw["kvngc"]),
                  full(lw["wqt"]), full(lw["wukt"]), tab_t, tab_t],
        out_specs=[o[3] for o in outs],
        out_shape=[jax.ShapeDtypeStruct(o[1], o[2]) for o in outs],
        compiler_params=pltpu.CompilerParams(dimension_semantics=("arbitrary",),
                                             vmem_limit_bytes=VMEM_LIMIT_BYTES),
        name="project_prompt",
    )(x2d, lw["ng"], lw["wt"], lw["qngc"], lw["kvngc"], lw["wqt"], lw["wukt"], cost, sint)
    return {o[0]: r for o, r in zip(outs, res)}


NEG_FLT_MAX = -3.4028234663852886e38
KEY_NEG_FLT_MAX = INT_MIN + (1 << 23)


def _key_to_float(k):
    k = jnp.maximum(k, KEY_NEG_FLT_MAX)
    return pltpu.bitcast(k ^ ((k >> 31) & 0x7FFFFFFF), F32)


def _count(pred):
    return jnp.sum(jnp.where(pred, 1.0, 0.0), axis=1, keepdims=True)


SAMPLE_STREAMS = 2
MAX_TIE_SWEEPS = 8.0
SAFE_DENOM_MIN = 2.0 ** -90
SAFE_DENOM_MAX = 2.0 ** 40


def _fold8(x, op=jnp.add):
    parts = [x[i:i + 8] for i in range(0, x.shape[0], 8)]
    while len(parts) > 1:
        parts = [op(a, b) for a, b in zip(parts[::2], parts[1::2])]
    return parts[0]


def _prompt_attn_kernel(qcatT_ref, qbT_ref, qidxT_ref, widxT_ref, kcat_ref, kb_ref, kidx_ref,
                        vmlaT_ref, vdsaT_ref, bias_ref, olatT_ref, obT_ref,
                        qa_ref, qbp_ref, qip_ref, sc_ref, m_ref, acc_ref, mb_ref, accb_ref, kn_ref, *, n_top):
    qi = pl.program_id(1)
    nblk = qi + 1
    tq = TQ
    krow = lax.broadcasted_iota(I32, (tq, tq), 0)
    qcol = lax.broadcasted_iota(I32, (tq, tq), 1)
    shift = CHUNK.bit_length() - 1
    diag_ok = (qcol >> shift) >= (krow >> shift)
    hcols = lambda h: slice(h * tq, (h + 1) * tq)

    def per_head(fn):
        return jnp.concatenate([fn(h) for h in range(A_HEADS)], axis=1)

    qbp_ref[...] = jnp.zeros(qbp_ref.shape, BF16)
    qip_ref[...] = jnp.zeros(qip_ref.shape, BF16)
    for h in range(A_HEADS):
        g = h // B_GROUP
        qa_ref[:, hcols(h)] = qcatT_ref[h]
        qbp_ref[g * B_HEAD_DIM:(g + 1) * B_HEAD_DIM, hcols(h)] = qbT_ref[h * B_HEAD_DIM:(h + 1) * B_HEAD_DIM, :]
        qip_ref[:IDX_DIM, hcols(h)] = qidxT_ref[h * IDX_DIM:(h + 1) * IDX_DIM, :]
    w_all = per_head(lambda h: widxT_ref[h:h + 1, :])

    @pl.when(qi == 0)
    def _():
        def max_row_norm2(k_ref):
            k = k_ref[...].astype(F32)
            return jnp.max(jnp.sum(k * k, axis=1, keepdims=True))
        kn_ref[0:1, :] = jnp.full((1, LANES), max_row_norm2(kcat_ref), F32)
        kn_ref[1:2, :] = jnp.full((1, LANES), max_row_norm2(kb_ref), F32)
        kn_ref[2:3, :] = jnp.full((1, LANES), jnp.max(bias_ref[...]), F32)

    def col_norm(q_ref):
        q = q_ref[...].astype(F32)
        return jnp.sqrt(jnp.sum(q * q, axis=0, keepdims=True))

    shift_a = jnp.sqrt(kn_ref[0:1, 0:1]) * col_norm(qa_ref)
    shift_b = jnp.sqrt(kn_ref[1:2, 0:1]) * col_norm(qbp_ref) + jnp.maximum(kn_ref[2:3, 0:1], 0.0)

    def accumulate(s_t, values, shift_or_m, acc_r, exact):
        if exact:
            m_prev = shift_or_m[0:1, :]
            m_new = jnp.maximum(m_prev, jnp.max(s_t, axis=0, keepdims=True))
            alpha = jnp.exp2(m_prev - m_new)
            shift_or_m[0:1, :] = m_new
        else:
            m_new = shift_or_m
        p_t = jnp.exp2(s_t - m_new).astype(BF16)
        for v_t, lanes in values:
            pv = _dot(v_t, p_t[:, lanes])
            acc_r[:, lanes] = (alpha[:, lanes] * acc_r[:, lanes] if exact else acc_r[:, lanes]) + pv

    def unsafe(l):
        return jnp.max(jnp.where((l >= SAFE_DENOM_MIN) & (l <= SAFE_DENOM_MAX), 0.0, 1.0)) > 0.0

    all_lanes = slice(0, A_HEADS * tq)

    def for_blocks(n, block):
        def pair(p, c):
            block(2 * p)
            block(2 * p + 1)
            return c

        lax.fori_loop(0, lax.shift_right_logical(n, 1), pair, 0)

        @pl.when((n & 1) == 1)
        def _():
            block(n - 1)

    def mla_pass(exact):
        acc_ref[...] = jnp.zeros(acc_ref.shape, F32)
        if exact:
            m_ref[...] = jnp.full(m_ref.shape, NEG_BIG, F32)

        def block(j, masked):
            start = pl.multiple_of(j * tq, tq)
            s_t = _dot(kcat_ref[pl.ds(start, tq), :], qa_ref[...])
            if masked:
                s_t = per_head(lambda h: jnp.where(diag_ok, s_t[:, hcols(h)], NEG_BIG))
            accumulate(s_t, [(vmlaT_ref[j], all_lanes)], m_ref if exact else shift_a, acc_ref, exact)
            if exact:
                return
            r = jnp.maximum(_dot(kidx_ref[pl.ds(start, tq), :], qip_ref[...]), 0.0) * w_all
            score = r[:, hcols(0)]
            for h in range(1, IDX_HEADS):
                score = score + r[:, hcols(h)]
            if masked:
                score = jnp.where(diag_ok, score, -jnp.inf)
            sc_ref[j] = score

        for_blocks(qi, lambda j: block(j, False))
        block(qi, True)

    mla_pass(False)

    @pl.when(unsafe(acc_ref[A_KV_LORA:A_KV_LORA + 1, :]))
    def _():
        mla_pass(True)

    o_t = acc_ref[:A_KV_LORA, :] * (1.0 / acc_ref[A_KV_LORA:A_KV_LORA + 1, :])
    for h in range(A_HEADS):
        olatT_ref[h * LANES:(h + 1) * LANES, :] = o_t[:, hcols(h)].astype(olatT_ref.dtype)

    def count(pred):
        def one(j, c):
            return c + _fold8(jnp.where(pred(sc_ref[j], j), 1.0, 0.0))
        part = lax.fori_loop(0, lax.shift_right_logical(nblk, 1), lambda p, c: one(2 * p + 1, one(2 * p, c)),
                             jnp.zeros((8, tq), F32))
        part = lax.cond((nblk & 1) == 1, lambda c: one(nblk - 1, c), lambda c: c, part)
        return jnp.sum(part, axis=0, keepdims=True)

    kf = float(n_top)

    def bis_body(it, carry):
        lo, cnt_lo = carry
        cand = lo + lax.shift_left(jnp.int32(1), 31 - it)
        cand_f = _key_to_float(cand)
        cnt = count(lambda s, j: s >= cand_f)
        take = cnt >= kf
        return jnp.where(take, cand, lo), jnp.where(take, cnt, cnt_lo)

    n_steps = jnp.where(nblk * tq <= n_top, 0, 32)
    lo, cnt_ge = lax.fori_loop(
        0, n_steps, bis_body, (jnp.full((1, tq), INT_MIN, I32), jnp.full((1, tq), 1e9, F32)))
    few = lo == INT_MIN
    thr = _key_to_float(lo)
    excess0 = jnp.where(few, 0.0, cnt_ge - kf)
    max_excess = jnp.max(excess0)

    def drop_from(cut):
        def body(j, c):
            s = sc_ref[j]
            sc_ref[j] = jnp.where((s == thr) & ((krow + j * tq) >= cut), -jnp.inf, s)
            return c
        lax.fori_loop(0, nblk, body, 0)

    @pl.when((max_excess > 0.0) & (max_excess <= MAX_TIE_SWEEPS))
    def _():
        def last_tie_below(cut):
            def body(j, m):
                pos = krow + j * tq
                hit = jnp.where((sc_ref[j] == thr) & (pos < cut), pos, -1)
                return jnp.maximum(m, _fold8(hit, jnp.maximum))
            part = lax.fori_loop(0, nblk, body, jnp.full((8, tq), -1, I32))
            return jnp.max(part, axis=0, keepdims=True)

        def sweep(c):
            excess, cut = c
            last = last_tie_below(cut)
            live = excess > 0.0
            return jnp.where(live, excess - 1.0, excess), jnp.where(live, last, cut)

        _, cut = lax.while_loop(lambda c: jnp.max(c[0]) > 0.0, sweep,
                                (excess0, jnp.full((1, tq), 2 ** 30, I32)))
        drop_from(cut)

    @pl.when(max_excess > MAX_TIE_SWEEPS)
    def _():
        need = kf - count(lambda s, j: s > thr)
        n_bits = (sc_ref.shape[0] * tq).bit_length()

        def cut_body(it, cpos):
            cand = cpos + lax.shift_left(jnp.int32(1), n_bits - 1 - it)
            cnt = count(lambda s, j: (s == thr) & ((krow + j * tq) < cand))
            return jnp.where(cnt < need, cand, cpos)

        keep = lax.fori_loop(0, n_bits, cut_body, jnp.zeros((1, tq), I32))
        drop_from(jnp.where(excess0 > 0.0, keep + 1, 2 ** 30))

    group_lanes = [slice(g * B_GROUP * tq, (g + 1) * B_GROUP * tq) for g in range(B_KV_HEADS)]

    def dsa_pass(exact):
        accb_ref[...] = jnp.zeros(accb_ref.shape, F32)
        if exact:
            mb_ref[...] = jnp.full(mb_ref.shape, NEG_BIG, F32)

        def block(j, near):
            start = pl.multiple_of(j * tq, tq)
            sel = sc_ref[j] >= thr
            s_t = _dot(kb_ref[pl.ds(start, tq), :], qbp_ref[...])
            if near:
                s_t = s_t + bias_ref[qi - j]
            s_t = per_head(lambda h: jnp.where(sel, s_t[:, hcols(h)], NEG_BIG))
            v_all = vdsaT_ref[j]
            values = [(jnp.concatenate([v_all[g * B_HEAD_DIM:(g + 1) * B_HEAD_DIM], v_all[LANES:]], axis=0),
                       group_lanes[g]) for g in range(B_KV_HEADS)]
            accumulate(s_t, values, mb_ref if exact else shift_b, accb_ref, exact)

        for_blocks(jnp.maximum(qi - 1, 0), lambda j: block(j, False))

        @pl.when(qi >= 1)
        def _():
            block(qi - 1, True)

        block(qi, True)

    dsa_pass(False)

    @pl.when(unsafe(accb_ref[B_HEAD_DIM:B_HEAD_DIM + 1, :]))
    def _():
        dsa_pass(True)

    inv_b = 1.0 / accb_ref[B_HEAD_DIM:B_HEAD_DIM + 1, :]
    for h in range(B_HEADS):
        obT_ref[h * B_HEAD_DIM:(h + 1) * B_HEAD_DIM, :] = (
            accb_ref[:B_HEAD_DIM, hcols(h)] * inv_b[:, hcols(h)]).astype(obT_ref.dtype)


def _prompt_attention(pr, bias_p, b, t, n_top):
    tq = TQ
    assert t % tq == 0
    nq = t // tq
    r3 = lambda a: a.reshape(b, t, a.shape[-1])
    qrows = lambda r: pl.BlockSpec((None, r, tq), lambda bi, qi: (bi, 0, qi))
    kall = lambda c: pl.BlockSpec((None, t, c), lambda bi, qi: (bi, 0, 0))
    vall = pl.BlockSpec((None, nq, V_EXT, tq), lambda bi, qi: (bi, 0, 0, 0))
    olat, ob = pl.pallas_call(
        functools.partial(_prompt_attn_kernel, n_top=n_top),
        grid=(b, nq),
        in_specs=[pl.BlockSpec((None, A_HEADS, QCAT, tq), lambda bi, qi: (bi, 0, 0, qi)),
                  qrows(B_WIDTH), qrows(IDX_HEADS * IDX_DIM), qrows(IDX_HEADS),
                  kall(QCAT), kall(LANES), kall(LANES), vall, vall,
                  pl.BlockSpec(bias_p.shape, lambda bi, qi: (0, 0, 0))],
        out_specs=[qrows(A_HEADS * A_KV_LORA), qrows(B_WIDTH)],
        out_shape=[jax.ShapeDtypeStruct((b, A_HEADS * A_KV_LORA, t), BF16),
                   jax.ShapeDtypeStruct((b, B_WIDTH, t), BF16)],
        scratch_shapes=[
            pltpu.VMEM((QCAT, A_HEADS * tq), BF16),
            pltpu.VMEM((LANES, B_HEADS * tq), BF16),
            pltpu.VMEM((LANES, IDX_HEADS * tq), BF16),
            pltpu.VMEM((nq, tq, tq), F32),
            pltpu.VMEM((8, A_HEADS * tq), F32),
            pltpu.VMEM((V_EXT, A_HEADS * tq), F32),
            pltpu.VMEM((8, B_HEADS * tq), F32),
            pltpu.VMEM((B_HEAD_DIM + ONES_ROWS, B_HEADS * tq), F32),
            pltpu.VMEM((8, LANES), F32),
        ],
        compiler_params=pltpu.CompilerParams(dimension_semantics=("arbitrary", "arbitrary"),
                                             vmem_limit_bytes=VMEM_LIMIT_BYTES),
        name="prompt_attention",
    )(pr["qcatT"], pr["qbT"], pr["qidxT"], pr["widxT"],
      r3(pr["kcat"]), r3(pr["kb16"]), r3(pr["kidx16"]), pr["vmlaT"], pr["vdsaT"], bias_p)
    return olat, ob


def _sample_attn_kernel(qcat_ref, qb_ref, qidx_ref, widx_ref, kcatn_ref, kbn_ref, vbn_ref, kidxn_ref,
                        cckv_ref, ckpeT_ref, ckT_ref, cvT_ref, ckidxT_ref, bias_ref,
                        olat_ref, ob_ref, *, n_top, t_new, past):
    tq = t_new
    pad = LANES
    n_keys = past + pad
    kf = float(n_top)
    n_bits = n_keys.bit_length()
    n_far = n_keys - bias_ref.shape[-1]

    def padrows(a):
        return jnp.concatenate([a, jnp.zeros((pad - t_new, a.shape[1]), a.dtype)], axis=0)

    def softmax(s):
        m = jnp.max(s, axis=1, keepdims=True)
        p = jnp.exp2(s - m)
        return p.astype(BF16), jnp.sum(p, axis=1, keepdims=True)

    def new_cols(rows):
        return lax.broadcasted_iota(I32, (rows, pad), 1) < t_new

    def mla_and_scores(i):
        ckv_c = cckv_ref[i].astype(BF16)
        kpe_t = ckpeT_ref[i].astype(BF16)
        kcat_n = padrows(kcatn_ref[i])
        qs = jnp.concatenate([qcat_ref[i, :, h * QCAT:(h + 1) * QCAT] for h in range(A_HEADS)], axis=0)
        s_c = _dot_nt(qs[:, :A_KV_LORA], ckv_c) + _dot(qs[:, A_KV_LORA:A_KV_LORA + A_ROPE], kpe_t)
        s_n = jnp.where(new_cols(A_HEADS * tq), _dot_nt(qs, kcat_n), NEG_BIG)
        pb, l = softmax(jnp.concatenate([s_c, s_n], axis=1))
        o = (_dot(pb[:, :past], ckv_c) + _dot(pb[:, past:], kcat_n[:, :A_KV_LORA])) / l
        for h in range(A_HEADS):
            olat_ref[i, :, h * LANES:(h + 1) * LANES] = o[h * tq:(h + 1) * tq].astype(olat_ref.dtype)

        kidx_t = ckidxT_ref[i].astype(BF16)
        kidx_n = padrows(kidxn_ref[i])
        qis = jnp.concatenate(
            [qidx_ref[i, :, h * IDX_DIM:(h + 1) * IDX_DIM] for h in range(IDX_HEADS)], axis=0)

        def head_sum(dots):
            acc = jnp.maximum(dots[:tq], 0.0) * widx_ref[i, :, 0:1]
            for h in range(1, IDX_HEADS):
                acc = acc + jnp.maximum(dots[h * tq:(h + 1) * tq], 0.0) * widx_ref[i, :, h:h + 1]
            return acc

        return jnp.concatenate(
            [head_sum(_dot(qis, kidx_t)),
             jnp.where(new_cols(tq), head_sum(_dot_nt(qis, kidx_n)), -jnp.inf)], axis=1)

    def search_step(it, score, lo, cnt_lo):
        cand = lo + jnp.int32(INT_MIN if it == 0 else 1 << (31 - it))
        cnt = _count(score >= _key_to_float(cand))
        take = cnt >= kf
        return jnp.where(take, cand, lo), jnp.where(take, cnt, cnt_lo)

    def attend_selected(i, score, lo, cnt_ge):
        few = lo == INT_MIN
        thr = _key_to_float(lo)
        has_tie = jnp.max(jnp.where((cnt_ge > kf) & (~few), 1.0, 0.0))

        def drop_ties():
            cols = lax.broadcasted_iota(I32, (tq, n_keys), 1)
            need = kf - _count(score > thr)
            eq = score == thr

            def cut_body(it, cpos):
                cand = cpos + lax.shift_left(jnp.int32(1), n_bits - 1 - it)
                cnt = _count(eq & (cols < cand))
                return jnp.where(cnt < need, cand, cpos)

            keep = lax.fori_loop(0, n_bits, cut_body, jnp.zeros((tq, 1), I32))
            return jnp.where(eq & (cols > keep) & (cnt_ge > kf) & (~few), -jnp.inf, score)

        kept = lax.cond(has_tie > 0.0, drop_ties, lambda: score)
        sel_g = jnp.concatenate([kept] * B_GROUP, axis=0) >= jnp.concatenate([thr] * B_GROUP, axis=0)

        k_t = ckT_ref[i].astype(BF16)
        v_t = cvT_ref[i].astype(BF16)
        k_n = padrows(kbn_ref[i])
        v_n = padrows(vbn_ref[i])
        for g in range(B_KV_HEADS):
            feats = slice(g * B_HEAD_DIM, (g + 1) * B_HEAD_DIM)
            heads = range(g * B_GROUP, (g + 1) * B_GROUP)
            qg = jnp.concatenate([qb_ref[i, :, h * B_HEAD_DIM:(h + 1) * B_HEAD_DIM] for h in heads], axis=0)
            near = jnp.concatenate([bias_ref[0, h] - bias_ref[1, h] for h in heads], axis=0)
            sg = _dot(qg, k_t[feats])
            sg = jnp.concatenate([sg[:, :n_far], sg[:, n_far:] + near[:, :past - n_far],
                                  _dot_nt(qg, k_n[:, feats]) + near[:, past - n_far:]], axis=1)
            pb, l = softmax(jnp.where(sel_g, sg, NEG_BIG))
            og = (_dot_nt(pb[:, :past], v_t[feats]) + _dot(pb[:, past:], v_n[:, feats])) / l
            for hh, h in enumerate(heads):
                ob_ref[i, :, h * B_HEAD_DIM:(h + 1) * B_HEAD_DIM] = og[hh * tq:(hh + 1) * tq]

    streams = range(SAMPLE_STREAMS)
    scores = [mla_and_scores(i) for i in streams]
    state = [(jnp.full((tq, 1), INT_MIN, I32), jnp.full((tq, 1), 1e9, F32)) for _ in streams]
    for it in range(32):
        state = [search_step(it, scores[i], *state[i]) for i in streams]
    for i in streams:
        attend_selected(i, scores[i], *state[i])


def _sample_attention(pr, caches, bias_s, b, t_new, past, n_top):
    r3 = lambda a: a.reshape(b, t_new, a.shape[-1])
    ns = SAMPLE_STREAMS
    assert b % ns == 0
    per_b = lambda n, c: pl.BlockSpec((ns, n, c), lambda bi: (bi, 0, 0))
    news = [pr["qcat"], pr["qb"], pr["qidx"], pr["widx"], pr["kcat"], pr["kb16"], pr["vb16"], pr["kidx16"]]
    olat, ob = pl.pallas_call(
        functools.partial(_sample_attn_kernel, n_top=n_top, t_new=t_new, past=past),
        grid=(b // ns,),
        in_specs=[per_b(t_new, a.shape[-1]) for a in news]
                 + [per_b(c.shape[1], c.shape[2]) for c in caches]
                 + [pl.BlockSpec(bias_s.shape, lambda bi: (0, 0, 0, 0))],
        out_specs=[per_b(t_new, A_HEADS * A_KV_LORA), per_b(t_new, B_WIDTH)],
        out_shape=[jax.ShapeDtypeStruct((b, t_new, A_HEADS * A_KV_LORA), BF16),
                   jax.ShapeDtypeStruct((b, t_new, B_WIDTH), F32)],
        compiler_params=pltpu.CompilerParams(dimension_semantics=("arbitrary",),
                                             vmem_limit_bytes=VMEM_LIMIT_BYTES),
        name="sample_attention",
    )(*[r3(a) for a in news], *caches, bias_s)
    return olat.reshape(b * t_new, -1), ob.reshape(b * t_new, -1)


def _combine_kernel(x_ref, olat_ref, ob_ref, sga_ref, sgb_ref, wuv_ref, wout_ref, fg_ref, y_ref, *, final):
    o_a = _dot(olat_ref[...], wuv_ref[...])
    mix = jnp.concatenate([o_a * sga_ref[...], ob_ref[...] * sgb_ref[...]], axis=1)
    y = x_ref[...] + _dot(mix.astype(BF16), wout_ref[...])
    if final:
        y = _rms(y, fg_ref[...])
    y_ref[...] = y


def _combine(x2d, olat, ob, pr, lw, fg, final):
    n, d = x2d.shape
    tm = PROJ_TM
    tok = lambda c: pl.BlockSpec((tm, c), lambda i: (i, 0))
    full = lambda a: pl.BlockSpec(a.shape, lambda i: (0,) * a.ndim)
    return pl.pallas_call(
        functools.partial(_combine_kernel, final=final),
        grid=(n // tm,),
        in_specs=[tok(d), tok(olat.shape[1]), tok(ob.shape[1]), tok(A_WIDTH), tok(B_WIDTH),
                  full(lw["wuv"]), full(lw["wout"]), full(fg)],
        out_specs=tok(d),
        out_shape=jax.ShapeDtypeStruct((n, d), F32),
        compiler_params=pltpu.CompilerParams(dimension_semantics=("arbitrary",),
                                             vmem_limit_bytes=VMEM_LIMIT_BYTES),
        name="combine",
    )(x2d, olat, ob, pr["sga"], pr["sgb"], lw["wuv"], lw["wout"], fg)


def _combine_t_kernel(x_ref, olatT_ref, obT_ref, sgaT_ref, sgbT_ref, wuvt_ref, wout_ref, fg_ref, y_ref, *, final):
    o_a = jnp.concatenate([_dot(wuvt_ref[h], olatT_ref[h * A_KV_LORA:(h + 1) * A_KV_LORA, :])
                           for h in range(A_HEADS)], axis=0)
    mix_t = jnp.concatenate([o_a * sgaT_ref[...].astype(F32),
                             obT_ref[...].astype(F32) * sgbT_ref[...].astype(F32)], axis=0).astype(BF16)
    y = x_ref[...] + lax.dot_general(mix_t, wout_ref[...], (((0,), (0,)), ((), ())),
                                     preferred_element_type=F32)
    if final:
        y = _rms(y, fg_ref[...])
    y_ref[...] = y


def _combine_t(x3d, olat_t, ob_t, pr, lw, fg, final):
    b, t, d = x3d.shape
    tm = COMBINE_TM
    rows_t = lambda r: pl.BlockSpec((None, r, tm), lambda bi, ti: (bi, 0, ti))
    full = lambda a: pl.BlockSpec(a.shape, lambda bi, ti: (0,) * a.ndim)
    xblk = pl.BlockSpec((None, tm, d), lambda bi, ti: (bi, ti, 0))
    return pl.pallas_call(
        functools.partial(_combine_t_kernel, final=final),
        grid=(b, t // tm),
        in_specs=[xblk, rows_t(A_HEADS * A_KV_LORA), rows_t(B_WIDTH), rows_t(A_WIDTH), rows_t(B_WIDTH),
                  full(lw["wuvt"]), full(lw["wout"]), full(fg)],
        out_specs=xblk,
        out_shape=jax.ShapeDtypeStruct((b, t, d), F32),
        compiler_params=pltpu.CompilerParams(dimension_semantics=("arbitrary", "arbitrary"),
                                             vmem_limit_bytes=VMEM_LIMIT_BYTES),
        name="combine_prompt",
    )(x3d, olat_t, ob_t, pr["sgaT"], pr["sgbT"], lw["wuvt"], lw["wout"], fg)


def _layer_weights(norm_g, w_in, q_norm_g, kv_norm_g, w_uq, w_uk, w_uv, w_out):
    d = w_in.shape[0]
    o = np.cumsum([0, A_Q_LORA, A_KV_LORA, A_ROPE, A_WIDTH, B_WIDTH, B_KV_HEADS * B_HEAD_DIM,
                   B_KV_HEADS * B_HEAD_DIM, IDX_HEADS * IDX_DIM, IDX_DIM, IDX_HEADS, B_WIDTH])
    w_t = w_in.T.astype(BF16)
    seg = lambda i: w_t[int(o[i]):int(o[i + 1])]
    zpad = lambda r: jnp.zeros((r, d), BF16)
    win = jnp.concatenate([seg(0), seg(1), seg(2), seg(8), seg(9), zpad(LANES - A_ROPE - IDX_DIM - IDX_HEADS),
                           seg(3), seg(4), seg(5), seg(6), seg(7), seg(10)], axis=0)
    assert win.shape[0] == IN_PAD
    half = A_ROPE // 2
    wq = jnp.concatenate([
        w_uq[:, :, :A_NOPE].reshape(A_Q_LORA, A_HEADS * A_NOPE),
        w_uq[:, :, A_NOPE:A_NOPE + half].reshape(A_Q_LORA, A_HEADS * half),
        w_uq[:, :, A_NOPE + half:].reshape(A_Q_LORA, A_HEADS * half)], axis=1)
    eye = jnp.eye(A_HEADS, dtype=w_uk.dtype)
    wuk = jnp.einsum('chn,hg->hngc', w_uk, eye).reshape(A_HEADS * A_NOPE, A_HEADS * A_KV_LORA)
    wuv = jnp.einsum('chv,hg->hcgv', w_uv, eye).reshape(A_HEADS * A_KV_LORA, A_HEADS * A_V)
    pm = np.zeros((2 * LANES, A_HEADS * LANES), np.float32)
    for h in range(A_HEADS):
        for i in range(half):
            pm[h * half + i, h * LANES + i] = 1.0
            pm[LANES + h * half + i, h * LANES + half + i] = 1.0
    wt = jnp.concatenate([seg(0), seg(1), seg(6), seg(4), seg(7), seg(3), seg(10), seg(9),
                          zpad(R_KB - R_WI - IDX_HEADS), seg(5), seg(8), seg(2)], axis=0)
    assert wt.shape[0] == ROWS_T
    bc = lambda g: jnp.broadcast_to(g.reshape(-1, 1), (g.shape[0], PROJ_T_TM))
    return {
        "wt": wt, "qngc": bc(q_norm_g), "kvngc": bc(kv_norm_g),
        "wqt": wq.T.astype(BF16), "wukt": wuk.T.astype(BF16),
        "wuvt": w_uv.transpose(1, 2, 0).astype(BF16),
        "ng": norm_g.reshape(1, -1), "win": win,
        "qng": q_norm_g.reshape(1, -1), "kvng": kv_norm_g.reshape(1, -1),
        "wq": wq.astype(BF16), "wuk": wuk.astype(BF16), "wuv": wuv.astype(BF16),
        "pmat": jnp.asarray(pm, BF16), "wout": w_out.astype(BF16),
    }


def _rope_tables(pos):
    half = A_ROPE // 2
    inv = ROPE_THETA ** (-jnp.arange(half, dtype=F32) / half)
    ang = pos.astype(F32)[:, None] * inv[None, :]
    cos, sin = jnp.cos(ang), jnp.sin(ang)
    z = jnp.zeros((pos.shape[0], LANES - A_ROPE), F32)
    zh = jnp.zeros_like(cos)
    cosq = jnp.tile(cos, (1, A_HEADS))
    sinq = jnp.tile(sin, (1, A_HEADS))
    rc = jnp.concatenate([cos, cos, z], axis=1)
    rs1 = jnp.concatenate([zh, sin, z], axis=1)
    rs2 = jnp.concatenate([-sin, zh, z], axis=1)
    return cosq, sinq, rc, rs1, rs2


def _rope_tables_t(pos):
    half = A_ROPE // 2
    inv = ROPE_THETA ** (-jnp.arange(half, dtype=F32) / half)
    ang = pos.astype(F32)[:, None] * inv[None, :]
    return jnp.tile(jnp.cos(ang).T, (A_HEADS, 1)), jnp.tile(jnp.sin(ang).T, (A_HEADS, 1))


def kernel(x_prompt, x_sample, cache_mla_ckv, cache_mla_kpe, cache_dsa_k, cache_dsa_v, cache_dsa_kidx,
           norm_g, w_in, mla_q_norm_g, mla_kv_norm_g, mla_w_uq, mla_w_uk, mla_w_uv, rel_bias, w_out,
           final_norm_g):
    bp, tp, d = x_prompt.shape
    bs, ts, _ = x_sample.shape
    depth = w_in.shape[0]
    past = cache_mla_ckv.shape[2]
    n_top_p = min(TOP_K_MAX, tp // 4)
    n_top_s = min(TOP_K_MAX, (past + ts) // 4)
    assert ts <= CHUNK and past % CHUNK == 0 and past % LANES == 0

    rope_p = _rope_tables_t(jnp.arange(tp, dtype=jnp.int32))
    reps = PROJ_TM // ts
    rope_s = tuple(jnp.tile(a, (reps, 1)) for a in _rope_tables(past + jnp.arange(ts, dtype=jnp.int32)))

    bias_p = _bias_tables(rel_bias, (0, -TQ, -3 * TQ), TQ, TQ, True)
    win_s = 2 * LANES
    bias_s = _bias_tables(rel_bias, (-(win_s - LANES), -(past + win_s)), ts, win_s, False)
    fg = final_norm_g.reshape(1, -1)

    xp = x_prompt
    xs = x_sample.reshape(bs * ts, d)
    outs_p, outs_s = [], []
    for l in range(depth):
        lw = _layer_weights(norm_g[l], w_in[l], mla_q_norm_g[l], mla_kv_norm_g[l],
                            mla_w_uq[l], mla_w_uk[l], mla_w_uv[l], w_out[l])
        final = l == depth - 1
        pr = _project_t(xp.reshape(bp * tp, d), rope_p, lw, bp, tp)
        olat, ob = _prompt_attention(pr, bias_p, bp, tp, n_top_p)
        xp = _combine_t(xp, olat, ob, pr, lw, fg, final)
        heads_t = lambda a: a.reshape(bp, B_KV_HEADS, B_HEAD_DIM, tp).transpose(0, 3, 1, 2)
        outs_p.append((pr["ckv"].reshape(bp, tp, A_KV_LORA), pr["kpeT"].transpose(0, 2, 1),
                       heads_t(pr["kbT"]), heads_t(pr["vbT"]), pr["kidxT"].transpose(0, 2, 1)))
        ps = _project(xs, rope_s, lw, period=PROJ_TM)
        feat_t = lambda a: a.transpose(0, 2, 3, 1).reshape(bs, B_KV_HEADS * B_HEAD_DIM, past)
        caches = (cache_mla_ckv[l], cache_mla_kpe[l].transpose(0, 2, 1), feat_t(cache_dsa_k[l]),
                  feat_t(cache_dsa_v[l]), cache_dsa_kidx[l].transpose(0, 2, 1))
        olat, ob = _sample_attention(ps, caches, bias_s, bs, ts, past, n_top_s)
        xs = _combine(xs, olat, ob, ps, lw, fg, final)
        outs_s.append((ps["ckv"].reshape(bs, ts, A_KV_LORA), ps["kpe"].reshape(bs, ts, A_ROPE),
                       ps["kb"].reshape(bs, ts, B_KV_HEADS, B_HEAD_DIM),
                       ps["vb"].reshape(bs, ts, B_KV_HEADS, B_HEAD_DIM),
                       ps["kidx"].reshape(bs, ts, IDX_DIM)))

    stack = lambda outs, i: jnp.stack([o[i] for o in outs])
    return ((xp, xs.reshape(bs, ts, d))
            + tuple(stack(outs_p, i) for i in range(5)) + tuple(stack(outs_s, i) for i in range(5)))
```

```python
import functools
import math

import jax
import jax.numpy as jnp
import numpy as np
from jax import lax
from jax.experimental import pallas as pl
from jax.experimental.pallas import tpu as pltpu

F32 = jnp.float32
BF16 = jnp.bfloat16
I32 = jnp.int32

CHUNK = 64
EPS = 1e-6
A_HEADS = 8
A_NOPE = 64
A_ROPE = 32
A_V = 64
A_Q_LORA = 256
A_KV_LORA = 128
ROPE_THETA = 10000.0
A_WIDTH = A_HEADS * A_V
B_HEADS = 8
B_KV_HEADS = 2
B_HEAD_DIM = 64
B_WIDTH = B_HEADS * B_HEAD_DIM
B_GROUP = B_HEADS // B_KV_HEADS
IDX_HEADS = 8
IDX_DIM = 64
TOP_K_MAX = 256
N_BUCKETS = 32
MAX_DISTANCE = 128

LANES = 128
VMEM_LIMIT_BYTES = 56 * 1024 * 1024

LOG2E = 1.4426950408889634
NEG_BIG = -1e30
INT_MIN = -(2 ** 31)

C_CQ = 0
C_CKV = C_CQ + A_Q_LORA
C_MISC = C_CKV + A_KV_LORA
C_GA = C_MISC + LANES
C_QB = C_GA + A_WIDTH
C_KB = C_QB + B_WIDTH
C_VB = C_KB + B_KV_HEADS * B_HEAD_DIM
C_QI = C_VB + B_KV_HEADS * B_HEAD_DIM
C_GB = C_QI + IDX_HEADS * IDX_DIM
IN_PAD = C_GB + B_WIDTH
M_KPE = 0
M_KIDX = A_ROPE
M_WIDX = A_ROPE + IDX_DIM

QCAT = 2 * LANES
TQ = 256
PROJ_TM = 256
COMBINE_TM = 1024
PROJ_T_TM = 1024


def _dot(a, b):
    return jnp.dot(a, b, preferred_element_type=F32)


def _dot_nt(a, b):
    return lax.dot_general(a, b, (((1,), (1,)), ((), ())), preferred_element_type=F32)


def _rms(x, g):
    return x * lax.rsqrt(jnp.mean(x * x, axis=-1, keepdims=True) + EPS) * g


def _bias_kernel(rb_ref, out_ref, *, offsets, keys_on_rows):
    nb = N_BUCKETS // 2
    max_exact = nb // 2
    n_r = out_ref.shape[1] if keys_on_rows else out_ref.shape[2]
    n_c = out_ref.shape[2] // B_HEADS if keys_on_rows else out_ref.shape[3]
    row = lax.broadcasted_iota(I32, (n_r, n_c), 0)
    col = lax.broadcasted_iota(I32, (n_r, n_c), 1)

    def bucket_of(off):
        rel = off + (row - col if keys_on_rows else col - row)
        ret = jnp.where(rel > 0, nb, 0)
        n = jnp.abs(rel)
        nf = jnp.maximum(n, 1).astype(F32)
        large = max_exact + (jnp.log(nf / max_exact) / math.log(MAX_DISTANCE / max_exact)
                             * (nb - max_exact)).astype(I32)
        large = jnp.minimum(large, nb - 1)
        return ret + jnp.where(n < max_exact, n, large)

    def lookup(bucket, h):
        table = jnp.broadcast_to(rb_ref[h:h + 1, :], (n_r, LANES))
        return jnp.concatenate(
            [jnp.take_along_axis(table, bucket[:, c:c + LANES], axis=1) for c in range(0, n_c, LANES)], axis=1)

    buckets = [bucket_of(off) for off in offsets]
    for h in range(B_HEADS):
        if keys_on_rows:
            far = lookup(buckets[-1], h)
            for p in range(len(offsets) - 1):
                out_ref[p, :, h * n_c:(h + 1) * n_c] = (lookup(buckets[p], h) - far) * LOG2E
        else:
            for p in range(len(offsets)):
                out_ref[p, h] = lookup(buckets[p], h) * LOG2E


def _bias_tables(rel_bias, offsets, n_r, n_c, keys_on_rows):
    shape = (len(offsets) - 1, n_r, B_HEADS * n_c) if keys_on_rows else (len(offsets), B_HEADS, n_r, n_c)
    return pl.pallas_call(
        functools.partial(_bias_kernel, offsets=tuple(offsets), keys_on_rows=keys_on_rows),
        out_shape=jax.ShapeDtypeStruct(shape, F32),
        in_specs=[pl.BlockSpec(memory_space=pltpu.VMEM)],
        out_specs=pl.BlockSpec(memory_space=pltpu.VMEM),
        compiler_params=pltpu.CompilerParams(vmem_limit_bytes=VMEM_LIMIT_BYTES),
        name="bias_tables",
    )(jnp.pad(rel_bias.T, ((0, 0), (0, LANES - N_BUCKETS))))


def _proj_kernel(x_ref, ng_ref, win_ref, qng_ref, kvng_ref, wq_ref, wuk_ref, pmat_ref,
                 cosq_ref, sinq_ref, rc_ref, rs1_ref, rs2_ref,
                 ckv_ref, kpe_ref, kb_ref, vb_ref, kidx_ref,
                 kcat_ref, kb16_ref, vb16_ref, kidx16_ref,
                 qcat_ref, qb_ref, qidx_ref, widx_ref, sga_ref, sgb_ref):
    x = x_ref[...]
    h = _rms(x, ng_ref[...])
    z = _dot_nt(h.astype(BF16), win_ref[...])

    cq = _rms(z[:, C_CQ:C_CQ + A_Q_LORA], qng_ref[...])
    q = _dot(cq.astype(BF16), wq_ref[...])
    n_nope = A_HEADS * A_NOPE
    x1 = q[:, n_nope:n_nope + LANES]
    x2 = q[:, n_nope + LANES:n_nope + 2 * LANES]
    cos8, sin8 = cosq_ref[...], sinq_ref[...]
    o1 = x1 * cos8 - x2 * sin8
    o2 = x1 * sin8 + x2 * cos8
    mla_scale = (A_NOPE + A_ROPE) ** -0.5 * LOG2E
    q_lat = _dot(q[:, :n_nope].astype(BF16), wuk_ref[...]) * mla_scale
    pe = jnp.concatenate([o1, o2], axis=1) * mla_scale
    q_pe = _dot(pe.astype(BF16), pmat_ref[...])
    for hh in range(A_HEADS):
        qcat_ref[:, hh * QCAT:hh * QCAT + LANES] = q_lat[:, hh * LANES:(hh + 1) * LANES].astype(BF16)
        qcat_ref[:, hh * QCAT + LANES:(hh + 1) * QCAT] = q_pe[:, hh * LANES:(hh + 1) * LANES].astype(BF16)

    ckv = _rms(z[:, C_CKV:C_CKV + A_KV_LORA], kvng_ref[...])
    ckv_ref[...] = ckv
    misc = z[:, C_MISC:C_MISC + LANES]
    rot = (misc * rc_ref[...] + pltpu.roll(misc, A_ROPE // 2, 1) * rs1_ref[...]
           + pltpu.roll(misc, LANES - A_ROPE // 2, 1) * rs2_ref[...])
    kpe_ref[...] = rot[:, :A_ROPE]
    kcat_ref[:, :LANES] = ckv.astype(BF16)
    kcat_ref[:, LANES:] = rot.astype(BF16)

    kidx = misc[:, M_KIDX:M_KIDX + IDX_DIM]
    kidx_ref[...] = kidx
    kidx16_ref[...] = kidx.astype(BF16)
    widx_ref[...] = misc[:, M_WIDX:M_WIDX + IDX_HEADS] * (IDX_HEADS ** -0.5)
    kb = z[:, C_KB:C_KB + LANES]
    vb = z[:, C_VB:C_VB + LANES]
    kb_ref[...] = kb
    vb_ref[...] = vb
    kb16_ref[...] = kb.astype(BF16)
    vb16_ref[...] = vb.astype(BF16)
    qb_ref[...] = (z[:, C_QB:C_QB + B_WIDTH] * (B_HEAD_DIM ** -0.5 * LOG2E)).astype(BF16)
    qidx_ref[...] = (z[:, C_QI:C_QI + IDX_HEADS * IDX_DIM] * (IDX_DIM ** -0.5)).astype(BF16)
    sga_ref[...] = jax.nn.silu(z[:, C_GA:C_GA + A_WIDTH])
    sgb_ref[...] = jax.nn.silu(z[:, C_GB:C_GB + B_WIDTH])


def _project(x2d, rope_tabs, lw, *, period):
    n, d = x2d.shape
    tm = PROJ_TM
    assert n % tm == 0 and period % tm == 0
    n_rep = period // tm
    tok = lambda c: pl.BlockSpec((tm, c), lambda i: (i, 0))
    full = lambda a: pl.BlockSpec(a.shape, lambda i: (0,) * a.ndim)
    tab = pl.BlockSpec((tm, LANES), lambda i: (i % n_rep, 0))
    outs = [
        ("ckv", A_KV_LORA, F32), ("kpe", A_ROPE, F32), ("kb", LANES, F32), ("vb", LANES, F32),
        ("kidx", IDX_DIM, F32),
        ("kcat", QCAT, BF16), ("kb16", LANES, BF16), ("vb16", LANES, BF16), ("kidx16", IDX_DIM, BF16),
        ("qcat", A_HEADS * QCAT, BF16), ("qb", B_WIDTH, BF16), ("qidx", IDX_HEADS * IDX_DIM, BF16),
        ("widx", IDX_HEADS, F32), ("sga", A_WIDTH, F32), ("sgb", B_WIDTH, F32),
    ]
    res = pl.pallas_call(
        _proj_kernel,
        grid=(n // tm,),
        in_specs=[tok(d), full(lw["ng"]), full(lw["win"]), full(lw["qng"]), full(lw["kvng"]),
                  full(lw["wq"]), full(lw["wuk"]), full(lw["pmat"]), tab, tab, tab, tab, tab],
        out_specs=[tok(c) for _, c, _ in outs],
        out_shape=[jax.ShapeDtypeStruct((n, c), dt) for _, c, dt in outs],
        compiler_params=pltpu.CompilerParams(dimension_semantics=("arbitrary",),
                                             vmem_limit_bytes=VMEM_LIMIT_BYTES),
        name="project",
    )(x2d, lw["ng"], lw["win"], lw["qng"], lw["kvng"], lw["wq"], lw["wuk"], lw["pmat"], *rope_tabs)
    return {name: r for (name, _, _), r in zip(outs, res)}


R_CQ = 0
R_CKV = R_CQ + A_Q_LORA
R_VB = R_CKV + A_KV_LORA
R_QB = R_VB + LANES
R_QI = R_QB + B_WIDTH
R_GA = R_QI + IDX_HEADS * IDX_DIM
R_GB = R_GA + A_WIDTH
R_WI = R_GB + B_WIDTH
R_KB = R_WI + 16
R_KI = R_KB + LANES
R_KPE = R_KI + IDX_DIM
ROWS_T = R_KPE + A_ROPE
MK_KPE = IDX_DIM
ONES_ROWS = 16
V_EXT = A_KV_LORA + ONES_ROWS


def _proj_t_kernel(x_ref, ng_ref, wt_ref, qng_ref, kvngc_ref, wqt_ref, wukt_ref, cost_ref, sint_ref,
                   ckv_ref, kcat_ref, kb16_ref, kidx16_ref,
                   kpeT_ref, kbT_ref, vbT_ref, kidxT_ref,
                   qa_ref, qbp_ref, qip_ref, widxT_ref, vmlaT_ref, vdsaT_ref, sgaT_ref, sgbT_ref):
    x = x_ref[...]
    tm = x.shape[0]
    hb = _rms(x, ng_ref[...]).astype(BF16)

    zt = _dot_nt(wt_ref[...], hb)

    def rms_t(c, g):
        return c * lax.rsqrt(jnp.mean(c * c, axis=0, keepdims=True) + EPS) * g

    cq = rms_t(zt[R_CQ:R_CQ + A_Q_LORA], qng_ref[...])
    qt = _dot(wqt_ref[...], cq.astype(BF16))
    n_nope = A_HEADS * A_NOPE
    x1 = qt[n_nope:n_nope + LANES]
    x2 = qt[n_nope + LANES:n_nope + 2 * LANES]
    cos8, sin8 = cost_ref[...], sint_ref[...]
    mla_scale = (A_NOPE + A_ROPE) ** -0.5 * LOG2E
    o1 = (x1 * cos8 - x2 * sin8) * mla_scale
    o2 = (x1 * sin8 + x2 * cos8) * mla_scale
    q_lat = _dot(wukt_ref[...], qt[:n_nope].astype(BF16)) * mla_scale
    half = A_ROPE // 2
    qb_t = (zt[R_QB:R_QB + B_WIDTH] * (B_HEAD_DIM ** -0.5 * LOG2E)).astype(BF16)
    qi_t = (zt[R_QI:R_QI + IDX_HEADS * IDX_DIM] * (IDX_DIM ** -0.5)).astype(BF16)
    for c in range(tm // TQ):
        blk = slice(c * TQ, (c + 1) * TQ)
        qbp_ref[c] = jnp.zeros(qbp_ref.shape[1:], BF16)
        qip_ref[c] = jnp.zeros(qip_ref.shape[1:], BF16)
        for h in range(A_HEADS):
            g = h // B_GROUP
            lanes = slice(h * TQ, (h + 1) * TQ)
            qa_ref[c, :LANES, lanes] = q_lat[h * LANES:(h + 1) * LANES, blk].astype(BF16)
            qa_ref[c, LANES:LANES + half, lanes] = o1[h * half:(h + 1) * half, blk].astype(BF16)
            qa_ref[c, LANES + half:LANES + A_ROPE, lanes] = o2[h * half:(h + 1) * half, blk].astype(BF16)
            qa_ref[c, LANES + A_ROPE:, lanes] = jnp.zeros((QCAT - LANES - A_ROPE, TQ), BF16)
            qbp_ref[c, g * B_HEAD_DIM:(g + 1) * B_HEAD_DIM, lanes] = qb_t[h * B_HEAD_DIM:(h + 1) * B_HEAD_DIM, blk]
            qip_ref[c, :IDX_DIM, lanes] = qi_t[h * IDX_DIM:(h + 1) * IDX_DIM, blk]
    ones = jnp.ones((ONES_ROWS, TQ), BF16)
    ckv_t = rms_t(zt[R_CKV:R_CKV + A_KV_LORA], kvngc_ref[...])
    vb_t = zt[R_VB:R_VB + LANES]
    for c in range(tm // TQ):
        blk = slice(c * TQ, (c + 1) * TQ)
        vmlaT_ref[c, :A_KV_LORA, :] = ckv_t[:, blk].astype(BF16)
        vmlaT_ref[c, A_KV_LORA:, :] = ones
        vdsaT_ref[c, :LANES, :] = vb_t[:, blk].astype(BF16)
        vdsaT_ref[c, LANES:, :] = ones
    kb_t = zt[R_KB:R_KB + LANES]
    kidx_t = zt[R_KI:R_KI + IDX_DIM]
    k1, k2 = zt[R_KPE:R_KPE + half], zt[R_KPE + half:R_KPE + A_ROPE]
    cos1, sin1 = cos8[:half], sin8[:half]
    kpe_t = jnp.concatenate([k1 * cos1 - k2 * sin1, k1 * sin1 + k2 * cos1], axis=0)
    vbT_ref[...] = vb_t
    kbT_ref[...] = kb_t
    kidxT_ref[...] = kidx_t
    kpeT_ref[...] = kpe_t
    ckv = ckv_t.T
    misc = jnp.concatenate([kidx_t, kpe_t, jnp.zeros((LANES - IDX_DIM - A_ROPE, tm), F32)], axis=0).T
    lane = lax.broadcasted_iota(I32, (tm, LANES), 1)
    ckv_ref[...] = ckv
    kcat_ref[:, :LANES] = ckv.astype(BF16)
    kcat_ref[:, LANES:] = pltpu.roll(
        jnp.where((lane >= MK_KPE) & (lane < MK_KPE + A_ROPE), misc, 0.0), LANES - MK_KPE, 1).astype(BF16)
    kidx16_ref[...] = jnp.where(lane < IDX_DIM, misc, 0.0).astype(BF16)
    kb16_ref[...] = kb_t.T.astype(BF16)
    sgaT_ref[...] = jax.nn.silu(zt[R_GA:R_GA + A_WIDTH]).astype(BF16)
    sgbT_ref[...] = jax.nn.silu(zt[R_GB:R_GB + B_WIDTH]).astype(BF16)
    widxT_ref[...] = zt[R_WI:R_WI + IDX_HEADS] * (IDX_HEADS ** -0.5)


def _project_t(x2d, tabs, lw, b, t):
    n, d = x2d.shape
    tm = PROJ_T_TM
    assert t % tm == 0 and tm % TQ == 0
    nt = t // tm
    tok = lambda c: pl.BlockSpec((tm, c), lambda i: (i, 0))
    full = lambda a: pl.BlockSpec(a.shape, lambda i: (0,) * a.ndim)
    vblk = pl.BlockSpec((None, tm // TQ, V_EXT, TQ), lambda i: (i // nt, i % nt, 0, 0))
    qslab = lambda r: pl.BlockSpec((None, tm // TQ, r, A_HEADS * TQ), lambda i: (i // nt, i % nt, 0, 0))
    tab_t = pl.BlockSpec((LANES, tm), lambda i: (0, i % nt))
    rows_t = lambda r: pl.BlockSpec((None, r, tm), lambda i: (i // nt, 0, i % nt))
    outs = [
        ("ckv", (n, A_KV_LORA), F32, tok(A_KV_LORA)),
        ("kcat", (n, QCAT), BF16, tok(QCAT)), ("kb16", (n, LANES), BF16, tok(LANES)),
        ("kidx16", (n, LANES), BF16, tok(LANES)),
        ("kpeT", (b, A_ROPE, t), F32, rows_t(A_ROPE)), ("kbT", (b, LANES, t), F32, rows_t(LANES)),
        ("vbT", (b, LANES, t), F32, rows_t(LANES)), ("kidxT", (b, IDX_DIM, t), F32, rows_t(IDX_DIM)),
        ("qa", (b, t // TQ, QCAT, A_HEADS * TQ), BF16, qslab(QCAT)),
        ("qbp", (b, t // TQ, LANES, B_HEADS * TQ), BF16, qslab(LANES)),
        ("qip", (b, t // TQ, LANES, IDX_HEADS * TQ), BF16, qslab(LANES)),
        ("widxT", (b, IDX_HEADS, t), F32, rows_t(IDX_HEADS)),
        ("vmlaT", (b, t // TQ, V_EXT, TQ), BF16, vblk), ("vdsaT", (b, t // TQ, V_EXT, TQ), BF16, vblk),
        ("sgaT", (b, A_WIDTH, t), BF16, rows_t(A_WIDTH)), ("sgbT", (b, B_WIDTH, t), BF16, rows_t(B_WIDTH)),
    ]
    cost, sint = tabs
    res = pl.pallas_call(
        _proj_t_kernel,
        grid=(n // tm,),
        in_specs=[tok(d), full(lw["ng"]), full(lw["wt"]), full(lw["qngc"]), full(lw["kvngc"]),
                  full(lw["wqt"]), full(lw["wukt"]), tab_t, tab_t],
        out_specs=[o[3] for o in outs],
        out_shape=[jax.ShapeDtypeStruct(o[1], o[2]) for o in outs],
        compiler_params=pltpu.CompilerParams(dimension_semantics=("arbitrary",),
                                             vmem_limit_bytes=VMEM_LIMIT_BYTES),
        name="project_prompt",
    )(x2d, lw["ng"], lw["wt"], lw["qngc"], lw["kvngc"], lw["wqt"], lw["wukt"], cost, sint)
    return {o[0]: r for o, r in zip(outs, res)}


NEG_FLT_MAX = -3.4028234663852886e38
KEY_NEG_FLT_MAX = INT_MIN + (1 << 23)


def _key_to_float(k):
    k = jnp.maximum(k, KEY_NEG_FLT_MAX)
    return pltpu.bitcast(k ^ ((k >> 31) & 0x7FFFFFFF), F32)


def _count(pred):
    return jnp.sum(jnp.where(pred, 1.0, 0.0), axis=1, keepdims=True)


SAMPLE_STREAMS = 2
MAX_TIE_SWEEPS = 8.0
SAFE_DENOM_MIN = 2.0 ** -90
SAFE_DENOM_MAX = 2.0 ** 40


def _fold8(x, op=jnp.add):
    parts = [x[i:i + 8] for i in range(0, x.shape[0], 8)]
    while len(parts) > 1:
        parts = [op(a, b) for a, b in zip(parts[::2], parts[1::2])]
    return parts[0]


def _prompt_attn_kernel(qa_ref, qbp_ref, qip_ref, widxT_ref, kcat_ref, kb_ref, kidx_ref,
                        vmlaT_ref, vdsaT_ref, bias_ref, olatT_ref, obT_ref,
                        sc_ref, m_ref, acc_ref, mb_ref, accb_ref, kn_ref, *, n_top):
    qi = pl.program_id(1)
    nblk = qi + 1
    tq = TQ
    krow = lax.broadcasted_iota(I32, (tq, tq), 0)
    qcol = lax.broadcasted_iota(I32, (tq, tq), 1)
    shift = CHUNK.bit_length() - 1
    diag_ok = (qcol >> shift) >= (krow >> shift)
    hcols = lambda h: slice(h * tq, (h + 1) * tq)

    def per_head(fn):
        return jnp.concatenate([fn(h) for h in range(A_HEADS)], axis=1)

    w_all = per_head(lambda h: widxT_ref[h:h + 1, :])

    @pl.when(qi == 0)
    def _():
        def max_row_norm2(k_ref):
            k = k_ref[...].astype(F32)
            return jnp.max(jnp.sum(k * k, axis=1, keepdims=True))
        kn_ref[0:1, :] = jnp.full((1, LANES), max_row_norm2(kcat_ref), F32)
        kn_ref[1:2, :] = jnp.full((1, LANES), max_row_norm2(kb_ref), F32)
        kn_ref[2:3, :] = jnp.full((1, LANES), jnp.max(bias_ref[...]), F32)

    def col_norm(q_ref):
        q = q_ref[...].astype(F32)
        return jnp.sqrt(jnp.sum(q * q, axis=0, keepdims=True))

    shift_a = jnp.sqrt(kn_ref[0:1, 0:1]) * col_norm(qa_ref)
    shift_b = jnp.sqrt(kn_ref[1:2, 0:1]) * col_norm(qbp_ref) + jnp.maximum(kn_ref[2:3, 0:1], 0.0)

    def accumulate(s_t, values, shift_or_m, acc_r, exact):
        if exact:
            m_prev = shift_or_m[0:1, :]
            m_new = jnp.maximum(m_prev, jnp.max(s_t, axis=0, keepdims=True))
            alpha = jnp.exp2(m_prev - m_new)
            shift_or_m[0:1, :] = m_new
        else:
            m_new = shift_or_m
        p_t = jnp.exp2(s_t - m_new).astype(BF16)
        for v_t, lanes in values:
            pv = _dot(v_t, p_t[:, lanes])
            acc_r[:, lanes] = (alpha[:, lanes] * acc_r[:, lanes] if exact else acc_r[:, lanes]) + pv

    def unsafe(l):
        return jnp.max(jnp.where((l >= SAFE_DENOM_MIN) & (l <= SAFE_DENOM_MAX), 0.0, 1.0)) > 0.0

    all_lanes = slice(0, A_HEADS * tq)

    def for_blocks(n, block):
        def pair(p, c):
            block(2 * p)
            block(2 * p + 1)
            return c

        lax.fori_loop(0, lax.shift_right_logical(n, 1), pair, 0)

        @pl.when((n & 1) == 1)
        def _():
            block(n - 1)

    def mla_pass(exact):
        acc_ref[...] = jnp.zeros(acc_ref.shape, F32)
        if exact:
            m_ref[...] = jnp.full(m_ref.shape, NEG_BIG, F32)

        def block(j, masked):
            start = pl.multiple_of(j * tq, tq)
            s_t = _dot(kcat_ref[pl.ds(start, tq), :], qa_ref[...])
            if masked:
                s_t = per_head(lambda h: jnp.where(diag_ok, s_t[:, hcols(h)], NEG_BIG))
            accumulate(s_t, [(vmlaT_ref[j], all_lanes)], m_ref if exact else shift_a, acc_ref, exact)
            if exact:
                return
            r = jnp.maximum(_dot(kidx_ref[pl.ds(start, tq), :], qip_ref[...]), 0.0) * w_all
            score = r[:, hcols(0)]
            for h in range(1, IDX_HEADS):
                score = score + r[:, hcols(h)]
            if masked:
                score = jnp.where(diag_ok, score, -jnp.inf)
            sc_ref[j] = score

        for_blocks(qi, lambda j: block(j, False))
        block(qi, True)

    mla_pass(False)

    @pl.when(unsafe(acc_ref[A_KV_LORA:A_KV_LORA + 1, :]))
    def _():
        mla_pass(True)

    o_t = acc_ref[:A_KV_LORA, :] * (1.0 / acc_ref[A_KV_LORA:A_KV_LORA + 1, :])
    for h in range(A_HEADS):
        olatT_ref[h * LANES:(h + 1) * LANES, :] = o_t[:, hcols(h)].astype(olatT_ref.dtype)

    def count(pred):
        def one(j, c):
            return c + _fold8(jnp.where(pred(sc_ref[j], j), 1.0, 0.0))
        part = lax.fori_loop(0, lax.shift_right_logical(nblk, 1), lambda p, c: one(2 * p + 1, one(2 * p, c)),
                             jnp.zeros((8, tq), F32))
        part = lax.cond((nblk & 1) == 1, lambda c: one(nblk - 1, c), lambda c: c, part)
        return jnp.sum(part, axis=0, keepdims=True)

    kf = float(n_top)

    def bis_body(it, carry):
        lo, cnt_lo = carry
        cand = lo + lax.shift_left(jnp.int32(1), 31 - it)
        cand_f = _key_to_float(cand)
        cnt = count(lambda s, j: s >= cand_f)
        take = cnt >= kf
        return jnp.where(take, cand, lo), jnp.where(take, cnt, cnt_lo)

    n_steps = jnp.where(nblk * tq <= n_top, 0, 32)
    lo, cnt_ge = lax.fori_loop(
        0, n_steps, bis_body, (jnp.full((1, tq), INT_MIN, I32), jnp.full((1, tq), 1e9, F32)))
    few = lo == INT_MIN
    thr = _key_to_float(lo)
    excess0 = jnp.where(few, 0.0, cnt_ge - kf)
    max_excess = jnp.max(excess0)

    def drop_from(cut):
        def body(j, c):
            s = sc_ref[j]
            sc_ref[j] = jnp.where((s == thr) & ((krow + j * tq) >= cut), -jnp.inf, s)
            return c
        lax.fori_loop(0, nblk, body, 0)

    @pl.when((max_excess > 0.0) & (max_excess <= MAX_TIE_SWEEPS))
    def _():
        def last_tie_below(cut):
            def body(j, m):
                pos = krow + j * tq
                hit = jnp.where((sc_ref[j] == thr) & (pos < cut), pos, -1)
                return jnp.maximum(m, _fold8(hit, jnp.maximum))
            part = lax.fori_loop(0, nblk, body, jnp.full((8, tq), -1, I32))
            return jnp.max(part, axis=0, keepdims=True)

        def sweep(c):
            excess, cut = c
            last = last_tie_below(cut)
            live = excess > 0.0
            return jnp.where(live, excess - 1.0, excess), jnp.where(live, last, cut)

        _, cut = lax.while_loop(lambda c: jnp.max(c[0]) > 0.0, sweep,
                                (excess0, jnp.full((1, tq), 2 ** 30, I32)))
        drop_from(cut)

    @pl.when(max_excess > MAX_TIE_SWEEPS)
    def _():
        need = kf - count(lambda s, j: s > thr)
        n_bits = (sc_ref.shape[0] * tq).bit_length()

        def cut_body(it, cpos):
            cand = cpos + lax.shift_left(jnp.int32(1), n_bits - 1 - it)
            cnt = count(lambda s, j: (s == thr) & ((krow + j * tq) < cand))
            return jnp.where(cnt < need, cand, cpos)

        keep = lax.fori_loop(0, n_bits, cut_body, jnp.zeros((1, tq), I32))
        drop_from(jnp.where(excess0 > 0.0, keep + 1, 2 ** 30))

    group_lanes = [slice(g * B_GROUP * tq, (g + 1) * B_GROUP * tq) for g in range(B_KV_HEADS)]

    def dsa_pass(exact):
        accb_ref[...] = jnp.zeros(accb_ref.shape, F32)
        if exact:
            mb_ref[...] = jnp.full(mb_ref.shape, NEG_BIG, F32)

        def block(j, near):
            start = pl.multiple_of(j * tq, tq)
            sel = sc_ref[j] >= thr
            s_t = _dot(kb_ref[pl.ds(start, tq), :], qbp_ref[...])
            if near:
                s_t = s_t + bias_ref[qi - j]
            s_t = per_head(lambda h: jnp.where(sel, s_t[:, hcols(h)], NEG_BIG))
            v_all = vdsaT_ref[j]
            values = [(jnp.concatenate([v_all[g * B_HEAD_DIM:(g + 1) * B_HEAD_DIM], v_all[LANES:]], axis=0),
                       group_lanes[g]) for g in range(B_KV_HEADS)]
            accumulate(s_t, values, mb_ref if exact else shift_b, accb_ref, exact)

        for_blocks(jnp.maximum(qi - 1, 0), lambda j: block(j, False))

        @pl.when(qi >= 1)
        def _():
            block(qi - 1, True)

        block(qi, True)

    dsa_pass(False)

    @pl.when(unsafe(accb_ref[B_HEAD_DIM:B_HEAD_DIM + 1, :]))
    def _():
        dsa_pass(True)

    inv_b = 1.0 / accb_ref[B_HEAD_DIM:B_HEAD_DIM + 1, :]
    for h in range(B_HEADS):
        obT_ref[h * B_HEAD_DIM:(h + 1) * B_HEAD_DIM, :] = (
            accb_ref[:B_HEAD_DIM, hcols(h)] * inv_b[:, hcols(h)]).astype(obT_ref.dtype)


def _prompt_attention(pr, bias_p, b, t, n_top):
    tq = TQ
    assert t % tq == 0
    nq = t // tq
    r3 = lambda a: a.reshape(b, t, a.shape[-1])
    qrows = lambda r: pl.BlockSpec((None, r, tq), lambda bi, qi: (bi, 0, qi))
    qslab = lambda r: pl.BlockSpec((None, None, r, A_HEADS * tq), lambda bi, qi: (bi, qi, 0, 0))
    kall =lambda c: pl.BlockSpec((None, t, c), lambda bi, qi: (bi, 0, 0))
    vall = pl.BlockSpec((None, nq, V_EXT, tq), lambda bi, qi: (bi, 0, 0, 0))
    olat, ob = pl.pallas_call(
        functools.partial(_prompt_attn_kernel, n_top=n_top),
        grid=(b, nq),
        in_specs=[qslab(QCAT), qslab(LANES), qslab(LANES), qrows(IDX_HEADS),
                  kall(QCAT), kall(LANES), kall(LANES), vall, vall,
                  pl.BlockSpec(bias_p.shape, lambda bi, qi: (0, 0, 0))],
        out_specs=[qrows(A_HEADS * A_KV_LORA), qrows(B_WIDTH)],
        out_shape=[jax.ShapeDtypeStruct((b, A_HEADS * A_KV_LORA, t), BF16),
                   jax.ShapeDtypeStruct((b, B_WIDTH, t), BF16)],
        scratch_shapes=[
            pltpu.VMEM((nq, tq, tq), F32),
            pltpu.VMEM((8, A_HEADS * tq), F32),
            pltpu.VMEM((V_EXT, A_HEADS * tq), F32),
            pltpu.VMEM((8, B_HEADS * tq), F32),
            pltpu.VMEM((B_HEAD_DIM + ONES_ROWS, B_HEADS * tq), F32),
            pltpu.VMEM((8, LANES), F32),
        ],
        compiler_params=pltpu.CompilerParams(dimension_semantics=("arbitrary", "arbitrary"),
                                             vmem_limit_bytes=VMEM_LIMIT_BYTES),
        name="prompt_attention",
    )(pr["qa"], pr["qbp"], pr["qip"], pr["widxT"],
      r3(pr["kcat"]), r3(pr["kb16"]), r3(pr["kidx16"]), pr["vmlaT"], pr["vdsaT"], bias_p)
    return olat, ob


def _sample_attn_kernel(qcat_ref, qb_ref, qidx_ref, widx_ref, kcatn_ref, kbn_ref, vbn_ref, kidxn_ref,
                        cckv_ref, ckpeT_ref, ckT_ref, cvT_ref, ckidxT_ref, bias_ref,
                        olat_ref, ob_ref, *, n_top, t_new, past):
    tq = t_new
    pad = LANES
    n_keys = past + pad
    kf = float(n_top)
    n_bits = n_keys.bit_length()
    n_far = n_keys - bias_ref.shape[-1]

    def padrows(a):
        return jnp.concatenate([a, jnp.zeros((pad - t_new, a.shape[1]), a.dtype)], axis=0)

    def softmax(s):
        m = jnp.max(s, axis=1, keepdims=True)
        p = jnp.exp2(s - m)
        return p.astype(BF16), jnp.sum(p, axis=1, keepdims=True)

    def new_cols(rows):
        return lax.broadcasted_iota(I32, (rows, pad), 1) < t_new

    def mla_and_scores(i):
        ckv_c = cckv_ref[i].astype(BF16)
        kpe_t = ckpeT_ref[i].astype(BF16)
        kcat_n = padrows(kcatn_ref[i])
        qs = jnp.concatenate([qcat_ref[i, :, h * QCAT:(h + 1) * QCAT] for h in range(A_HEADS)], axis=0)
        s_c = _dot_nt(qs[:, :A_KV_LORA], ckv_c) + _dot(qs[:, A_KV_LORA:A_KV_LORA + A_ROPE], kpe_t)
        s_n = jnp.where(new_cols(A_HEADS * tq), _dot_nt(qs, kcat_n), NEG_BIG)
        pb, l = softmax(jnp.concatenate([s_c, s_n], axis=1))
        o = (_dot(pb[:, :past], ckv_c) + _dot(pb[:, past:], kcat_n[:, :A_KV_LORA])) / l
        for h in range(A_HEADS):
            olat_ref[i, :, h * LANES:(h + 1) * LANES] = o[h * tq:(h + 1) * tq].astype(olat_ref.dtype)

        kidx_t = ckidxT_ref[i].astype(BF16)
        kidx_n = padrows(kidxn_ref[i])
        qis = jnp.concatenate(
            [qidx_ref[i, :, h * IDX_DIM:(h + 1) * IDX_DIM] for h in range(IDX_HEADS)], axis=0)

        def head_sum(dots):
            acc = jnp.maximum(dots[:tq], 0.0) * widx_ref[i, :, 0:1]
            for h in range(1, IDX_HEADS):
                acc = acc + jnp.maximum(dots[h * tq:(h + 1) * tq], 0.0) * widx_ref[i, :, h:h + 1]
            return acc

        return jnp.concatenate(
            [head_sum(_dot(qis, kidx_t)),
             jnp.where(new_cols(tq), head_sum(_dot_nt(qis, kidx_n)), -jnp.inf)], axis=1)

    def search_step(it, score, lo, cnt_lo):
        cand = lo + jnp.int32(INT_MIN if it == 0 else 1 << (31 - it))
        cnt = _count(score >= _key_to_float(cand))
        take = cnt >= kf
        return jnp.where(take, cand, lo), jnp.where(take, cnt, cnt_lo)

    def attend_selected(i, score, lo, cnt_ge):
        few = lo == INT_MIN
        thr = _key_to_float(lo)
        has_tie = jnp.max(jnp.where((cnt_ge > kf) & (~few), 1.0, 0.0))

        def drop_ties():
            cols = lax.broadcasted_iota(I32, (tq, n_keys), 1)
            need = kf - _count(score > thr)
            eq = score == thr

            def cut_body(it, cpos):
                cand = cpos + lax.shift_left(jnp.int32(1), n_bits - 1 - it)
                cnt = _count(eq & (cols < cand))
                return jnp.where(cnt < need, cand, cpos)

            keep = lax.fori_loop(0, n_bits, cut_body, jnp.zeros((tq, 1), I32))
            return jnp.where(eq & (cols > keep) & (cnt_ge > kf) & (~few), -jnp.inf, score)

        kept = lax.cond(has_tie > 0.0, drop_ties, lambda: score)
        sel_g = jnp.concatenate([kept] * B_GROUP, axis=0) >= jnp.concatenate([thr] * B_GROUP, axis=0)

        k_t = ckT_ref[i].astype(BF16)
        v_t = cvT_ref[i].astype(BF16)
        k_n = padrows(kbn_ref[i])
        v_n = padrows(vbn_ref[i])
        for g in range(B_KV_HEADS):
            feats = slice(g * B_HEAD_DIM, (g + 1) * B_HEAD_DIM)
            heads = range(g * B_GROUP, (g + 1) * B_GROUP)
            qg = jnp.concatenate([qb_ref[i, :, h * B_HEAD_DIM:(h + 1) * B_HEAD_DIM] for h in heads], axis=0)
            near = jnp.concatenate([bias_ref[0, h] - bias_ref[1, h] for h in heads], axis=0)
            sg = _dot(qg, k_t[feats])
            sg = jnp.concatenate([sg[:, :n_far], sg[:, n_far:] + near[:, :past - n_far],
                                  _dot_nt(qg, k_n[:, feats]) + near[:, past - n_far:]], axis=1)
            pb, l = softmax(jnp.where(sel_g, sg, NEG_BIG))
            og = (_dot_nt(pb[:, :past], v_t[feats]) + _dot(pb[:, past:], v_n[:, feats])) / l
            for hh, h in enumerate(heads):
                ob_ref[i, :, h * B_HEAD_DIM:(h + 1) * B_HEAD_DIM] = og[hh * tq:(hh + 1) * tq]

    streams = range(SAMPLE_STREAMS)
    scores = [mla_and_scores(i) for i in streams]
    state = [(jnp.full((tq, 1), INT_MIN, I32), jnp.full((tq, 1), 1e9, F32)) for _ in streams]
    for it in range(32):
        state = [search_step(it, scores[i], *state[i]) for i in streams]
    for i in streams:
        attend_selected(i, scores[i], *state[i])


def _sample_attention(pr, caches, bias_s, b, t_new, past, n_top):
    r3 = lambda a: a.reshape(b, t_new, a.shape[-1])
    ns = SAMPLE_STREAMS
    assert b % ns == 0
    per_b = lambda n, c: pl.BlockSpec((ns, n, c), lambda bi: (bi, 0, 0))
    news = [pr["qcat"], pr["qb"], pr["qidx"], pr["widx"], pr["kcat"], pr["kb16"], pr["vb16"], pr["kidx16"]]
    olat, ob = pl.pallas_call(
        functools.partial(_sample_attn_kernel, n_top=n_top, t_new=t_new, past=past),
        grid=(b // ns,),
        in_specs=[per_b(t_new, a.shape[-1]) for a in news]
                 + [per_b(c.shape[1], c.shape[2]) for c in caches]
                 + [pl.BlockSpec(bias_s.shape, lambda bi: (0, 0, 0, 0))],
        out_specs=[per_b(t_new, A_HEADS * A_KV_LORA), per_b(t_new, B_WIDTH)],
        out_shape=[jax.ShapeDtypeStruct((b, t_new, A_HEADS * A_KV_LORA), BF16),
                   jax.ShapeDtypeStruct((b, t_new, B_WIDTH), F32)],
        compiler_params=pltpu.CompilerParams(dimension_semantics=("arbitrary",),
                                             vmem_limit_bytes=VMEM_LIMIT_BYTES),
        name="sample_attention",
    )(*[r3(a) for a in news], *caches, bias_s)
    return olat.reshape(b * t_new, -1), ob.reshape(b * t_new, -1)


def _combine_kernel(x_ref, olat_ref, ob_ref, sga_ref, sgb_ref, wuv_ref, wout_ref, fg_ref, y_ref, *, final):
    o_a = _dot(olat_ref[...], wuv_ref[...])
    mix = jnp.concatenate([o_a * sga_ref[...], ob_ref[...] * sgb_ref[...]], axis=1)
    y = x_ref[...] + _dot(mix.astype(BF16), wout_ref[...])
    if final:
        y = _rms(y, fg_ref[...])
    y_ref[...] = y


def _combine(x2d, olat, ob, pr, lw, fg, final):
    n, d = x2d.shape
    tm = PROJ_TM
    tok = lambda c: pl.BlockSpec((tm, c), lambda i: (i, 0))
    full = lambda a: pl.BlockSpec(a.shape, lambda i: (0,) * a.ndim)
    return pl.pallas_call(
        functools.partial(_combine_kernel, final=final),
        grid=(n // tm,),
        in_specs=[tok(d), tok(olat.shape[1]), tok(ob.shape[1]), tok(A_WIDTH), tok(B_WIDTH),
                  full(lw["wuv"]), full(lw["wout"]), full(fg)],
        out_specs=tok(d),
        out_shape=jax.ShapeDtypeStruct((n, d), F32),
        compiler_params=pltpu.CompilerParams(dimension_semantics=("arbitrary",),
                                             vmem_limit_bytes=VMEM_LIMIT_BYTES),
        name="combine",
    )(x2d, olat, ob, pr["sga"], pr["sgb"], lw["wuv"], lw["wout"], fg)


def _combine_t_kernel(x_ref, olatT_ref, obT_ref, sgaT_ref, sgbT_ref, wuvt_ref, wout_ref, fg_ref, y_ref, *, final):
    o_a = jnp.concatenate([_dot(wuvt_ref[h], olatT_ref[h * A_KV_LORA:(h + 1) * A_KV_LORA, :])
                           for h in range(A_HEADS)], axis=0)
    mix_t = jnp.concatenate([o_a * sgaT_ref[...].astype(F32),
                             obT_ref[...].astype(F32) * sgbT_ref[...].astype(F32)], axis=0).astype(BF16)
    y = x_ref[...] + lax.dot_general(mix_t, wout_ref[...], (((0,), (0,)), ((), ())),
                                     preferred_element_type=F32)
    if final:
        y = _rms(y, fg_ref[...])
    y_ref[...] = y


def _combine_t(x3d, olat_t, ob_t, pr, lw, fg, final):
    b, t, d = x3d.shape
    tm = COMBINE_TM
    rows_t = lambda r: pl.BlockSpec((None, r, tm), lambda bi, ti: (bi, 0, ti))
    full = lambda a: pl.BlockSpec(a.shape, lambda bi, ti: (0,) * a.ndim)
    xblk = pl.BlockSpec((None, tm, d), lambda bi, ti: (bi, ti, 0))
    return pl.pallas_call(
        functools.partial(_combine_t_kernel, final=final),
        grid=(b, t // tm),
        in_specs=[xblk, rows_t(A_HEADS * A_KV_LORA), rows_t(B_WIDTH), rows_t(A_WIDTH), rows_t(B_WIDTH),
                  full(lw["wuvt"]), full(lw["wout"]), full(fg)],
        out_specs=xblk,
        out_shape=jax.ShapeDtypeStruct((b, t, d), F32),
        compiler_params=pltpu.CompilerParams(dimension_semantics=("arbitrary", "arbitrary"),
                                             vmem_limit_bytes=VMEM_LIMIT_BYTES),
        name="combine_prompt",
    )(x3d, olat_t, ob_t, pr["sgaT"], pr["sgbT"], lw["wuvt"], lw["wout"], fg)


def _layer_weights(norm_g, w_in, q_norm_g, kv_norm_g, w_uq, w_uk, w_uv, w_out):
    d = w_in.shape[0]
    o = np.cumsum([0, A_Q_LORA, A_KV_LORA, A_ROPE, A_WIDTH, B_WIDTH, B_KV_HEADS * B_HEAD_DIM,
                   B_KV_HEADS * B_HEAD_DIM, IDX_HEADS * IDX_DIM, IDX_DIM, IDX_HEADS, B_WIDTH])
    w_t = w_in.T.astype(BF16)
    seg = lambda i: w_t[int(o[i]):int(o[i + 1])]
    zpad = lambda r: jnp.zeros((r, d), BF16)
    win = jnp.concatenate([seg(0), seg(1), seg(2), seg(8), seg(9), zpad(LANES - A_ROPE - IDX_DIM - IDX_HEADS),
                           seg(3), seg(4), seg(5), seg(6), seg(7), seg(10)], axis=0)
    assert win.shape[0] == IN_PAD
    half = A_ROPE // 2
    wq = jnp.concatenate([
        w_uq[:, :, :A_NOPE].reshape(A_Q_LORA, A_HEADS * A_NOPE),
        w_uq[:, :, A_NOPE:A_NOPE + half].reshape(A_Q_LORA, A_HEADS * half),
        w_uq[:, :, A_NOPE + half:].reshape(A_Q_LORA, A_HEADS * half)], axis=1)
    eye = jnp.eye(A_HEADS, dtype=w_uk.dtype)
    wuk = jnp.einsum('chn,hg->hngc', w_uk, eye).reshape(A_HEADS * A_NOPE, A_HEADS * A_KV_LORA)
    wuv = jnp.einsum('chv,hg->hcgv', w_uv, eye).reshape(A_HEADS * A_KV_LORA, A_HEADS * A_V)
    pm = np.zeros((2 * LANES, A_HEADS * LANES), np.float32)
    for h in range(A_HEADS):
        for i in range(half):
            pm[h * half + i, h * LANES + i] = 1.0
            pm[LANES + h * half + i, h * LANES + half + i] = 1.0
    wt = jnp.concatenate([seg(0), seg(1), seg(6), seg(4), seg(7), seg(3), seg(10), seg(9),
                          zpad(R_KB - R_WI - IDX_HEADS), seg(5), seg(8), seg(2)], axis=0)
    assert wt.shape[0] == ROWS_T
    bc = lambda g: jnp.broadcast_to(g.reshape(-1, 1), (g.shape[0], PROJ_T_TM))
    return {
        "wt": wt, "qngc": bc(q_norm_g), "kvngc": bc(kv_norm_g),
        "wqt": wq.T.astype(BF16), "wukt": wuk.T.astype(BF16),
        "wuvt": w_uv.transpose(1, 2, 0).astype(BF16),
        "ng": norm_g.reshape(1, -1), "win": win,
        "qng": q_norm_g.reshape(1, -1), "kvng": kv_norm_g.reshape(1, -1),
        "wq": wq.astype(BF16), "wuk": wuk.astype(BF16), "wuv": wuv.astype(BF16),
        "pmat": jnp.asarray(pm, BF16), "wout": w_out.astype(BF16),
    }


def _rope_tables(pos):
    half = A_ROPE // 2
    inv = ROPE_THETA ** (-jnp.arange(half, dtype=F32) / half)
    ang = pos.astype(F32)[:, None] * inv[None, :]
    cos, sin = jnp.cos(ang), jnp.sin(ang)
    z = jnp.zeros((pos.shape[0], LANES - A_ROPE), F32)
    zh = jnp.zeros_like(cos)
    cosq = jnp.tile(cos, (1, A_HEADS))
    sinq = jnp.tile(sin, (1, A_HEADS))
    rc = jnp.concatenate([cos, cos, z], axis=1)
    rs1 = jnp.concatenate([zh, sin, z], axis=1)
    rs2 = jnp.concatenate([-sin, zh, z], axis=1)
    return cosq, sinq, rc, rs1, rs2


def _rope_tables_t(pos):
    half = A_ROPE // 2
    inv = ROPE_THETA ** (-jnp.arange(half, dtype=F32) / half)
    ang = pos.astype(F32)[:, None] * inv[None, :]
    return jnp.tile(jnp.cos(ang).T, (A_HEADS, 1)), jnp.tile(jnp.sin(ang).T, (A_HEADS, 1))


def kernel(x_prompt, x_sample, cache_mla_ckv, cache_mla_kpe, cache_dsa_k, cache_dsa_v, cache_dsa_kidx,
           norm_g, w_in, mla_q_norm_g, mla_kv_norm_g, mla_w_uq, mla_w_uk, mla_w_uv, rel_bias, w_out,
           final_norm_g):
    bp, tp, d = x_prompt.shape
    bs, ts, _ = x_sample.shape
    depth = w_in.shape[0]
    past = cache_mla_ckv.shape[2]
    n_top_p = min(TOP_K_MAX, tp // 4)
    n_top_s = min(TOP_K_MAX, (past + ts) // 4)
    assert ts <= CHUNK and past % CHUNK == 0 and past % LANES == 0

    rope_p = _rope_tables_t(jnp.arange(tp, dtype=jnp.int32))
    reps = PROJ_TM // ts
    rope_s = tuple(jnp.tile(a, (reps, 1)) for a in _rope_tables(past + jnp.arange(ts, dtype=jnp.int32)))

    bias_p = _bias_tables(rel_bias, (0, -TQ, -3 * TQ), TQ, TQ, True)
    win_s = 2 * LANES
    bias_s = _bias_tables(rel_bias, (-(win_s - LANES), -(past + win_s)), ts, win_s, False)
    fg = final_norm_g.reshape(1, -1)

    xp = x_prompt
    xs = x_sample.reshape(bs * ts, d)
    outs_p, outs_s = [], []
    for l in range(depth):
        lw = _layer_weights(norm_g[l], w_in[l], mla_q_norm_g[l], mla_kv_norm_g[l],
                            mla_w_uq[l], mla_w_uk[l], mla_w_uv[l], w_out[l])
        final = l == depth - 1
        pr = _project_t(xp.reshape(bp * tp, d), rope_p, lw, bp, tp)
        olat, ob = _prompt_attention(pr, bias_p, bp, tp, n_top_p)
        xp = _combine_t(xp, olat, ob, pr, lw, fg, final)
        heads_t = lambda a: a.reshape(bp, B_KV_HEADS, B_HEAD_DIM, tp).transpose(0, 3, 1, 2)
        outs_p.append((pr["ckv"].reshape(bp, tp, A_KV_LORA), pr["kpeT"].transpose(0, 2, 1),
                       heads_t(pr["kbT"]), heads_t(pr["vbT"]), pr["kidxT"].transpose(0, 2, 1)))
        ps = _project(xs, rope_s, lw, period=PROJ_TM)
        feat_t = lambda a: a.transpose(0, 2, 3, 1).reshape(bs, B_KV_HEADS * B_HEAD_DIM, past)
        caches = (cache_mla_ckv[l], cache_mla_kpe[l].transpose(0, 2, 1), feat_t(cache_dsa_k[l]),
                  feat_t(cache_dsa_v[l]), cache_dsa_kidx[l].transpose(0, 2, 1))
        olat, ob = _sample_attention(ps, caches, bias_s, bs, ts, past, n_top_s)
        xs = _combine(xs, olat, ob, ps, lw, fg, final)
        outs_s.append((ps["ckv"].reshape(bs, ts, A_KV_LORA), ps["kpe"].reshape(bs, ts, A_ROPE),
                       ps["kb"].reshape(bs, ts, B_KV_HEADS, B_HEAD_DIM),
                       ps["vb"].reshape(bs, ts, B_KV_HEADS, B_HEAD_DIM),
                       ps["kidx"].reshape(bs, ts, IDX_DIM)))

    stack = lambda outs, i: jnp.stack([o[i] for o in outs])
    return ((xp, xs.reshape(bs, ts, d))
            + tuple(stack(outs_p, i) for i in range(5)) + tuple(stack(outs_s, i) for i in range(5)))
```

```python
import functools
import math

import jax
import jax.numpy as jnp
import numpy as np
from jax import lax
from jax.experimental import pallas as pl
from jax.experimental.pallas import tpu as pltpu

F32 = jnp.float32
BF16 = jnp.bfloat16
I32 = jnp.int32

CHUNK = 64
EPS = 1e-6
A_HEADS = 8
A_NOPE = 64
A_ROPE = 32
A_V = 64
A_Q_LORA = 256
A_KV_LORA = 128
ROPE_THETA = 10000.0
A_WIDTH = A_HEADS * A_V
B_HEADS = 8
B_KV_HEADS = 2
B_HEAD_DIM = 64
B_WIDTH = B_HEADS * B_HEAD_DIM
B_GROUP = B_HEADS // B_KV_HEADS
IDX_HEADS = 8
IDX_DIM = 64
TOP_K_MAX = 256
N_BUCKETS = 32
MAX_DISTANCE = 128

LANES = 128
VMEM_LIMIT_BYTES = 56 * 1024 * 1024

LOG2E = 1.4426950408889634
NEG_BIG = -1e30
INT_MIN = -(2 ** 31)

C_CQ = 0
C_CKV = C_CQ + A_Q_LORA
C_MISC = C_CKV + A_KV_LORA
C_GA = C_MISC + LANES
C_QB = C_GA + A_WIDTH
C_KB = C_QB + B_WIDTH
C_VB = C_KB + B_KV_HEADS * B_HEAD_DIM
C_QI = C_VB + B_KV_HEADS * B_HEAD_DIM
C_GB = C_QI + IDX_HEADS * IDX_DIM
IN_PAD = C_GB + B_WIDTH
M_KPE = 0
M_KIDX = A_ROPE
M_WIDX = A_ROPE + IDX_DIM

QCAT = 2 * LANES
TQ = 256
PROJ_TM = 256
COMBINE_TM = 1024
PROJ_T_TM = 1024


def _dot(a, b):
    return jnp.dot(a, b, preferred_element_type=F32)


def _dot_nt(a, b):
    return lax.dot_general(a, b, (((1,), (1,)), ((), ())), preferred_element_type=F32)


def _rms(x, g):
    return x * lax.rsqrt(jnp.mean(x * x, axis=-1, keepdims=True) + EPS) * g


def _bias_kernel(rb_ref, out_ref, *, offsets, keys_on_rows):
    nb = N_BUCKETS // 2
    max_exact = nb // 2
    n_r = out_ref.shape[1] if keys_on_rows else out_ref.shape[2]
    n_c = out_ref.shape[2] // B_HEADS if keys_on_rows else out_ref.shape[3]
    row = lax.broadcasted_iota(I32, (n_r, n_c), 0)
    col = lax.broadcasted_iota(I32, (n_r, n_c), 1)

    def bucket_of(off):
        rel = off + (row - col if keys_on_rows else col - row)
        ret = jnp.where(rel > 0, nb, 0)
        n = jnp.abs(rel)
        nf = jnp.maximum(n, 1).astype(F32)
        large = max_exact + (jnp.log(nf / max_exact) / math.log(MAX_DISTANCE / max_exact)
                             * (nb - max_exact)).astype(I32)
        large = jnp.minimum(large, nb - 1)
        return ret + jnp.where(n < max_exact, n, large)

    def lookup(bucket, h):
        table = jnp.broadcast_to(rb_ref[h:h + 1, :], (n_r, LANES))
        return jnp.concatenate(
            [jnp.take_along_axis(table, bucket[:, c:c + LANES], axis=1) for c in range(0, n_c, LANES)], axis=1)

    buckets = [bucket_of(off) for off in offsets]
    for h in range(B_HEADS):
        if keys_on_rows:
            far = lookup(buckets[-1], h)
            for p in range(len(offsets) - 1):
                out_ref[p, :, h * n_c:(h + 1) * n_c] = (lookup(buckets[p], h) - far) * LOG2E
        else:
            for p in range(len(offsets)):
                out_ref[p, h] = lookup(buckets[p], h) * LOG2E


def _bias_tables(rel_bias, offsets, n_r, n_c, keys_on_rows):
    shape = (len(offsets) - 1, n_r, B_HEADS * n_c) if keys_on_rows else (len(offsets), B_HEADS, n_r, n_c)
    return pl.pallas_call(
        functools.partial(_bias_kernel, offsets=tuple(offsets), keys_on_rows=keys_on_rows),
        out_shape=jax.ShapeDtypeStruct(shape, F32),
        in_specs=[pl.BlockSpec(memory_space=pltpu.VMEM)],
        out_specs=pl.BlockSpec(memory_space=pltpu.VMEM),
        compiler_params=pltpu.CompilerParams(vmem_limit_bytes=VMEM_LIMIT_BYTES),
        name="bias_tables",
    )(jnp.pad(rel_bias.T, ((0, 0), (0, LANES - N_BUCKETS))))


def _proj_kernel(x_ref, ng_ref, win_ref, qng_ref, kvng_ref, wq_ref, wuk_ref, pmat_ref,
                 cosq_ref, sinq_ref, rc_ref, rs1_ref, rs2_ref,
                 ckv_ref, kpe_ref, kb_ref, vb_ref, kidx_ref,
                 kcat_ref, kb16_ref, vb16_ref, kidx16_ref,
                 qcat_ref, qb_ref, qidx_ref, widx_ref, sga_ref, sgb_ref):
    x = x_ref[...]
    h = _rms(x, ng_ref[...])
    z = _dot_nt(h.astype(BF16), win_ref[...])

    cq = _rms(z[:, C_CQ:C_CQ + A_Q_LORA], qng_ref[...])
    q = _dot(cq.astype(BF16), wq_ref[...])
    n_nope = A_HEADS * A_NOPE
    x1 = q[:, n_nope:n_nope + LANES]
    x2 = q[:, n_nope + LANES:n_nope + 2 * LANES]
    cos8, sin8 = cosq_ref[...], sinq_ref[...]
    o1 = x1 * cos8 - x2 * sin8
    o2 = x1 * sin8 + x2 * cos8
    mla_scale = (A_NOPE + A_ROPE) ** -0.5 * LOG2E
    q_lat = _dot(q[:, :n_nope].astype(BF16), wuk_ref[...]) * mla_scale
    pe = jnp.concatenate([o1, o2], axis=1) * mla_scale
    q_pe = _dot(pe.astype(BF16), pmat_ref[...])
    for hh in range(A_HEADS):
        qcat_ref[:, hh * QCAT:hh * QCAT + LANES] = q_lat[:, hh * LANES:(hh + 1) * LANES].astype(BF16)
        qcat_ref[:, hh * QCAT + LANES:(hh + 1) * QCAT] = q_pe[:, hh * LANES:(hh + 1) * LANES].astype(BF16)

    ckv = _rms(z[:, C_CKV:C_CKV + A_KV_LORA], kvng_ref[...])
    ckv_ref[...] = ckv
    misc = z[:, C_MISC:C_MISC + LANES]
    rot = (misc * rc_ref[...] + pltpu.roll(misc, A_ROPE // 2, 1) * rs1_ref[...]
           + pltpu.roll(misc, LANES - A_ROPE // 2, 1) * rs2_ref[...])
    kpe_ref[...] = rot[:, :A_ROPE]
    kcat_ref[:, :LANES] = ckv.astype(BF16)
    kcat_ref[:, LANES:] = rot.astype(BF16)

    kidx = misc[:, M_KIDX:M_KIDX + IDX_DIM]
    kidx_ref[...] = kidx
    kidx16_ref[...] = kidx.astype(BF16)
    widx_ref[...] = misc[:, M_WIDX:M_WIDX + IDX_HEADS] * (IDX_HEADS ** -0.5)
    kb = z[:, C_KB:C_KB + LANES]
    vb = z[:, C_VB:C_VB + LANES]
    kb_ref[...] = kb
    vb_ref[...] = vb
    kb16_ref[...] = kb.astype(BF16)
    vb16_ref[...] = vb.astype(BF16)
    qb_ref[...] = (z[:, C_QB:C_QB + B_WIDTH] * (B_HEAD_DIM ** -0.5 * LOG2E)).astype(BF16)
    qidx_ref[...] = (z[:, C_QI:C_QI + IDX_HEADS * IDX_DIM] * (IDX_DIM ** -0.5)).astype(BF16)
    sga_ref[...] = jax.nn.silu(z[:, C_GA:C_GA + A_WIDTH])
    sgb_ref[...] = jax.nn.silu(z[:, C_GB:C_GB + B_WIDTH])


def _project(x2d, rope_tabs, lw, *, period):
    n, d = x2d.shape
    tm = PROJ_TM
    assert n % tm == 0 and period % tm == 0
    n_rep = period // tm
    tok = lambda c: pl.BlockSpec((tm, c), lambda i: (i, 0))
    full = lambda a: pl.BlockSpec(a.shape, lambda i: (0,) * a.ndim)
    tab = pl.BlockSpec((tm, LANES), lambda i: (i % n_rep, 0))
    outs = [
        ("ckv", A_KV_LORA, F32), ("kpe", A_ROPE, F32), ("kb", LANES, F32), ("vb", LANES, F32),
        ("kidx", IDX_DIM, F32),
        ("kcat", QCAT, BF16), ("kb16", LANES, BF16), ("vb16", LANES, BF16), ("kidx16", IDX_DIM, BF16),
        ("qcat", A_HEADS * QCAT, BF16), ("qb", B_WIDTH, BF16), ("qidx", IDX_HEADS * IDX_DIM, BF16),
        ("widx", IDX_HEADS, F32), ("sga", A_WIDTH, F32), ("sgb", B_WIDTH, F32),
    ]
    res = pl.pallas_call(
        _proj_kernel,
        grid=(n // tm,),
        in_specs=[tok(d), full(lw["ng"]), full(lw["win"]), full(lw["qng"]), full(lw["kvng"]),
                  full(lw["wq"]), full(lw["wuk"]), full(lw["pmat"]), tab, tab, tab, tab, tab],
        out_specs=[tok(c) for _, c, _ in outs],
        out_shape=[jax.ShapeDtypeStruct((n, c), dt) for _, c, dt in outs],
        compiler_params=pltpu.CompilerParams(dimension_semantics=("arbitrary",),
                                             vmem_limit_bytes=VMEM_LIMIT_BYTES),
        name="project",
    )(x2d, lw["ng"], lw["win"], lw["qng"], lw["kvng"], lw["wq"], lw["wuk"], lw["pmat"], *rope_tabs)
    return {name: r for (name, _, _), r in zip(outs, res)}


R_CQ = 0
R_CKV = R_CQ + A_Q_LORA
R_VB = R_CKV + A_KV_LORA
R_QB = R_VB + LANES
R_QI = R_QB + B_WIDTH
R_GA = R_QI + IDX_HEADS * IDX_DIM
R_GB = R_GA + A_WIDTH
R_WI = R_GB + B_WIDTH
R_KB = R_WI + 16
R_KI = R_KB + LANES
R_KPE = R_KI + IDX_DIM
ROWS_T = R_KPE + A_ROPE
MK_KPE = IDX_DIM
ONES_ROWS = 16
V_EXT = A_KV_LORA + ONES_ROWS


def _proj_t_kernel(x_ref, ng_ref, wt_ref, qng_ref, kvngc_ref, wqt_ref, wukt_ref, cost_ref, sint_ref,
                   ckv_ref, kcat_ref, kb16_ref, kidx16_ref,
                   kpeT_ref, kbT_ref, vbT_ref, kidxT_ref,
                   qa_ref, qbp_ref, qip_ref, widxT_ref, vmlaT_ref, vdsaT_ref, sgaT_ref, sgbT_ref):
    x = x_ref[...]
    tm = x.shape[0]
    hb = _rms(x, ng_ref[...]).astype(BF16)

    zt = _dot_nt(wt_ref[...], hb)

    def rms_t(c, g):
        return c * lax.rsqrt(jnp.mean(c * c, axis=0, keepdims=True) + EPS) * g

    cq = rms_t(zt[R_CQ:R_CQ + A_Q_LORA], qng_ref[...])
    qt = _dot(wqt_ref[...], cq.astype(BF16))
    n_nope = A_HEADS * A_NOPE
    x1 = qt[n_nope:n_nope + LANES]
    x2 = qt[n_nope + LANES:n_nope + 2 * LANES]
    cos8, sin8 = cost_ref[...], sint_ref[...]
    mla_scale = (A_NOPE + A_ROPE) ** -0.5 * LOG2E
    o1 = (x1 * cos8 - x2 * sin8) * mla_scale
    o2 = (x1 * sin8 + x2 * cos8) * mla_scale
    q_nope = qt[:n_nope].astype(BF16)
    q_lat = jnp.concatenate([_dot(wukt_ref[h], q_nope[h * A_NOPE:(h + 1) * A_NOPE])
                             for h in range(A_HEADS)], axis=0) * mla_scale
    half = A_ROPE // 2
    qb_t = (zt[R_QB:R_QB + B_WIDTH] * (B_HEAD_DIM ** -0.5 * LOG2E)).astype(BF16)
    qi_t = (zt[R_QI:R_QI + IDX_HEADS * IDX_DIM] * (IDX_DIM ** -0.5)).astype(BF16)
    for c in range(tm // TQ):
        blk = slice(c * TQ, (c + 1) * TQ)
        qbp_ref[c] = jnp.zeros(qbp_ref.shape[1:], BF16)
        qip_ref[c] = jnp.zeros(qip_ref.shape[1:], BF16)
        for h in range(A_HEADS):
            g = h // B_GROUP
            lanes = slice(h * TQ, (h + 1) * TQ)
            qa_ref[c, :LANES, lanes] = q_lat[h * LANES:(h + 1) * LANES, blk].astype(BF16)
            qa_ref[c, LANES:LANES + half, lanes] = o1[h * half:(h + 1) * half, blk].astype(BF16)
            qa_ref[c, LANES + half:LANES + A_ROPE, lanes] = o2[h * half:(h + 1) * half, blk].astype(BF16)
            qa_ref[c, LANES + A_ROPE:, lanes] = jnp.zeros((QCAT - LANES - A_ROPE, TQ), BF16)
            qbp_ref[c, g * B_HEAD_DIM:(g + 1) * B_HEAD_DIM, lanes] = qb_t[h * B_HEAD_DIM:(h + 1) * B_HEAD_DIM, blk]
            qip_ref[c, :IDX_DIM, lanes] = qi_t[h * IDX_DIM:(h + 1) * IDX_DIM, blk]
    ones = jnp.ones((ONES_ROWS, TQ), BF16)
    ckv_t = rms_t(zt[R_CKV:R_CKV + A_KV_LORA], kvngc_ref[...])
    vb_t = zt[R_VB:R_VB + LANES]
    for c in range(tm // TQ):
        blk = slice(c * TQ, (c + 1) * TQ)
        vmlaT_ref[c, :A_KV_LORA, :] = ckv_t[:, blk].astype(BF16)
        vmlaT_ref[c, A_KV_LORA:, :] = ones
        vdsaT_ref[c, :LANES, :] = vb_t[:, blk].astype(BF16)
        vdsaT_ref[c, LANES:, :] = ones
    kb_t = zt[R_KB:R_KB + LANES]
    kidx_t = zt[R_KI:R_KI + IDX_DIM]
    k1, k2 = zt[R_KPE:R_KPE + half], zt[R_KPE + half:R_KPE + A_ROPE]
    cos1, sin1 = cos8[:half], sin8[:half]
    kpe_t = jnp.concatenate([k1 * cos1 - k2 * sin1, k1 * sin1 + k2 * cos1], axis=0)
    vbT_ref[...] = vb_t
    kbT_ref[...] = kb_t
    kidxT_ref[...] = kidx_t
    kpeT_ref[...] = kpe_t
    ckv = ckv_t.T
    misc = jnp.concatenate([kidx_t, kpe_t, jnp.zeros((LANES - IDX_DIM - A_ROPE, tm), F32)], axis=0).T
    lane = lax.broadcasted_iota(I32, (tm, LANES), 1)
    ckv_ref[...] = ckv
    kcat_ref[:, :LANES] = ckv.astype(BF16)
    kcat_ref[:, LANES:] = pltpu.roll(
        jnp.where((lane >= MK_KPE) & (lane < MK_KPE + A_ROPE), misc, 0.0), LANES - MK_KPE, 1).astype(BF16)
    kidx16_ref[...] = jnp.where(lane < IDX_DIM, misc, 0.0).astype(BF16)
    kb16_ref[...] = kb_t.T.astype(BF16)
    sgaT_ref[...] = jax.nn.silu(zt[R_GA:R_GA + A_WIDTH]).astype(BF16)
    sgbT_ref[...] = jax.nn.silu(zt[R_GB:R_GB + B_WIDTH]).astype(BF16)
    widxT_ref[...] = zt[R_WI:R_WI + IDX_HEADS] * (IDX_HEADS ** -0.5)


def _project_t(x2d, tabs, lw, b, t):
    n, d = x2d.shape
    tm = PROJ_T_TM
    assert t % tm == 0 and tm % TQ == 0
    nt = t // tm
    tok = lambda c: pl.BlockSpec((tm, c), lambda i: (i, 0))
    full = lambda a: pl.BlockSpec(a.shape, lambda i: (0,) * a.ndim)
    vblk = pl.BlockSpec((None, tm // TQ, V_EXT, TQ), lambda i: (i // nt, i % nt, 0, 0))
    qslab = lambda r: pl.BlockSpec((None, tm // TQ, r, A_HEADS * TQ), lambda i: (i // nt, i % nt, 0, 0))
    tab_t = pl.BlockSpec((LANES, tm), lambda i: (0, i % nt))
    rows_t = lambda r: pl.BlockSpec((None, r, tm), lambda i: (i // nt, 0, i % nt))
    outs = [
        ("ckv", (n, A_KV_LORA), F32, tok(A_KV_LORA)),
        ("kcat", (n, QCAT), BF16, tok(QCAT)), ("kb16", (n, LANES), BF16, tok(LANES)),
        ("kidx16", (n, LANES), BF16, tok(LANES)),
        ("kpeT", (b, A_ROPE, t), F32, rows_t(A_ROPE)), ("kbT", (b, LANES, t), F32, rows_t(LANES)),
        ("vbT", (b, LANES, t), F32, rows_t(LANES)), ("kidxT", (b, IDX_DIM, t), F32, rows_t(IDX_DIM)),
        ("qa", (b, t // TQ, QCAT, A_HEADS * TQ), BF16, qslab(QCAT)),
        ("qbp", (b, t // TQ, LANES, B_HEADS * TQ), BF16, qslab(LANES)),
        ("qip", (b, t // TQ, LANES, IDX_HEADS * TQ), BF16, qslab(LANES)),
        ("widxT", (b, IDX_HEADS, t), F32, rows_t(IDX_HEADS)),
        ("vmlaT", (b, t // TQ, V_EXT, TQ), BF16, vblk), ("vdsaT", (b, t // TQ, V_EXT, TQ), BF16, vblk),
        ("sgaT", (b, A_WIDTH, t), BF16, rows_t(A_WIDTH)), ("sgbT", (b, B_WIDTH, t), BF16, rows_t(B_WIDTH)),
    ]
    cost, sint = tabs
    res = pl.pallas_call(
        _proj_t_kernel,
        grid=(n // tm,),
        in_specs=[tok(d), full(lw["ng"]), full(lw["wt"]), full(lw["qngc"]), full(lw["kvngc"]),
                  full(lw["wqt"]), full(lw["wukt"]), tab_t, tab_t],
        out_specs=[o[3] for o in outs],
        out_shape=[jax.ShapeDtypeStruct(o[1], o[2]) for o in outs],
        compiler_params=pltpu.CompilerParams(dimension_semantics=("arbitrary",),
                                             vmem_limit_bytes=VMEM_LIMIT_BYTES),
        name="project_prompt",
    )(x2d, lw["ng"], lw["wt"], lw["qngc"], lw["kvngc"], lw["wqt"], lw["wukt"], cost, sint)
    return {o[0]: r for o, r in zip(outs, res)}


NEG_FLT_MAX = -3.4028234663852886e38
KEY_NEG_FLT_MAX = INT_MIN + (1 << 23)


def _key_to_float(k):
    k = jnp.maximum(k, KEY_NEG_FLT_MAX)
    return pltpu.bitcast(k ^ ((k >> 31) & 0x7FFFFFFF), F32)


def _count(pred):
    return jnp.sum(jnp.where(pred, 1.0, 0.0), axis=1, keepdims=True)


SAMPLE_STREAMS = 2
MAX_TIE_SWEEPS = 8.0
SAFE_DENOM_MIN = 2.0 ** -90
SAFE_DENOM_MAX = 2.0 ** 40


def _fold8(x, op=jnp.add):
    parts = [x[i:i + 8] for i in range(0, x.shape[0], 8)]
    while len(parts) > 1:
        parts = [op(a, b) for a, b in zip(parts[::2], parts[1::2])]
    return parts[0]


def _prompt_attn_kernel(qa_ref, qbp_ref, qip_ref, widxT_ref, kcat_ref, kb_ref, kidx_ref,
                        vmlaT_ref, vdsaT_ref, bias_ref, olatT_ref, obT_ref,
                        sc_ref, m_ref, acc_ref, mb_ref, accb_ref, kn_ref, *, n_top):
    qi = pl.program_id(1)
    nblk = qi + 1
    tq = TQ
    krow = lax.broadcasted_iota(I32, (tq, tq), 0)
    qcol = lax.broadcasted_iota(I32, (tq, tq), 1)
    shift = CHUNK.bit_length() - 1
    diag_ok = (qcol >> shift) >= (krow >> shift)
    hcols = lambda h: slice(h * tq, (h + 1) * tq)

    def per_head(fn):
        return jnp.concatenate([fn(h) for h in range(A_HEADS)], axis=1)

    w_all = per_head(lambda h: widxT_ref[h:h + 1, :])

    @pl.when(qi == 0)
    def _():
        def max_row_norm2(k_ref):
            k = k_ref[...].astype(F32)
            return jnp.max(jnp.sum(k * k, axis=1, keepdims=True))
        kn_ref[0:1, :] = jnp.full((1, LANES), max_row_norm2(kcat_ref), F32)
        kn_ref[1:2, :] = jnp.full((1, LANES), max_row_norm2(kb_ref), F32)
        kn_ref[2:3, :] = jnp.full((1, LANES), jnp.max(bias_ref[...]), F32)

    def col_norm(q_ref):
        q = q_ref[...].astype(F32)
        return jnp.sqrt(jnp.sum(q * q, axis=0, keepdims=True))

    shift_a = jnp.sqrt(kn_ref[0:1, 0:1]) * col_norm(qa_ref)
    shift_b = jnp.sqrt(kn_ref[1:2, 0:1]) * col_norm(qbp_ref) + jnp.maximum(kn_ref[2:3, 0:1], 0.0)

    def accumulate(s_t, values, shift_or_m, acc_r, exact):
        if exact:
            m_prev = shift_or_m[0:1, :]
            m_new = jnp.maximum(m_prev, jnp.max(s_t, axis=0, keepdims=True))
            alpha = jnp.exp2(m_prev - m_new)
            shift_or_m[0:1, :] = m_new
        else:
            m_new = shift_or_m
        p_t = jnp.exp2(s_t - m_new).astype(BF16)
        for v_t, lanes in values:
            pv = _dot(v_t, p_t[:, lanes])
            acc_r[:, lanes] = (alpha[:, lanes] * acc_r[:, lanes] if exact else acc_r[:, lanes]) + pv

    def unsafe(l):
        return jnp.max(jnp.where((l >= SAFE_DENOM_MIN) & (l <= SAFE_DENOM_MAX), 0.0, 1.0)) > 0.0

    all_lanes = slice(0, A_HEADS * tq)

    def for_blocks(n, block):
        def pair(p, c):
            block(2 * p)
            block(2 * p + 1)
            return c

        lax.fori_loop(0, lax.shift_right_logical(n, 1), pair, 0)

        @pl.when((n & 1) == 1)
        def _():
            block(n - 1)

    def mla_pass(exact):
        acc_ref[...] = jnp.zeros(acc_ref.shape, F32)
        if exact:
            m_ref[...] = jnp.full(m_ref.shape, NEG_BIG, F32)

        def block(j, masked):
            start = pl.multiple_of(j * tq, tq)
            s_t = _dot(kcat_ref[pl.ds(start, tq), :], qa_ref[...])
            if masked:
                s_t = per_head(lambda h: jnp.where(diag_ok, s_t[:, hcols(h)], NEG_BIG))
            accumulate(s_t, [(vmlaT_ref[j], all_lanes)], m_ref if exact else shift_a, acc_ref, exact)
            if exact:
                return
            r = jnp.maximum(_dot(kidx_ref[pl.ds(start, tq), :], qip_ref[...]), 0.0) * w_all
            score = r[:, hcols(0)]
            for h in range(1, IDX_HEADS):
                score = score + r[:, hcols(h)]
            if masked:
                score = jnp.where(diag_ok, score, -jnp.inf)
            sc_ref[j] = score

        for_blocks(qi, lambda j: block(j, False))
        block(qi, True)

    mla_pass(False)

    @pl.when(unsafe(acc_ref[A_KV_LORA:A_KV_LORA + 1, :]))
    def _():
        mla_pass(True)

    o_t = acc_ref[:A_KV_LORA, :] * (1.0 / acc_ref[A_KV_LORA:A_KV_LORA + 1, :])
    for h in range(A_HEADS):
        olatT_ref[h * LANES:(h + 1) * LANES, :] = o_t[:, hcols(h)].astype(olatT_ref.dtype)

    def count(pred):
        def one(j, c):
            return c + _fold8(jnp.where(pred(sc_ref[j], j), 1.0, 0.0))
        part = lax.fori_loop(0, lax.shift_right_logical(nblk, 1), lambda p, c: one(2 * p + 1, one(2 * p, c)),
                             jnp.zeros((8, tq), F32))
        part = lax.cond((nblk & 1) == 1, lambda c: one(nblk - 1, c), lambda c: c, part)
        return jnp.sum(part, axis=0, keepdims=True)

    kf = float(n_top)

    def bis_body(it, carry):
        lo, cnt_lo = carry
        cand = lo + lax.shift_left(jnp.int32(1), 31 - it)
        cand_f = _key_to_float(cand)
        cnt = count(lambda s, j: s >= cand_f)
        take = cnt >= kf
        return jnp.where(take, cand, lo), jnp.where(take, cnt, cnt_lo)

    n_steps = jnp.where(nblk * tq <= n_top, 0, 32)
    lo, cnt_ge = lax.fori_loop(
        0, n_steps, bis_body, (jnp.full((1, tq), INT_MIN, I32), jnp.full((1, tq), 1e9, F32)))
    few = lo == INT_MIN
    thr = _key_to_float(lo)
    excess0 = jnp.where(few, 0.0, cnt_ge - kf)
    max_excess = jnp.max(excess0)

    def drop_from(cut):
        def body(j, c):
            s = sc_ref[j]
            sc_ref[j] = jnp.where((s == thr) & ((krow + j * tq) >= cut), -jnp.inf, s)
            return c
        lax.fori_loop(0, nblk, body, 0)

    @pl.when((max_excess > 0.0) & (max_excess <= MAX_TIE_SWEEPS))
    def _():
        def last_tie_below(cut):
            def body(j, m):
                pos = krow + j * tq
                hit = jnp.where((sc_ref[j] == thr) & (pos < cut), pos, -1)
                return jnp.maximum(m, _fold8(hit, jnp.maximum))
            part = lax.fori_loop(0, nblk, body, jnp.full((8, tq), -1, I32))
            return jnp.max(part, axis=0, keepdims=True)

        def sweep(c):
            excess, cut = c
            last = last_tie_below(cut)
            live = excess > 0.0
            return jnp.where(live, excess - 1.0, excess), jnp.where(live, last, cut)

        _, cut = lax.while_loop(lambda c: jnp.max(c[0]) > 0.0, sweep,
                                (excess0, jnp.full((1, tq), 2 ** 30, I32)))
        drop_from(cut)

    @pl.when(max_excess > MAX_TIE_SWEEPS)
    def _():
        need = kf - count(lambda s, j: s > thr)
        n_bits = (sc_ref.shape[0] * tq).bit_length()

        def cut_body(it, cpos):
            cand = cpos + lax.shift_left(jnp.int32(1), n_bits - 1 - it)
            cnt = count(lambda s, j: (s == thr) & ((krow + j * tq) < cand))
            return jnp.where(cnt < need, cand, cpos)

        keep = lax.fori_loop(0, n_bits, cut_body, jnp.zeros((1, tq), I32))
        drop_from(jnp.where(excess0 > 0.0, keep + 1, 2 ** 30))

    group_lanes = [slice(g * B_GROUP * tq, (g + 1) * B_GROUP * tq) for g in range(B_KV_HEADS)]

    def dsa_pass(exact):
        accb_ref[...] = jnp.zeros(accb_ref.shape, F32)
        if exact:
            mb_ref[...] = jnp.full(mb_ref.shape, NEG_BIG, F32)

        def block(j, near):
            start = pl.multiple_of(j * tq, tq)
            sel = sc_ref[j] >= thr
            s_t = _dot(kb_ref[pl.ds(start, tq), :], qbp_ref[...])
            if near:
                s_t = s_t + bias_ref[qi - j]
            s_t = per_head(lambda h: jnp.where(sel, s_t[:, hcols(h)], NEG_BIG))
            v_all = vdsaT_ref[j]
            values = [(jnp.concatenate([v_all[g * B_HEAD_DIM:(g + 1) * B_HEAD_DIM], v_all[LANES:]], axis=0),
                       group_lanes[g]) for g in range(B_KV_HEADS)]
            accumulate(s_t, values, mb_ref if exact else shift_b, accb_ref, exact)

        for_blocks(jnp.maximum(qi - 1, 0), lambda j: block(j, False))

        @pl.when(qi >= 1)
        def _():
            block(qi - 1, True)

        block(qi, True)

    dsa_pass(False)

    @pl.when(unsafe(accb_ref[B_HEAD_DIM:B_HEAD_DIM + 1, :]))
    def _():
        dsa_pass(True)

    inv_b = 1.0 / accb_ref[B_HEAD_DIM:B_HEAD_DIM + 1, :]
    for h in range(B_HEADS):
        obT_ref[h * B_HEAD_DIM:(h + 1) * B_HEAD_DIM, :] = (
            accb_ref[:B_HEAD_DIM, hcols(h)] * inv_b[:, hcols(h)]).astype(obT_ref.dtype)


def _prompt_attention(pr, bias_p, b, t, n_top):
    tq = TQ
    assert t % tq == 0
    nq = t // tq
    r3 = lambda a: a.reshape(b, t, a.shape[-1])
    qrows = lambda r: pl.BlockSpec((None, r, tq), lambda bi, qi: (bi, 0, qi))
    qslab = lambda r: pl.BlockSpec((None, None, r, A_HEADS * tq), lambda bi, qi: (bi, qi, 0, 0))
    kall =lambda c: pl.BlockSpec((None, t, c), lambda bi, qi: (bi, 0, 0))
    vall = pl.BlockSpec((None, nq, V_EXT, tq), lambda bi, qi: (bi, 0, 0, 0))
    olat, ob = pl.pallas_call(
        functools.partial(_prompt_attn_kernel, n_top=n_top),
        grid=(b, nq),
        in_specs=[qslab(QCAT), qslab(LANES), qslab(LANES), qrows(IDX_HEADS),
                  kall(QCAT), kall(LANES), kall(LANES), vall, vall,
                  pl.BlockSpec(bias_p.shape, lambda bi, qi: (0, 0, 0))],
        out_specs=[qrows(A_HEADS * A_KV_LORA), qrows(B_WIDTH)],
        out_shape=[jax.ShapeDtypeStruct((b, A_HEADS * A_KV_LORA, t), BF16),
                   jax.ShapeDtypeStruct((b, B_WIDTH, t), BF16)],
        scratch_shapes=[
            pltpu.VMEM((nq, tq, tq), F32),
            pltpu.VMEM((8, A_HEADS * tq), F32),
            pltpu.VMEM((V_EXT, A_HEADS * tq), F32),
            pltpu.VMEM((8, B_HEADS * tq), F32),
            pltpu.VMEM((B_HEAD_DIM + ONES_ROWS, B_HEADS * tq), F32),
            pltpu.VMEM((8, LANES), F32),
        ],
        compiler_params=pltpu.CompilerParams(dimension_semantics=("arbitrary", "arbitrary"),
                                             vmem_limit_bytes=VMEM_LIMIT_BYTES),
        name="prompt_attention",
    )(pr["qa"], pr["qbp"], pr["qip"], pr["widxT"],
      r3(pr["kcat"]), r3(pr["kb16"]), r3(pr["kidx16"]), pr["vmlaT"], pr["vdsaT"], bias_p)
    return olat, ob


def _sample_attn_kernel(qcat_ref, qb_ref, qidx_ref, widx_ref, kcatn_ref, kbn_ref, vbn_ref, kidxn_ref,
                        cckv_ref, ckpeT_ref, ckT_ref, cvT_ref, ckidxT_ref, bias_ref,
                        olat_ref, ob_ref, *, n_top, t_new, past):
    tq = t_new
    pad = LANES
    n_keys = past + pad
    kf = float(n_top)
    n_bits = n_keys.bit_length()
    n_far = n_keys - bias_ref.shape[-1]

    def padrows(a):
        return jnp.concatenate([a, jnp.zeros((pad - t_new, a.shape[1]), a.dtype)], axis=0)

    def softmax(s):
        m = jnp.max(s, axis=1, keepdims=True)
        p = jnp.exp2(s - m)
        return p.astype(BF16), jnp.sum(p, axis=1, keepdims=True)

    def new_cols(rows):
        return lax.broadcasted_iota(I32, (rows, pad), 1) < t_new

    def mla_and_scores(i):
        ckv_c = cckv_ref[i].astype(BF16)
        kpe_t = ckpeT_ref[i].astype(BF16)
        kcat_n = padrows(kcatn_ref[i])
        qs = jnp.concatenate([qcat_ref[i, :, h * QCAT:(h + 1) * QCAT] for h in range(A_HEADS)], axis=0)
        s_c = _dot_nt(qs[:, :A_KV_LORA], ckv_c) + _dot(qs[:, A_KV_LORA:A_KV_LORA + A_ROPE], kpe_t)
        s_n = jnp.where(new_cols(A_HEADS * tq), _dot_nt(qs, kcat_n), NEG_BIG)
        pb, l = softmax(jnp.concatenate([s_c, s_n], axis=1))
        o = (_dot(pb[:, :past], ckv_c) + _dot(pb[:, past:], kcat_n[:, :A_KV_LORA])) / l
        for h in range(A_HEADS):
            olat_ref[i, :, h * LANES:(h + 1) * LANES] = o[h * tq:(h + 1) * tq].astype(olat_ref.dtype)

        kidx_t = ckidxT_ref[i].astype(BF16)
        kidx_n = padrows(kidxn_ref[i])
        qis = jnp.concatenate(
            [qidx_ref[i, :, h * IDX_DIM:(h + 1) * IDX_DIM] for h in range(IDX_HEADS)], axis=0)

        def head_sum(dots):
            acc = jnp.maximum(dots[:tq], 0.0) * widx_ref[i, :, 0:1]
            for h in range(1, IDX_HEADS):
                acc = acc + jnp.maximum(dots[h * tq:(h + 1) * tq], 0.0) * widx_ref[i, :, h:h + 1]
            return acc

        return jnp.concatenate(
            [head_sum(_dot(qis, kidx_t)),
             jnp.where(new_cols(tq), head_sum(_dot_nt(qis, kidx_n)), -jnp.inf)], axis=1)

    def search_step(it, score, lo, cnt_lo):
        cand = lo + jnp.int32(INT_MIN if it == 0 else 1 << (31 - it))
        cnt = _count(score >= _key_to_float(cand))
        take = cnt >= kf
        return jnp.where(take, cand, lo), jnp.where(take, cnt, cnt_lo)

    def attend_selected(i, score, lo, cnt_ge):
        few = lo == INT_MIN
        thr = _key_to_float(lo)
        has_tie = jnp.max(jnp.where((cnt_ge > kf) & (~few), 1.0, 0.0))

        def drop_ties():
            cols = lax.broadcasted_iota(I32, (tq, n_keys), 1)
            need = kf - _count(score > thr)
            eq = score == thr

            def cut_body(it, cpos):
                cand = cpos + lax.shift_left(jnp.int32(1), n_bits - 1 - it)
                cnt = _count(eq & (cols < cand))
                return jnp.where(cnt < need, cand, cpos)

            keep = lax.fori_loop(0, n_bits, cut_body, jnp.zeros((tq, 1), I32))
            return jnp.where(eq & (cols > keep) & (cnt_ge > kf) & (~few), -jnp.inf, score)

        kept = lax.cond(has_tie > 0.0, drop_ties, lambda: score)
        sel_g = jnp.concatenate([kept] * B_GROUP, axis=0) >= jnp.concatenate([thr] * B_GROUP, axis=0)

        k_t = ckT_ref[i].astype(BF16)
        v_t = cvT_ref[i].astype(BF16)
        k_n = padrows(kbn_ref[i])
        v_n = padrows(vbn_ref[i])
        for g in range(B_KV_HEADS):
            feats = slice(g * B_HEAD_DIM, (g + 1) * B_HEAD_DIM)
            heads = range(g * B_GROUP, (g + 1) * B_GROUP)
            qg = jnp.concatenate([qb_ref[i, :, h * B_HEAD_DIM:(h + 1) * B_HEAD_DIM] for h in heads], axis=0)
            near = jnp.concatenate([bias_ref[0, h] - bias_ref[1, h] for h in heads], axis=0)
            sg = _dot(qg, k_t[feats])
            sg = jnp.concatenate([sg[:, :n_far], sg[:, n_far:] + near[:, :past - n_far],
                                  _dot_nt(qg, k_n[:, feats]) + near[:, past - n_far:]], axis=1)
            pb, l = softmax(jnp.where(sel_g, sg, NEG_BIG))
            og = (_dot_nt(pb[:, :past], v_t[feats]) + _dot(pb[:, past:], v_n[:, feats])) / l
            for hh, h in enumerate(heads):
                ob_ref[i, :, h * B_HEAD_DIM:(h + 1) * B_HEAD_DIM] = og[hh * tq:(hh + 1) * tq]

    streams = range(SAMPLE_STREAMS)
    scores = [mla_and_scores(i) for i in streams]
    state = [(jnp.full((tq, 1), INT_MIN, I32), jnp.full((tq, 1), 1e9, F32)) for _ in streams]
    for it in range(32):
        state = [search_step(it, scores[i], *state[i]) for i in streams]
    for i in streams:
        attend_selected(i, scores[i], *state[i])


def _sample_attention(pr, caches, bias_s, b, t_new, past, n_top):
    r3 = lambda a: a.reshape(b, t_new, a.shape[-1])
    ns = SAMPLE_STREAMS
    assert b % ns == 0
    per_b = lambda n, c: pl.BlockSpec((ns, n, c), lambda bi: (bi, 0, 0))
    news = [pr["qcat"], pr["qb"], pr["qidx"], pr["widx"], pr["kcat"], pr["kb16"], pr["vb16"], pr["kidx16"]]
    olat, ob = pl.pallas_call(
        functools.partial(_sample_attn_kernel, n_top=n_top, t_new=t_new, past=past),
        grid=(b // ns,),
        in_specs=[per_b(t_new, a.shape[-1]) for a in news]
                 + [per_b(c.shape[1], c.shape[2]) for c in caches]
                 + [pl.BlockSpec(bias_s.shape, lambda bi: (0, 0, 0, 0))],
        out_specs=[per_b(t_new, A_HEADS * A_KV_LORA), per_b(t_new, B_WIDTH)],
        out_shape=[jax.ShapeDtypeStruct((b, t_new, A_HEADS * A_KV_LORA), BF16),
                   jax.ShapeDtypeStruct((b, t_new, B_WIDTH), F32)],
        compiler_params=pltpu.CompilerParams(dimension_semantics=("arbitrary",),
                                             vmem_limit_bytes=VMEM_LIMIT_BYTES),
        name="sample_attention",
    )(*[r3(a) for a in news], *caches, bias_s)
    return olat.reshape(b * t_new, -1), ob.reshape(b * t_new, -1)


def _combine_kernel(x_ref, olat_ref, ob_ref, sga_ref, sgb_ref, wuv_ref, wout_ref, fg_ref, y_ref, *, final):
    o_a = _dot(olat_ref[...], wuv_ref[...])
    mix = jnp.concatenate([o_a * sga_ref[...], ob_ref[...] * sgb_ref[...]], axis=1)
    y = x_ref[...] + _dot(mix.astype(BF16), wout_ref[...])
    if final:
        y = _rms(y, fg_ref[...])
    y_ref[...] = y


def _combine(x2d, olat, ob, pr, lw, fg, final):
    n, d = x2d.shape
    tm = PROJ_TM
    tok = lambda c: pl.BlockSpec((tm, c), lambda i: (i, 0))
    full = lambda a: pl.BlockSpec(a.shape, lambda i: (0,) * a.ndim)
    return pl.pallas_call(
        functools.partial(_combine_kernel, final=final),
        grid=(n // tm,),
        in_specs=[tok(d), tok(olat.shape[1]), tok(ob.shape[1]), tok(A_WIDTH), tok(B_WIDTH),
                  full(lw["wuv"]), full(lw["wout"]), full(fg)],
        out_specs=tok(d),
        out_shape=jax.ShapeDtypeStruct((n, d), F32),
        compiler_params=pltpu.CompilerParams(dimension_semantics=("arbitrary",),
                                             vmem_limit_bytes=VMEM_LIMIT_BYTES),
        name="combine",
    )(x2d, olat, ob, pr["sga"], pr["sgb"], lw["wuv"], lw["wout"], fg)


def _combine_t_kernel(x_ref, olatT_ref, obT_ref, sgaT_ref, sgbT_ref, wuvt_ref, wout_ref, fg_ref, y_ref, *, final):
    o_a = jnp.concatenate([_dot(wuvt_ref[h], olatT_ref[h * A_KV_LORA:(h + 1) * A_KV_LORA, :])
                           for h in range(A_HEADS)], axis=0)
    mix_t = jnp.concatenate([o_a * sgaT_ref[...].astype(F32),
                             obT_ref[...].astype(F32) * sgbT_ref[...].astype(F32)], axis=0).astype(BF16)
    y = x_ref[...] + lax.dot_general(mix_t, wout_ref[...], (((0,), (0,)), ((), ())),
                                     preferred_element_type=F32)
    if final:
        y = _rms(y, fg_ref[...])
    y_ref[...] = y


def _combine_t(x3d, olat_t, ob_t, pr, lw, fg, final):
    b, t, d = x3d.shape
    tm = COMBINE_TM
    rows_t = lambda r: pl.BlockSpec((None, r, tm), lambda bi, ti: (bi, 0, ti))
    full = lambda a: pl.BlockSpec(a.shape, lambda bi, ti: (0,) * a.ndim)
    xblk = pl.BlockSpec((None, tm, d), lambda bi, ti: (bi, ti, 0))
    return pl.pallas_call(
        functools.partial(_combine_t_kernel, final=final),
        grid=(b, t // tm),
        in_specs=[xblk, rows_t(A_HEADS * A_KV_LORA), rows_t(B_WIDTH), rows_t(A_WIDTH), rows_t(B_WIDTH),
                  full(lw["wuvt"]), full(lw["wout"]), full(fg)],
        out_specs=xblk,
        out_shape=jax.ShapeDtypeStruct((b, t, d), F32),
        compiler_params=pltpu.CompilerParams(dimension_semantics=("arbitrary", "arbitrary"),
                                             vmem_limit_bytes=VMEM_LIMIT_BYTES),
        name="combine_prompt",
    )(x3d, olat_t, ob_t, pr["sgaT"], pr["sgbT"], lw["wuvt"], lw["wout"], fg)


def _layer_weights(norm_g, w_in, q_norm_g, kv_norm_g, w_uq, w_uk, w_uv, w_out):
    d = w_in.shape[0]
    o = np.cumsum([0, A_Q_LORA, A_KV_LORA, A_ROPE, A_WIDTH, B_WIDTH, B_KV_HEADS * B_HEAD_DIM,
                   B_KV_HEADS * B_HEAD_DIM, IDX_HEADS * IDX_DIM, IDX_DIM, IDX_HEADS, B_WIDTH])
    w_t = w_in.T.astype(BF16)
    seg = lambda i: w_t[int(o[i]):int(o[i + 1])]
    zpad = lambda r: jnp.zeros((r, d), BF16)
    win = jnp.concatenate([seg(0), seg(1), seg(2), seg(8), seg(9), zpad(LANES - A_ROPE - IDX_DIM - IDX_HEADS),
                           seg(3), seg(4), seg(5), seg(6), seg(7), seg(10)], axis=0)
    assert win.shape[0] == IN_PAD
    half = A_ROPE // 2
    wq = jnp.concatenate([
        w_uq[:, :, :A_NOPE].reshape(A_Q_LORA, A_HEADS * A_NOPE),
        w_uq[:, :, A_NOPE:A_NOPE + half].reshape(A_Q_LORA, A_HEADS * half),
        w_uq[:, :, A_NOPE + half:].reshape(A_Q_LORA, A_HEADS * half)], axis=1)
    eye = jnp.eye(A_HEADS, dtype=w_uk.dtype)
    wuk = jnp.einsum('chn,hg->hngc', w_uk, eye).reshape(A_HEADS * A_NOPE, A_HEADS * A_KV_LORA)
    wuv = jnp.einsum('chv,hg->hcgv', w_uv, eye).reshape(A_HEADS * A_KV_LORA, A_HEADS * A_V)
    pm = np.zeros((2 * LANES, A_HEADS * LANES), np.float32)
    for h in range(A_HEADS):
        for i in range(half):
            pm[h * half + i, h * LANES + i] = 1.0
            pm[LANES + h * half + i, h * LANES + half + i] = 1.0
    wt = jnp.concatenate([seg(0), seg(1), seg(6), seg(4), seg(7), seg(3), seg(10), seg(9),
                          zpad(R_KB - R_WI - IDX_HEADS), seg(5), seg(8), seg(2)], axis=0)
    assert wt.shape[0] == ROWS_T
    bc = lambda g: jnp.broadcast_to(g.reshape(-1, 1), (g.shape[0], PROJ_T_TM))
    return {
        "wt": wt, "qngc": bc(q_norm_g), "kvngc": bc(kv_norm_g),
        "wqt": wq.T.astype(BF16),
        "wukt": w_uk.transpose(1, 0, 2).astype(BF16),
        "wuvt": w_uv.transpose(1, 2, 0).astype(BF16),
        "ng": norm_g.reshape(1, -1), "win": win,
        "qng": q_norm_g.reshape(1, -1), "kvng": kv_norm_g.reshape(1, -1),
        "wq": wq.astype(BF16), "wuk": wuk.astype(BF16), "wuv": wuv.astype(BF16),
        "pmat": jnp.asarray(pm, BF16), "wout": w_out.astype(BF16),
    }


def _rope_tables(pos):
    half = A_ROPE // 2
    inv = ROPE_THETA ** (-jnp.arange(half, dtype=F32) / half)
    ang = pos.astype(F32)[:, None] * inv[None, :]
    cos, sin = jnp.cos(ang), jnp.sin(ang)
    z = jnp.zeros((pos.shape[0], LANES - A_ROPE), F32)
    zh = jnp.zeros_like(cos)
    cosq = jnp.tile(cos, (1, A_HEADS))
    sinq = jnp.tile(sin, (1, A_HEADS))
    rc = jnp.concatenate([cos, cos, z], axis=1)
    rs1 = jnp.concatenate([zh, sin, z], axis=1)
    rs2 = jnp.concatenate([-sin, zh, z], axis=1)
    return cosq, sinq, rc, rs1, rs2


def _rope_tables_t(pos):
    half = A_ROPE // 2
    inv = ROPE_THETA ** (-jnp.arange(half, dtype=F32) / half)
    ang = pos.astype(F32)[:, None] * inv[None, :]
    return jnp.tile(jnp.cos(ang).T, (A_HEADS, 1)), jnp.tile(jnp.sin(ang).T, (A_HEADS, 1))


def kernel(x_prompt, x_sample, cache_mla_ckv, cache_mla_kpe, cache_dsa_k, cache_dsa_v, cache_dsa_kidx,
           norm_g, w_in, mla_q_norm_g, mla_kv_norm_g, mla_w_uq, mla_w_uk, mla_w_uv, rel_bias, w_out,
           final_norm_g):
    bp, tp, d = x_prompt.shape
    bs, ts, _ = x_sample.shape
    depth = w_in.shape[0]
    past = cache_mla_ckv.shape[2]
    n_top_p = min(TOP_K_MAX, tp // 4)
    n_top_s = min(TOP_K_MAX, (past + ts) // 4)
    assert ts <= CHUNK and past % CHUNK == 0 and past % LANES == 0

    rope_p = _rope_tables_t(jnp.arange(tp, dtype=jnp.int32))
    reps = PROJ_TM // ts
    rope_s = tuple(jnp.tile(a, (reps, 1)) for a in _rope_tables(past + jnp.arange(ts, dtype=jnp.int32)))

    bias_p = _bias_tables(rel_bias, (0, -TQ, -3 * TQ), TQ, TQ, True)
    win_s = 2 * LANES
    bias_s = _bias_tables(rel_bias, (-(win_s - LANES), -(past + win_s)), ts, win_s, False)
    fg = final_norm_g.reshape(1, -1)

    xp = x_prompt
    xs = x_sample.reshape(bs * ts, d)
    outs_p, outs_s = [], []
    for l in range(depth):
        lw = _layer_weights(norm_g[l], w_in[l], mla_q_norm_g[l], mla_kv_norm_g[l],
                            mla_w_uq[l], mla_w_uk[l], mla_w_uv[l], w_out[l])
        final = l == depth - 1
        pr = _project_t(xp.reshape(bp * tp, d), rope_p, lw, bp, tp)
        olat, ob = _prompt_attention(pr, bias_p, bp, tp, n_top_p)
        xp = _combine_t(xp, olat, ob, pr, lw, fg, final)
        heads_t = lambda a: a.reshape(bp, B_KV_HEADS, B_HEAD_DIM, tp).transpose(0, 3, 1, 2)
        outs_p.append((pr["ckv"].reshape(bp, tp, A_KV_LORA), pr["kpeT"].transpose(0, 2, 1),
                       heads_t(pr["kbT"]), heads_t(pr["vbT"]), pr["kidxT"].transpose(0, 2, 1)))
        ps = _project(xs, rope_s, lw, period=PROJ_TM)
        feat_t = lambda a: a.transpose(0, 2, 3, 1).reshape(bs, B_KV_HEADS * B_HEAD_DIM, past)
        caches = (cache_mla_ckv[l], cache_mla_kpe[l].transpose(0, 2, 1), feat_t(cache_dsa_k[l]),
                  feat_t(cache_dsa_v[l]), cache_dsa_kidx[l].transpose(0, 2, 1))
        olat, ob = _sample_attention(ps, caches, bias_s, bs, ts, past, n_top_s)
        xs = _combine(xs, olat, ob, ps, lw, fg, final)
        outs_s.append((ps["ckv"].reshape(bs, ts, A_KV_LORA), ps["kpe"].reshape(bs, ts, A_ROPE),
                       ps["kb"].reshape(bs, ts, B_KV_HEADS, B_HEAD_DIM),
                       ps["vb"].reshape(bs, ts, B_KV_HEADS, B_HEAD_DIM),
                       ps["kidx"].reshape(bs, ts, IDX_DIM)))

    stack = lambda outs, i: jnp.stack([o[i] for o in outs])
    return ((xp, xs.reshape(bs, ts, d))
            + tuple(stack(outs_p, i) for i in range(5)) + tuple(stack(outs_s, i) for i in range(5)))
```
